```python
import math
import jax
import jax.numpy as jnp
from jax import lax
import numpy as np

D_MODEL = 1024
BATCH = 8
SEQ = 4096
DEPTH = 4

HEAD_DIM = 64
MIX_WIDTH = 1024
A_HEADS = 4
A_KV_HEADS = 2
B_HEADS = 4
B_QK_DIM = 32
B_V_DIM = 64
C_HEADS = 4
C_Q_LORA = 256
C_KV_LORA = 128
C_NOPE = 64
C_ROPE = 32
C_V_DIM = 64
D_HEADS = 4
DILATED_CONFIGS = ((128, 1), (512, 4), (2048, 16))
D_FF = 2816
GRID_W = 64
Q_BLOCK = 128
ROPE_THETA = 10000.0
NORM_EPS = 1e-6

IN_SPLITS = (
    A_HEADS * HEAD_DIM, A_KV_HEADS * HEAD_DIM, A_KV_HEADS * HEAD_DIM,
    B_HEADS * 2 * B_QK_DIM, B_HEADS * 2 * B_QK_DIM, B_HEADS * B_V_DIM,
    C_Q_LORA, C_KV_LORA, C_ROPE,
    D_HEADS * HEAD_DIM, D_HEADS * HEAD_DIM, D_HEADS * HEAD_DIM,
)
IN_WIDTH = sum(IN_SPLITS)

kernel_name = 'hybrid_parallel_heads_encoder'


def _rmsnorm(x, g):
    xf = x.astype(jnp.float32)
    y = xf * lax.rsqrt(jnp.mean(xf * xf, axis=-1, keepdims=True) + NORM_EPS)
    return (y * g.astype(jnp.float32)).astype(x.dtype)


def _split_cols(a, sizes):
    parts, off = [], 0
    for n in sizes:
        parts.append(a[..., off:off + n])
        off += n
    return parts


def _heads(a, n_heads):
    b, s, _ = a.shape
    return a.reshape(b, s, n_heads, -1).transpose(0, 2, 1, 3)


def _rope_freqs(dim):
    return 1.0 / (ROPE_THETA ** (jnp.arange(0, dim, 2, dtype=jnp.float32) / dim))


def _rotate(x, ang):
    n = ang.shape[-1]
    x1 = x[..., :n].astype(jnp.float32)
    x2 = x[..., n:].astype(jnp.float32)
    c, s = jnp.cos(ang), jnp.sin(ang)
    return jnp.concatenate([x1 * c - x2 * s, x1 * s + x2 * c], axis=-1).astype(x.dtype)


def _axial_rope(x, ang_row, ang_col):
    half = x.shape[-1] // 2
    return jnp.concatenate([_rotate(x[..., :half], ang_row), _rotate(x[..., half:], ang_col)], axis=-1)


def _alibi_slopes(n):
    return 2.0 ** (-8.0 * jnp.arange(1, n + 1, dtype=jnp.float32) / n)


def _to_blocks(a):
    *lead, s, d = a.shape
    return jnp.moveaxis(a.reshape(*lead, s // Q_BLOCK, Q_BLOCK, d), -3, 0)


def _from_blocks(a):
    a = jnp.moveaxis(a, 0, -3)
    *lead, nb, qb, d = a.shape
    return a.reshape(*lead, nb * qb, d)


def _dense_attention(q, k, v, scale):
    def block(qb):
        s = jnp.einsum('bhgqd,bhkd->bhgqk', qb, k).astype(jnp.float32) * scale
        p = jax.nn.softmax(s, axis=-1)
        return jnp.einsum('bhgqk,bhkd->bhgqd', p.astype(v.dtype), v)
    return _from_blocks(lax.map(block, _to_blocks(q)))


def _diff_attention(q1, q2, k1, k2, v, lam, slopes, scale):
    s_len = k1.shape[-2]
    kpos = jnp.arange(s_len)

    def block(args):
        q1b, q2b, qpos = args
        dist = jnp.abs(qpos[:, None] - kpos[None, :]).astype(jnp.float32)
        bias = -slopes[:, None, None] * dist
        a1 = jax.nn.softmax(jnp.einsum('bhqd,bhkd->bhqk', q1b, k1).astype(jnp.float32) * scale + bias, axis=-1)
        a2 = jax.nn.softmax(jnp.einsum('bhqd,bhkd->bhqk', q2b, k2).astype(jnp.float32) * scale + bias, axis=-1)
        return jnp.einsum('bhqk,bhkd->bhqd', (a1 - lam * a2).astype(v.dtype), v)

    qpos = jnp.arange(s_len).reshape(-1, Q_BLOCK)
    return _from_blocks(lax.map(block, (_to_blocks(q1), _to_blocks(q2), qpos)))


def _banded_attention(q, k, v, radius, step, slopes, scale):
    b, h, r, l_len, dh = q.shape
    qb = min(Q_BLOCK, l_len)
    nb = -(-l_len // qb)
    lp = nb * qb

    def pad(a, lo, hi):
        return jnp.pad(a, ((0, 0), (0, 0), (0, 0), (lo, hi), (0, 0)))

    qp = pad(q, 0, lp - l_len).reshape(b, h, r, nb, qb, dh)
    kp = pad(k, radius, lp - l_len + radius)
    vp = pad(v, radius, lp - l_len + radius)
    band = jnp.arange(nb)[:, None] * qb + jnp.arange(qb + 2 * radius)[None, :]
    kb = kp[:, :, :, band]
    vb = vp[:, :, :, band]
    qpos = jnp.arange(lp).reshape(nb, qb)
    kpos = band - radius
    rel = jnp.abs(qpos[:, :, None] - kpos[:, None, :])
    in_range = ((kpos >= 0) & (kpos < l_len))[:, None, :]
    allowed = (rel <= radius) & (in_range | (rel == 0))
    s = jnp.einsum('bhrnqd,bhrnkd->bhrnqk', qp, kb).astype(jnp.float32) * scale
    s = s - slopes[:, None, None, None, None] * (rel * step).astype(jnp.float32)
    s = jnp.where(allowed, s, -jnp.inf)
    lse = jax.nn.logsumexp(s, axis=-1, keepdims=True)
    p = jnp.exp(s - lse)
    out = jnp.einsum('bhrnqk,bhrnkd->bhrnqd', p.astype(v.dtype), vb)
    out = out.reshape(b, h, r, lp, dh)[:, :, :, :l_len]
    lse = lse.reshape(b, h, r, lp)[:, :, :, :l_len]
    return out, lse


def _dilated_attention(q, k, v, slopes, scale):
    b, h, s_len, dh = q.shape
    outs, lses = [], []
    for window, dilation in DILATED_CONFIGS:
        l_len = s_len // dilation

        def by_residue(a):
            return a.reshape(b, h, l_len, dilation, dh).transpose(0, 1, 3, 2, 4)

        o, lse = _banded_attention(by_residue(q), by_residue(k), by_residue(v),
                                   window // (2 * dilation), dilation, slopes, scale)
        outs.append(o.transpose(0, 1, 3, 2, 4).reshape(b, h, s_len, dh))
        lses.append(lse.transpose(0, 1, 3, 2).reshape(b, h, s_len))
    w = jax.nn.softmax(jnp.stack(lses), axis=0)
    out = jnp.sum(w[..., None] * jnp.stack(outs).astype(jnp.float32), axis=0)
    return out.astype(q.dtype)


def _dwconv3(u, w, bias):
    up = jnp.pad(u, ((0, 0), (1, 1), (0, 0)))
    return up[:, :-2] * w[0] + up[:, 1:-1] * w[1] + up[:, 2:] * w[2] + bias


def setup_inputs(seed: int = 0) -> dict:
    key = jax.random.key(seed)
    ks = jax.random.split(key, 26)
    f32 = jnp.float32

    def nrm(k, shape, fan_in):
        return jax.random.normal(k, shape, f32) * fan_in ** -0.5

    def gain(k, n):
        return 1.0 + 0.02 * jax.random.normal(k, (DEPTH, n), f32)

    return {
        'x': jax.random.normal(ks[0], (BATCH, SEQ, D_MODEL), f32),
        'norm1_g': gain(ks[1], D_MODEL),
        'w_in': nrm(ks[2], (DEPTH, D_MODEL, IN_WIDTH), D_MODEL),
        'a_qn_g': gain(ks[3], HEAD_DIM),
        'a_kn_g': gain(ks[4], HEAD_DIM),
        'b_qn_g': gain(ks[5], B_QK_DIM),
        'b_kn_g': gain(ks[6], B_QK_DIM),
        'b_lam_q1': 0.1 * jax.random.normal(ks[7], (DEPTH, B_QK_DIM), f32),
        'b_lam_k1': 0.1 * jax.random.normal(ks[8], (DEPTH, B_QK_DIM), f32),
        'b_lam_q2': 0.1 * jax.random.normal(ks[9], (DEPTH, B_QK_DIM), f32),
        'b_lam_k2': 0.1 * jax.random.normal(ks[10], (DEPTH, B_QK_DIM), f32),
        'b_sub_g': gain(ks[11], B_V_DIM),
        'c_qa_g': gain(ks[12], C_Q_LORA),
        'c_kva_g': gain(ks[13], C_KV_LORA),
        'c_wqb': nrm(ks[14], (DEPTH, C_Q_LORA, C_HEADS * (C_NOPE + C_ROPE)), C_Q_LORA),
        'c_wkvb': nrm(ks[15], (DEPTH, C_KV_LORA, C_HEADS * (C_NOPE + C_V_DIM)), C_KV_LORA),
        'c_qn_g': gain(ks[16], C_NOPE + C_ROPE),
        'c_kn_g': gain(ks[17], C_NOPE + C_ROPE),
        'd_qn_g': gain(ks[18], HEAD_DIM),
        'd_kn_g': gain(ks[19], HEAD_DIM),
        'w_out': nrm(ks[20], (DEPTH, MIX_WIDTH, D_MODEL), MIX_WIDTH),
        'norm2_g': gain(ks[21], D_MODEL),
        'w_up': nrm(ks[22], (DEPTH, D_MODEL, 2 * D_FF), D_MODEL),
        'conv_w': nrm(ks[23], (DEPTH, 3, 2 * D_FF), 3),
        'conv_b': 0.02 * jax.random.normal(ks[24], (DEPTH, 2 * D_FF), f32),
        'w_down': nrm(ks[25], (DEPTH, D_FF, D_MODEL), D_FF),
    }


def reference(x, norm1_g, w_in, a_qn_g, a_kn_g, b_qn_g, b_kn_g, b_lam_q1, b_lam_k1,
              b_lam_q2, b_lam_k2, b_sub_g, c_qa_g, c_kva_g, c_wqb, c_wkvb, c_qn_g, c_kn_g,
              d_qn_g, d_kn_g, w_out, norm2_g, w_up, conv_w, conv_b, w_down):
    b, s_len, _ = x.shape
    n_rows = s_len // GRID_W
    rows, cols = jnp.meshgrid(jnp.arange(n_rows), jnp.arange(GRID_W), indexing='ij')
    rows = rows.reshape(-1).astype(jnp.float32)
    cols = cols.reshape(-1).astype(jnp.float32)
    pos = jnp.arange(s_len, dtype=jnp.float32)
    ax_freq = _rope_freqs(HEAD_DIM // 2)
    ang_row = rows[:, None] * ax_freq
    ang_col = cols[:, None] * ax_freq
    ang_mla = pos[:, None] * _rope_freqs(C_ROPE)
    slopes = _alibi_slopes(B_HEADS + D_HEADS)
    slopes_b, slopes_d = slopes[:B_HEADS], slopes[B_HEADS:]

    for l in range(DEPTH):
        h = _rmsnorm(x, norm1_g[l])
        proj = h @ w_in[l]
        (aq, ak, av, bq, bk, bv, cql, ckvl, ckr, dq, dk, dv) = _split_cols(proj, IN_SPLITS)

        qa = _axial_rope(_rmsnorm(_heads(aq, A_HEADS), a_qn_g[l]), ang_row, ang_col)
        ka = _axial_rope(_rmsnorm(_heads(ak, A_KV_HEADS), a_kn_g[l]), ang_row, ang_col)
        qa = qa.reshape(b, A_KV_HEADS, A_HEADS // A_KV_HEADS, s_len, HEAD_DIM)
        oa = _dense_attention(qa, ka, _heads(av, A_KV_HEADS), HEAD_DIM ** -0.5)
        oa = oa.reshape(b, A_HEADS, s_len, HEAD_DIM)

        qb = _heads(bq, B_HEADS)
        kb = _heads(bk, B_HEADS)
        q1 = _rmsnorm(qb[..., :B_QK_DIM], b_qn_g[l])
        q2 = _rmsnorm(qb[..., B_QK_DIM:], b_qn_g[l])
        k1 = _rmsnorm(kb[..., :B_QK_DIM], b_kn_g[l])
        k2 = _rmsnorm(kb[..., B_QK_DIM:], b_kn_g[l])
        lam_init = 0.8 - 0.6 * math.exp(-0.3 * l)
        lam = (jnp.exp(jnp.sum(b_lam_q1[l] * b_lam_k1[l]).astype(jnp.float32))
               - jnp.exp(jnp.sum(b_lam_q2[l] * b_lam_k2[l]).astype(jnp.float32)) + lam_init)
        ob = _diff_attention(q1, q2, k1, k2, _heads(bv, B_HEADS), lam, slopes_b, B_QK_DIM ** -0.5)
        ob = _rmsnorm(ob, b_sub_g[l]) * (1.0 - lam_init)

        qc = _heads(_rmsnorm(cql, c_qa_g[l]) @ c_wqb[l], C_HEADS)
        kvc = _heads(_rmsnorm(ckvl, c_kva_g[l]) @ c_wkvb[l], C_HEADS)
        k_nope, vc = kvc[..., :C_NOPE], kvc[..., C_NOPE:]
        k_rope = jnp.broadcast_to(ckr[:, None], (b, C_HEADS, s_len, C_ROPE))
        kc = jnp.concatenate([k_nope, k_rope], axis=-1)
        qc = _rmsnorm(qc, c_qn_g[l])
        kc = _rmsnorm(kc, c_kn_g[l])
        qc = jnp.concatenate([qc[..., :C_NOPE], _rotate(qc[..., C_NOPE:], ang_mla)], axis=-1)
        kc = jnp.concatenate([kc[..., :C_NOPE], _rotate(kc[..., C_NOPE:], ang_mla)], axis=-1)
        oc = _dense_attention(qc[:, :, None], kc, vc, (C_NOPE + C_ROPE) ** -0.5)[:, :, 0]

        qd = _rmsnorm(_heads(dq, D_HEADS), d_qn_g[l])
        kd = _rmsnorm(_heads(dk, D_HEADS), d_kn_g[l])
        od = _dilated_attention(qd, kd, _heads(dv, D_HEADS), slopes_d, HEAD_DIM ** -0.5)

        mix = jnp.concatenate([oa, ob, oc, od], axis=1)
        mix = mix.transpose(0, 2, 1, 3).reshape(b, s_len, MIX_WIDTH)
        x = x + mix @ w_out[l]

        h = _rmsnorm(x, norm2_g[l])
        u = _dwconv3(h @ w_up[l], conv_w[l], conv_b[l])
        gate, val = u[..., :D_FF], u[..., D_FF:]
        x = x + (jax.nn.silu(gate) * val) @ w_down[l]
    return x
```

```python
import functools
import math

import jax
import jax.numpy as jnp
from jax import lax
from jax.experimental import pallas as pl
from jax.experimental.pallas import tpu as pltpu

F32 = jnp.float32
BF16 = jnp.bfloat16

HEAD_DIM = 64
N_HEADS = 4
A_KV_HEADS = 2
B_QK_DIM = 32
C_Q_LORA = 256
C_KV_LORA = 128
C_NOPE = 64
C_ROPE = 32
C_QK = C_NOPE + C_ROPE
DILATED_CONFIGS = ((128, 1), (512, 4), (2048, 16))
GRID_W = 64
ROPE_THETA = 10000.0
NORM_EPS = 1e-6
LOG2E = 1.4426950408889634

_SPLITS = (256, 128, 128, 256, 256, 256, C_Q_LORA, C_KV_LORA, C_ROPE, 256, 256, 256)
_OFF = [0]
for _n in _SPLITS:
    _OFF.append(_OFF[-1] + _n)
(O_AQ, O_AK, O_AV, O_BQ, O_BK, O_BV, O_CQL, O_CKVL, O_CKR, O_DQ, O_DK, O_DV, IN_WIDTH) = _OFF

_GAIN_SIZES = (("n1", 1024), ("aq", 64), ("ak", 64), ("bq", 32), ("bk", 32), ("cqa", 256), ("ckva", 128),
               ("cqn", 96), ("ckn", 96), ("dq", 64), ("dk", 64))
G_OFF = {}
_o = 0
for _k, _n in _GAIN_SIZES:
    G_OFF[_k] = (_o, _n)
    _o += _n
G_TOTAL = _o

LANES = 128
TOK_TILE = 512
TQ = 256
TK = 512
BAND_Q = 128
BAND_RADIUS = 64
VMEM_LIMIT = 56 * 1024 * 1024


def _cparams(sem):
    return pltpu.CompilerParams(dimension_semantics=sem, vmem_limit_bytes=VMEM_LIMIT)


def _rms_rows(v, g_col):
    ms = jnp.mean(v * v, axis=0, keepdims=True)
    return v * lax.rsqrt(ms + NORM_EPS) * g_col


def _swap_halves(x, n):
    parts = []
    for i in range(0, x.shape[0], 2 * n):
        parts.append(x[i + n:i + 2 * n])
        parts.append(x[i:i + n])
    return jnp.concatenate(parts, axis=0)


def _rope_rows(x, cos, sin_signed, n):
    return x * cos + _swap_halves(x, n) * sin_signed


def _transpose_body(x_ref, o_ref):
    o_ref[0] = x_ref[0].T


def _transpose_last2(x, tr, tc):
    b, r, c = x.shape
    return pl.pallas_call(
        _transpose_body,
        out_shape=jax.ShapeDtypeStruct((b, c, r), x.dtype),
        grid=(b, r // tr, c // tc),
        in_specs=[pl.BlockSpec((1, tr, tc), lambda i, j, k: (i, j, k))],
        out_specs=pl.BlockSpec((1, tc, tr), lambda i, j, k: (i, k, j)),
        compiler_params=_cparams(("parallel", "arbitrary", "arbitrary")),
        name="transpose_tokens",
    )(x)


def _proj_body(x_ref, w_ref, wqb_ref, wkvb_ref, g_ref, cosa_ref, sina_ref, cosm_ref, sinm_ref,
               qa_ref, ka_ref, va_ref, qb_ref, kb_ref, vb_ref, qc_ref, kc_ref, vc_ref,
               dq_ref, dk_ref, dv_ref):
    tm = x_ref.shape[2]

    def gain(name):
        o, n = G_OFF[name]
        return g_ref[0, o:o + n, :]

    x = x_ref[0]
    h = _rms_rows(x, gain("n1")).astype(BF16)
    proj = jnp.dot(w_ref[0], h, preferred_element_type=F32)

    cosa, sina = cosa_ref[...], sina_ref[...]
    cosm, sinm = cosm_ref[...], sinm_ref[...]
    zeros64 = jnp.zeros((64, tm), F32)
    zeros32 = jnp.zeros((32, tm), F32)

    sa = HEAD_DIM ** -0.5 * LOG2E
    for hh in range(N_HEADS):
        q = proj[O_AQ + 64 * hh:O_AQ + 64 * hh + 64]
        q = _rope_rows(_rms_rows(q, gain("aq")), cosa, sina, 16) * sa
        qa_ref[0, 64 * hh:64 * hh + 64, :] = q.astype(BF16)
    for g in range(A_KV_HEADS):
        k = proj[O_AK + 64 * g:O_AK + 64 * g + 64]
        k = _rope_rows(_rms_rows(k, gain("ak")), cosa, sina, 16)
        kpad = jnp.concatenate([k, zeros64], axis=0)
        ka_ref[0, :, LANES * g:LANES * (g + 1)] = kpad.T.astype(BF16)
    va_ref[0, 0] = proj[O_AV:O_AV + 128].astype(BF16)

    sb = B_QK_DIM ** -0.5 * LOG2E
    for i in range(2 * N_HEADS):
        q = proj[O_BQ + 32 * i:O_BQ + 32 * i + 32]
        qb_ref[0, 32 * i:32 * i + 32, :] = (_rms_rows(q, gain("bq")) * sb).astype(BF16)
    for hh in range(N_HEADS):
        k1 = _rms_rows(proj[O_BK + 64 * hh:O_BK + 64 * hh + 32], gain("bk"))
        k2 = _rms_rows(proj[O_BK + 64 * hh + 32:O_BK + 64 * hh + 64], gain("bk"))
        kpad = jnp.concatenate([k1, k2, zeros64], axis=0)
        kb_ref[0, :, LANES * hh:LANES * (hh + 1)] = kpad.T.astype(BF16)
    vb_ref[0, 0] = proj[O_BV:O_BV + 256].astype(BF16)

    sc = C_QK ** -0.5 * LOG2E
    cq = _rms_rows(proj[O_CQL:O_CQL + C_Q_LORA], gain("cqa")).astype(BF16)
    qc = jnp.dot(wqb_ref[0], cq, preferred_element_type=F32)
    ckv = _rms_rows(proj[O_CKVL:O_CKVL + C_KV_LORA], gain("ckva")).astype(BF16)
    kvc = jnp.dot(wkvb_ref[0], ckv, preferred_element_type=F32)
    kr = proj[O_CKR:O_CKR + C_ROPE]
    for hh in range(N_HEADS):
        q = _rms_rows(qc[C_QK * hh:C_QK * (hh + 1)], gain("cqn"))
        q = jnp.concatenate([q[:C_NOPE], _rope_rows(q[C_NOPE:], cosm, sinm, 16)], axis=0) * sc
        qc_ref[0, C_QK * hh:C_QK * (hh + 1), :] = q.astype(BF16)
        kk = jnp.concatenate([kvc[128 * hh:128 * hh + C_NOPE], kr], axis=0)
        kk = _rms_rows(kk, gain("ckn"))
        kk = jnp.concatenate([kk[:C_NOPE], _rope_rows(kk[C_NOPE:], cosm, sinm, 16), zeros32], axis=0)
        kc_ref[0, :, LANES * hh:LANES * (hh + 1)] = kk.T.astype(BF16)
        vc_ref[0, 0, 64 * hh:64 * hh + 64, :] = kvc[128 * hh + C_NOPE:128 * (hh + 1)].astype(BF16)

    sd = HEAD_DIM ** -0.5 * LOG2E
    for g in range(2):
        qs, ks = [], []
        for hh in (2 * g, 2 * g + 1):
            qs.append(_rms_rows(proj[O_DQ + 64 * hh:O_DQ + 64 * hh + 64], gain("dq")) * sd)
            ks.append(_rms_rows(proj[O_DK + 64 * hh:O_DK + 64 * hh + 64], gain("dk")))
        dq_ref[0, :, LANES * g:LANES * (g + 1)] = jnp.concatenate(qs, axis=0).T.astype(BF16)
        dk_ref[0, :, LANES * g:LANES * (g + 1)] = jnp.concatenate(ks, axis=0).T.astype(BF16)
        dv_ref[0, :, LANES * g:LANES * (g + 1)] = proj[O_DV + 128 * g:O_DV + 128 * (g + 1)].T.astype(BF16)


def _project(xt, l, w_in_t, wqb_t, wkvb_t, gains, cosa, sina, cosm, sinm):
    b, d, s = xt.shape
    tm = TOK_TILE
    nt = s // tm
    fm = lambda rows: jax.ShapeDtypeStruct((b, rows, s), BF16)
    tmj = lambda cols: jax.ShapeDtypeStruct((b, s, cols), BF16)
    vblk = lambda rows: jax.ShapeDtypeStruct((b, nt, rows, tm), BF16)
    out_shape = (fm(256), tmj(256), vblk(128),
                 fm(256), tmj(512), vblk(256),
                 fm(384), tmj(512), vblk(256),
                 tmj(256), tmj(256), tmj(256))
    fm_spec = lambda rows: pl.BlockSpec((1, rows, tm), lambda i, j: (i, 0, j))
    tm_spec = lambda cols: pl.BlockSpec((1, tm, cols), lambda i, j: (i, j, 0))
    v_spec = lambda rows: pl.BlockSpec((1, 1, rows, tm), lambda i, j: (i, j, 0, 0))
    out_specs = (fm_spec(256), tm_spec(256), v_spec(128),
                 fm_spec(256), tm_spec(512), v_spec(256),
                 fm_spec(384), tm_spec(512), v_spec(256),
                 tm_spec(256), tm_spec(256), tm_spec(256))
    lay = lambda i, j: (l, 0, 0)
    in_specs = [
        pl.BlockSpec((1, d, tm), lambda i, j: (i, 0, j)),
        pl.BlockSpec((1, IN_WIDTH, d), lay),
        pl.BlockSpec((1, N_HEADS * C_QK, C_Q_LORA), lay),
        pl.BlockSpec((1, N_HEADS * 128, C_KV_LORA), lay),
        pl.BlockSpec((1, G_TOTAL, 1), lay),
        pl.BlockSpec((64, tm), lambda i, j: (0, j)),
        pl.BlockSpec((64, tm), lambda i, j: (0, j)),
        pl.BlockSpec((32, tm), lambda i, j: (0, j)),
        pl.BlockSpec((32, tm), lambda i, j: (0, j)),
    ]
    return pl.pallas_call(
        _proj_body,
        out_shape=out_shape,
        grid=(b, nt),
        in_specs=in_specs,
        out_specs=out_specs,
        compiler_params=_cparams(("parallel", "arbitrary")),
        name="project_heads",
    )(xt, w_in_t, wqb_t, wkvb_t, gains, cosa, sina, cosm, sinm)


def _online_update(s, v, carry):
    m, l, acc = carry
    m_new = jnp.maximum(m, jnp.max(s, axis=0, keepdims=True))
    p = jnp.exp2(s - m_new)
    alpha = jnp.exp2(m - m_new)
    l = alpha * l + jnp.sum(p, axis=0, keepdims=True)
    acc = alpha * acc + jnp.dot(v, p.astype(BF16), preferred_element_type=F32)
    return m_new, l, acc


def _init_carry(dv, tq):
    return (jnp.full((1, tq), -jnp.inf, F32), jnp.zeros((1, tq), F32), jnp.zeros((dv, tq), F32))


def _dense_body(q_ref, k_ref, v_ref, o_ref):
    dq, tq = q_ref.shape[1], q_ref.shape[2]
    nkb, dv, tk = v_ref.shape[1], v_ref.shape[2], v_ref.shape[3]
    q = q_ref[0]
    qpad = jnp.concatenate([q, jnp.zeros((LANES - dq, tq), BF16)], axis=0)

    def step(kb, carry):
        k = k_ref[0, pl.ds(pl.multiple_of(kb * tk, tk), tk), :]
        s = jnp.dot(k, qpad, preferred_element_type=F32)
        return _online_update(s, v_ref[0, kb], carry)

    m, l, acc = lax.fori_loop(0, nkb, step, _init_carry(dv, tq))
    o_ref[0] = (acc / l).astype(o_ref.dtype)


def _dense_attention(q, k, v, dq, q_per_kv):
    b, rows, s = q.shape
    nh = rows // dq
    nkb, tk = v.shape[1], v.shape[3]
    return pl.pallas_call(
        _dense_body,
        out_shape=jax.ShapeDtypeStruct((b, nh * HEAD_DIM, s), BF16),
        grid=(b, nh, s // TQ),
        in_specs=[
            pl.BlockSpec((1, dq, TQ), lambda i, h, j: (i, h, j)),
            pl.BlockSpec((1, s, LANES), lambda i, h, j: (i, 0, h // q_per_kv)),
            pl.BlockSpec((1, nkb, HEAD_DIM, tk), lambda i, h, j: (i, 0, h // q_per_kv, 0)),
        ],
        out_specs=pl.BlockSpec((1, HEAD_DIM, TQ), lambda i, h, j: (i, h, j)),
        compiler_params=_cparams(("parallel", "arbitrary", "arbitrary")),
        name="dense_attention",
    )(q, k, v)


def _diff_body(sc_ref, q_ref, k_ref, v_ref, g_ref, o_ref):
    tq = q_ref.shape[2]
    nkb, dv, tk = v_ref.shape[1], v_ref.shape[2], v_ref.shape[3]
    hh = pl.program_id(1)
    jq = pl.program_id(2)
    lam = sc_ref[0]
    out_scale = sc_ref[1]
    slope = sc_ref[2 + hh] * LOG2E
    q = q_ref[0]
    z32 = jnp.zeros((B_QK_DIM, tq), BF16)
    z64 = jnp.zeros((64, tq), BF16)
    q1 = jnp.concatenate([q[:B_QK_DIM], z32, z64], axis=0)
    q2 = jnp.concatenate([z32, q[B_QK_DIM:], z64], axis=0)
    base = (lax.broadcasted_iota(jnp.int32, (tk, tq), 0) - lax.broadcasted_iota(jnp.int32, (tk, tq), 1)).astype(F32)

    def step(kb, carry):
        c1, c2 = carry
        k = k_ref[0, pl.ds(pl.multiple_of(kb * tk, tk), tk), :]
        off = (kb * tk - jq * tq).astype(F32)
        bias = jnp.abs(base + off) * (-slope)
        v = v_ref[0, kb]
        s1 = jnp.dot(k, q1, preferred_element_type=F32) + bias
        s2 = jnp.dot(k, q2, preferred_element_type=F32) + bias
        return _online_update(s1, v, c1), _online_update(s2, v, c2)

    (m1, l1, a1), (m2, l2, a2) = lax.fori_loop(0, nkb, step, (_init_carry(dv, tq), _init_carry(dv, tq)))
    o = a1 / l1 - lam * (a2 / l2)
    o = _rms_rows(o, g_ref[0]) * out_scale
    o_ref[0] = o.astype(o_ref.dtype)


def _diff_attention(scal, q, k, v, sub_g, l):
    b, rows, s = q.shape
    nkb, tk = v.shape[1], v.shape[3]
    return pl.pallas_call(
        _diff_body,
        out_shape=jax.ShapeDtypeStruct((b, rows, s), BF16),
        grid=(b, N_HEADS, s // TQ),
        in_specs=[
            pl.BlockSpec(memory_space=pltpu.SMEM),
            pl.BlockSpec((1, 64, TQ), lambda i, h, j: (i, h, j)),
            pl.BlockSpec((1, s, LANES), lambda i, h, j: (i, 0, h)),
            pl.BlockSpec((1, nkb, HEAD_DIM, tk), lambda i, h, j: (i, 0, h, 0)),
            pl.BlockSpec((1, HEAD_DIM, 1), lambda i, h, j: (l, 0, 0)),
        ],
        out_specs=pl.BlockSpec((1, HEAD_DIM, TQ), lambda i, h, j: (i, h, j)),
        compiler_params=_cparams(("parallel", "arbitrary", "arbitrary")),
        name="diff_attention",
    )(scal, q, k, v, sub_g)


def _band_body(q_ref, k_ref, v_ref, o_ref, lse_ref, *, step_len, slopes):
    qb = q_ref.shape[1]
    l_len = k_ref.shape[1]
    win = min(qb + 2 * BAND_RADIUS, l_len)
    jb = pl.program_id(2)
    start = jnp.clip(jb * qb - BAND_RADIUS, 0, l_len - win)
    start = pl.multiple_of(start, BAND_RADIUS)
    q = q_ref[0]
    kwin = k_ref[0, pl.ds(start, win), :]
    vwin = v_ref[0, pl.ds(start, win), :]
    qpos = jb * qb + lax.broadcasted_iota(jnp.int32, (qb, win), 0)
    kpos = start + lax.broadcasted_iota(jnp.int32, (qb, win), 1)
    rel_i = jnp.abs(qpos - kpos)
    allowed = rel_i <= BAND_RADIUS
    rel = rel_i.astype(F32)
    lane = lax.broadcasted_iota(jnp.int32, (qb, LANES), 1)
    for g in range(2):
        q2 = q[:, LANES * g:LANES * (g + 1)].astype(F32)
        k2 = kwin[:, LANES * g:LANES * (g + 1)]
        v2 = vwin[:, LANES * g:LANES * (g + 1)]
        out_g = jnp.zeros((qb, LANES), F32)
        lse_g = jnp.zeros((qb, LANES), F32)
        for hh in range(2):
            own = (lane >= HEAD_DIM * hh) & (lane < HEAD_DIM * (hh + 1))
            qm = jnp.where(own, q2, 0.0).astype(BF16)
            s = lax.dot_general(qm, k2, (((1,), (1,)), ((), ())), preferred_element_type=F32)
            s = s - (slopes[2 * g + hh] * step_len * LOG2E) * rel
            s = jnp.where(allowed, s, -jnp.inf)
            m = jnp.max(s, axis=1, keepdims=True)
            p = jnp.exp2(s - m)
            lsum = jnp.sum(p, axis=1, keepdims=True)
            o = jnp.dot(p.astype(BF16), v2, preferred_element_type=F32) / lsum
            lse = m + jnp.log2(lsum)
            out_g = jnp.where(own, o, out_g)
            lse_g = jnp.where(own, lse, lse_g)
        o_ref[0, :, LANES * g:LANES * (g + 1)] = out_g
        lse_ref[0, :, LANES * g:LANES * (g + 1)] = lse_g


def _band_attention(q, k, v, dil, slopes):
    b, s, c = q.shape
    l_len = s // dil
    qb = min(BAND_Q, l_len)
    view = lambda a: a.reshape(b, l_len, dil * c)
    body = functools.partial(_band_body, step_len=float(dil), slopes=slopes)
    o, lse = pl.pallas_call(
        body,
        out_shape=(jax.ShapeDtypeStruct((b, l_len, dil * c), F32),) * 2,
        grid=(b, dil, l_len // qb),
        in_specs=[
            pl.BlockSpec((1, qb, c), lambda i, r, j: (i, j, r)),
            pl.BlockSpec((1, l_len, c), lambda i, r, j: (i, 0, r)),
            pl.BlockSpec((1, l_len, c), lambda i, r, j: (i, 0, r)),
        ],
        out_specs=(pl.BlockSpec((1, qb, c), lambda i, r, j: (i, j, r)),) * 2,
        compiler_params=_cparams(("parallel", "arbitrary", "arbitrary")),
        name="band_attention_d%d" % dil,
    )(view(q), view(k), view(v))
    return o.reshape(b, s, c), lse.reshape(b, s, c)


def _band_combine_body(*refs):
    n = (len(refs) - 1) // 2
    outs = [refs[2 * i][0] for i in range(n)]
    lses = [refs[2 * i + 1][0] for i in range(n)]
    m = functools.reduce(jnp.maximum, lses)
    ws = [jnp.exp2(x - m) for x in lses]
    num = functools.reduce(lambda a, c: a + c, [w * o for w, o in zip(ws, outs)])
    den = functools.reduce(lambda a, c: a + c, ws)
    refs[-1][0] = (num / den).astype(refs[-1].dtype)


def _band_combine(pairs):
    b, s, c = pairs[0][0].shape
    spec = pl.BlockSpec((1, TOK_TILE, c), lambda i, j: (i, j, 0))
    flat = [a for p in pairs for a in p]
    return pl.pallas_call(
        _band_combine_body,
        out_shape=jax.ShapeDtypeStruct((b, s, c), BF16),
        grid=(b, s // TOK_TILE),
        in_specs=[spec] * len(flat),
        out_specs=spec,
        compiler_params=_cparams(("parallel", "arbitrary")),
        name="band_combine",
    )(*flat)


def _outproj_body(x_ref, oa_ref, ob_ref, oc_ref, od_ref, w_ref, g_ref, x1_ref, h2_ref):
    mix = jnp.concatenate([oa_ref[0], ob_ref[0], oc_ref[0]], axis=0)
    w = w_ref[0]
    y = jnp.dot(w[:, :768], mix, preferred_element_type=F32)
    y = y + lax.dot_general(w[:, 768:], od_ref[0], (((1,), (1,)), ((), ())), preferred_element_type=F32)
    x1 = x_ref[0] + y
    x1_ref[0] = x1
    h2_ref[0] = _rms_rows(x1, g_ref[0]).astype(BF16)


def _out_project(xt, oa, ob, oc, od, w_out_t, g2, l):
    b, d, s = xt.shape
    tm = TOK_TILE
    fm = lambda rows: pl.BlockSpec((1, rows, tm), lambda i, j: (i, 0, j))
    return pl.pallas_call(
        _outproj_body,
        out_shape=(jax.ShapeDtypeStruct((b, d, s), F32), jax.ShapeDtypeStruct((b, d, s), BF16)),
        grid=(b, s // tm),
        in_specs=[fm(d), fm(256), fm(256), fm(256),
                  pl.BlockSpec((1, tm, 256), lambda i, j: (i, j, 0)),
                  pl.BlockSpec((1, d, d), lambda i, j: (l, 0, 0)),
                  pl.BlockSpec((1, d, 1), lambda i, j: (l, 0, 0))],
        out_specs=(fm(d), fm(d)),
        compiler_params=_cparams(("parallel", "arbitrary")),
        name="out_project",
    )(xt, oa, ob, oc, od, w_out_t, g2)


def _halo_body(w_ref, h_ref, o_ref):
    o_ref[...] = jnp.dot(w_ref[0], h_ref[...], preferred_element_type=F32)


def _halo_up(w_up_t, halo_h, l):
    _, ff2, d = w_up_t.shape
    nh = halo_h.shape[1]
    rows = ff2 // 4
    return pl.pallas_call(
        _halo_body,
        out_shape=jax.ShapeDtypeStruct((ff2, nh), F32),
        grid=(4,),
        in_specs=[pl.BlockSpec((1, rows, d), lambda i: (l, i, 0)),
                  pl.BlockSpec((d, nh), lambda i: (0, 0))],
        out_specs=pl.BlockSpec((rows, nh), lambda i: (i, 0)),
        compiler_params=_cparams(("arbitrary",)),
        name="mlp_halo_up",
    )(w_up_t, halo_h)


def _mlp_body(x1_ref, h2_ref, wup_ref, wdn_ref, par_ref, halo_ref, o_ref, *, n_chunks):
    tf = h2_ref.shape[2]
    ff = wdn_ref.shape[2]
    rc = ff // n_chunks
    nt = pl.num_programs(1)
    col = (pl.program_id(0) * nt + pl.program_id(1)) * 2
    grp = col // LANES
    cin = col % LANES
    shift_l = (LANES - cin) % LANES
    shift_r = LANES - 2 - cin
    h = h2_ref[0]
    lane = lax.broadcasted_iota(jnp.int32, (rc, tf), 1)
    first = lane == 0
    last = lane == tf - 1
    reps = tf // LANES

    def conv(u, r0):
        hal = halo_ref[grp, r0:r0 + rc, :]
        left = pltpu.roll(hal, shift_l, 1)
        right = pltpu.roll(hal, shift_r, 1)
        prev = jnp.where(first, jnp.concatenate([left] * reps, axis=1), pltpu.roll(u, 1, 1))
        nxt = jnp.where(last, jnp.concatenate([right] * reps, axis=1), pltpu.roll(u, tf - 1, 1))
        p = par_ref[0, r0:r0 + rc, :]
        return p[:, 0:1] * prev + p[:, 1:2] * u + p[:, 2:3] * nxt + p[:, 3:4]

    y = x1_ref[0]
    for c in range(n_chunks):
        ug = jnp.dot(wup_ref[0, c * rc:(c + 1) * rc, :], h, preferred_element_type=F32)
        uv = jnp.dot(wup_ref[0, ff + c * rc:ff + (c + 1) * rc, :], h, preferred_element_type=F32)
        gate = conv(ug, c * rc)
        val = conv(uv, ff + c * rc)
        act = gate / (1.0 + jnp.exp(-gate)) * val
        y = y + jnp.dot(wdn_ref[0, :, c * rc:(c + 1) * rc], act.astype(BF16), preferred_element_type=F32)
    o_ref[0] = y


def _mlp(x1t, h2t, w_up_t, w_down_t, conv_par, halo_up, l, n_chunks):
    b, d, s = x1t.shape
    tf = TOK_TILE
    _, ff2, _ = w_up_t.shape
    ngrp = halo_up.shape[0]
    lay = lambda i, j: (l, 0, 0)
    body = functools.partial(_mlp_body, n_chunks=n_chunks)
    const = dict(pipeline_mode=pl.Buffered(1))
    return pl.pallas_call(
        body,
        out_shape=jax.ShapeDtypeStruct((b, d, s), F32),
        grid=(b, s // tf),
        in_specs=[pl.BlockSpec((1, d, tf), lambda i, j: (i, 0, j)),
                  pl.BlockSpec((1, d, tf), lambda i, j: (i, 0, j)),
                  pl.BlockSpec((1, ff2, d), lay, **const),
                  pl.BlockSpec((1, d, ff2 // 2), lay, **const),
                  pl.BlockSpec((1, ff2, 8), lay, **const),
                  pl.BlockSpec((ngrp, ff2, LANES), lambda i, j: (0, 0, 0), **const)],
        out_specs=pl.BlockSpec((1, d, tf), lambda i, j: (i, 0, j)),
        compiler_params=_cparams(("arbitrary", "arbitrary")),
        name="gated_conv_mlp",
    )(x1t, h2t, w_up_t, w_down_t, conv_par, halo_up)


def _halo_columns(h2t, tf):
    b, d, s = h2t.shape
    nt = s // tf
    zero = jnp.zeros((b, d, 1), h2t.dtype)
    left = jnp.concatenate([zero, h2t[:, :, tf - 1:s - 1:tf]], axis=2)
    right = jnp.concatenate([h2t[:, :, tf::tf], zero], axis=2)
    cols = jnp.stack([left, right], axis=3)
    cols = cols.transpose(1, 0, 2, 3).reshape(d, b * nt * 2)
    pad = (-cols.shape[1]) % LANES
    return jnp.pad(cols, ((0, 0), (0, pad)))


def _rope_tables(s):
    def freqs(dim):
        return 1.0 / (ROPE_THETA ** (jnp.arange(0, dim, 2, dtype=F32) / dim))
    t = jnp.arange(s)
    rows = (t // GRID_W).astype(F32)
    cols = (t % GRID_W).astype(F32)
    pos = t.astype(F32)
    fa = freqs(HEAD_DIM // 2)
    ar, ac = fa[:, None] * rows[None, :], fa[:, None] * cols[None, :]
    am = freqs(C_ROPE)[:, None] * pos[None, :]
    cosa = jnp.concatenate([jnp.cos(ar), jnp.cos(ar), jnp.cos(ac), jnp.cos(ac)], axis=0)
    sina = jnp.concatenate([-jnp.sin(ar), jnp.sin(ar), -jnp.sin(ac), jnp.sin(ac)], axis=0)
    cosm = jnp.concatenate([jnp.cos(am), jnp.cos(am)], axis=0)
    sinm = jnp.concatenate([-jnp.sin(am), jnp.sin(am)], axis=0)
    return cosa, sina, cosm, sinm


def kernel(x, norm1_g, w_in, a_qn_g, a_kn_g, b_qn_g, b_kn_g, b_lam_q1, b_lam_k1, b_lam_q2, b_lam_k2, b_sub_g,
           c_qa_g, c_kva_g, c_wqb, c_wkvb, c_qn_g, c_kn_g, d_qn_g, d_kn_g, w_out, norm2_g, w_up, conv_w, conv_b,
           w_down):
    b, s, d = x.shape
    depth = w_in.shape[0]
    ff = w_down.shape[1]
    assert s % TOK_TILE == 0 and s % TK == 0 and TOK_TILE == TK and s % GRID_W == 0
    for window, dil in DILATED_CONFIGS:
        assert window // (2 * dil) == BAND_RADIUS and (s // dil) % min(BAND_Q, s // dil) == 0

    tr = lambda w: jnp.swapaxes(w, 1, 2).astype(BF16)
    w_in_t, wqb_t, wkvb_t, w_out_t, w_up_t, w_down_t = (tr(w) for w in (w_in, c_wqb, c_wkvb, w_out, w_up, w_down))
    gains = jnp.concatenate([norm1_g, a_qn_g, a_kn_g, b_qn_g, b_kn_g, c_qa_g, c_kva_g, c_qn_g, c_kn_g,
                             d_qn_g, d_kn_g], axis=1)[:, :, None]
    g2 = norm2_g[:, :, None]
    sub_g = b_sub_g[:, :, None]
    conv_par = jnp.concatenate([jnp.swapaxes(conv_w, 1, 2), conv_b[:, :, None],
                                jnp.zeros((depth, 2 * ff, 4), F32)], axis=2)
    cosa, sina, cosm, sinm = _rope_tables(s)
    slopes = [2.0 ** (-8.0 * i / (2 * N_HEADS)) for i in range(1, 2 * N_HEADS + 1)]
    slopes_b, slopes_d = slopes[:N_HEADS], tuple(slopes[N_HEADS:])
    n_chunks = 2 if (ff // 2) % LANES == 0 else 1

    xt = _transpose_last2(x, TOK_TILE, d)
    for l in range(depth):
        lam_init = 0.8 - 0.6 * math.exp(-0.3 * l)
        lam = (jnp.exp(jnp.sum(b_lam_q1[l] * b_lam_k1[l])) - jnp.exp(jnp.sum(b_lam_q2[l] * b_lam_k2[l])) + lam_init)
        scal = jnp.concatenate([jnp.stack([lam, jnp.asarray(1.0 - lam_init, F32)]),
                                jnp.asarray(slopes_b, F32), jnp.zeros((2,), F32)]).astype(F32)

        (qa, ka, va, qb, kb, vb, qc, kc, vc, dq, dk, dv) = _project(
            xt, l, w_in_t, wqb_t, wkvb_t, gains, cosa, sina, cosm, sinm)
        oa = _dense_attention(qa, ka, va, HEAD_DIM, N_HEADS // A_KV_HEADS)
        ob = _diff_attention(scal, qb, kb, vb, sub_g, l)
        oc = _dense_attention(qc, kc, vc, C_QK, 1)
        od = _band_combine([_band_attention(dq, dk, dv, dil, slopes_d) for _, dil in DILATED_CONFIGS])
        x1t, h2t = _out_project(xt, oa, ob, oc, od, w_out_t, g2, l)

        halo = _halo_up(w_up_t, _halo_columns(h2t, TOK_TILE), l)
        halo = halo.reshape(2 * ff, -1, LANES).transpose(1, 0, 2)
        xt = _mlp(x1t, h2t, w_up_t, w_down_t, conv_par, halo, l, n_chunks)
    return _transpose_last2(xt, d, TOK_TILE)
```

```python
import functools
import math

import jax
import jax.numpy as jnp
from jax import lax
from jax.experimental import pallas as pl
from jax.experimental.pallas import tpu as pltpu

F32 = jnp.float32
BF16 = jnp.bfloat16

HEAD_DIM = 64
N_HEADS = 4
A_KV_HEADS = 2
B_QK_DIM = 32
C_Q_LORA = 256
C_KV_LORA = 128
C_NOPE = 64
C_ROPE = 32
C_QK = C_NOPE + C_ROPE
DILATED_CONFIGS = ((128, 1), (512, 4), (2048, 16))
D_WIDTH = N_HEADS * HEAD_DIM
GRID_W = 64
ROPE_THETA = 10000.0
NORM_EPS = 1e-6
LOG2E = 1.4426950408889634

_SPLITS = (256, 128, 128, 256, 256, 256, C_Q_LORA, C_KV_LORA, C_ROPE, 256, 256, 256)
_OFF = [0]
for _n in _SPLITS:
    _OFF.append(_OFF[-1] + _n)
(O_AQ, O_AK, O_AV, O_BQ, O_BK, O_BV, O_CQL, O_CKVL, O_CKR, O_DQ, O_DK, O_DV, IN_WIDTH) = _OFF

_GAIN_SIZES = (("n1", 1024), ("aq", 64), ("ak", 64), ("bq", 32), ("bk", 32), ("cqa", 256), ("ckva", 128),
               ("cqn", 96), ("ckn", 96), ("dq", 64), ("dk", 64))
G_OFF = {}
_o = 0
for _k, _n in _GAIN_SIZES:
    G_OFF[_k] = (_o, _n)
    _o += _n
G_TOTAL = _o

LANES = 128
BF16_ROWS = 16
TOK_TILE = 512
TQ = 512
TK = 512
BAND_Q = 128
BAND_RADIUS = 64
VMEM_LIMIT = 56 * 1024 * 1024


def _cparams(sem):
    return pltpu.CompilerParams(dimension_semantics=sem, vmem_limit_bytes=VMEM_LIMIT)


def _rms_rows(v, g_col):
    ms = jnp.mean(v * v, axis=0, keepdims=True)
    return v * lax.rsqrt(ms + NORM_EPS) * g_col


def _swap_halves(x, n):
    parts = []
    for i in range(0, x.shape[0], 2 * n):
        parts.append(x[i + n:i + 2 * n])
        parts.append(x[i:i + n])
    return jnp.concatenate(parts, axis=0)


def _rope_rows(x, cos, sin_signed, n):
    return x * cos + _swap_halves(x, n) * sin_signed


def _transpose_body(x_ref, o_ref):
    o_ref[0] = x_ref[0].T


def _transpose_last2(x, tr, tc):
    b, r, c = x.shape
    return pl.pallas_call(
        _transpose_body,
        out_shape=jax.ShapeDtypeStruct((b, c, r), x.dtype),
        grid=(b, r // tr, c // tc),
        in_specs=[pl.BlockSpec((1, tr, tc), lambda i, j, k: (i, j, k))],
        out_specs=pl.BlockSpec((1, tc, tr), lambda i, j, k: (i, k, j)),
        compiler_params=_cparams(("parallel", "arbitrary", "arbitrary")),
        name="transpose_tokens",
    )(x)


def _store_dilated(scr_ref, val, out_refs):
    tm, c = val.shape
    for g in range(c // LANES):
        scr_ref[g] = val[:, LANES * g:LANES * (g + 1)]
    for (_, dil), out in zip(DILATED_CONFIGS, out_refs):
        if dil == 1:
            out[0] = val.astype(BF16)
            continue
        n = tm // dil
        for r in range(dil):
            for g in range(c // LANES):
                lo = c * r + LANES * g
                out[0, :, lo:lo + LANES] = scr_ref[g, pl.ds(r, n, stride=dil), :].astype(BF16)


def _proj_body(x_ref, w_ref, wqb_ref, wkvb_ref, g_ref, cosa_ref, sina_ref, cosm_ref, sinm_ref,
               qa_ref, ka_ref, va_ref, qb_ref, kb_ref, vb_ref, qc_ref, kc_ref, vc_ref,
               dq1_ref, dq4_ref, dq16_ref, dk1_ref, dk4_ref, dk16_ref, dv1_ref, dv4_ref, dv16_ref,
               scr_ref):
    tm = x_ref.shape[2]

    def gain(name):
        o, n = G_OFF[name]
        return g_ref[o:o + n, :]

    x = x_ref[0]
    h = _rms_rows(x, gain("n1")).astype(BF16)
    proj = jnp.dot(w_ref[...], h, preferred_element_type=F32)

    cosa, sina = cosa_ref[...], sina_ref[...]
    cosm, sinm = cosm_ref[...], sinm_ref[...]
    zeros64 = jnp.zeros((64, tm), F32)
    zeros32 = jnp.zeros((32, tm), F32)

    sa = HEAD_DIM ** -0.5 * LOG2E
    for hh in range(N_HEADS):
        q = proj[O_AQ + 64 * hh:O_AQ + 64 * hh + 64]
        q = _rope_rows(_rms_rows(q, gain("aq")), cosa, sina, 16) * sa
        qa_ref[0, 64 * hh:64 * hh + 64, :] = q.astype(BF16)
    for g in range(A_KV_HEADS):
        k = proj[O_AK + 64 * g:O_AK + 64 * g + 64]
        k = _rope_rows(_rms_rows(k, gain("ak")), cosa, sina, 16)
        kpad = jnp.concatenate([k, zeros64], axis=0)
        ka_ref[0, :, LANES * g:LANES * (g + 1)] = kpad.T.astype(BF16)
    va_ref[0, 0] = proj[O_AV:O_AV + 128].astype(BF16)

    sb = B_QK_DIM ** -0.5 * LOG2E
    for i in range(2 * N_HEADS):
        q = proj[O_BQ + 32 * i:O_BQ + 32 * i + 32]
        qb_ref[0, 32 * i:32 * i + 32, :] = (_rms_rows(q, gain("bq")) * sb).astype(BF16)
    for hh in range(N_HEADS):
        k1 = _rms_rows(proj[O_BK + 64 * hh:O_BK + 64 * hh + 32], gain("bk"))
        k2 = _rms_rows(proj[O_BK + 64 * hh + 32:O_BK + 64 * hh + 64], gain("bk"))
        kpad = jnp.concatenate([k1, k2, zeros64], axis=0)
        kb_ref[0, :, LANES * hh:LANES * (hh + 1)] = kpad.T.astype(BF16)
    vb_ref[0, 0] = proj[O_BV:O_BV + 256].astype(BF16)

    sc = C_QK ** -0.5 * LOG2E
    cq = _rms_rows(proj[O_CQL:O_CQL + C_Q_LORA], gain("cqa")).astype(BF16)
    qc = jnp.dot(wqb_ref[...], cq, preferred_element_type=F32)
    ckv = _rms_rows(proj[O_CKVL:O_CKVL + C_KV_LORA], gain("ckva")).astype(BF16)
    kvc = jnp.dot(wkvb_ref[...], ckv, preferred_element_type=F32)
    kr = proj[O_CKR:O_CKR + C_ROPE]
    for hh in range(N_HEADS):
        q = _rms_rows(qc[C_QK * hh:C_QK * (hh + 1)], gain("cqn"))
        q = jnp.concatenate([q[:C_NOPE], _rope_rows(q[C_NOPE:], cosm, sinm, 16)], axis=0) * sc
        qc_ref[0, C_QK * hh:C_QK * (hh + 1), :] = q.astype(BF16)
        kk = jnp.concatenate([kvc[128 * hh:128 * hh + C_NOPE], kr], axis=0)
        kk = _rms_rows(kk, gain("ckn"))
        kk = jnp.concatenate([kk[:C_NOPE], _rope_rows(kk[C_NOPE:], cosm, sinm, 16), zeros32], axis=0)
        kc_ref[0, :, LANES * hh:LANES * (hh + 1)] = kk.T.astype(BF16)
        vc_ref[0, 0, 64 * hh:64 * hh + 64, :] = kvc[128 * hh + C_NOPE:128 * (hh + 1)].astype(BF16)

    sd = HEAD_DIM ** -0.5 * LOG2E
    qs, ks, vs = [], [], []
    for g in range(2):
        qg, kg = [], []
        for hh in (2 * g, 2 * g + 1):
            qg.append(_rms_rows(proj[O_DQ + 64 * hh:O_DQ + 64 * hh + 64], gain("dq")) * sd)
            kg.append(_rms_rows(proj[O_DK + 64 * hh:O_DK + 64 * hh + 64], gain("dk")))
        qs.append(jnp.concatenate(qg, axis=0).T)
        ks.append(jnp.concatenate(kg, axis=0).T)
        vs.append(proj[O_DV + 128 * g:O_DV + 128 * (g + 1)].T)
    _store_dilated(scr_ref, jnp.concatenate(qs, axis=1), (dq1_ref, dq4_ref, dq16_ref))
    _store_dilated(scr_ref, jnp.concatenate(ks, axis=1), (dk1_ref, dk4_ref, dk16_ref))
    _store_dilated(scr_ref, jnp.concatenate(vs, axis=1), (dv1_ref, dv4_ref, dv16_ref))


def _project(xt, w_in_t, wqb_t, wkvb_t, gains, cosa, sina, cosm, sinm):
    b, d, s = xt.shape
    tm = TOK_TILE
    nt = s // tm
    fm = lambda rows: jax.ShapeDtypeStruct((b, rows, s), BF16)
    tmj = lambda cols: jax.ShapeDtypeStruct((b, s, cols), BF16)
    vblk = lambda rows: jax.ShapeDtypeStruct((b, nt, rows, tm), BF16)
    dviews = tuple(jax.ShapeDtypeStruct((b, s // dil, dil * D_WIDTH), BF16) for _, dil in DILATED_CONFIGS)
    out_shape = (fm(256), tmj(256), vblk(128),
                 fm(256), tmj(512), vblk(256),
                 fm(384), tmj(512), vblk(256)) + dviews * 3
    fm_spec = lambda rows: pl.BlockSpec((1, rows, tm), lambda i, j: (i, 0, j))
    tm_spec = lambda cols: pl.BlockSpec((1, tm, cols), lambda i, j: (i, j, 0))
    v_spec = lambda rows: pl.BlockSpec((1, 1, rows, tm), lambda i, j: (i, j, 0, 0))
    dspecs = tuple(pl.BlockSpec((1, tm // dil, dil * D_WIDTH), lambda i, j: (i, j, 0)) for _, dil in DILATED_CONFIGS)
    out_specs = (fm_spec(256), tm_spec(256), v_spec(128),
                 fm_spec(256), tm_spec(512), v_spec(256),
                 fm_spec(384), tm_spec(512), v_spec(256)) + dspecs * 3
    whole = lambda i, j: (0, 0)
    in_specs = [
        pl.BlockSpec((1, d, tm), lambda i, j: (i, 0, j)),
        pl.BlockSpec((IN_WIDTH, d), whole),
        pl.BlockSpec((N_HEADS * C_QK, C_Q_LORA), whole),
        pl.BlockSpec((N_HEADS * 128, C_KV_LORA), whole),
        pl.BlockSpec((G_TOTAL, 1), whole),
        pl.BlockSpec((64, tm), lambda i, j: (0, j)),
        pl.BlockSpec((64, tm), lambda i, j: (0, j)),
        pl.BlockSpec((32, tm), lambda i, j: (0, j)),
        pl.BlockSpec((32, tm), lambda i, j: (0, j)),
    ]
    return pl.pallas_call(
        _proj_body,
        out_shape=out_shape,
        grid=(b, nt),
        in_specs=in_specs,
        out_specs=out_specs,
        scratch_shapes=[pltpu.VMEM((D_WIDTH // LANES, tm, LANES), F32)],
        compiler_params=_cparams(("parallel", "arbitrary")),
        name="project_heads",
    )(xt, w_in_t, wqb_t, wkvb_t, gains, cosa, sina, cosm, sinm)


def _online_update(s, v_aug, carry):
    m, acc = carry
    m_new = jnp.maximum(m, jnp.max(s, axis=0, keepdims=True))
    p = jnp.exp2(s - m_new).astype(BF16)
    alpha = jnp.exp2(m - m_new)
    acc = alpha * acc + jnp.dot(v_aug, p, preferred_element_type=F32)
    return m_new, acc


def _init_carry(dv, tq):
    return (jnp.full((1, tq), -jnp.inf, F32), jnp.zeros((dv + BF16_ROWS, tq), F32))


def _finish(carry, dv):
    _, acc = carry
    return acc[:dv] / acc[dv:dv + 1]


def _dense_body(q_ref, k_ref, v_ref, o_ref):
    dq, tq = q_ref.shape[1], q_ref.shape[2]
    nkb, dv, tk = v_ref.shape[1], v_ref.shape[2], v_ref.shape[3]
    q = q_ref[0]
    qpad = jnp.concatenate([q, jnp.zeros((LANES - dq, tq), BF16)], axis=0)
    ones = jnp.ones((BF16_ROWS, tk), BF16)

    carry = _init_carry(dv, tq)
    for kb in range(nkb):
        s = jnp.dot(k_ref[0, kb * tk:(kb + 1) * tk, :], qpad, preferred_element_type=F32)
        carry = _online_update(s, jnp.concatenate([v_ref[0, kb], ones], axis=0), carry)
    o_ref[0] = _finish(carry, dv).astype(o_ref.dtype)


def _dense_attention(q, k, v, dq, q_per_kv):
    b, rows, s = q.shape
    nh = rows // dq
    nkb, tk = v.shape[1], v.shape[3]
    return pl.pallas_call(
        _dense_body,
        out_shape=jax.ShapeDtypeStruct((b, nh * HEAD_DIM, s), BF16),
        grid=(b, nh, s // TQ),
        in_specs=[
            pl.BlockSpec((1, dq, TQ), lambda i, h, j: (i, h, j)),
            pl.BlockSpec((1, s, LANES), lambda i, h, j: (i, 0, h // q_per_kv)),
            pl.BlockSpec((1, nkb, HEAD_DIM, tk), lambda i, h, j: (i, 0, h // q_per_kv, 0)),
        ],
        out_specs=pl.BlockSpec((1, HEAD_DIM, TQ), lambda i, h, j: (i, h, j)),
        compiler_params=_cparams(("parallel", "arbitrary", "arbitrary")),
        name="dense_attention",
    )(q, k, v)


def _diff_body(sc_ref, q_ref, k_ref, v_ref, g_ref, o_ref):
    tq = q_ref.shape[2]
    nkb, dv, tk = v_ref.shape[1], v_ref.shape[2], v_ref.shape[3]
    hh = pl.program_id(1)
    jq = pl.program_id(2)
    lam = sc_ref[0]
    out_scale = sc_ref[1]
    slope = sc_ref[2 + hh] * LOG2E
    q = q_ref[0]
    z32 = jnp.zeros((B_QK_DIM, tq), BF16)
    z64 = jnp.zeros((64, tq), BF16)
    q1 = jnp.concatenate([q[:B_QK_DIM], z32, z64], axis=0)
    q2 = jnp.concatenate([z32, q[B_QK_DIM:], z64], axis=0)
    ones = jnp.ones((BF16_ROWS, tk), BF16)
    base = (lax.broadcasted_iota(jnp.int32, (tk, tq), 0) - lax.broadcasted_iota(jnp.int32, (tk, tq), 1)).astype(F32)

    c1, c2 = _init_carry(dv, tq), _init_carry(dv, tq)
    for kb in range(nkb):
        k = k_ref[0, kb * tk:(kb + 1) * tk, :]
        off = (kb * tk - jq * tq).astype(F32)
        bias = jnp.abs(base + off) * (-slope)
        v_aug = jnp.concatenate([v_ref[0, kb], ones], axis=0)
        s1 = jnp.dot(k, q1, preferred_element_type=F32) + bias
        s2 = jnp.dot(k, q2, preferred_element_type=F32) + bias
        c1 = _online_update(s1, v_aug, c1)
        c2 = _online_update(s2, v_aug, c2)
    o = _finish(c1, dv) - lam * _finish(c2, dv)
    o = _rms_rows(o, g_ref[...]) * out_scale
    o_ref[0] = o.astype(o_ref.dtype)


def _diff_attention(scal, q, k, v, sub_g):
    b, rows, s = q.shape
    nkb, tk = v.shape[1], v.shape[3]
    return pl.pallas_call(
        _diff_body,
        out_shape=jax.ShapeDtypeStruct((b, rows, s), BF16),
        grid=(b, N_HEADS, s // TQ),
        in_specs=[
            pl.BlockSpec(memory_space=pltpu.SMEM),
            pl.BlockSpec((1, 64, TQ), lambda i, h, j: (i, h, j)),
            pl.BlockSpec((1, s, LANES), lambda i, h, j: (i, 0, h)),
            pl.BlockSpec((1, nkb, HEAD_DIM, tk), lambda i, h, j: (i, 0, h, 0)),
            pl.BlockSpec((HEAD_DIM, 1), lambda i, h, j: (0, 0)),
        ],
        out_specs=pl.BlockSpec((1, HEAD_DIM, TQ), lambda i, h, j: (i, h, j)),
        compiler_params=_cparams(("parallel", "arbitrary", "arbitrary")),
        name="diff_attention",
    )(scal, q, k, v, sub_g)


def _band_body(q_ref, k_ref, v_ref, o_ref, lse_ref, *, step_len, slopes):
    qb = q_ref.shape[1]
    l_len = k_ref.shape[1]
    win = min(qb + 2 * BAND_RADIUS, l_len)
    jb = pl.program_id(2)
    start = jnp.clip(jb * qb - BAND_RADIUS, 0, l_len - win)
    start = pl.multiple_of(start, BAND_RADIUS)
    q = q_ref[0]
    kwin = k_ref[0, pl.ds(start, win), :]
    vwin = v_ref[0, pl.ds(start, win), :]
    qpos = jb * qb + lax.broadcasted_iota(jnp.int32, (qb, win), 0)
    kpos = start + lax.broadcasted_iota(jnp.int32, (qb, win), 1)
    rel_i = jnp.abs(qpos - kpos)
    allowed = rel_i <= BAND_RADIUS
    rel = rel_i.astype(F32)
    lane = lax.broadcasted_iota(jnp.int32, (qb, LANES), 1)
    for g in range(2):
        q2 = q[:, LANES * g:LANES * (g + 1)].astype(F32)
        k2 = kwin[:, LANES * g:LANES * (g + 1)]
        v2 = vwin[:, LANES * g:LANES * (g + 1)]
        out_g = jnp.zeros((qb, LANES), F32)
        lse_g = jnp.zeros((qb, LANES), F32)
        for hh in range(2):
            own = (lane >= HEAD_DIM * hh) & (lane < HEAD_DIM * (hh + 1))
            qm = jnp.where(own, q2, 0.0).astype(BF16)
            s = lax.dot_general(qm, k2, (((1,), (1,)), ((), ())), preferred_element_type=F32)
            s = s - (slopes[2 * g + hh] * step_len * LOG2E) * rel
            s = jnp.where(allowed, s, -jnp.inf)
            m = jnp.max(s, axis=1, keepdims=True)
            p = jnp.exp2(s - m)
            lsum = jnp.sum(p, axis=1, keepdims=True)
            o = jnp.dot(p.astype(BF16), v2, preferred_element_type=F32) / lsum
            lse = m + jnp.log2(lsum)
            out_g = jnp.where(own, o, out_g)
            lse_g = jnp.where(own, lse, lse_g)
        o_ref[0, :, LANES * g:LANES * (g + 1)] = out_g
        lse_ref[0, :, LANES * g:LANES * (g + 1)] = lse_g


def _band_attention(q, k, v, dil, slopes):
    b, l_len, cw = q.shape
    c = cw // dil
    qb = min(BAND_Q, l_len)
    body = functools.partial(_band_body, step_len=float(dil), slopes=slopes)
    return pl.pallas_call(
        body,
        out_shape=(jax.ShapeDtypeStruct((b, l_len, cw), F32),) * 2,
        grid=(b, dil, l_len // qb),
        in_specs=[
            pl.BlockSpec((1, qb, c), lambda i, r, j: (i, j, r)),
            pl.BlockSpec((1, l_len, c), lambda i, r, j: (i, 0, r)),
            pl.BlockSpec((1, l_len, c), lambda i, r, j: (i, 0, r)),
        ],
        out_specs=(pl.BlockSpec((1, qb, c), lambda i, r, j: (i, j, r)),) * 2,
        compiler_params=_cparams(("parallel", "arbitrary", "arbitrary")),
        name="band_attention_d%d" % dil,
    )(q, k, v)


def _band_combine_body(*refs):
    n = len(DILATED_CONFIGS)
    in_refs, o_ref, scr_ref = refs[:2 * n], refs[2 * n], refs[2 * n + 1]
    tm, c = o_ref.shape[1], o_ref.shape[2]
    vals = []
    for idx, (_, dil) in enumerate(DILATED_CONFIGS):
        for which in range(2):
            src = in_refs[2 * idx + which]
            if dil == 1:
                vals.append(src[0])
                continue
            groups = []
            for g in range(c // LANES):
                slot = scr_ref.at[(2 * idx + which) * (c // LANES) + g]
                for r in range(dil):
                    lo = c * r + LANES * g
                    slot[pl.ds(r, tm // dil, stride=dil), :] = src[0, :, lo:lo + LANES]
                groups.append(slot[...])
            vals.append(jnp.concatenate(groups, axis=1))
    outs, lses = vals[0::2], vals[1::2]
    m = functools.reduce(jnp.maximum, lses)
    ws = [jnp.exp2(x - m) for x in lses]
    num = functools.reduce(lambda a, t: a + t, [w * o for w, o in zip(ws, outs)])
    den = functools.reduce(lambda a, t: a + t, ws)
    o_ref[0] = (num / den).astype(o_ref.dtype)


def _band_combine(pairs):
    b, s, c = pairs[0][0].shape
    tm = TOK_TILE
    specs = []
    for (_, dil) in DILATED_CONFIGS:
        specs += [pl.BlockSpec((1, tm // dil, dil * c), lambda i, j: (i, j, 0))] * 2
    flat = [a for p in pairs for a in p]
    return pl.pallas_call(
        _band_combine_body,
        out_shape=jax.ShapeDtypeStruct((b, s, c), BF16),
        grid=(b, s // tm),
        in_specs=specs,
        out_specs=pl.BlockSpec((1, tm, c), lambda i, j: (i, j, 0)),
        scratch_shapes=[pltpu.VMEM((2 * len(DILATED_CONFIGS) * (c // LANES), tm, LANES), F32)],
        compiler_params=_cparams(("parallel", "arbitrary")),
        name="band_combine",
    )(*flat)


def _outproj_body(x_ref, oa_ref, ob_ref, oc_ref, od_ref, w_ref, g_ref, x1_ref, h2_ref, edge_ref):
    d, tm = x_ref.shape[1], x_ref.shape[2]
    j = pl.program_id(1)
    mix = jnp.concatenate([oa_ref[0], ob_ref[0], oc_ref[0]], axis=0)
    w = w_ref[...]
    y = jnp.dot(w[:, :768], mix, preferred_element_type=F32)
    y = y + lax.dot_general(w[:, 768:], od_ref[0], (((1,), (1,)), ((), ())), preferred_element_type=F32)
    x1 = x_ref[0] + y
    x1_ref[0] = x1
    h2 = _rms_rows(x1, g_ref[...]).astype(BF16)
    h2_ref[0] = h2

    @pl.when(j == 0)
    def _():
        edge_ref[0] = jnp.zeros((d, LANES), F32)

    lane = lax.broadcasted_iota(jnp.int32, (d, LANES), 1)
    first = jnp.broadcast_to(h2[:, 0:1].astype(F32), (d, LANES))
    last = jnp.broadcast_to(h2[:, tm - 1:tm].astype(F32), (d, LANES))
    e = edge_ref[0]
    e = jnp.where(lane == 2 * j, first, e)
    e = jnp.where(lane == 2 * j + 1, last, e)
    edge_ref[0] = e


def _out_project(xt, oa, ob, oc, od, w_out_t, g2):
    b, d, s = xt.shape
    tm = TOK_TILE
    assert 2 * (s // tm) <= LANES
    fm = lambda rows: pl.BlockSpec((1, rows, tm), lambda i, j: (i, 0, j))
    return pl.pallas_call(
        _outproj_body,
        out_shape=(jax.ShapeDtypeStruct((b, d, s), F32), jax.ShapeDtypeStruct((b, d, s), BF16),
                   jax.ShapeDtypeStruct((b, d, LANES), F32)),
        grid=(b, s // tm),
        in_specs=[fm(d), fm(256), fm(256), fm(256),
                  pl.BlockSpec((1, tm, D_WIDTH), lambda i, j: (i, j, 0)),
                  pl.BlockSpec((d, d), lambda i, j: (0, 0)),
                  pl.BlockSpec((d, 1), lambda i, j: (0, 0))],
        out_specs=(fm(d), fm(d), pl.BlockSpec((1, d, LANES), lambda i, j: (i, 0, 0))),
        compiler_params=_cparams(("parallel", "arbitrary")),
        name="out_project",
    )(xt, oa, ob, oc, od, w_out_t, g2)


def _halo_body(w_ref, h_ref, o_ref):
    o_ref[...] = jnp.dot(w_ref[...], h_ref[...], preferred_element_type=F32)


def _halo_up(w_up_t, halo_h):
    ff2, d = w_up_t.shape
    nh = halo_h.shape[1]
    rows = ff2 // 4
    return pl.pallas_call(
        _halo_body,
        out_shape=jax.ShapeDtypeStruct((ff2, nh), F32),
        grid=(4,),
        in_specs=[pl.BlockSpec((rows, d), lambda i: (i, 0)),
                  pl.BlockSpec((d, nh), lambda i: (0, 0))],
        out_specs=pl.BlockSpec((rows, nh), lambda i: (i, 0)),
        compiler_params=_cparams(("arbitrary",)),
        name="mlp_halo_up",
    )(w_up_t, halo_h)


def _mlp_body(x1_ref, h2_ref, wup_ref, wdn_ref, par_ref, halo_ref, o_ref, *, n_chunks):
    tf = h2_ref.shape[2]
    ff = wdn_ref.shape[1]
    rc = ff // n_chunks
    nt = pl.num_programs(1)
    col = (pl.program_id(0) * nt + pl.program_id(1)) * 2
    grp = col // LANES
    cin = col % LANES
    shift_l = (LANES - cin) % LANES
    shift_r = LANES - 2 - cin
    h = h2_ref[0]
    lane = lax.broadcasted_iota(jnp.int32, (rc, tf), 1)
    first = lane == 0
    last = lane == tf - 1
    reps = tf // LANES

    def conv(u, r0):
        hal = halo_ref[grp, r0:r0 + rc, :]
        left = pltpu.roll(hal, shift_l, 1)
        right = pltpu.roll(hal, shift_r, 1)
        prev = jnp.where(first, jnp.concatenate([left] * reps, axis=1), pltpu.roll(u, 1, 1))
        nxt = jnp.where(last, jnp.concatenate([right] * reps, axis=1), pltpu.roll(u, tf - 1, 1))
        p = par_ref[r0:r0 + rc, :]
        return p[:, 0:1] * prev + p[:, 1:2] * u + p[:, 2:3] * nxt + p[:, 3:4]

    y = x1_ref[0]
    for c in range(n_chunks):
        ug = jnp.dot(wup_ref[c * rc:(c + 1) * rc, :], h, preferred_element_type=F32)
        uv = jnp.dot(wup_ref[ff + c * rc:ff + (c + 1) * rc, :], h, preferred_element_type=F32)
        gate = conv(ug, c * rc)
        val = conv(uv, ff + c * rc)
        act = gate / (1.0 + jnp.exp(-gate)) * val
        y = y + jnp.dot(wdn_ref[:, c * rc:(c + 1) * rc], act.astype(BF16), preferred_element_type=F32)
    o_ref[0] = y


def _mlp(x1t, h2t, w_up_t, w_down_t, conv_par, halo_up, n_chunks):
    b, d, s = x1t.shape
    tf = TOK_TILE
    ff2 = w_up_t.shape[0]
    ngrp = halo_up.shape[0]
    whole = lambda i, j: (0, 0)
    body = functools.partial(_mlp_body, n_chunks=n_chunks)
    const = dict(pipeline_mode=pl.Buffered(1))
    return pl.pallas_call(
        body,
        out_shape=jax.ShapeDtypeStruct((b, d, s), F32),
        grid=(b, s // tf),
        in_specs=[pl.BlockSpec((1, d, tf), lambda i, j: (i, 0, j)),
                  pl.BlockSpec((1, d, tf), lambda i, j: (i, 0, j)),
                  pl.BlockSpec((ff2, d), whole, **const),
                  pl.BlockSpec((d, ff2 // 2), whole, **const),
                  pl.BlockSpec((ff2, 8), whole, **const),
                  pl.BlockSpec((ngrp, ff2, LANES), lambda i, j: (0, 0, 0), **const)],
        out_specs=pl.BlockSpec((1, d, tf), lambda i, j: (i, 0, j)),
        compiler_params=_cparams(("arbitrary", "arbitrary")),
        name="gated_conv_mlp",
    )(x1t, h2t, w_up_t, w_down_t, conv_par, halo_up)


def _halo_columns(edges, nt):
    b, d, _ = edges.shape
    ez = jnp.concatenate([edges[:, :, :2 * nt], jnp.zeros((b, d, 1), edges.dtype)], axis=2)
    idx = []
    for j in range(nt):
        idx += [2 * j - 1 if j > 0 else 2 * nt, 2 * j + 2 if j < nt - 1 else 2 * nt]
    cols = jnp.take(ez, jnp.asarray(idx, jnp.int32), axis=2)
    cols = cols.transpose(1, 0, 2).reshape(d, b * nt * 2)
    pad = (-cols.shape[1]) % LANES
    return jnp.pad(cols, ((0, 0), (0, pad))).astype(BF16)


def _rope_tables(s):
    def freqs(dim):
        return 1.0 / (ROPE_THETA ** (jnp.arange(0, dim, 2, dtype=F32) / dim))
    t = jnp.arange(s)
    rows = (t // GRID_W).astype(F32)
    cols = (t % GRID_W).astype(F32)
    pos = t.astype(F32)
    fa = freqs(HEAD_DIM // 2)
    ar, ac = fa[:, None] * rows[None, :], fa[:, None] * cols[None, :]
    am = freqs(C_ROPE)[:, None] * pos[None, :]
    cosa = jnp.concatenate([jnp.cos(ar), jnp.cos(ar), jnp.cos(ac), jnp.cos(ac)], axis=0)
    sina = jnp.concatenate([-jnp.sin(ar), jnp.sin(ar), -jnp.sin(ac), jnp.sin(ac)], axis=0)
    cosm = jnp.concatenate([jnp.cos(am), jnp.cos(am)], axis=0)
    sinm = jnp.concatenate([-jnp.sin(am), jnp.sin(am)], axis=0)
    return cosa, sina, cosm, sinm


def kernel(x, norm1_g, w_in, a_qn_g, a_kn_g, b_qn_g, b_kn_g, b_lam_q1, b_lam_k1, b_lam_q2, b_lam_k2, b_sub_g,
           c_qa_g, c_kva_g, c_wqb, c_wkvb, c_qn_g, c_kn_g, d_qn_g, d_kn_g, w_out, norm2_g, w_up, conv_w, conv_b,
           w_down):
    b, s, d = x.shape
    depth = w_in.shape[0]
    ff = w_down.shape[1]
    nt = s // TOK_TILE
    assert s % TOK_TILE == 0 and s % TK == 0 and TOK_TILE == TK and s % TQ == 0 and s % GRID_W == 0
    for window, dil in DILATED_CONFIGS:
        assert window // (2 * dil) == BAND_RADIUS and (s // dil) % min(BAND_Q, s // dil) == 0
        assert TOK_TILE % (dil * BF16_ROWS) == 0

    tr = lambda w: jnp.swapaxes(w, 1, 2).astype(BF16)
    gains = jnp.concatenate([norm1_g, a_qn_g, a_kn_g, b_qn_g, b_kn_g, c_qa_g, c_kva_g, c_qn_g, c_kn_g,
                             d_qn_g, d_kn_g], axis=1)[:, :, None]
    conv_par = jnp.concatenate([jnp.swapaxes(conv_w, 1, 2), conv_b[:, :, None],
                                jnp.zeros((depth, 2 * ff, 4), F32)], axis=2)
    slopes = [2.0 ** (-8.0 * i / (2 * N_HEADS)) for i in range(1, 2 * N_HEADS + 1)]
    slopes_b, slopes_d = slopes[:N_HEADS], tuple(slopes[N_HEADS:])
    lam_init = jnp.asarray([0.8 - 0.6 * math.exp(-0.3 * l) for l in range(depth)], F32)
    lam = (jnp.exp(jnp.sum(b_lam_q1 * b_lam_k1, axis=1)) - jnp.exp(jnp.sum(b_lam_q2 * b_lam_k2, axis=1)) + lam_init)
    scal = jnp.concatenate([lam[:, None], (1.0 - lam_init)[:, None],
                            jnp.broadcast_to(jnp.asarray(slopes_b, F32), (depth, N_HEADS)),
                            jnp.zeros((depth, 2), F32)], axis=1)
    layers = dict(w_in=tr(w_in), wqb=tr(c_wqb), wkvb=tr(c_wkvb), w_out=tr(w_out), w_up=tr(w_up), w_down=tr(w_down),
                  gains=gains, g2=norm2_g[:, :, None], sub_g=b_sub_g[:, :, None], conv_par=conv_par, scal=scal)
    cosa, sina, cosm, sinm = _rope_tables(s)
    n_chunks = 2 if (ff // 2) % LANES == 0 else 1

    def layer(xt, p):
        (qa, ka, va, qb, kb, vb, qc, kc, vc, dq1, dq4, dq16, dk1, dk4, dk16, dv1, dv4, dv16) = _project(
            xt, p["w_in"], p["wqb"], p["wkvb"], p["gains"], cosa, sina, cosm, sinm)
        oa = _dense_attention(qa, ka, va, HEAD_DIM, N_HEADS // A_KV_HEADS)
        ob = _diff_attention(p["scal"], qb, kb, vb, p["sub_g"])
        oc = _dense_attention(qc, kc, vc, C_QK, 1)
        branches = [_band_attention(q_, k_, v_, dil, slopes_d)
                    for (_, dil), q_, k_, v_ in zip(DILATED_CONFIGS, (dq1, dq4, dq16), (dk1, dk4, dk16), (dv1, dv4, dv16))]
        od = _band_combine(branches)
        x1t, h2t, edges = _out_project(xt, oa, ob, oc, od, p["w_out"], p["g2"])
        halo = _halo_up(p["w_up"], _halo_columns(edges, nt))
        halo = halo.reshape(2 * ff, -1, LANES).transpose(1, 0, 2)
        return _mlp(x1t, h2t, p["w_up"], p["w_down"], p["conv_par"], halo, n_chunks), None

    xt = _transpose_last2(x, TOK_TILE, d)
    xt, _ = lax.scan(layer, xt, layers)
    return _transpose_last2(xt, d, TOK_TILE)
```

```python
import functools
import math

import jax
import jax.numpy as jnp
import numpy as np
from jax import lax
from jax.experimental import pallas as pl
from jax.experimental.pallas import tpu as pltpu

F32 = jnp.float32
BF16 = jnp.bfloat16

HEAD_DIM = 64
N_HEADS = 4
A_KV_HEADS = 2
B_QK_DIM = 32
C_Q_LORA = 256
C_KV_LORA = 128
C_NOPE = 64
C_ROPE = 32
C_QK = C_NOPE + C_ROPE
DILATED_CONFIGS = ((128, 1), (512, 4), (2048, 16))
D_WIDTH = N_HEADS * HEAD_DIM
GRID_W = 64
ROPE_THETA = 10000.0
NORM_EPS = 1e-6
LOG2E = 1.4426950408889634


def _bf16_pieces(x, n):
    out = []
    for _ in range(n):
        u = np.float32(x).view(np.uint32)
        u = np.uint32((int(u) + 0x7FFF + ((int(u) >> 16) & 1)) & 0xFFFF0000)
        piece = float(u.view(np.float32))
        out.append(piece)
        x = x - piece
    return tuple(out)


LOG2E_PIECES = _bf16_pieces(LOG2E, 3)
POS_SPLIT_BITS = 6
BF16_MANTISSA_BITS = 8

_SPLITS = (256, 128, 128, 256, 256, 256, C_Q_LORA, C_KV_LORA, C_ROPE, 256, 256, 256)
_OFF = [0]
for _n in _SPLITS:
    _OFF.append(_OFF[-1] + _n)
(O_AQ, O_AK, O_AV, O_BQ, O_BK, O_BV, O_CQL, O_CKVL, O_CKR, O_DQ, O_DK, O_DV, IN_WIDTH) = _OFF

_GAIN_SIZES = (("n1", 1024), ("aq", 64), ("ak", 64), ("bq", 32), ("bk", 32), ("cqa", 256), ("ckva", 128),
               ("cqn", 96), ("ckn", 96), ("dq", 64), ("dk", 64))
G_OFF = {}
_o = 0
for _k, _n in _GAIN_SIZES:
    G_OFF[_k] = (_o, _n)
    _o += _n
G_TOTAL = _o

LANES = 128
BF16_ROWS = 16
TOK_TILE = 512
TQ = 512
KSTEP = 256
BAND_Q = 128
BAND_RADIUS = 64
VMEM_LIMIT = 56 * 1024 * 1024


def _cparams(sem):
    return pltpu.CompilerParams(dimension_semantics=sem, vmem_limit_bytes=VMEM_LIMIT)


def _rms_rows(v, g_col):
    ms = jnp.mean(v * v, axis=0, keepdims=True)
    return v * lax.rsqrt(ms + NORM_EPS) * g_col


def _swap_halves(x, n):
    parts = []
    for i in range(0, x.shape[0], 2 * n):
        parts.append(x[i + n:i + 2 * n])
        parts.append(x[i:i + n])
    return jnp.concatenate(parts, axis=0)


def _rope_rows(x, cos, sin_signed, n):
    return x * cos + _swap_halves(x, n) * sin_signed


def _transpose_body(x_ref, o_ref):
    o_ref[0] = x_ref[0].T


def _transpose_last2(x, tr, tc):
    b, r, c = x.shape
    return pl.pallas_call(
        _transpose_body,
        out_shape=jax.ShapeDtypeStruct((b, c, r), x.dtype),
        grid=(b, r // tr, c // tc),
        in_specs=[pl.BlockSpec((1, tr, tc), lambda i, j, k: (i, j, k))],
        out_specs=pl.BlockSpec((1, tc, tr), lambda i, j, k: (i, k, j)),
        compiler_params=_cparams(("parallel", "arbitrary", "arbitrary")),
        name="transpose_tokens",
    )(x)


def _store_dilated(scr_ref, val, out_refs):
    tm, c = val.shape
    for g in range(c // LANES):
        scr_ref[g] = val[:, LANES * g:LANES * (g + 1)]
    for (_, dil), out in zip(DILATED_CONFIGS, out_refs):
        if dil == 1:
            out[0] = val.astype(BF16)
            continue
        n = tm // dil
        for r in range(dil):
            for g in range(c // LANES):
                lo = c * r + LANES * g
                out[0, :, lo:lo + LANES] = scr_ref[g, pl.ds(r, n, stride=dil), :].astype(BF16)


def _store_key_blocks(v_ref, v):
    for c in range(v.shape[1] // KSTEP):
        v_ref[0, c] = v[:, c * KSTEP:(c + 1) * KSTEP]


def _proj_body(x_ref, w_ref, wqb_ref, wkvb_ref, g_ref, cosa_ref, sina_ref, cosm_ref, sinm_ref,
               qa_ref, ka_ref, va_ref, qb_ref, kb_ref, vb_ref, qc_ref, kc_ref, vc_ref,
               dq1_ref, dq4_ref, dq16_ref, dk1_ref, dk4_ref, dk16_ref, dv1_ref, dv4_ref, dv16_ref,
               scr_ref):
    tm = x_ref.shape[2]

    def gain(name):
        o, n = G_OFF[name]
        return g_ref[o:o + n, :]

    x = x_ref[0]
    h = _rms_rows(x, gain("n1")).astype(BF16)
    proj = jnp.dot(w_ref[...], h, preferred_element_type=F32)

    cosa, sina = cosa_ref[...], sina_ref[...]
    cosm, sinm = cosm_ref[...], sinm_ref[...]
    zeros64 = jnp.zeros((64, tm), F32)
    zeros32 = jnp.zeros((32, tm), F32)

    sa = HEAD_DIM ** -0.5 * LOG2E
    for hh in range(N_HEADS):
        q = proj[O_AQ + 64 * hh:O_AQ + 64 * hh + 64]
        q = _rope_rows(_rms_rows(q, gain("aq")), cosa, sina, 16) * sa
        qa_ref[0, 64 * hh:64 * hh + 64, :] = q.astype(BF16)
    for g in range(A_KV_HEADS):
        k = proj[O_AK + 64 * g:O_AK + 64 * g + 64]
        k = _rope_rows(_rms_rows(k, gain("ak")), cosa, sina, 16)
        kpad = jnp.concatenate([k, zeros64], axis=0)
        ka_ref[0, :, LANES * g:LANES * (g + 1)] = kpad.T.astype(BF16)
    _store_key_blocks(va_ref, proj[O_AV:O_AV + 128].astype(BF16))

    sb = B_QK_DIM ** -0.5 * LOG2E
    for i in range(2 * N_HEADS):
        q = proj[O_BQ + 32 * i:O_BQ + 32 * i + 32]
        qb_ref[0, 32 * i:32 * i + 32, :] = (_rms_rows(q, gain("bq")) * sb).astype(BF16)
    kpos = pl.program_id(1) * tm + lax.broadcasted_iota(jnp.int32, (BF16_ROWS, tm), 1)
    frow = lax.broadcasted_iota(jnp.int32, (BF16_ROWS, tm), 0)
    pos_hi = ((kpos >> POS_SPLIT_BITS) << POS_SPLIT_BITS).astype(F32)
    pos_lo = (kpos & ((1 << POS_SPLIT_BITS) - 1)).astype(F32)
    kfeat = jnp.where(frow < 3, pos_hi, jnp.where(frow < 6, pos_lo, 0.0))
    zeros48 = jnp.zeros((48, tm), F32)
    for hh in range(N_HEADS):
        k1 = _rms_rows(proj[O_BK + 64 * hh:O_BK + 64 * hh + 32], gain("bk"))
        k2 = _rms_rows(proj[O_BK + 64 * hh + 32:O_BK + 64 * hh + 64], gain("bk"))
        kpad = jnp.concatenate([k1, k2, kfeat, zeros48], axis=0)
        kb_ref[0, :, LANES * hh:LANES * (hh + 1)] = kpad.T.astype(BF16)
    _store_key_blocks(vb_ref, proj[O_BV:O_BV + 256].astype(BF16))

    sc = C_QK ** -0.5 * LOG2E
    cq = _rms_rows(proj[O_CQL:O_CQL + C_Q_LORA], gain("cqa")).astype(BF16)
    qc = jnp.dot(wqb_ref[...], cq, preferred_element_type=F32)
    ckv = _rms_rows(proj[O_CKVL:O_CKVL + C_KV_LORA], gain("ckva")).astype(BF16)
    kvc = jnp.dot(wkvb_ref[...], ckv, preferred_element_type=F32)
    kr = proj[O_CKR:O_CKR + C_ROPE]
    vcs = []
    for hh in range(N_HEADS):
        q = _rms_rows(qc[C_QK * hh:C_QK * (hh + 1)], gain("cqn"))
        q = jnp.concatenate([q[:C_NOPE], _rope_rows(q[C_NOPE:], cosm, sinm, 16)], axis=0) * sc
        qc_ref[0, C_QK * hh:C_QK * (hh + 1), :] = q.astype(BF16)
        kk = jnp.concatenate([kvc[128 * hh:128 * hh + C_NOPE], kr], axis=0)
        kk = _rms_rows(kk, gain("ckn"))
        kk = jnp.concatenate([kk[:C_NOPE], _rope_rows(kk[C_NOPE:], cosm, sinm, 16), zeros32], axis=0)
        kc_ref[0, :, LANES * hh:LANES * (hh + 1)] = kk.T.astype(BF16)
        vcs.append(kvc[128 * hh + C_NOPE:128 * (hh + 1)])
    _store_key_blocks(vc_ref, jnp.concatenate(vcs, axis=0).astype(BF16))

    sd = HEAD_DIM ** -0.5 * LOG2E
    qs, ks, vs = [], [], []
    for g in range(2):
        qg, kg = [], []
        for hh in (2 * g, 2 * g + 1):
            qg.append(_rms_rows(proj[O_DQ + 64 * hh:O_DQ + 64 * hh + 64], gain("dq")) * sd)
            kg.append(_rms_rows(proj[O_DK + 64 * hh:O_DK + 64 * hh + 64], gain("dk")))
        qs.append(jnp.concatenate(qg, axis=0).T)
        ks.append(jnp.concatenate(kg, axis=0).T)
        vs.append(proj[O_DV + 128 * g:O_DV + 128 * (g + 1)].T)
    _store_dilated(scr_ref, jnp.concatenate(qs, axis=1), (dq1_ref, dq4_ref, dq16_ref))
    _store_dilated(scr_ref, jnp.concatenate(ks, axis=1), (dk1_ref, dk4_ref, dk16_ref))
    _store_dilated(scr_ref, jnp.concatenate(vs, axis=1), (dv1_ref, dv4_ref, dv16_ref))


def _project(xt, w_in_t, wqb_t, wkvb_t, gains, cosa, sina, cosm, sinm):
    b, d, s = xt.shape
    tm = TOK_TILE
    nt = s // tm
    fm = lambda rows: jax.ShapeDtypeStruct((b, rows, s), BF16)
    tmj = lambda cols: jax.ShapeDtypeStruct((b, s, cols), BF16)
    vblk = lambda rows: jax.ShapeDtypeStruct((b, s // KSTEP, rows, KSTEP), BF16)
    dviews = tuple(jax.ShapeDtypeStruct((b, s // dil, dil * D_WIDTH), BF16) for _, dil in DILATED_CONFIGS)
    out_shape = (fm(256), tmj(256), vblk(128),
                 fm(256), tmj(512), vblk(256),
                 fm(384), tmj(512), vblk(256)) + dviews * 3
    fm_spec = lambda rows: pl.BlockSpec((1, rows, tm), lambda i, j: (i, 0, j))
    tm_spec = lambda cols: pl.BlockSpec((1, tm, cols), lambda i, j: (i, j, 0))
    v_spec = lambda rows: pl.BlockSpec((1, tm // KSTEP, rows, KSTEP), lambda i, j: (i, j, 0, 0))
    dspecs = tuple(pl.BlockSpec((1, tm // dil, dil * D_WIDTH), lambda i, j: (i, j, 0)) for _, dil in DILATED_CONFIGS)
    out_specs = (fm_spec(256), tm_spec(256), v_spec(128),
                 fm_spec(256), tm_spec(512), v_spec(256),
                 fm_spec(384), tm_spec(512), v_spec(256)) + dspecs * 3
    whole = lambda i, j: (0, 0)
    in_specs = [
        pl.BlockSpec((1, d, tm), lambda i, j: (i, 0, j)),
        pl.BlockSpec((IN_WIDTH, d), whole),
        pl.BlockSpec((N_HEADS * C_QK, C_Q_LORA), whole),
        pl.BlockSpec((N_HEADS * 128, C_KV_LORA), whole),
        pl.BlockSpec((G_TOTAL, 1), whole),
        pl.BlockSpec((64, tm), lambda i, j: (0, j)),
        pl.BlockSpec((64, tm), lambda i, j: (0, j)),
        pl.BlockSpec((32, tm), lambda i, j: (0, j)),
        pl.BlockSpec((32, tm), lambda i, j: (0, j)),
    ]
    return pl.pallas_call(
        _proj_body,
        out_shape=out_shape,
        grid=(b, nt),
        in_specs=in_specs,
        out_specs=out_specs,
        scratch_shapes=[pltpu.VMEM((D_WIDTH // LANES, tm, LANES), F32)],
        compiler_params=_cparams(("parallel", "arbitrary")),
        name="project_heads",
    )(xt, w_in_t, wqb_t, wkvb_t, gains, cosa, sina, cosm, sinm)


def _online_update(s, v_aug, carry, shift=None):
    m, acc = carry
    smax = jnp.max(s, axis=0, keepdims=True)
    if shift is None:
        m_new = jnp.maximum(m, smax)
        p = jnp.exp2(s - m_new).astype(BF16)
    else:
        m_new = jnp.maximum(m, smax + shift)
        p = jnp.exp2(s - (m_new - shift)).astype(BF16)
    alpha = jnp.exp2(m - m_new)
    acc = alpha * acc + jnp.dot(v_aug, p, preferred_element_type=F32)
    return m_new, acc


def _init_carry(dv, tq):
    return (jnp.full((1, tq), -jnp.inf, F32), jnp.zeros((dv + BF16_ROWS, tq), F32))


def _finish(carry, dv):
    _, acc = carry
    return acc[:dv] / acc[dv:dv + 1]


def _dense_body(q_ref, k_ref, v_ref, o_ref):
    dq, tq = q_ref.shape[1], q_ref.shape[2]
    nkb, dv, ks = v_ref.shape[1], v_ref.shape[2], v_ref.shape[3]
    q = q_ref[0]
    qpad = jnp.concatenate([q, jnp.zeros((LANES - dq, tq), BF16)], axis=0)
    ones = jnp.ones((BF16_ROWS, ks), BF16)

    def scores(i):
        return jnp.dot(k_ref[0, i * ks:(i + 1) * ks, :], qpad, preferred_element_type=F32)

    carry = _init_carry(dv, tq)
    s_next = scores(0)
    for i in range(nkb):
        s = s_next
        if i + 1 < nkb:
            s_next = scores(i + 1)
        carry = _online_update(s, jnp.concatenate([v_ref[0, i], ones], axis=0), carry)
    o_ref[0] = _finish(carry, dv).astype(o_ref.dtype)


def _dense_attention(q, k, v, dq, q_per_kv):
    b, rows, s = q.shape
    nh = rows // dq
    nkb, tk = v.shape[1], v.shape[3]
    return pl.pallas_call(
        _dense_body,
        out_shape=jax.ShapeDtypeStruct((b, nh * HEAD_DIM, s), BF16),
        grid=(b, nh, s // TQ),
        in_specs=[
            pl.BlockSpec((1, dq, TQ), lambda i, h, j: (i, h, j)),
            pl.BlockSpec((1, s, LANES), lambda i, h, j: (i, 0, h // q_per_kv)),
            pl.BlockSpec((1, nkb, HEAD_DIM, tk), lambda i, h, j: (i, 0, h // q_per_kv, 0)),
        ],
        out_specs=pl.BlockSpec((1, HEAD_DIM, TQ), lambda i, h, j: (i, h, j)),
        compiler_params=_cparams(("parallel", "arbitrary", "arbitrary")),
        name="dense_attention",
    )(q, k, v)


def _diff_body(sc_ref, q_ref, k_ref, v_ref, g_ref, o_ref):
    tq = q_ref.shape[2]
    nkb, dv, ks = v_ref.shape[1], v_ref.shape[2], v_ref.shape[3]
    hh = pl.program_id(1)
    jq = pl.program_id(2)
    lam = sc_ref[0]
    out_scale = sc_ref[1]
    slope = sc_ref[2 + hh]
    q = q_ref[0]
    z32 = jnp.zeros((B_QK_DIM, tq), BF16)
    z48 = jnp.zeros((48, tq), BF16)
    zfeat = jnp.zeros((BF16_ROWS, tq), BF16)
    ones = jnp.ones((BF16_ROWS, ks), BF16)
    frow = lax.broadcasted_iota(jnp.int32, (BF16_ROWS, tq), 0)
    piece = jnp.where(frow % 3 == 0, LOG2E_PIECES[0], jnp.where(frow % 3 == 1, LOG2E_PIECES[1], LOG2E_PIECES[2]))
    cfeat = jnp.where(frow < 6, piece, 0.0)
    qpos = (jq * tq + lax.broadcasted_iota(jnp.int32, (1, tq), 1)).astype(F32)
    base = (lax.broadcasted_iota(jnp.int32, (ks, tq), 0) - lax.broadcasted_iota(jnp.int32, (ks, tq), 1)).astype(F32)

    n_mixed = tq // ks
    first = jq * n_mixed

    def block_index(ii):
        return lax.rem(first + ii, nkb)

    def side(ii):
        return jnp.where(block_index(ii) < first, 1.0, -1.0).astype(F32)

    def scores(ii):
        i = block_index(ii)
        k = k_ref[0, pl.ds(pl.multiple_of(i * ks, ks), ks), :]
        feat = zfeat if ii < n_mixed else (cfeat * (side(ii) * slope)).astype(BF16)
        q1 = jnp.concatenate([q[:B_QK_DIM], z32, feat, z48], axis=0)
        q2 = jnp.concatenate([z32, q[B_QK_DIM:], feat, z48], axis=0)
        return (jnp.dot(k, q1, preferred_element_type=F32), jnp.dot(k, q2, preferred_element_type=F32))

    c1, c2 = _init_carry(dv, tq), _init_carry(dv, tq)
    s_next = scores(0)
    for ii in range(nkb):
        s1, s2 = s_next
        if ii + 1 < nkb:
            s_next = scores(ii + 1)
        v_aug = jnp.concatenate([v_ref[0, block_index(ii)], ones], axis=0)
        if ii < n_mixed:
            bias = jnp.abs(base + float(ii * ks)) * (-(slope * LOG2E))
            c1 = _online_update(s1 + bias, v_aug, c1)
            c2 = _online_update(s2 + bias, v_aug, c2)
        else:
            shift = qpos * (-(side(ii) * slope * LOG2E))
            c1 = _online_update(s1, v_aug, c1, shift)
            c2 = _online_update(s2, v_aug, c2, shift)
    o = _finish(c1, dv) - lam * _finish(c2, dv)
    o = _rms_rows(o, g_ref[...]) * out_scale
    o_ref[0] = o.astype(o_ref.dtype)


def _diff_attention(scal, q, k, v, sub_g):
    b, rows, s = q.shape
    nkb, tk = v.shape[1], v.shape[3]
    return pl.pallas_call(
        _diff_body,
        out_shape=jax.ShapeDtypeStruct((b, rows, s), BF16),
        grid=(b, N_HEADS, s // TQ),
        in_specs=[
            pl.BlockSpec(memory_space=pltpu.SMEM),
            pl.BlockSpec((1, 64, TQ), lambda i, h, j: (i, h, j)),
            pl.BlockSpec((1, s, LANES), lambda i, h, j: (i, 0, h)),
            pl.BlockSpec((1, nkb, HEAD_DIM, tk), lambda i, h, j: (i, 0, h, 0)),
            pl.BlockSpec((HEAD_DIM, 1), lambda i, h, j: (0, 0)),
        ],
        out_specs=pl.BlockSpec((1, HEAD_DIM, TQ), lambda i, h, j: (i, h, j)),
        compiler_params=_cparams(("parallel", "arbitrary", "arbitrary")),
        name="diff_attention",
    )(scal, q, k, v, sub_g)


def _band_body(q_ref, k_ref, v_ref, o_ref, lse_ref, *, step_len, slopes):
    qb = q_ref.shape[1]
    l_len = k_ref.shape[1]
    win = min(qb + 2 * BAND_RADIUS, l_len)
    jb = pl.program_id(2)
    start = jnp.clip(jb * qb - BAND_RADIUS, 0, l_len - win)
    start = pl.multiple_of(start, BAND_RADIUS)
    q = q_ref[0]
    kwin = k_ref[0, pl.ds(start, win), :]
    vwin = v_ref[0, pl.ds(start, win), :]
    qpos = jb * qb + lax.broadcasted_iota(jnp.int32, (qb, win), 0)
    kpos = start + lax.broadcasted_iota(jnp.int32, (qb, win), 1)
    rel_i = jnp.abs(qpos - kpos)
    allowed = rel_i <= BAND_RADIUS
    rel = rel_i.astype(F32)
    lane = lax.broadcasted_iota(jnp.int32, (qb, LANES), 1)
    for g in range(2):
        q2 = q[:, LANES * g:LANES * (g + 1)].astype(F32)
        k2 = kwin[:, LANES * g:LANES * (g + 1)]
        v2 = vwin[:, LANES * g:LANES * (g + 1)]
        out_g = jnp.zeros((qb, LANES), F32)
        lse_g = jnp.zeros((qb, LANES), F32)
        for hh in range(2):
            own = (lane >= HEAD_DIM * hh) & (lane < HEAD_DIM * (hh + 1))
            qm = jnp.where(own, q2, 0.0).astype(BF16)
            s = lax.dot_general(qm, k2, (((1,), (1,)), ((), ())), preferred_element_type=F32)
            s = s - (slopes[2 * g + hh] * step_len * LOG2E) * rel
            s = jnp.where(allowed, s, -jnp.inf)
            m = jnp.max(s, axis=1, keepdims=True)
            p = jnp.exp2(s - m)
            lsum = jnp.sum(p, axis=1, keepdims=True)
            o = jnp.dot(p.astype(BF16), v2, preferred_element_type=F32) / lsum
            lse = m + jnp.log2(lsum)
            out_g = jnp.where(own, o, out_g)
            lse_g = jnp.where(own, lse, lse_g)
        o_ref[0, :, LANES * g:LANES * (g + 1)] = out_g
        lse_ref[0, :, LANES * g:LANES * (g + 1)] = lse_g


def _band_attention(q, k, v, dil, slopes):
    b, l_len, cw = q.shape
    c = cw // dil
    qb = min(BAND_Q, l_len)
    body = functools.partial(_band_body, step_len=float(dil), slopes=slopes)
    return pl.pallas_call(
        body,
        out_shape=(jax.ShapeDtypeStruct((b, l_len, cw), F32),) * 2,
        grid=(b, dil, l_len // qb),
        in_specs=[
            pl.BlockSpec((1, qb, c), lambda i, r, j: (i, j, r)),
            pl.BlockSpec((1, l_len, c), lambda i, r, j: (i, 0, r)),
            pl.BlockSpec((1, l_len, c), lambda i, r, j: (i, 0, r)),
        ],
        out_specs=(pl.BlockSpec((1, qb, c), lambda i, r, j: (i, j, r)),) * 2,
        compiler_params=_cparams(("parallel", "arbitrary", "arbitrary")),
        name="band_attention_d%d" % dil,
    )(q, k, v)


def _band_combine_body(*refs):
    n = len(DILATED_CONFIGS)
    in_refs, o_ref, scr_ref = refs[:2 * n], refs[2 * n], refs[2 * n + 1]
    tm, c = o_ref.shape[1], o_ref.shape[2]
    vals = []
    for idx, (_, dil) in enumerate(DILATED_CONFIGS):
        for which in range(2):
            src = in_refs[2 * idx + which]
            if dil == 1:
                vals.append(src[0])
                continue
            groups = []
            for g in range(c // LANES):
                slot = scr_ref.at[(2 * idx + which) * (c // LANES) + g]
                for r in range(dil):
                    lo = c * r + LANES * g
                    slot[pl.ds(r, tm // dil, stride=dil), :] = src[0, :, lo:lo + LANES]
                groups.append(slot[...])
            vals.append(jnp.concatenate(groups, axis=1))
    outs, lses = vals[0::2], vals[1::2]
    m = functools.reduce(jnp.maximum, lses)
    ws = [jnp.exp2(x - m) for x in lses]
    num = functools.reduce(lambda a, t: a + t, [w * o for w, o in zip(ws, outs)])
    den = functools.reduce(lambda a, t: a + t, ws)
    o_ref[0] = (num / den).astype(o_ref.dtype)


def _band_combine(pairs):
    b, s, c = pairs[0][0].shape
    tm = TOK_TILE
    specs = []
    for (_, dil) in DILATED_CONFIGS:
        specs += [pl.BlockSpec((1, tm // dil, dil * c), lambda i, j: (i, j, 0))] * 2
    flat = [a for p in pairs for a in p]
    return pl.pallas_call(
        _band_combine_body,
        out_shape=jax.ShapeDtypeStruct((b, s, c), BF16),
        grid=(b, s // tm),
        in_specs=specs,
        out_specs=pl.BlockSpec((1, tm, c), lambda i, j: (i, j, 0)),
        scratch_shapes=[pltpu.VMEM((2 * len(DILATED_CONFIGS) * (c // LANES), tm, LANES), F32)],
        compiler_params=_cparams(("parallel", "arbitrary")),
        name="band_combine",
    )(*flat)


def _outproj_body(x_ref, oa_ref, ob_ref, oc_ref, od_ref, w_ref, g_ref, x1_ref, h2_ref, edge_ref):
    d, tm = x_ref.shape[1], x_ref.shape[2]
    j = pl.program_id(1)
    mix = jnp.concatenate([oa_ref[0], ob_ref[0], oc_ref[0]], axis=0)
    w = w_ref[...]
    y = jnp.dot(w[:, :768], mix, preferred_element_type=F32)
    y = y + lax.dot_general(w[:, 768:], od_ref[0], (((1,), (1,)), ((), ())), preferred_element_type=F32)
    x1 = x_ref[0] + y
    x1_ref[0] = x1
    h2 = _rms_rows(x1, g_ref[...]).astype(BF16)
    h2_ref[0] = h2

    @pl.when(j == 0)
    def _():
        edge_ref[0] = jnp.zeros((d, LANES), F32)

    lane = lax.broadcasted_iota(jnp.int32, (d, LANES), 1)
    first = jnp.broadcast_to(h2[:, 0:1].astype(F32), (d, LANES))
    last = jnp.broadcast_to(h2[:, tm - 1:tm].astype(F32), (d, LANES))
    e = edge_ref[0]
    e = jnp.where(lane == 2 * j, first, e)
    e = jnp.where(lane == 2 * j + 1, last, e)
    edge_ref[0] = e


def _out_project(xt, oa, ob, oc, od, w_out_t, g2):
    b, d, s = xt.shape
    tm = TOK_TILE
    assert 2 * (s // tm) <= LANES
    fm = lambda rows: pl.BlockSpec((1, rows, tm), lambda i, j: (i, 0, j))
    return pl.pallas_call(
        _outproj_body,
        out_shape=(jax.ShapeDtypeStruct((b, d, s), F32), jax.ShapeDtypeStruct((b, d, s), BF16),
                   jax.ShapeDtypeStruct((b, d, LANES), F32)),
        grid=(b, s // tm),
        in_specs=[fm(d), fm(256), fm(256), fm(256),
                  pl.BlockSpec((1, tm, D_WIDTH), lambda i, j: (i, j, 0)),
                  pl.BlockSpec((d, d), lambda i, j: (0, 0)),
                  pl.BlockSpec((d, 1), lambda i, j: (0, 0))],
        out_specs=(fm(d), fm(d), pl.BlockSpec((1, d, LANES), lambda i, j: (i, 0, 0))),
        compiler_params=_cparams(("parallel", "arbitrary")),
        name="out_project",
    )(xt, oa, ob, oc, od, w_out_t, g2)


def _halo_body(w_ref, h_ref, o_ref):
    o_ref[...] = jnp.dot(w_ref[...], h_ref[...], preferred_element_type=F32)


def _halo_up(w_up_t, halo_h):
    ff2, d = w_up_t.shape
    nh = halo_h.shape[1]
    rows = ff2 // 4
    return pl.pallas_call(
        _halo_body,
        out_shape=jax.ShapeDtypeStruct((ff2, nh), F32),
        grid=(4,),
        in_specs=[pl.BlockSpec((rows, d), lambda i: (i, 0)),
                  pl.BlockSpec((d, nh), lambda i: (0, 0))],
        out_specs=pl.BlockSpec((rows, nh), lambda i: (i, 0)),
        compiler_params=_cparams(("arbitrary",)),
        name="mlp_halo_up",
    )(w_up_t, halo_h)


def _mlp_body(x1_ref, h2_ref, wup_ref, wdn_ref, par_ref, halo_ref, o_ref, *, n_chunks):
    tf = h2_ref.shape[2]
    ff = wdn_ref.shape[1]
    rc = ff // n_chunks
    nt = pl.num_programs(1)
    col = (pl.program_id(0) * nt + pl.program_id(1)) * 2
    grp = col // LANES
    cin = col % LANES
    shift_l = (LANES - cin) % LANES
    shift_r = LANES - 2 - cin
    h = h2_ref[0]
    lane = lax.broadcasted_iota(jnp.int32, (rc, tf), 1)
    first = lane == 0
    last = lane == tf - 1
    reps = tf // LANES

    def conv(u, r0):
        hal = halo_ref[grp, r0:r0 + rc, :]
        left = pltpu.roll(hal, shift_l, 1)
        right = pltpu.roll(hal, shift_r, 1)
        prev = jnp.where(first, jnp.concatenate([left] * reps, axis=1), pltpu.roll(u, 1, 1))
        nxt = jnp.where(last, jnp.concatenate([right] * reps, axis=1), pltpu.roll(u, tf - 1, 1))
        p = par_ref[r0:r0 + rc, :]
        return p[:, 0:1] * prev + p[:, 1:2] * u + p[:, 2:3] * nxt + p[:, 3:4]

    y = x1_ref[0]
    for c in range(n_chunks):
        ug = jnp.dot(wup_ref[c * rc:(c + 1) * rc, :], h, preferred_element_type=F32)
        uv = jnp.dot(wup_ref[ff + c * rc:ff + (c + 1) * rc, :], h, preferred_element_type=F32)
        gate = conv(ug, c * rc)
        val = conv(uv, ff + c * rc)
        act = gate / (1.0 + jnp.exp(-gate)) * val
        y = y + jnp.dot(wdn_ref[:, c * rc:(c + 1) * rc], act.astype(BF16), preferred_element_type=F32)
    o_ref[0] = y


def _mlp(x1t, h2t, w_up_t, w_down_t, conv_par, halo_up, n_chunks):
    b, d, s = x1t.shape
    tf = TOK_TILE
    ff2 = w_up_t.shape[0]
    ngrp = halo_up.shape[0]
    whole = lambda i, j: (0, 0)
    body = functools.partial(_mlp_body, n_chunks=n_chunks)
    const = dict(pipeline_mode=pl.Buffered(1))
    return pl.pallas_call(
        body,
        out_shape=jax.ShapeDtypeStruct((b, d, s), F32),
        grid=(b, s // tf),
        in_specs=[pl.BlockSpec((1, d, tf), lambda i, j: (i, 0, j)),
                  pl.BlockSpec((1, d, tf), lambda i, j: (i, 0, j)),
                  pl.BlockSpec((ff2, d), whole, **const),
                  pl.BlockSpec((d, ff2 // 2), whole, **const),
                  pl.BlockSpec((ff2, 8), whole, **const),
                  pl.BlockSpec((ngrp, ff2, LANES), lambda i, j: (0, 0, 0), **const)],
        out_specs=pl.BlockSpec((1, d, tf), lambda i, j: (i, 0, j)),
        compiler_params=_cparams(("arbitrary", "arbitrary")),
        name="gated_conv_mlp",
    )(x1t, h2t, w_up_t, w_down_t, conv_par, halo_up)


def _halo_columns(edges, nt):
    b, d, _ = edges.shape
    ez = jnp.concatenate([edges[:, :, :2 * nt], jnp.zeros((b, d, 1), edges.dtype)], axis=2)
    idx = []
    for j in range(nt):
        idx += [2 * j - 1 if j > 0 else 2 * nt, 2 * j + 2 if j < nt - 1 else 2 * nt]
    cols = jnp.take(ez, jnp.asarray(idx, jnp.int32), axis=2)
    cols = cols.transpose(1, 0, 2).reshape(d, b * nt * 2)
    pad = (-cols.shape[1]) % LANES
    return jnp.pad(cols, ((0, 0), (0, pad))).astype(BF16)


def _rope_tables(s):
    def freqs(dim):
        return 1.0 / (ROPE_THETA ** (jnp.arange(0, dim, 2, dtype=F32) / dim))
    t = jnp.arange(s)
    rows = (t // GRID_W).astype(F32)
    cols = (t % GRID_W).astype(F32)
    pos = t.astype(F32)
    fa = freqs(HEAD_DIM // 2)
    ar, ac = fa[:, None] * rows[None, :], fa[:, None] * cols[None, :]
    am = freqs(C_ROPE)[:, None] * pos[None, :]
    cosa = jnp.concatenate([jnp.cos(ar), jnp.cos(ar), jnp.cos(ac), jnp.cos(ac)], axis=0)
    sina = jnp.concatenate([-jnp.sin(ar), jnp.sin(ar), -jnp.sin(ac), jnp.sin(ac)], axis=0)
    cosm = jnp.concatenate([jnp.cos(am), jnp.cos(am)], axis=0)
    sinm = jnp.concatenate([-jnp.sin(am), jnp.sin(am)], axis=0)
    return cosa, sina, cosm, sinm


def kernel(x, norm1_g, w_in, a_qn_g, a_kn_g, b_qn_g, b_kn_g, b_lam_q1, b_lam_k1, b_lam_q2, b_lam_k2, b_sub_g,
           c_qa_g, c_kva_g, c_wqb, c_wkvb, c_qn_g, c_kn_g, d_qn_g, d_kn_g, w_out, norm2_g, w_up, conv_w, conv_b,
           w_down):
    b, s, d = x.shape
    depth = w_in.shape[0]
    ff = w_down.shape[1]
    nt = s // TOK_TILE
    assert s % TOK_TILE == 0 and TOK_TILE % KSTEP == 0 and TQ % KSTEP == 0 and s % TQ == 0 and s % GRID_W == 0
    assert s <= 1 << (POS_SPLIT_BITS + BF16_MANTISSA_BITS)
    for window, dil in DILATED_CONFIGS:
        assert window // (2 * dil) == BAND_RADIUS and (s // dil) % min(BAND_Q, s // dil) == 0
        assert TOK_TILE % (dil * BF16_ROWS) == 0

    tr = lambda w: jnp.swapaxes(w, 1, 2).astype(BF16)
    gains = jnp.concatenate([norm1_g, a_qn_g, a_kn_g, b_qn_g, b_kn_g, c_qa_g, c_kva_g, c_qn_g, c_kn_g,
                             d_qn_g, d_kn_g], axis=1)[:, :, None]
    conv_par = jnp.concatenate([jnp.swapaxes(conv_w, 1, 2), conv_b[:, :, None],
                                jnp.zeros((depth, 2 * ff, 4), F32)], axis=2)
    slopes = [2.0 ** (-8.0 * i / (2 * N_HEADS)) for i in range(1, 2 * N_HEADS + 1)]
    slopes_b, slopes_d = slopes[:N_HEADS], tuple(slopes[N_HEADS:])
    lam_init = jnp.asarray([0.8 - 0.6 * math.exp(-0.3 * l) for l in range(depth)], F32)
    lam = (jnp.exp(jnp.sum(b_lam_q1 * b_lam_k1, axis=1)) - jnp.exp(jnp.sum(b_lam_q2 * b_lam_k2, axis=1)) + lam_init)
    scal = jnp.concatenate([lam[:, None], (1.0 - lam_init)[:, None],
                            jnp.broadcast_to(jnp.asarray(slopes_b, F32), (depth, N_HEADS)),
                            jnp.zeros((depth, 2), F32)], axis=1)
    layers = dict(w_in=tr(w_in), wqb=tr(c_wqb), wkvb=tr(c_wkvb), w_out=tr(w_out), w_up=tr(w_up), w_down=tr(w_down),
                  gains=gains, g2=norm2_g[:, :, None], sub_g=b_sub_g[:, :, None], conv_par=conv_par, scal=scal)
    cosa, sina, cosm, sinm = _rope_tables(s)
    n_chunks = 2 if (ff // 2) % LANES == 0 else 1

    def layer(xt, p):
        (qa, ka, va, qb, kb, vb, qc, kc, vc, dq1, dq4, dq16, dk1, dk4, dk16, dv1, dv4, dv16) = _project(
            xt, p["w_in"], p["wqb"], p["wkvb"], p["gains"], cosa, sina, cosm, sinm)
        oa = _dense_attention(qa, ka, va, HEAD_DIM, N_HEADS // A_KV_HEADS)
        ob = _diff_attention(p["scal"], qb, kb, vb, p["sub_g"])
        oc = _dense_attention(qc, kc, vc, C_QK, 1)
        branches = [_band_attention(q_, k_, v_, dil, slopes_d)
                    for (_, dil), q_, k_, v_ in zip(DILATED_CONFIGS, (dq1, dq4, dq16), (dk1, dk4, dk16), (dv1, dv4, dv16))]
        od = _band_combine(branches)
        x1t, h2t, edges = _out_project(xt, oa, ob, oc, od, p["w_out"], p["g2"])
        halo = _halo_up(p["w_up"], _halo_columns(edges, nt))
        halo = halo.reshape(2 * ff, -1, LANES).transpose(1, 0, 2)
        return _mlp(x1t, h2t, p["w_up"], p["w_down"], p["conv_par"], halo, n_chunks), None

    xt = _transpose_last2(x, TOK_TILE, d)
    xt, _ = lax.scan(layer, xt, layers)
    return _transpose_last2(xt, d, TOK_TILE)
```

```python
import functools
import math

import jax
import jax.numpy as jnp
import numpy as np
from jax import lax
from jax.experimental import pallas as pl
from jax.experimental.pallas import tpu as pltpu

F32 = jnp.float32
BF16 = jnp.bfloat16

HEAD_DIM = 64
N_HEADS = 4
A_KV_HEADS = 2
B_QK_DIM = 32
C_Q_LORA = 256
C_KV_LORA = 128
C_NOPE = 64
C_ROPE = 32
C_QK = C_NOPE + C_ROPE
DILATED_CONFIGS = ((128, 1), (512, 4), (2048, 16))
D_WIDTH = N_HEADS * HEAD_DIM
GRID_W = 64
ROPE_THETA = 10000.0
NORM_EPS = 1e-6
LOG2E = 1.4426950408889634


def _bf16_pieces(x, n):
    out = []
    for _ in range(n):
        u = np.float32(x).view(np.uint32)
        u = np.uint32((int(u) + 0x7FFF + ((int(u) >> 16) & 1)) & 0xFFFF0000)
        piece = float(u.view(np.float32))
        out.append(piece)
        x = x - piece
    return tuple(out)


LOG2E_PIECES = _bf16_pieces(LOG2E, 3)
POS_SPLIT_BITS = 6
BF16_MANTISSA_BITS = 8

_SPLITS = (256, 128, 128, 256, 256, 256, C_Q_LORA, C_KV_LORA, C_ROPE, 256, 256, 256)
_OFF = [0]
for _n in _SPLITS:
    _OFF.append(_OFF[-1] + _n)
(O_AQ, O_AK, O_AV, O_BQ, O_BK, O_BV, O_CQL, O_CKVL, O_CKR, O_DQ, O_DK, O_DV, IN_WIDTH) = _OFF

_GAIN_SIZES = (("n1", 1024), ("aq", 64), ("ak", 64), ("bq", 32), ("bk", 32), ("cqa", 256), ("ckva", 128),
               ("cqn", 96), ("ckn", 96), ("dq", 64), ("dk", 64))
G_OFF = {}
_o = 0
for _k, _n in _GAIN_SIZES:
    G_OFF[_k] = (_o, _n)
    _o += _n
G_TOTAL = _o

LANES = 128
BF16_ROWS = 16
TOK_TILE = 512
TQ = 512
KSTEP = 256
SCORES_AHEAD = 2
BAND_Q = 128
BAND_QSTEP = 512
BAND_RADIUS = 64
VMEM_LIMIT = 56 * 1024 * 1024


def _cparams(sem, flags=None):
    return pltpu.CompilerParams(dimension_semantics=sem, vmem_limit_bytes=VMEM_LIMIT, flags=flags)


def _rms_rows(v, g_col):
    ms = jnp.mean(v * v, axis=0, keepdims=True)
    return v * lax.rsqrt(ms + NORM_EPS) * g_col


def _swap_halves(x, n):
    parts = []
    for i in range(0, x.shape[0], 2 * n):
        parts.append(x[i + n:i + 2 * n])
        parts.append(x[i:i + n])
    return jnp.concatenate(parts, axis=0)


def _rope_rows(x, cos, sin_signed, n):
    return x * cos + _swap_halves(x, n) * sin_signed


def _transpose_body(x_ref, o_ref):
    o_ref[0] = x_ref[0].T


def _transpose_last2(x, tr, tc):
    b, r, c = x.shape
    return pl.pallas_call(
        _transpose_body,
        out_shape=jax.ShapeDtypeStruct((b, c, r), x.dtype),
        grid=(b, r // tr, c // tc),
        in_specs=[pl.BlockSpec((1, tr, tc), lambda i, j, k: (i, j, k))],
        out_specs=pl.BlockSpec((1, tc, tr), lambda i, j, k: (i, k, j)),
        compiler_params=_cparams(("parallel", "arbitrary", "arbitrary")),
        name="transpose_tokens",
    )(x)


def _store_dilated(scr_ref, val, out_refs):
    tm, c = val.shape
    for g in range(c // LANES):
        scr_ref[g] = val[:, LANES * g:LANES * (g + 1)]
    for (_, dil), out in zip(DILATED_CONFIGS, out_refs):
        if dil == 1:
            out[0] = val.astype(BF16)
            continue
        n = tm // dil
        for r in range(dil):
            for g in range(c // LANES):
                lo = c * r + LANES * g
                out[0, :, lo:lo + LANES] = scr_ref[g, pl.ds(r, n, stride=dil), :].astype(BF16)


def _store_key_blocks(v_ref, v):
    for c in range(v.shape[1] // KSTEP):
        v_ref[0, c] = v[:, c * KSTEP:(c + 1) * KSTEP]


def _proj_body(x_ref, w_ref, wqb_ref, wkvb_ref, g_ref, cosa_ref, sina_ref, cosm_ref, sinm_ref,
               qa_ref, ka_ref, va_ref, qb_ref, kb_ref, vb_ref, qc_ref, kc_ref, vc_ref,
               dq1_ref, dq4_ref, dq16_ref, dk1_ref, dk4_ref, dk16_ref, dv1_ref, dv4_ref, dv16_ref,
               scr_ref):
    tm = x_ref.shape[2]

    def gain(name):
        o, n = G_OFF[name]
        return g_ref[o:o + n, :]

    x = x_ref[0]
    h = _rms_rows(x, gain("n1")).astype(BF16)
    proj = jnp.dot(w_ref[...], h, preferred_element_type=F32)

    cosa, sina = cosa_ref[...], sina_ref[...]
    cosm, sinm = cosm_ref[...], sinm_ref[...]
    zeros64 = jnp.zeros((64, tm), F32)
    zeros32 = jnp.zeros((32, tm), F32)

    sa = HEAD_DIM ** -0.5 * LOG2E
    for hh in range(N_HEADS):
        q = proj[O_AQ + 64 * hh:O_AQ + 64 * hh + 64]
        q = _rope_rows(_rms_rows(q, gain("aq")), cosa, sina, 16) * sa
        qa_ref[0, 64 * hh:64 * hh + 64, :] = q.astype(BF16)
    for g in range(A_KV_HEADS):
        k = proj[O_AK + 64 * g:O_AK + 64 * g + 64]
        k = _rope_rows(_rms_rows(k, gain("ak")), cosa, sina, 16)
        kpad = jnp.concatenate([k, zeros64], axis=0)
        ka_ref[0, :, LANES * g:LANES * (g + 1)] = kpad.T.astype(BF16)
    _store_key_blocks(va_ref, proj[O_AV:O_AV + 128].astype(BF16))

    sb = B_QK_DIM ** -0.5 * LOG2E
    for i in range(2 * N_HEADS):
        q = proj[O_BQ + 32 * i:O_BQ + 32 * i + 32]
        qb_ref[0, 32 * i:32 * i + 32, :] = (_rms_rows(q, gain("bq")) * sb).astype(BF16)
    kpos = pl.program_id(1) * tm + lax.broadcasted_iota(jnp.int32, (BF16_ROWS, tm), 1)
    frow = lax.broadcasted_iota(jnp.int32, (BF16_ROWS, tm), 0)
    pos_hi = ((kpos >> POS_SPLIT_BITS) << POS_SPLIT_BITS).astype(F32)
    pos_lo = (kpos & ((1 << POS_SPLIT_BITS) - 1)).astype(F32)
    kfeat = jnp.where(frow < 3, pos_hi, jnp.where(frow < 6, pos_lo, 0.0))
    zeros48 = jnp.zeros((48, tm), F32)
    for hh in range(N_HEADS):
        k1 = _rms_rows(proj[O_BK + 64 * hh:O_BK + 64 * hh + 32], gain("bk"))
        k2 = _rms_rows(proj[O_BK + 64 * hh + 32:O_BK + 64 * hh + 64], gain("bk"))
        kpad = jnp.concatenate([k1, k2, kfeat, zeros48], axis=0)
        kb_ref[0, :, LANES * hh:LANES * (hh + 1)] = kpad.T.astype(BF16)
    _store_key_blocks(vb_ref, proj[O_BV:O_BV + 256].astype(BF16))

    sc = C_QK ** -0.5 * LOG2E
    cq = _rms_rows(proj[O_CQL:O_CQL + C_Q_LORA], gain("cqa")).astype(BF16)
    qc = jnp.dot(wqb_ref[...], cq, preferred_element_type=F32)
    ckv = _rms_rows(proj[O_CKVL:O_CKVL + C_KV_LORA], gain("ckva")).astype(BF16)
    kvc = jnp.dot(wkvb_ref[...], ckv, preferred_element_type=F32)
    kr = proj[O_CKR:O_CKR + C_ROPE]
    vcs = []
    for hh in range(N_HEADS):
        q = _rms_rows(qc[C_QK * hh:C_QK * (hh + 1)], gain("cqn"))
        q = jnp.concatenate([q[:C_NOPE], _rope_rows(q[C_NOPE:], cosm, sinm, 16)], axis=0) * sc
        qc_ref[0, C_QK * hh:C_QK * (hh + 1), :] = q.astype(BF16)
        kk = jnp.concatenate([kvc[128 * hh:128 * hh + C_NOPE], kr], axis=0)
        kk = _rms_rows(kk, gain("ckn"))
        kk = jnp.concatenate([kk[:C_NOPE], _rope_rows(kk[C_NOPE:], cosm, sinm, 16), zeros32], axis=0)
        kc_ref[0, :, LANES * hh:LANES * (hh + 1)] = kk.T.astype(BF16)
        vcs.append(kvc[128 * hh + C_NOPE:128 * (hh + 1)])
    _store_key_blocks(vc_ref, jnp.concatenate(vcs, axis=0).astype(BF16))

    sd = HEAD_DIM ** -0.5 * LOG2E
    qs, ks, vs = [], [], []
    for g in range(2):
        qg, kg = [], []
        for hh in (2 * g, 2 * g + 1):
            qg.append(_rms_rows(proj[O_DQ + 64 * hh:O_DQ + 64 * hh + 64], gain("dq")) * sd)
            kg.append(_rms_rows(proj[O_DK + 64 * hh:O_DK + 64 * hh + 64], gain("dk")))
        qs.append(jnp.concatenate(qg, axis=0).T)
        ks.append(jnp.concatenate(kg, axis=0).T)
        vs.append(proj[O_DV + 128 * g:O_DV + 128 * (g + 1)].T)
    _store_dilated(scr_ref, jnp.concatenate(qs, axis=1), (dq1_ref, dq4_ref, dq16_ref))
    _store_dilated(scr_ref, jnp.concatenate(ks, axis=1), (dk1_ref, dk4_ref, dk16_ref))
    _store_dilated(scr_ref, jnp.concatenate(vs, axis=1), (dv1_ref, dv4_ref, dv16_ref))


def _project(xt, w_in_t, wqb_t, wkvb_t, gains, cosa, sina, cosm, sinm):
    b, d, s = xt.shape
    tm = TOK_TILE
    nt = s // tm
    fm = lambda rows: jax.ShapeDtypeStruct((b, rows, s), BF16)
    tmj = lambda cols: jax.ShapeDtypeStruct((b, s, cols), BF16)
    vblk = lambda rows: jax.ShapeDtypeStruct((b, s // KSTEP, rows, KSTEP), BF16)
    dviews = tuple(jax.ShapeDtypeStruct((b, s // dil, dil * D_WIDTH), BF16) for _, dil in DILATED_CONFIGS)
    out_shape = (fm(256), tmj(256), vblk(128),
                 fm(256), tmj(512), vblk(256),
                 fm(384), tmj(512), vblk(256)) + dviews * 3
    fm_spec = lambda rows: pl.BlockSpec((1, rows, tm), lambda i, j: (i, 0, j))
    tm_spec = lambda cols: pl.BlockSpec((1, tm, cols), lambda i, j: (i, j, 0))
    v_spec = lambda rows: pl.BlockSpec((1, tm // KSTEP, rows, KSTEP), lambda i, j: (i, j, 0, 0))
    dspecs = tuple(pl.BlockSpec((1, tm // dil, dil * D_WIDTH), lambda i, j: (i, j, 0)) for _, dil in DILATED_CONFIGS)
    out_specs = (fm_spec(256), tm_spec(256), v_spec(128),
                 fm_spec(256), tm_spec(512), v_spec(256),
                 fm_spec(384), tm_spec(512), v_spec(256)) + dspecs * 3
    whole = lambda i, j: (0, 0)
    in_specs = [
        pl.BlockSpec((1, d, tm), lambda i, j: (i, 0, j)),
        pl.BlockSpec((IN_WIDTH, d), whole),
        pl.BlockSpec((N_HEADS * C_QK, C_Q_LORA), whole),
        pl.BlockSpec((N_HEADS * 128, C_KV_LORA), whole),
        pl.BlockSpec((G_TOTAL, 1), whole),
        pl.BlockSpec((64, tm), lambda i, j: (0, j)),
        pl.BlockSpec((64, tm), lambda i, j: (0, j)),
        pl.BlockSpec((32, tm), lambda i, j: (0, j)),
        pl.BlockSpec((32, tm), lambda i, j: (0, j)),
    ]
    return pl.pallas_call(
        _proj_body,
        out_shape=out_shape,
        grid=(b, nt),
        in_specs=in_specs,
        out_specs=out_specs,
        scratch_shapes=[pltpu.VMEM((D_WIDTH // LANES, tm, LANES), F32)],
        compiler_params=_cparams(("parallel", "arbitrary")),
        name="project_heads",
    )(xt, w_in_t, wqb_t, wkvb_t, gains, cosa, sina, cosm, sinm)


def _online_update(s, v_aug, carry, shift=None):
    m, acc = carry
    smax = jnp.max(s, axis=0, keepdims=True)
    if shift is None:
        m_new = jnp.maximum(m, smax)
        p = jnp.exp2(s - m_new).astype(BF16)
    else:
        m_new = jnp.maximum(m, smax + shift)
        p = jnp.exp2(s - (m_new - shift)).astype(BF16)
    alpha = jnp.exp2(m - m_new)
    acc = alpha * acc + jnp.dot(v_aug, p, preferred_element_type=F32)
    return m_new, acc


def _init_carry(dv, tq):
    return (jnp.full((1, tq), -jnp.inf, F32), jnp.zeros((dv + BF16_ROWS, tq), F32))


def _finish(carry, dv):
    _, acc = carry
    return acc[:dv] / acc[dv:dv + 1]


def _dense_body(q_ref, k_ref, v_ref, o_ref):
    dq, tq = q_ref.shape[1], q_ref.shape[2]
    nkb, dv, ks = v_ref.shape[1], v_ref.shape[2], v_ref.shape[3]
    q = q_ref[0]
    qpad = jnp.concatenate([q, jnp.zeros((LANES - dq, tq), BF16)], axis=0)
    ones = jnp.ones((BF16_ROWS, ks), BF16)

    def scores(i):
        return jnp.dot(k_ref[0, i * ks:(i + 1) * ks, :], qpad, preferred_element_type=F32)

    carry = _init_carry(dv, tq)
    ahead = SCORES_AHEAD
    pending = [scores(i) for i in range(min(ahead, nkb))]
    for i in range(nkb):
        s = pending.pop(0)
        if i + ahead < nkb:
            pending.append(scores(i + ahead))
        carry = _online_update(s, jnp.concatenate([v_ref[0, i], ones], axis=0), carry)
    o_ref[0] = _finish(carry, dv).astype(o_ref.dtype)


def _dense_attention(q, k, v, dq, q_per_kv):
    b, rows, s = q.shape
    nh = rows // dq
    nkb, tk = v.shape[1], v.shape[3]
    return pl.pallas_call(
        _dense_body,
        out_shape=jax.ShapeDtypeStruct((b, nh * HEAD_DIM, s), BF16),
        grid=(b, nh, s // TQ),
        in_specs=[
            pl.BlockSpec((1, dq, TQ), lambda i, h, j: (i, h, j)),
            pl.BlockSpec((1, s, LANES), lambda i, h, j: (i, 0, h // q_per_kv)),
            pl.BlockSpec((1, nkb, HEAD_DIM, tk), lambda i, h, j: (i, 0, h // q_per_kv, 0)),
        ],
        out_specs=pl.BlockSpec((1, HEAD_DIM, TQ), lambda i, h, j: (i, h, j)),
        compiler_params=_cparams(("parallel", "arbitrary", "arbitrary")),
        name="dense_attention",
    )(q, k, v)


def _diff_body(sc_ref, q_ref, k_ref, v_ref, g_ref, dist_ref, o_ref):
    tq = q_ref.shape[2]
    nkb, dv, ks = v_ref.shape[1], v_ref.shape[2], v_ref.shape[3]
    hh = pl.program_id(1)
    jq = pl.program_id(2)
    lam = sc_ref[0]
    out_scale = sc_ref[1]
    slope = sc_ref[2 + hh]
    q = q_ref[0]
    z32 = jnp.zeros((B_QK_DIM, tq), BF16)
    z48 = jnp.zeros((48, tq), BF16)
    zfeat = jnp.zeros((BF16_ROWS, tq), BF16)
    ones = jnp.ones((BF16_ROWS, ks), BF16)
    frow = lax.broadcasted_iota(jnp.int32, (BF16_ROWS, tq), 0)
    piece = jnp.where(frow % 3 == 0, LOG2E_PIECES[0], jnp.where(frow % 3 == 1, LOG2E_PIECES[1], LOG2E_PIECES[2]))
    cfeat = jnp.where(frow < 6, piece, 0.0)
    qpos = (jq * tq + lax.broadcasted_iota(jnp.int32, (1, tq), 1)).astype(F32)

    n_mixed = tq // ks
    first = jq * n_mixed

    def block_index(ii):
        return lax.rem(first + ii, nkb)

    def side(ii):
        return jnp.where(block_index(ii) < first, 1.0, -1.0).astype(F32)

    def scores(ii):
        i = block_index(ii)
        k = k_ref[0, pl.ds(pl.multiple_of(i * ks, ks), ks), :]
        feat = zfeat if ii < n_mixed else (cfeat * (side(ii) * slope)).astype(BF16)
        q1 = jnp.concatenate([q[:B_QK_DIM], z32, feat, z48], axis=0)
        q2 = jnp.concatenate([z32, q[B_QK_DIM:], feat, z48], axis=0)
        return (jnp.dot(k, q1, preferred_element_type=F32), jnp.dot(k, q2, preferred_element_type=F32))

    c1, c2 = _init_carry(dv, tq), _init_carry(dv, tq)
    s_next = scores(0)
    for ii in range(nkb):
        s1, s2 = s_next
        if ii + 1 < nkb:
            s_next = scores(ii + 1)
        v_aug = jnp.concatenate([v_ref[0, block_index(ii)], ones], axis=0)
        if ii < n_mixed:
            bias = dist_ref[ii] * (-(slope * LOG2E))
            c1 = _online_update(s1 + bias, v_aug, c1)
            c2 = _online_update(s2 + bias, v_aug, c2)
        else:
            shift = qpos * (-(side(ii) * slope * LOG2E))
            c1 = _online_update(s1, v_aug, c1, shift)
            c2 = _online_update(s2, v_aug, c2, shift)
    o = _finish(c1, dv) - lam * _finish(c2, dv)
    o_ref[0] = (_rms_rows(o, g_ref[...]) * out_scale).astype(o_ref.dtype)


def _diff_attention(scal, q, k, v, sub_g):
    b, rows, s = q.shape
    nkb, tk = v.shape[1], v.shape[3]
    n_mixed = TQ // tk
    jj = jnp.arange(tk)[None, :, None] + tk * jnp.arange(n_mixed)[:, None, None]
    dist = jnp.abs(jj - jnp.arange(TQ)[None, None, :]).astype(F32)
    return pl.pallas_call(
        _diff_body,
        out_shape=jax.ShapeDtypeStruct((b, rows, s), BF16),
        grid=(b, N_HEADS, s // TQ),
        in_specs=[
            pl.BlockSpec(memory_space=pltpu.SMEM),
            pl.BlockSpec((1, 2 * B_QK_DIM, TQ), lambda i, h, j: (i, h, j)),
            pl.BlockSpec((1, s, LANES), lambda i, h, j: (i, 0, h)),
            pl.BlockSpec((1, nkb, HEAD_DIM, tk), lambda i, h, j: (i, 0, h, 0)),
            pl.BlockSpec((HEAD_DIM, 1), lambda i, h, j: (0, 0)),
            pl.BlockSpec((n_mixed, tk, TQ), lambda i, h, j: (0, 0, 0)),
        ],
        out_specs=pl.BlockSpec((1, HEAD_DIM, TQ), lambda i, h, j: (i, h, j)),
        compiler_params=_cparams(("parallel", "arbitrary", "arbitrary")),
        name="diff_attention",
    )(scal, q, k, v, sub_g, dist)


def _band_body(q_ref, k_ref, v_ref, o_ref, lse_ref, *, step_len, slopes, qb):
    l_len = k_ref.shape[1]
    win = min(qb + 2 * BAND_RADIUS, l_len)
    lane = lax.broadcasted_iota(jnp.int32, (qb, LANES), 1)
    row = lax.broadcasted_iota(jnp.int32, (qb, win), 0)
    colw = lax.broadcasted_iota(jnp.int32, (qb, win), 1)
    for sb in range(q_ref.shape[1] // qb):
        jb = pl.program_id(2) * (q_ref.shape[1] // qb) + sb
        start = jnp.clip(jb * qb - BAND_RADIUS, 0, l_len - win)
        start = pl.multiple_of(start, BAND_RADIUS)
        q = q_ref[0, sb * qb:(sb + 1) * qb, :]
        kwin = k_ref[0, pl.ds(start, win), :]
        vwin = v_ref[0, pl.ds(start, win), :]
        rel_i = jnp.abs((jb * qb - start) + row - colw)
        allowed = rel_i <= BAND_RADIUS
        rel = rel_i.astype(F32)
        for g in range(2):
            q2 = q[:, LANES * g:LANES * (g + 1)].astype(F32)
            k2 = kwin[:, LANES * g:LANES * (g + 1)]
            v2 = vwin[:, LANES * g:LANES * (g + 1)]
            out_g = jnp.zeros((qb, LANES), F32)
            lse_g = jnp.zeros((qb, LANES), F32)
            for hh in range(2):
                own = (lane >= HEAD_DIM * hh) & (lane < HEAD_DIM * (hh + 1))
                qm = jnp.where(own, q2, 0.0).astype(BF16)
                s = lax.dot_general(qm, k2, (((1,), (1,)), ((), ())), preferred_element_type=F32)
                s = s - (slopes[2 * g + hh] * step_len * LOG2E) * rel
                s = jnp.where(allowed, s, -jnp.inf)
                m = jnp.max(s, axis=1, keepdims=True)
                p = jnp.exp2(s - m)
                lsum = jnp.sum(p, axis=1, keepdims=True)
                o = jnp.dot(p.astype(BF16), v2, preferred_element_type=F32) / lsum
                lse = m + jnp.log2(lsum)
                out_g = jnp.where(own, o, out_g)
                lse_g = jnp.where(own, lse, lse_g)
            o_ref[0, sb * qb:(sb + 1) * qb, LANES * g:LANES * (g + 1)] = out_g
            lse_ref[0, sb * qb:(sb + 1) * qb, LANES * g:LANES * (g + 1)] = lse_g


def _band_attention(q, k, v, dil, slopes):
    b, l_len, cw = q.shape
    c = cw // dil
    qb = min(BAND_Q, l_len)
    qstep = min(BAND_QSTEP, l_len)
    body = functools.partial(_band_body, step_len=float(dil), slopes=slopes, qb=qb)
    return pl.pallas_call(
        body,
        out_shape=(jax.ShapeDtypeStruct((b, l_len, cw), F32),) * 2,
        grid=(b, dil, l_len // qstep),
        in_specs=[
            pl.BlockSpec((1, qstep, c), lambda i, r, j: (i, j, r)),
            pl.BlockSpec((1, l_len, c), lambda i, r, j: (i, 0, r)),
            pl.BlockSpec((1, l_len, c), lambda i, r, j: (i, 0, r)),
        ],
        out_specs=(pl.BlockSpec((1, qstep, c), lambda i, r, j: (i, j, r)),) * 2,
        compiler_params=_cparams(("parallel", "arbitrary", "arbitrary")),
        name="band_attention_d%d" % dil,
    )(q, k, v)


def _band_combine_body(*refs):
    n = len(DILATED_CONFIGS)
    in_refs, o_ref, scr_ref = refs[:2 * n], refs[2 * n], refs[2 * n + 1]
    tm, c = o_ref.shape[1], o_ref.shape[2]
    vals = []
    for idx, (_, dil) in enumerate(DILATED_CONFIGS):
        for which in range(2):
            src = in_refs[2 * idx + which]
            if dil == 1:
                vals.append(src[0])
                continue
            groups = []
            for g in range(c // LANES):
                slot = scr_ref.at[(2 * idx + which) * (c // LANES) + g]
                for r in range(dil):
                    lo = c * r + LANES * g
                    slot[pl.ds(r, tm // dil, stride=dil), :] = src[0, :, lo:lo + LANES]
                groups.append(slot[...])
            vals.append(jnp.concatenate(groups, axis=1))
    outs, lses = vals[0::2], vals[1::2]
    m = functools.reduce(jnp.maximum, lses)
    ws = [jnp.exp2(x - m) for x in lses]
    num = functools.reduce(lambda a, t: a + t, [w * o for w, o in zip(ws, outs)])
    den = functools.reduce(lambda a, t: a + t, ws)
    o_ref[0] = (num / den).astype(o_ref.dtype)


def _band_combine(pairs):
    b, s, c = pairs[0][0].shape
    tm = TOK_TILE
    specs = []
    for (_, dil) in DILATED_CONFIGS:
        specs += [pl.BlockSpec((1, tm // dil, dil * c), lambda i, j: (i, j, 0))] * 2
    flat = [a for p in pairs for a in p]
    return pl.pallas_call(
        _band_combine_body,
        out_shape=jax.ShapeDtypeStruct((b, s, c), BF16),
        grid=(b, s // tm),
        in_specs=specs,
        out_specs=pl.BlockSpec((1, tm, c), lambda i, j: (i, j, 0)),
        scratch_shapes=[pltpu.VMEM((2 * len(DILATED_CONFIGS) * (c // LANES), tm, LANES), F32)],
        compiler_params=_cparams(("parallel", "arbitrary")),
        name="band_combine",
    )(*flat)


def _outproj_body(x_ref, oa_ref, ob_ref, oc_ref, od_ref, w_ref, g_ref, x1_ref, h2_ref, edge_ref):
    d, tm = x_ref.shape[1], x_ref.shape[2]
    j = pl.program_id(1)
    mix = jnp.concatenate([oa_ref[0], ob_ref[0], oc_ref[0]], axis=0)
    w = w_ref[...]
    y = jnp.dot(w[:, :768], mix, preferred_element_type=F32)
    y = y + lax.dot_general(w[:, 768:], od_ref[0], (((1,), (1,)), ((), ())), preferred_element_type=F32)
    x1 = x_ref[0] + y
    x1_ref[0] = x1
    h2 = _rms_rows(x1, g_ref[...]).astype(BF16)
    h2_ref[0] = h2

    @pl.when(j == 0)
    def _():
        edge_ref[0] = jnp.zeros((d, LANES), F32)

    lane = lax.broadcasted_iota(jnp.int32, (d, LANES), 1)
    first = jnp.broadcast_to(h2[:, 0:1].astype(F32), (d, LANES))
    last = jnp.broadcast_to(h2[:, tm - 1:tm].astype(F32), (d, LANES))
    e = edge_ref[0]
    e = jnp.where(lane == 2 * j, first, e)
    e = jnp.where(lane == 2 * j + 1, last, e)
    edge_ref[0] = e


def _out_project(xt, oa, ob, oc, od, w_out_t, g2):
    b, d, s = xt.shape
    tm = TOK_TILE
    assert 2 * (s // tm) <= LANES
    fm = lambda rows: pl.BlockSpec((1, rows, tm), lambda i, j: (i, 0, j))
    return pl.pallas_call(
        _outproj_body,
        out_shape=(jax.ShapeDtypeStruct((b, d, s), F32), jax.ShapeDtypeStruct((b, d, s), BF16),
                   jax.ShapeDtypeStruct((b, d, LANES), F32)),
        grid=(b, s // tm),
        in_specs=[fm(d), fm(256), fm(256), fm(256),
                  pl.BlockSpec((1, tm, D_WIDTH), lambda i, j: (i, j, 0)),
                  pl.BlockSpec((d, d), lambda i, j: (0, 0)),
                  pl.BlockSpec((d, 1), lambda i, j: (0, 0))],
        out_specs=(fm(d), fm(d), pl.BlockSpec((1, d, LANES), lambda i, j: (i, 0, 0))),
        compiler_params=_cparams(("parallel", "arbitrary")),
        name="out_project",
    )(xt, oa, ob, oc, od, w_out_t, g2)


def _halo_body(w_ref, h_ref, o_ref):
    o_ref[...] = jnp.dot(w_ref[...], h_ref[...], preferred_element_type=F32)


def _halo_up(w_up_t, halo_h):
    ff2, d = w_up_t.shape
    nh = halo_h.shape[1]
    rows = ff2 // 4
    return pl.pallas_call(
        _halo_body,
        out_shape=jax.ShapeDtypeStruct((ff2, nh), F32),
        grid=(4,),
        in_specs=[pl.BlockSpec((rows, d), lambda i: (i, 0)),
                  pl.BlockSpec((d, nh), lambda i: (0, 0))],
        out_specs=pl.BlockSpec((rows, nh), lambda i: (i, 0)),
        compiler_params=_cparams(("arbitrary",)),
        name="mlp_halo_up",
    )(w_up_t, halo_h)


def _mlp_body(x1_ref, h2_ref, wup_ref, wdn_ref, par_ref, halo_ref, o_ref, *, n_chunks):
    tf = h2_ref.shape[2]
    ff = wdn_ref.shape[1]
    rc = ff // n_chunks
    nt = pl.num_programs(1)
    col = (pl.program_id(0) * nt + pl.program_id(1)) * 2
    grp = col // LANES
    cin = col % LANES
    shift_l = (LANES - cin) % LANES
    shift_r = LANES - 2 - cin
    h = h2_ref[0]
    lane = lax.broadcasted_iota(jnp.int32, (rc, tf), 1)
    first = lane == 0
    last = lane == tf - 1
    reps = tf // LANES

    def conv(u, r0):
        hal = halo_ref[grp, r0:r0 + rc, :]
        left = pltpu.roll(hal, shift_l, 1)
        right = pltpu.roll(hal, shift_r, 1)
        prev = jnp.where(first, jnp.concatenate([left] * reps, axis=1), pltpu.roll(u, 1, 1))
        nxt = jnp.where(last, jnp.concatenate([right] * reps, axis=1), pltpu.roll(u, tf - 1, 1))
        p = par_ref[r0:r0 + rc, :]
        return p[:, 0:1] * prev + p[:, 1:2] * u + p[:, 2:3] * nxt + p[:, 3:4]

    y = x1_ref[0]
    for c in range(n_chunks):
        ug = jnp.dot(wup_ref[c * rc:(c + 1) * rc, :], h, preferred_element_type=F32)
        uv = jnp.dot(wup_ref[ff + c * rc:ff + (c + 1) * rc, :], h, preferred_element_type=F32)
        gate = conv(ug, c * rc)
        val = conv(uv, ff + c * rc)
        act = gate / (1.0 + jnp.exp(-gate)) * val
        y = y + jnp.dot(wdn_ref[:, c * rc:(c + 1) * rc], act.astype(BF16), preferred_element_type=F32)
    o_ref[0] = y


def _mlp(x1t, h2t, w_up_t, w_down_t, conv_par, halo_up, n_chunks):
    b, d, s = x1t.shape
    tf = TOK_TILE
    ff2 = w_up_t.shape[0]
    ngrp = halo_up.shape[0]
    whole = lambda i, j: (0, 0)
    body = functools.partial(_mlp_body, n_chunks=n_chunks)
    const = dict(pipeline_mode=pl.Buffered(1))
    return pl.pallas_call(
        body,
        out_shape=jax.ShapeDtypeStruct((b, d, s), F32),
        grid=(b, s // tf),
        in_specs=[pl.BlockSpec((1, d, tf), lambda i, j: (i, 0, j)),
                  pl.BlockSpec((1, d, tf), lambda i, j: (i, 0, j)),
                  pl.BlockSpec((ff2, d), whole, **const),
                  pl.BlockSpec((d, ff2 // 2), whole, **const),
                  pl.BlockSpec((ff2, 8), whole, **const),
                  pl.BlockSpec((ngrp, ff2, LANES), lambda i, j: (0, 0, 0), **const)],
        out_specs=pl.BlockSpec((1, d, tf), lambda i, j: (i, 0, j)),
        compiler_params=_cparams(("arbitrary", "arbitrary")),
        name="gated_conv_mlp",
    )(x1t, h2t, w_up_t, w_down_t, conv_par, halo_up)


def _halo_columns(edges, nt):
    b, d, _ = edges.shape
    ez = jnp.concatenate([edges[:, :, :2 * nt], jnp.zeros((b, d, 1), edges.dtype)], axis=2)
    idx = []
    for j in range(nt):
        idx += [2 * j - 1 if j > 0 else 2 * nt, 2 * j + 2 if j < nt - 1 else 2 * nt]
    cols = jnp.take(ez, jnp.asarray(idx, jnp.int32), axis=2)
    cols = cols.transpose(1, 0, 2).reshape(d, b * nt * 2)
    pad = (-cols.shape[1]) % LANES
    return jnp.pad(cols, ((0, 0), (0, pad))).astype(BF16)


def _rope_tables(s):
    def freqs(dim):
        return 1.0 / (ROPE_THETA ** (jnp.arange(0, dim, 2, dtype=F32) / dim))
    t = jnp.arange(s)
    rows = (t // GRID_W).astype(F32)
    cols = (t % GRID_W).astype(F32)
    pos = t.astype(F32)
    fa = freqs(HEAD_DIM // 2)
    ar, ac = fa[:, None] * rows[None, :], fa[:, None] * cols[None, :]
    am = freqs(C_ROPE)[:, None] * pos[None, :]
    cosa = jnp.concatenate([jnp.cos(ar), jnp.cos(ar), jnp.cos(ac), jnp.cos(ac)], axis=0)
    sina = jnp.concatenate([-jnp.sin(ar), jnp.sin(ar), -jnp.sin(ac), jnp.sin(ac)], axis=0)
    cosm = jnp.concatenate([jnp.cos(am), jnp.cos(am)], axis=0)
    sinm = jnp.concatenate([-jnp.sin(am), jnp.sin(am)], axis=0)
    return cosa, sina, cosm, sinm


def kernel(x, norm1_g, w_in, a_qn_g, a_kn_g, b_qn_g, b_kn_g, b_lam_q1, b_lam_k1, b_lam_q2, b_lam_k2, b_sub_g,
           c_qa_g, c_kva_g, c_wqb, c_wkvb, c_qn_g, c_kn_g, d_qn_g, d_kn_g, w_out, norm2_g, w_up, conv_w, conv_b,
           w_down):
    b, s, d = x.shape
    depth = w_in.shape[0]
    ff = w_down.shape[1]
    nt = s // TOK_TILE
    assert s % TOK_TILE == 0 and TOK_TILE % KSTEP == 0 and TQ % KSTEP == 0 and s % TQ == 0 and s % GRID_W == 0
    assert s <= 1 << (POS_SPLIT_BITS + BF16_MANTISSA_BITS)
    for window, dil in DILATED_CONFIGS:
        assert window // (2 * dil) == BAND_RADIUS and (s // dil) % min(BAND_Q, s // dil) == 0
        assert (s // dil) % min(BAND_QSTEP, s // dil) == 0 and BAND_QSTEP % BAND_Q == 0
        assert TOK_TILE % (dil * BF16_ROWS) == 0

    tr = lambda w: jnp.swapaxes(w, 1, 2).astype(BF16)
    gains = jnp.concatenate([norm1_g, a_qn_g, a_kn_g, b_qn_g, b_kn_g, c_qa_g, c_kva_g, c_qn_g, c_kn_g,
                             d_qn_g, d_kn_g], axis=1)[:, :, None]
    conv_par = jnp.concatenate([jnp.swapaxes(conv_w, 1, 2), conv_b[:, :, None],
                                jnp.zeros((depth, 2 * ff, 4), F32)], axis=2)
    slopes = [2.0 ** (-8.0 * i / (2 * N_HEADS)) for i in range(1, 2 * N_HEADS + 1)]
    slopes_b, slopes_d = slopes[:N_HEADS], tuple(slopes[N_HEADS:])
    lam_init = jnp.asarray([0.8 - 0.6 * math.exp(-0.3 * l) for l in range(depth)], F32)
    lam = (jnp.exp(jnp.sum(b_lam_q1 * b_lam_k1, axis=1)) - jnp.exp(jnp.sum(b_lam_q2 * b_lam_k2, axis=1)) + lam_init)
    scal = jnp.concatenate([lam[:, None], (1.0 - lam_init)[:, None],
                            jnp.broadcast_to(jnp.asarray(slopes_b, F32), (depth, N_HEADS)),
                            jnp.zeros((depth, 2), F32)], axis=1)
    layers = dict(w_in=tr(w_in), wqb=tr(c_wqb), wkvb=tr(c_wkvb), w_out=tr(w_out), w_up=tr(w_up), w_down=tr(w_down),
                  gains=gains, g2=norm2_g[:, :, None], sub_g=b_sub_g[:, :, None], conv_par=conv_par, scal=scal)
    cosa, sina, cosm, sinm = _rope_tables(s)
    n_chunks = 2 if (ff // 2) % LANES == 0 else 1

    def layer(xt, p):
        (qa, ka, va, qb, kb, vb, qc, kc, vc, dq1, dq4, dq16, dk1, dk4, dk16, dv1, dv4, dv16) = _project(
            xt, p["w_in"], p["wqb"], p["wkvb"], p["gains"], cosa, sina, cosm, sinm)
        oa = _dense_attention(qa, ka, va, HEAD_DIM, N_HEADS // A_KV_HEADS)
        ob = _diff_attention(p["scal"], qb, kb, vb, p["sub_g"])
        oc = _dense_attention(qc, kc, vc, C_QK, 1)
        branches = [_band_attention(q_, k_, v_, dil, slopes_d)
                    for (_, dil), q_, k_, v_ in zip(DILATED_CONFIGS, (dq1, dq4, dq16), (dk1, dk4, dk16), (dv1, dv4, dv16))]
        od = _band_combine(branches)
        x1t, h2t, edges = _out_project(xt, oa, ob, oc, od, p["w_out"], p["g2"])
        halo = _halo_up(p["w_up"], _halo_columns(edges, nt))
        halo = halo.reshape(2 * ff, -1, LANES).transpose(1, 0, 2)
        return _mlp(x1t, h2t, p["w_up"], p["w_down"], p["conv_par"], halo, n_chunks), None

    xt = _transpose_last2(x, TOK_TILE, d)
    xt, _ = lax.scan(layer, xt, layers)
    return _transpose_last2(xt, d, TOK_TILE)
```

```python
import functools
import math

import jax
import jax.numpy as jnp
import numpy as np
from jax import lax
from jax.experimental import pallas as pl
from jax.experimental.pallas import tpu as pltpu

F32 = jnp.float32
BF16 = jnp.bfloat16

HEAD_DIM = 64
N_HEADS = 4
A_KV_HEADS = 2
B_QK_DIM = 32
C_Q_LORA = 256
C_KV_LORA = 128
C_NOPE = 64
C_ROPE = 32
C_QK = C_NOPE + C_ROPE
DILATED_CONFIGS = ((128, 1), (512, 4), (2048, 16))
D_WIDTH = N_HEADS * HEAD_DIM
GRID_W = 64
ROPE_THETA = 10000.0
NORM_EPS = 1e-6
LOG2E = 1.4426950408889634


def _bf16_pieces(x, n):
    out = []
    for _ in range(n):
        u = np.float32(x).view(np.uint32)
        u = np.uint32((int(u) + 0x7FFF + ((int(u) >> 16) & 1)) & 0xFFFF0000)
        piece = float(u.view(np.float32))
        out.append(piece)
        x = x - piece
    return tuple(out)


LOG2E_PIECES = _bf16_pieces(LOG2E, 3)
POS_SPLIT_BITS = 6
BF16_MANTISSA_BITS = 8

_SPLITS = (256, 128, 128, 256, 256, 256, C_Q_LORA, C_KV_LORA, C_ROPE, 256, 256, 256)
_OFF = [0]
for _n in _SPLITS:
    _OFF.append(_OFF[-1] + _n)
(O_AQ, O_AK, O_AV, O_BQ, O_BK, O_BV, O_CQL, O_CKVL, O_CKR, O_DQ, O_DK, O_DV, IN_WIDTH) = _OFF

_GAIN_SIZES = (("n1", 1024), ("aq", 64), ("ak", 64), ("bq", 32), ("bk", 32), ("cqa", 256), ("ckva", 128),
               ("cqn", 96), ("ckn", 96), ("dq", 64), ("dk", 64))
G_OFF = {}
_o = 0
for _k, _n in _GAIN_SIZES:
    G_OFF[_k] = (_o, _n)
    _o += _n
G_TOTAL = _o

LANES = 128
BF16_ROWS = 16
TOK_TILE = 512
MLP_CHUNK = 1024
MLP_PAD = 8
TQ = 512
KSTEP = 256
SCORES_AHEAD = 2
BAND_Q = 128
BAND_QSTEP = 512
BAND_RADIUS = 64
VMEM_LIMIT = 56 * 1024 * 1024


def _cparams(sem, flags=None):
    return pltpu.CompilerParams(dimension_semantics=sem, vmem_limit_bytes=VMEM_LIMIT, flags=flags)


def _rms_rows(v, g_col):
    ms = jnp.mean(v * v, axis=0, keepdims=True)
    return v * lax.rsqrt(ms + NORM_EPS) * g_col


def _swap_halves(x, n):
    parts = []
    for i in range(0, x.shape[0], 2 * n):
        parts.append(x[i + n:i + 2 * n])
        parts.append(x[i:i + n])
    return jnp.concatenate(parts, axis=0)


def _rope_rows(x, cos, sin_signed, n):
    return x * cos + _swap_halves(x, n) * sin_signed


def _transpose_body(x_ref, o_ref):
    o_ref[0] = x_ref[0].T


def _transpose_last2(x, tr, tc):
    b, r, c = x.shape
    return pl.pallas_call(
        _transpose_body,
        out_shape=jax.ShapeDtypeStruct((b, c, r), x.dtype),
        grid=(b, r // tr, c // tc),
        in_specs=[pl.BlockSpec((1, tr, tc), lambda i, j, k: (i, j, k))],
        out_specs=pl.BlockSpec((1, tc, tr), lambda i, j, k: (i, k, j)),
        compiler_params=_cparams(("parallel", "arbitrary", "arbitrary")),
        name="transpose_tokens",
    )(x)


def _store_dilated(scr_ref, val, out_refs):
    tm, c = val.shape
    for g in range(c // LANES):
        scr_ref[g] = val[:, LANES * g:LANES * (g + 1)]
    for (_, dil), out in zip(DILATED_CONFIGS, out_refs):
        if dil == 1:
            out[0] = val.astype(BF16)
            continue
        n = tm // dil
        for r in range(dil):
            for g in range(c // LANES):
                lo = c * r + LANES * g
                out[0, :, lo:lo + LANES] = scr_ref[g, pl.ds(r, n, stride=dil), :].astype(BF16)


def _store_key_blocks(v_ref, v):
    for c in range(v.shape[1] // KSTEP):
        v_ref[0, c] = v[:, c * KSTEP:(c + 1) * KSTEP]


def _proj_body(x_ref, w_ref, wqb_ref, wkvb_ref, g_ref, cosa_ref, sina_ref, cosm_ref, sinm_ref,
               qa_ref, ka_ref, va_ref, qb_ref, kb_ref, vb_ref, qc_ref, kc_ref, vc_ref,
               dq1_ref, dq4_ref, dq16_ref, dk1_ref, dk4_ref, dk16_ref, dv1_ref, dv4_ref, dv16_ref,
               scr_ref):
    tm = x_ref.shape[2]

    def gain(name):
        o, n = G_OFF[name]
        return g_ref[o:o + n, :]

    x = x_ref[0]
    h = _rms_rows(x, gain("n1")).astype(BF16)
    proj = jnp.dot(w_ref[...], h, preferred_element_type=F32)

    cosa, sina = cosa_ref[...], sina_ref[...]
    cosm, sinm = cosm_ref[...], sinm_ref[...]
    zeros64 = jnp.zeros((64, tm), F32)
    zeros32 = jnp.zeros((32, tm), F32)

    sa = HEAD_DIM ** -0.5 * LOG2E
    for hh in range(N_HEADS):
        q = proj[O_AQ + 64 * hh:O_AQ + 64 * hh + 64]
        q = _rope_rows(_rms_rows(q, gain("aq")), cosa, sina, 16) * sa
        qa_ref[0, 64 * hh:64 * hh + 64, :] = q.astype(BF16)
    for g in range(A_KV_HEADS):
        k = proj[O_AK + 64 * g:O_AK + 64 * g + 64]
        k = _rope_rows(_rms_rows(k, gain("ak")), cosa, sina, 16)
        kpad = jnp.concatenate([k, zeros64], axis=0)
        ka_ref[0, :, LANES * g:LANES * (g + 1)] = kpad.T.astype(BF16)
    _store_key_blocks(va_ref, proj[O_AV:O_AV + 128].astype(BF16))

    sb = B_QK_DIM ** -0.5 * LOG2E
    for i in range(2 * N_HEADS):
        q = proj[O_BQ + 32 * i:O_BQ + 32 * i + 32]
        qb_ref[0, 32 * i:32 * i + 32, :] = (_rms_rows(q, gain("bq")) * sb).astype(BF16)
    kpos = pl.program_id(1) * tm + lax.broadcasted_iota(jnp.int32, (BF16_ROWS, tm), 1)
    frow = lax.broadcasted_iota(jnp.int32, (BF16_ROWS, tm), 0)
    pos_hi = ((kpos >> POS_SPLIT_BITS) << POS_SPLIT_BITS).astype(F32)
    pos_lo = (kpos & ((1 << POS_SPLIT_BITS) - 1)).astype(F32)
    kfeat = jnp.where(frow < 3, pos_hi, jnp.where(frow < 6, pos_lo, 0.0))
    zeros48 = jnp.zeros((48, tm), F32)
    for hh in range(N_HEADS):
        k1 = _rms_rows(proj[O_BK + 64 * hh:O_BK + 64 * hh + 32], gain("bk"))
        k2 = _rms_rows(proj[O_BK + 64 * hh + 32:O_BK + 64 * hh + 64], gain("bk"))
        kpad = jnp.concatenate([k1, k2, kfeat, zeros48], axis=0)
        kb_ref[0, :, LANES * hh:LANES * (hh + 1)] = kpad.T.astype(BF16)
    _store_key_blocks(vb_ref, proj[O_BV:O_BV + 256].astype(BF16))

    sc = C_QK ** -0.5 * LOG2E
    cq = _rms_rows(proj[O_CQL:O_CQL + C_Q_LORA], gain("cqa")).astype(BF16)
    qc = jnp.dot(wqb_ref[...], cq, preferred_element_type=F32)
    ckv = _rms_rows(proj[O_CKVL:O_CKVL + C_KV_LORA], gain("ckva")).astype(BF16)
    kvc = jnp.dot(wkvb_ref[...], ckv, preferred_element_type=F32)
    kr = proj[O_CKR:O_CKR + C_ROPE]
    vcs = []
    for hh in range(N_HEADS):
        q = _rms_rows(qc[C_QK * hh:C_QK * (hh + 1)], gain("cqn"))
        q = jnp.concatenate([q[:C_NOPE], _rope_rows(q[C_NOPE:], cosm, sinm, 16)], axis=0) * sc
        qc_ref[0, C_QK * hh:C_QK * (hh + 1), :] = q.astype(BF16)
        kk = jnp.concatenate([kvc[128 * hh:128 * hh + C_NOPE], kr], axis=0)
        kk = _rms_rows(kk, gain("ckn"))
        kk = jnp.concatenate([kk[:C_NOPE], _rope_rows(kk[C_NOPE:], cosm, sinm, 16), zeros32], axis=0)
        kc_ref[0, :, LANES * hh:LANES * (hh + 1)] = kk.T.astype(BF16)
        vcs.append(kvc[128 * hh + C_NOPE:128 * (hh + 1)])
    _store_key_blocks(vc_ref, jnp.concatenate(vcs, axis=0).astype(BF16))

    sd = HEAD_DIM ** -0.5 * LOG2E
    qs, ks, vs = [], [], []
    for g in range(2):
        qg, kg = [], []
        for hh in (2 * g, 2 * g + 1):
            qg.append(_rms_rows(proj[O_DQ + 64 * hh:O_DQ + 64 * hh + 64], gain("dq")) * sd)
            kg.append(_rms_rows(proj[O_DK + 64 * hh:O_DK + 64 * hh + 64], gain("dk")))
        qs.append(jnp.concatenate(qg, axis=0).T)
        ks.append(jnp.concatenate(kg, axis=0).T)
        vs.append(proj[O_DV + 128 * g:O_DV + 128 * (g + 1)].T)
    _store_dilated(scr_ref, jnp.concatenate(qs, axis=1), (dq1_ref, dq4_ref, dq16_ref))
    _store_dilated(scr_ref, jnp.concatenate(ks, axis=1), (dk1_ref, dk4_ref, dk16_ref))
    _store_dilated(scr_ref, jnp.concatenate(vs, axis=1), (dv1_ref, dv4_ref, dv16_ref))


def _project(xt, w_in_t, wqb_t, wkvb_t, gains, cosa, sina, cosm, sinm):
    b, d, s = xt.shape
    tm = TOK_TILE
    nt = s // tm
    fm = lambda rows: jax.ShapeDtypeStruct((b, rows, s), BF16)
    tmj = lambda cols: jax.ShapeDtypeStruct((b, s, cols), BF16)
    vblk = lambda rows: jax.ShapeDtypeStruct((b, s // KSTEP, rows, KSTEP), BF16)
    dviews = tuple(jax.ShapeDtypeStruct((b, s // dil, dil * D_WIDTH), BF16) for _, dil in DILATED_CONFIGS)
    out_shape = (fm(256), tmj(256), vblk(128),
                 fm(256), tmj(512), vblk(256),
                 fm(384), tmj(512), vblk(256)) + dviews * 3
    fm_spec = lambda rows: pl.BlockSpec((1, rows, tm), lambda i, j: (i, 0, j))
    tm_spec = lambda cols: pl.BlockSpec((1, tm, cols), lambda i, j: (i, j, 0))
    v_spec = lambda rows: pl.BlockSpec((1, tm // KSTEP, rows, KSTEP), lambda i, j: (i, j, 0, 0))
    dspecs = tuple(pl.BlockSpec((1, tm // dil, dil * D_WIDTH), lambda i, j: (i, j, 0)) for _, dil in DILATED_CONFIGS)
    out_specs = (fm_spec(256), tm_spec(256), v_spec(128),
                 fm_spec(256), tm_spec(512), v_spec(256),
                 fm_spec(384), tm_spec(512), v_spec(256)) + dspecs * 3
    whole = lambda i, j: (0, 0)
    in_specs = [
        pl.BlockSpec((1, d, tm), lambda i, j: (i, 0, j)),
        pl.BlockSpec((IN_WIDTH, d), whole),
        pl.BlockSpec((N_HEADS * C_QK, C_Q_LORA), whole),
        pl.BlockSpec((N_HEADS * 128, C_KV_LORA), whole),
        pl.BlockSpec((G_TOTAL, 1), whole),
        pl.BlockSpec((64, tm), lambda i, j: (0, j)),
        pl.BlockSpec((64, tm), lambda i, j: (0, j)),
        pl.BlockSpec((32, tm), lambda i, j: (0, j)),
        pl.BlockSpec((32, tm), lambda i, j: (0, j)),
    ]
    return pl.pallas_call(
        _proj_body,
        out_shape=out_shape,
        grid=(b, nt),
        in_specs=in_specs,
        out_specs=out_specs,
        scratch_shapes=[pltpu.VMEM((D_WIDTH // LANES, tm, LANES), F32)],
        compiler_params=_cparams(("parallel", "arbitrary")),
        name="project_heads",
    )(xt, w_in_t, wqb_t, wkvb_t, gains, cosa, sina, cosm, sinm)


def _online_update(s, v_aug, carry, shift=None):
    m, acc = carry
    smax = jnp.max(s, axis=0, keepdims=True)
    if shift is None:
        m_new = jnp.maximum(m, smax)
        p = jnp.exp2(s - m_new).astype(BF16)
    else:
        m_new = jnp.maximum(m, smax + shift)
        p = jnp.exp2(s - (m_new - shift)).astype(BF16)
    alpha = jnp.exp2(m - m_new)
    acc = alpha * acc + jnp.dot(v_aug, p, preferred_element_type=F32)
    return m_new, acc


def _init_carry(dv, tq):
    return (jnp.full((1, tq), -jnp.inf, F32), jnp.zeros((dv + BF16_ROWS, tq), F32))


def _finish(carry, dv):
    _, acc = carry
    return acc[:dv] / acc[dv:dv + 1]


def _dense_body(q_ref, k_ref, v_ref, o_ref):
    dq, tq = q_ref.shape[1], q_ref.shape[2]
    nkb, dv, ks = v_ref.shape[1], v_ref.shape[2], v_ref.shape[3]
    q = q_ref[0]
    qpad = jnp.concatenate([q, jnp.zeros((LANES - dq, tq), BF16)], axis=0)
    ones = jnp.ones((BF16_ROWS, ks), BF16)

    def scores(i):
        return jnp.dot(k_ref[0, i * ks:(i + 1) * ks, :], qpad, preferred_element_type=F32)

    carry = _init_carry(dv, tq)
    ahead = SCORES_AHEAD
    pending = [scores(i) for i in range(min(ahead, nkb))]
    for i in range(nkb):
        s = pending.pop(0)
        if i + ahead < nkb:
            pending.append(scores(i + ahead))
        carry = _online_update(s, jnp.concatenate([v_ref[0, i], ones], axis=0), carry)
    o_ref[0] = _finish(carry, dv).astype(o_ref.dtype)


def _dense_attention(q, k, v, dq, q_per_kv):
    b, rows, s = q.shape
    nh = rows // dq
    nkb, tk = v.shape[1], v.shape[3]
    return pl.pallas_call(
        _dense_body,
        out_shape=jax.ShapeDtypeStruct((b, nh * HEAD_DIM, s), BF16),
        grid=(b, nh, s // TQ),
        in_specs=[
            pl.BlockSpec((1, dq, TQ), lambda i, h, j: (i, h, j)),
            pl.BlockSpec((1, s, LANES), lambda i, h, j: (i, 0, h // q_per_kv)),
            pl.BlockSpec((1, nkb, HEAD_DIM, tk), lambda i, h, j: (i, 0, h // q_per_kv, 0)),
        ],
        out_specs=pl.BlockSpec((1, HEAD_DIM, TQ), lambda i, h, j: (i, h, j)),
        compiler_params=_cparams(("parallel", "arbitrary", "arbitrary")),
        name="dense_attention",
    )(q, k, v)


def _diff_body(sc_ref, q_ref, k_ref, v_ref, g_ref, dist_ref, o_ref):
    tq = q_ref.shape[2]
    nkb, dv, ks = v_ref.shape[1], v_ref.shape[2], v_ref.shape[3]
    hh = pl.program_id(1)
    jq = pl.program_id(2)
    lam = sc_ref[0]
    out_scale = sc_ref[1]
    slope = sc_ref[2 + hh]
    q = q_ref[0]
    z32 = jnp.zeros((B_QK_DIM, tq), BF16)
    z48 = jnp.zeros((48, tq), BF16)
    zfeat = jnp.zeros((BF16_ROWS, tq), BF16)
    ones = jnp.ones((BF16_ROWS, ks), BF16)
    frow = lax.broadcasted_iota(jnp.int32, (BF16_ROWS, tq), 0)
    piece = jnp.where(frow % 3 == 0, LOG2E_PIECES[0], jnp.where(frow % 3 == 1, LOG2E_PIECES[1], LOG2E_PIECES[2]))
    cfeat = jnp.where(frow < 6, piece, 0.0)
    qpos = (jq * tq + lax.broadcasted_iota(jnp.int32, (1, tq), 1)).astype(F32)

    n_mixed = tq // ks
    first = jq * n_mixed

    def block_index(ii):
        return lax.rem(first + ii, nkb)

    def side(ii):
        return jnp.where(block_index(ii) < first, 1.0, -1.0).astype(F32)

    def scores(ii):
        i = block_index(ii)
        k = k_ref[0, pl.ds(pl.multiple_of(i * ks, ks), ks), :]
        feat = zfeat if ii < n_mixed else (cfeat * (side(ii) * slope)).astype(BF16)
        q1 = jnp.concatenate([q[:B_QK_DIM], z32, feat, z48], axis=0)
        q2 = jnp.concatenate([z32, q[B_QK_DIM:], feat, z48], axis=0)
        return (jnp.dot(k, q1, preferred_element_type=F32), jnp.dot(k, q2, preferred_element_type=F32))

    c1, c2 = _init_carry(dv, tq), _init_carry(dv, tq)
    s_next = scores(0)
    for ii in range(nkb):
        s1, s2 = s_next
        if ii + 1 < nkb:
            s_next = scores(ii + 1)
        v_aug = jnp.concatenate([v_ref[0, block_index(ii)], ones], axis=0)
        if ii < n_mixed:
            bias = dist_ref[ii] * (-(slope * LOG2E))
            c1 = _online_update(s1 + bias, v_aug, c1)
            c2 = _online_update(s2 + bias, v_aug, c2)
        else:
            shift = qpos * (-(side(ii) * slope * LOG2E))
            c1 = _online_update(s1, v_aug, c1, shift)
            c2 = _online_update(s2, v_aug, c2, shift)
    o = _finish(c1, dv) - lam * _finish(c2, dv)
    o_ref[0] = (_rms_rows(o, g_ref[...]) * out_scale).astype(o_ref.dtype)


def _diff_attention(scal, q, k, v, sub_g):
    b, rows, s = q.shape
    nkb, tk = v.shape[1], v.shape[3]
    n_mixed = TQ // tk
    jj = jnp.arange(tk)[None, :, None] + tk * jnp.arange(n_mixed)[:, None, None]
    dist = jnp.abs(jj - jnp.arange(TQ)[None, None, :]).astype(F32)
    return pl.pallas_call(
        _diff_body,
        out_shape=jax.ShapeDtypeStruct((b, rows, s), BF16),
        grid=(b, N_HEADS, s // TQ),
        in_specs=[
            pl.BlockSpec(memory_space=pltpu.SMEM),
            pl.BlockSpec((1, 2 * B_QK_DIM, TQ), lambda i, h, j: (i, h, j)),
            pl.BlockSpec((1, s, LANES), lambda i, h, j: (i, 0, h)),
            pl.BlockSpec((1, nkb, HEAD_DIM, tk), lambda i, h, j: (i, 0, h, 0)),
            pl.BlockSpec((HEAD_DIM, 1), lambda i, h, j: (0, 0)),
            pl.BlockSpec((n_mixed, tk, TQ), lambda i, h, j: (0, 0, 0)),
        ],
        out_specs=pl.BlockSpec((1, HEAD_DIM, TQ), lambda i, h, j: (i, h, j)),
        compiler_params=_cparams(("parallel", "arbitrary", "arbitrary")),
        name="diff_attention",
    )(scal, q, k, v, sub_g, dist)


def _band_body(q_ref, k_ref, v_ref, o_ref, lse_ref, *, step_len, slopes, qb):
    l_len = k_ref.shape[1]
    win = min(qb + 2 * BAND_RADIUS, l_len)
    lane = lax.broadcasted_iota(jnp.int32, (qb, LANES), 1)
    row = lax.broadcasted_iota(jnp.int32, (qb, win), 0)
    colw = lax.broadcasted_iota(jnp.int32, (qb, win), 1)
    for sb in range(q_ref.shape[1] // qb):
        jb = pl.program_id(2) * (q_ref.shape[1] // qb) + sb
        start = jnp.clip(jb * qb - BAND_RADIUS, 0, l_len - win)
        start = pl.multiple_of(start, BAND_RADIUS)
        q = q_ref[0, sb * qb:(sb + 1) * qb, :]
        kwin = k_ref[0, pl.ds(start, win), :]
        vwin = v_ref[0, pl.ds(start, win), :]
        rel_i = jnp.abs((jb * qb - start) + row - colw)
        allowed = rel_i <= BAND_RADIUS
        rel = rel_i.astype(F32)
        for g in range(2):
            q2 = q[:, LANES * g:LANES * (g + 1)].astype(F32)
            k2 = kwin[:, LANES * g:LANES * (g + 1)]
            v2 = vwin[:, LANES * g:LANES * (g + 1)]
            out_g = jnp.zeros((qb, LANES), F32)
            lse_g = jnp.zeros((qb, LANES), F32)
            for hh in range(2):
                own = (lane >= HEAD_DIM * hh) & (lane < HEAD_DIM * (hh + 1))
                qm = jnp.where(own, q2, 0.0).astype(BF16)
                s = lax.dot_general(qm, k2, (((1,), (1,)), ((), ())), preferred_element_type=F32)
                s = s - (slopes[2 * g + hh] * step_len * LOG2E) * rel
                s = jnp.where(allowed, s, -jnp.inf)
                m = jnp.max(s, axis=1, keepdims=True)
                p = jnp.exp2(s - m)
                lsum = jnp.sum(p, axis=1, keepdims=True)
                o = jnp.dot(p.astype(BF16), v2, preferred_element_type=F32) / lsum
                lse = m + jnp.log2(lsum)
                out_g = jnp.where(own, o, out_g)
                lse_g = jnp.where(own, lse, lse_g)
            o_ref[0, sb * qb:(sb + 1) * qb, LANES * g:LANES * (g + 1)] = out_g
            lse_ref[0, sb * qb:(sb + 1) * qb, LANES * g:LANES * (g + 1)] = lse_g


def _band_attention(q, k, v, dil, slopes):
    b, l_len, cw = q.shape
    c = cw // dil
    qb = min(BAND_Q, l_len)
    qstep = min(BAND_QSTEP, l_len)
    body = functools.partial(_band_body, step_len=float(dil), slopes=slopes, qb=qb)
    return pl.pallas_call(
        body,
        out_shape=(jax.ShapeDtypeStruct((b, l_len, cw), F32),) * 2,
        grid=(b, dil, l_len // qstep),
        in_specs=[
            pl.BlockSpec((1, qstep, c), lambda i, r, j: (i, j, r)),
            pl.BlockSpec((1, l_len, c), lambda i, r, j: (i, 0, r)),
            pl.BlockSpec((1, l_len, c), lambda i, r, j: (i, 0, r)),
        ],
        out_specs=(pl.BlockSpec((1, qstep, c), lambda i, r, j: (i, j, r)),) * 2,
        compiler_params=_cparams(("parallel", "arbitrary", "arbitrary")),
        name="band_attention_d%d" % dil,
    )(q, k, v)


def _band_combine_body(*refs):
    n = len(DILATED_CONFIGS)
    in_refs, o_ref, scr_ref = refs[:2 * n], refs[2 * n], refs[2 * n + 1]
    tm, c = o_ref.shape[1], o_ref.shape[2]
    vals = []
    for idx, (_, dil) in enumerate(DILATED_CONFIGS):
        for which in range(2):
            src = in_refs[2 * idx + which]
            if dil == 1:
                vals.append(src[0])
                continue
            groups = []
            for g in range(c // LANES):
                slot = scr_ref.at[(2 * idx + which) * (c // LANES) + g]
                for r in range(dil):
                    lo = c * r + LANES * g
                    slot[pl.ds(r, tm // dil, stride=dil), :] = src[0, :, lo:lo + LANES]
                groups.append(slot[...])
            vals.append(jnp.concatenate(groups, axis=1))
    outs, lses = vals[0::2], vals[1::2]
    m = functools.reduce(jnp.maximum, lses)
    ws = [jnp.exp2(x - m) for x in lses]
    num = functools.reduce(lambda a, t: a + t, [w * o for w, o in zip(ws, outs)])
    den = functools.reduce(lambda a, t: a + t, ws)
    o_ref[0] = (num / den).astype(o_ref.dtype)


def _band_combine(pairs):
    b, s, c = pairs[0][0].shape
    tm = TOK_TILE
    specs = []
    for (_, dil) in DILATED_CONFIGS:
        specs += [pl.BlockSpec((1, tm // dil, dil * c), lambda i, j: (i, j, 0))] * 2
    flat = [a for p in pairs for a in p]
    return pl.pallas_call(
        _band_combine_body,
        out_shape=jax.ShapeDtypeStruct((b, s, c), BF16),
        grid=(b, s // tm),
        in_specs=specs,
        out_specs=pl.BlockSpec((1, tm, c), lambda i, j: (i, j, 0)),
        scratch_shapes=[pltpu.VMEM((2 * len(DILATED_CONFIGS) * (c // LANES), tm, LANES), F32)],
        compiler_params=_cparams(("parallel", "arbitrary")),
        name="band_combine",
    )(*flat)


def _outproj_body(x_ref, oa_ref, ob_ref, oc_ref, od_ref, w_ref, g_ref, x1_ref, h2_ref, edge_ref):
    d, tm = x_ref.shape[1], x_ref.shape[2]
    j = pl.program_id(1)
    mix = jnp.concatenate([oa_ref[0], ob_ref[0], oc_ref[0]], axis=0)
    w = w_ref[...]
    y = jnp.dot(w[:, :768], mix, preferred_element_type=F32)
    y = y + lax.dot_general(w[:, 768:], od_ref[0], (((1,), (1,)), ((), ())), preferred_element_type=F32)
    x1 = x_ref[0] + y
    h2 = _rms_rows(x1, g_ref[...])
    x1_ref[0] = x1.T
    h2t = h2.T
    h2_ref[0] = h2t.astype(BF16)
    edge_ref[0, pl.ds(2 * j, 1), :] = h2t[0:1].astype(BF16).astype(F32)
    edge_ref[0, pl.ds(2 * j + 1, 1), :] = h2t[tm - 1:tm].astype(BF16).astype(F32)


def _out_project(xt, oa, ob, oc, od, w_out_t, g2):
    b, d, s = xt.shape
    tm = TOK_TILE
    nt = s // tm
    fm = lambda rows: pl.BlockSpec((1, rows, tm), lambda i, j: (i, 0, j))
    tmj = pl.BlockSpec((1, tm, d), lambda i, j: (i, j, 0))
    return pl.pallas_call(
        _outproj_body,
        out_shape=(jax.ShapeDtypeStruct((b, s, d), F32), jax.ShapeDtypeStruct((b, s, d), BF16),
                   jax.ShapeDtypeStruct((b, 2 * nt, d), F32)),
        grid=(b, nt),
        in_specs=[fm(d), fm(256), fm(256), fm(256),
                  pl.BlockSpec((1, tm, D_WIDTH), lambda i, j: (i, j, 0)),
                  pl.BlockSpec((d, d), lambda i, j: (0, 0)),
                  pl.BlockSpec((d, 1), lambda i, j: (0, 0))],
        out_specs=(tmj, tmj, pl.BlockSpec((1, 2 * nt, d), lambda i, j: (i, 0, 0))),
        compiler_params=_cparams(("parallel", "arbitrary")),
        name="out_project",
    )(xt, oa, ob, oc, od, w_out_t, g2)


def _halo_body(h_ref, w_ref, o_ref):
    o_ref[...] = jnp.dot(h_ref[...], w_ref[...], preferred_element_type=F32)


def _halo_up(halo_h, w_up):
    nh, d = halo_h.shape
    ff2 = w_up.shape[1]
    n_blk = 4
    cols = ff2 // n_blk
    return pl.pallas_call(
        _halo_body,
        out_shape=jax.ShapeDtypeStruct((nh, ff2), F32),
        grid=(n_blk,),
        in_specs=[pl.BlockSpec((nh, d), lambda i: (0, 0)),
                  pl.BlockSpec((d, cols), lambda i: (0, i))],
        out_specs=pl.BlockSpec((nh, cols), lambda i: (0, i)),
        compiler_params=_cparams(("arbitrary",)),
        name="mlp_halo_up",
    )(halo_h, w_up)


def _mlp_body(x1_ref, h2_ref, wup_ref, wdn_ref, par_ref, halo_ref, o_ref, gbuf_ref, vbuf_ref, *, chunks):
    tf = h2_ref.shape[1]
    ff = wdn_ref.shape[0]
    nt = pl.num_programs(1)
    row = (pl.program_id(0) * nt + pl.program_id(1)) * 2
    h = h2_ref[0]

    def conv(buf_ref, u, c0, width):
        buf_ref[MLP_PAD:MLP_PAD + tf, :width] = u
        buf_ref[MLP_PAD - 1:MLP_PAD, :width] = halo_ref[pl.ds(row, 1), c0:c0 + width]
        buf_ref[MLP_PAD + tf:MLP_PAD + tf + 1, :width] = halo_ref[pl.ds(row + 1, 1), c0:c0 + width]
        prev = buf_ref[MLP_PAD - 1:MLP_PAD - 1 + tf, :width]
        nxt = buf_ref[MLP_PAD + 1:MLP_PAD + 1 + tf, :width]
        p = par_ref[:, c0:c0 + width]
        return p[0:1] * prev + p[1:2] * u + p[2:3] * nxt + p[3:4]

    y = x1_ref[0]
    for c0, width in chunks:
        ug = jnp.dot(h, wup_ref[:, c0:c0 + width], preferred_element_type=F32)
        uv = jnp.dot(h, wup_ref[:, ff + c0:ff + c0 + width], preferred_element_type=F32)
        gate = conv(gbuf_ref, ug, c0, width)
        val = conv(vbuf_ref, uv, ff + c0, width)
        act = gate / (1.0 + jnp.exp(-gate)) * val
        y = y + jnp.dot(act.astype(BF16), wdn_ref[c0:c0 + width, :], preferred_element_type=F32)
    o_ref[0] = y.T


def _mlp(x1, h2, w_up, w_down, conv_par, halo_up, chunks):
    b, s, d = x1.shape
    tf = TOK_TILE
    ff2 = w_up.shape[1]
    nh = halo_up.shape[0]
    whole = lambda i, j: (0, 0)
    body = functools.partial(_mlp_body, chunks=chunks)
    const = dict(pipeline_mode=pl.Buffered(1))
    wmax = max(w for _, w in chunks)
    return pl.pallas_call(
        body,
        out_shape=jax.ShapeDtypeStruct((b, d, s), F32),
        grid=(b, s // tf),
        in_specs=[pl.BlockSpec((1, tf, d), lambda i, j: (i, j, 0)),
                  pl.BlockSpec((1, tf, d), lambda i, j: (i, j, 0)),
                  pl.BlockSpec((d, ff2), whole, **const),
                  pl.BlockSpec((ff2 // 2, d), whole, **const),
                  pl.BlockSpec((8, ff2), whole, **const),
                  pl.BlockSpec((nh, ff2), whole, **const)],
        out_specs=pl.BlockSpec((1, d, tf), lambda i, j: (i, 0, j)),
        scratch_shapes=[pltpu.VMEM((tf + 2 * MLP_PAD, wmax), F32)] * 2,
        compiler_params=_cparams(("arbitrary", "arbitrary")),
        name="gated_conv_mlp",
    )(x1, h2, w_up, w_down, conv_par, halo_up)


def _halo_rows(edges, nt):
    b, _, d = edges.shape
    ez = jnp.concatenate([edges, jnp.zeros((b, 1, d), edges.dtype)], axis=1)
    idx = []
    for j in range(nt):
        idx += [2 * j - 1 if j > 0 else 2 * nt, 2 * j + 2 if j < nt - 1 else 2 * nt]
    rows = jnp.take(ez, jnp.asarray(idx, jnp.int32), axis=1).reshape(b * nt * 2, d)
    pad = (-rows.shape[0]) % BF16_ROWS
    return jnp.pad(rows, ((0, pad), (0, 0))).astype(BF16)


def _rope_tables(s):
    def freqs(dim):
        return 1.0 / (ROPE_THETA ** (jnp.arange(0, dim, 2, dtype=F32) / dim))
    t = jnp.arange(s)
    rows = (t // GRID_W).astype(F32)
    cols = (t % GRID_W).astype(F32)
    pos = t.astype(F32)
    fa = freqs(HEAD_DIM // 2)
    ar, ac = fa[:, None] * rows[None, :], fa[:, None] * cols[None, :]
    am = freqs(C_ROPE)[:, None] * pos[None, :]
    cosa = jnp.concatenate([jnp.cos(ar), jnp.cos(ar), jnp.cos(ac), jnp.cos(ac)], axis=0)
    sina = jnp.concatenate([-jnp.sin(ar), jnp.sin(ar), -jnp.sin(ac), jnp.sin(ac)], axis=0)
    cosm = jnp.concatenate([jnp.cos(am), jnp.cos(am)], axis=0)
    sinm = jnp.concatenate([-jnp.sin(am), jnp.sin(am)], axis=0)
    return cosa, sina, cosm, sinm


def kernel(x, norm1_g, w_in, a_qn_g, a_kn_g, b_qn_g, b_kn_g, b_lam_q1, b_lam_k1, b_lam_q2, b_lam_k2, b_sub_g,
           c_qa_g, c_kva_g, c_wqb, c_wkvb, c_qn_g, c_kn_g, d_qn_g, d_kn_g, w_out, norm2_g, w_up, conv_w, conv_b,
           w_down):
    b, s, d = x.shape
    depth = w_in.shape[0]
    ff = w_down.shape[1]
    nt = s // TOK_TILE
    assert s % TOK_TILE == 0 and TOK_TILE % KSTEP == 0 and TQ % KSTEP == 0 and s % TQ == 0 and s % GRID_W == 0
    assert s <= 1 << (POS_SPLIT_BITS + BF16_MANTISSA_BITS)
    for window, dil in DILATED_CONFIGS:
        assert window // (2 * dil) == BAND_RADIUS and (s // dil) % min(BAND_Q, s // dil) == 0
        assert (s // dil) % min(BAND_QSTEP, s // dil) == 0 and BAND_QSTEP % BAND_Q == 0
        assert TOK_TILE % (dil * BF16_ROWS) == 0

    tr = lambda w: jnp.swapaxes(w, 1, 2).astype(BF16)
    gains = jnp.concatenate([norm1_g, a_qn_g, a_kn_g, b_qn_g, b_kn_g, c_qa_g, c_kva_g, c_qn_g, c_kn_g,
                             d_qn_g, d_kn_g], axis=1)[:, :, None]
    conv_par = jnp.concatenate([conv_w, conv_b[:, None, :], jnp.zeros((depth, 4, 2 * ff), F32)], axis=1)
    slopes = [2.0 ** (-8.0 * i / (2 * N_HEADS)) for i in range(1, 2 * N_HEADS + 1)]
    slopes_b, slopes_d = slopes[:N_HEADS], tuple(slopes[N_HEADS:])
    lam_init = jnp.asarray([0.8 - 0.6 * math.exp(-0.3 * l) for l in range(depth)], F32)
    lam = (jnp.exp(jnp.sum(b_lam_q1 * b_lam_k1, axis=1)) - jnp.exp(jnp.sum(b_lam_q2 * b_lam_k2, axis=1)) + lam_init)
    scal = jnp.concatenate([lam[:, None], (1.0 - lam_init)[:, None],
                            jnp.broadcast_to(jnp.asarray(slopes_b, F32), (depth, N_HEADS)),
                            jnp.zeros((depth, 2), F32)], axis=1)
    layers = dict(w_in=tr(w_in), wqb=tr(c_wqb), wkvb=tr(c_wkvb), w_out=tr(w_out),
                  w_up=w_up.astype(BF16), w_down=w_down.astype(BF16),
                  gains=gains, g2=norm2_g[:, :, None], sub_g=b_sub_g[:, :, None], conv_par=conv_par, scal=scal)
    cosa, sina, cosm, sinm = _rope_tables(s)
    assert ff % LANES == 0
    chunks = tuple((c0, min(MLP_CHUNK, ff - c0)) for c0 in range(0, ff, MLP_CHUNK))

    def layer(xt, p):
        (qa, ka, va, qb, kb, vb, qc, kc, vc, dq1, dq4, dq16, dk1, dk4, dk16, dv1, dv4, dv16) = _project(
            xt, p["w_in"], p["wqb"], p["wkvb"], p["gains"], cosa, sina, cosm, sinm)
        oa = _dense_attention(qa, ka, va, HEAD_DIM, N_HEADS // A_KV_HEADS)
        ob = _diff_attention(p["scal"], qb, kb, vb, p["sub_g"])
        oc = _dense_attention(qc, kc, vc, C_QK, 1)
        branches = [_band_attention(q_, k_, v_, dil, slopes_d)
                    for (_, dil), q_, k_, v_ in zip(DILATED_CONFIGS, (dq1, dq4, dq16), (dk1, dk4, dk16), (dv1, dv4, dv16))]
        od = _band_combine(branches)
        x1, h2, edges = _out_project(xt, oa, ob, oc, od, p["w_out"], p["g2"])
        halo = _halo_up(_halo_rows(edges, nt), p["w_up"])
        return _mlp(x1, h2, p["w_up"], p["w_down"], p["conv_par"], halo, chunks), None

    xt = _transpose_last2(x, TOK_TILE, d)
    xt, _ = lax.scan(layer, xt, layers)
    return _transpose_last2(xt, d, TOK_TILE)
```

```python
import functools
import math

import jax
import jax.numpy as jnp
import numpy as np
from jax import lax
from jax.experimental import pallas as pl
from jax.experimental.pallas import tpu as pltpu

F32 = jnp.float32
BF16 = jnp.bfloat16

HEAD_DIM = 64
N_HEADS = 4
A_KV_HEADS = 2
B_QK_DIM = 32
C_Q_LORA = 256
C_KV_LORA = 128
C_NOPE = 64
C_ROPE = 32
C_QK = C_NOPE + C_ROPE
DILATED_CONFIGS = ((128, 1), (512, 4), (2048, 16))
D_WIDTH = N_HEADS * HEAD_DIM
GRID_W = 64
ROPE_THETA = 10000.0
NORM_EPS = 1e-6
LOG2E = 1.4426950408889634


def _bf16_pieces(x, n):
    out = []
    for _ in range(n):
        u = np.float32(x).view(np.uint32)
        u = np.uint32((int(u) + 0x7FFF + ((int(u) >> 16) & 1)) & 0xFFFF0000)
        piece = float(u.view(np.float32))
        out.append(piece)
        x = x - piece
    return tuple(out)


LOG2E_PIECES = _bf16_pieces(LOG2E, 3)
POS_SPLIT_BITS = 6
BF16_MANTISSA_BITS = 8

_SPLITS = (256, 128, 128, 256, 256, 256, C_Q_LORA, C_KV_LORA, C_ROPE, 256, 256, 256)
_OFF = [0]
for _n in _SPLITS:
    _OFF.append(_OFF[-1] + _n)
(O_AQ, O_AK, O_AV, O_BQ, O_BK, O_BV, O_CQL, O_CKVL, O_CKR, O_DQ, O_DK, O_DV, IN_WIDTH) = _OFF

_GAIN_SIZES = (("n1", 1024), ("aq", 64), ("ak", 64), ("bq", 32), ("bk", 32), ("cqa", 256), ("ckva", 128),
               ("cqn", 96), ("ckn", 96), ("dq", 64), ("dk", 64))
G_OFF = {}
_o = 0
for _k, _n in _GAIN_SIZES:
    G_OFF[_k] = (_o, _n)
    _o += _n
G_TOTAL = _o

LANES = 128
BF16_ROWS = 16
TOK_TILE = 512
MLP_CHUNK = 1024
MLP_PAD = 8
TQ = 512
TQ_STEP = 1024
KSTEP = 256
SCORES_AHEAD = 2
BAND_Q = 128
BAND_QSTEP = 512
BAND_RADIUS = 64
VMEM_LIMIT = 56 * 1024 * 1024


def _cparams(sem, flags=None):
    return pltpu.CompilerParams(dimension_semantics=sem, vmem_limit_bytes=VMEM_LIMIT, flags=flags)


def _rms_rows(v, g_col):
    ms = jnp.mean(v * v, axis=0, keepdims=True)
    return v * lax.rsqrt(ms + NORM_EPS) * g_col


def _swap_halves(x, n):
    parts = []
    for i in range(0, x.shape[0], 2 * n):
        parts.append(x[i + n:i + 2 * n])
        parts.append(x[i:i + n])
    return jnp.concatenate(parts, axis=0)


def _rope_rows(x, cos, sin_signed, n):
    return x * cos + _swap_halves(x, n) * sin_signed


def _transpose_body(x_ref, o_ref):
    o_ref[0] = x_ref[0].T


def _transpose_last2(x, tr, tc):
    b, r, c = x.shape
    return pl.pallas_call(
        _transpose_body,
        out_shape=jax.ShapeDtypeStruct((b, c, r), x.dtype),
        grid=(b, r // tr, c // tc),
        in_specs=[pl.BlockSpec((1, tr, tc), lambda i, j, k: (i, j, k))],
        out_specs=pl.BlockSpec((1, tc, tr), lambda i, j, k: (i, k, j)),
        compiler_params=_cparams(("parallel", "arbitrary", "arbitrary")),
        name="transpose_tokens",
    )(x)


def _store_dilated(scr_ref, val, out_refs):
    tm, c = val.shape
    for g in range(c // LANES):
        scr_ref[g] = val[:, LANES * g:LANES * (g + 1)]
    for (_, dil), out in zip(DILATED_CONFIGS, out_refs):
        if dil == 1:
            out[0] = val.astype(BF16)
            continue
        n = tm // dil
        for r in range(dil):
            for g in range(c // LANES):
                lo = c * r + LANES * g
                out[0, :, lo:lo + LANES] = scr_ref[g, pl.ds(r, n, stride=dil), :].astype(BF16)


def _store_key_blocks(v_ref, v):
    for c in range(v.shape[1] // KSTEP):
        v_ref[0, c] = v[:, c * KSTEP:(c + 1) * KSTEP]


def _proj_body(x_ref, w_ref, wqb_ref, wkvb_ref, g_ref, cosa_ref, sina_ref, cosm_ref, sinm_ref,
               qa_ref, ka_ref, va_ref, qb_ref, kb_ref, vb_ref, qc_ref, kc_ref, vc_ref,
               dq1_ref, dq4_ref, dq16_ref, dk1_ref, dk4_ref, dk16_ref, dv1_ref, dv4_ref, dv16_ref,
               scr_ref):
    tm = x_ref.shape[2]

    def gain(name):
        o, n = G_OFF[name]
        return g_ref[o:o + n, :]

    x = x_ref[0]
    h = _rms_rows(x, gain("n1")).astype(BF16)
    proj = jnp.dot(w_ref[...], h, preferred_element_type=F32)

    cosa, sina = cosa_ref[...], sina_ref[...]
    cosm, sinm = cosm_ref[...], sinm_ref[...]
    zeros64 = jnp.zeros((64, tm), F32)
    zeros32 = jnp.zeros((32, tm), F32)

    sa = HEAD_DIM ** -0.5 * LOG2E
    for hh in range(N_HEADS):
        q = proj[O_AQ + 64 * hh:O_AQ + 64 * hh + 64]
        q = _rope_rows(_rms_rows(q, gain("aq")), cosa, sina, 16) * sa
        qa_ref[0, 64 * hh:64 * hh + 64, :] = q.astype(BF16)
    for g in range(A_KV_HEADS):
        k = proj[O_AK + 64 * g:O_AK + 64 * g + 64]
        k = _rope_rows(_rms_rows(k, gain("ak")), cosa, sina, 16)
        kpad = jnp.concatenate([k, zeros64], axis=0)
        ka_ref[0, :, LANES * g:LANES * (g + 1)] = kpad.T.astype(BF16)
    _store_key_blocks(va_ref, proj[O_AV:O_AV + 128].astype(BF16))

    sb = B_QK_DIM ** -0.5 * LOG2E
    for i in range(2 * N_HEADS):
        q = proj[O_BQ + 32 * i:O_BQ + 32 * i + 32]
        qb_ref[0, 32 * i:32 * i + 32, :] = (_rms_rows(q, gain("bq")) * sb).astype(BF16)
    kpos = pl.program_id(1) * tm + lax.broadcasted_iota(jnp.int32, (BF16_ROWS, tm), 1)
    frow = lax.broadcasted_iota(jnp.int32, (BF16_ROWS, tm), 0)
    pos_hi = ((kpos >> POS_SPLIT_BITS) << POS_SPLIT_BITS).astype(F32)
    pos_lo = (kpos & ((1 << POS_SPLIT_BITS) - 1)).astype(F32)
    kfeat = jnp.where(frow < 3, pos_hi, jnp.where(frow < 6, pos_lo, 0.0))
    zeros48 = jnp.zeros((48, tm), F32)
    for hh in range(N_HEADS):
        k1 = _rms_rows(proj[O_BK + 64 * hh:O_BK + 64 * hh + 32], gain("bk"))
        k2 = _rms_rows(proj[O_BK + 64 * hh + 32:O_BK + 64 * hh + 64], gain("bk"))
        kpad = jnp.concatenate([k1, k2, kfeat, zeros48], axis=0)
        kb_ref[0, :, LANES * hh:LANES * (hh + 1)] = kpad.T.astype(BF16)
    _store_key_blocks(vb_ref, proj[O_BV:O_BV + 256].astype(BF16))

    sc = C_QK ** -0.5 * LOG2E
    cq = _rms_rows(proj[O_CQL:O_CQL + C_Q_LORA], gain("cqa")).astype(BF16)
    qc = jnp.dot(wqb_ref[...], cq, preferred_element_type=F32)
    ckv = _rms_rows(proj[O_CKVL:O_CKVL + C_KV_LORA], gain("ckva")).astype(BF16)
    kvc = jnp.dot(wkvb_ref[...], ckv, preferred_element_type=F32)
    kr = proj[O_CKR:O_CKR + C_ROPE]
    vcs = []
    for hh in range(N_HEADS):
        q = _rms_rows(qc[C_QK * hh:C_QK * (hh + 1)], gain("cqn"))
        q = jnp.concatenate([q[:C_NOPE], _rope_rows(q[C_NOPE:], cosm, sinm, 16)], axis=0) * sc
        qc_ref[0, C_QK * hh:C_QK * (hh + 1), :] = q.astype(BF16)
        kk = jnp.concatenate([kvc[128 * hh:128 * hh + C_NOPE], kr], axis=0)
        kk = _rms_rows(kk, gain("ckn"))
        kk = jnp.concatenate([kk[:C_NOPE], _rope_rows(kk[C_NOPE:], cosm, sinm, 16), zeros32], axis=0)
        kc_ref[0, :, LANES * hh:LANES * (hh + 1)] = kk.T.astype(BF16)
        vcs.append(kvc[128 * hh + C_NOPE:128 * (hh + 1)])
    _store_key_blocks(vc_ref, jnp.concatenate(vcs, axis=0).astype(BF16))

    sd = HEAD_DIM ** -0.5 * LOG2E
    qs, ks, vs = [], [], []
    for g in range(2):
        qg, kg = [], []
        for hh in (2 * g, 2 * g + 1):
            qg.append(_rms_rows(proj[O_DQ + 64 * hh:O_DQ + 64 * hh + 64], gain("dq")) * sd)
            kg.append(_rms_rows(proj[O_DK + 64 * hh:O_DK + 64 * hh + 64], gain("dk")))
        qs.append(jnp.concatenate(qg, axis=0).T)
        ks.append(jnp.concatenate(kg, axis=0).T)
        vs.append(proj[O_DV + 128 * g:O_DV + 128 * (g + 1)].T)
    _store_dilated(scr_ref, jnp.concatenate(qs, axis=1), (dq1_ref, dq4_ref, dq16_ref))
    _store_dilated(scr_ref, jnp.concatenate(ks, axis=1), (dk1_ref, dk4_ref, dk16_ref))
    _store_dilated(scr_ref, jnp.concatenate(vs, axis=1), (dv1_ref, dv4_ref, dv16_ref))


def _project(xt, w_in_t, wqb_t, wkvb_t, gains, cosa, sina, cosm, sinm):
    b, d, s = xt.shape
    tm = TOK_TILE
    nt = s // tm
    fm = lambda rows: jax.ShapeDtypeStruct((b, rows, s), BF16)
    tmj = lambda cols: jax.ShapeDtypeStruct((b, s, cols), BF16)
    vblk = lambda rows: jax.ShapeDtypeStruct((b, s // KSTEP, rows, KSTEP), BF16)
    dviews = tuple(jax.ShapeDtypeStruct((b, s // dil, dil * D_WIDTH), BF16) for _, dil in DILATED_CONFIGS)
    out_shape = (fm(256), tmj(256), vblk(128),
                 fm(256), tmj(512), vblk(256),
                 fm(384), tmj(512), vblk(256)) + dviews * 3
    fm_spec = lambda rows: pl.BlockSpec((1, rows, tm), lambda i, j: (i, 0, j))
    tm_spec = lambda cols: pl.BlockSpec((1, tm, cols), lambda i, j: (i, j, 0))
    v_spec = lambda rows: pl.BlockSpec((1, tm // KSTEP, rows, KSTEP), lambda i, j: (i, j, 0, 0))
    dspecs = tuple(pl.BlockSpec((1, tm // dil, dil * D_WIDTH), lambda i, j: (i, j, 0)) for _, dil in DILATED_CONFIGS)
    out_specs = (fm_spec(256), tm_spec(256), v_spec(128),
                 fm_spec(256), tm_spec(512), v_spec(256),
                 fm_spec(384), tm_spec(512), v_spec(256)) + dspecs * 3
    whole = lambda i, j: (0, 0)
    in_specs = [
        pl.BlockSpec((1, d, tm), lambda i, j: (i, 0, j)),
        pl.BlockSpec((IN_WIDTH, d), whole),
        pl.BlockSpec((N_HEADS * C_QK, C_Q_LORA), whole),
        pl.BlockSpec((N_HEADS * 128, C_KV_LORA), whole),
        pl.BlockSpec((G_TOTAL, 1), whole),
        pl.BlockSpec((64, tm), lambda i, j: (0, j)),
        pl.BlockSpec((64, tm), lambda i, j: (0, j)),
        pl.BlockSpec((32, tm), lambda i, j: (0, j)),
        pl.BlockSpec((32, tm), lambda i, j: (0, j)),
    ]
    return pl.pallas_call(
        _proj_body,
        out_shape=out_shape,
        grid=(b, nt),
        in_specs=in_specs,
        out_specs=out_specs,
        scratch_shapes=[pltpu.VMEM((D_WIDTH // LANES, tm, LANES), F32)],
        compiler_params=_cparams(("parallel", "arbitrary")),
        name="project_heads",
    )(xt, w_in_t, wqb_t, wkvb_t, gains, cosa, sina, cosm, sinm)


def _online_update(s, v_aug, carry, shift=None):
    m, acc = carry
    smax = jnp.max(s, axis=0, keepdims=True)
    if shift is None:
        m_new = jnp.maximum(m, smax)
        p = jnp.exp2(s - m_new).astype(BF16)
    else:
        m_new = jnp.maximum(m, smax + shift)
        p = jnp.exp2(s - (m_new - shift)).astype(BF16)
    alpha = jnp.exp2(m - m_new)
    acc = alpha * acc + jnp.dot(v_aug, p, preferred_element_type=F32)
    return m_new, acc


def _init_carry(dv, tq):
    return (jnp.full((1, tq), -jnp.inf, F32), jnp.zeros((dv + BF16_ROWS, tq), F32))


def _finish(carry, dv):
    _, acc = carry
    return acc[:dv] / acc[dv:dv + 1]


def _run_pipelined(n_jobs, n_blocks, scores_fn, update_fn, init_fn, finish_fn):
    steps = [(t, i) for t in range(n_jobs) for i in range(n_blocks)]
    pending = [scores_fn(*steps[n]) for n in range(min(SCORES_AHEAD, len(steps)))]
    carry = None
    for n, (t, i) in enumerate(steps):
        s = pending.pop(0)
        if n + SCORES_AHEAD < len(steps):
            pending.append(scores_fn(*steps[n + SCORES_AHEAD]))
        if i == 0:
            carry = init_fn()
        carry = update_fn(t, i, s, carry)
        if i == n_blocks - 1:
            finish_fn(t, carry)


def _dense_body(q_ref, k_ref, v_ref, o_ref):
    dq = q_ref.shape[1]
    nkb, dv, ks = v_ref.shape[1], v_ref.shape[2], v_ref.shape[3]
    tq = TQ
    ones = jnp.ones((BF16_ROWS, ks), BF16)
    zpad = jnp.zeros((LANES - dq, tq), BF16)
    qpads = [jnp.concatenate([q_ref[0, :, t * tq:(t + 1) * tq], zpad], axis=0) for t in range(q_ref.shape[2] // tq)]

    def scores(t, i):
        return jnp.dot(k_ref[0, i * ks:(i + 1) * ks, :], qpads[t], preferred_element_type=F32)

    def update(t, i, s, carry):
        return _online_update(s, jnp.concatenate([v_ref[0, i], ones], axis=0), carry)

    def finish(t, carry):
        o_ref[0, :, t * tq:(t + 1) * tq] = _finish(carry, dv).astype(o_ref.dtype)

    _run_pipelined(len(qpads), nkb, scores, update, lambda: _init_carry(dv, tq), finish)


def _dense_attention(q, k, v, dq, q_per_kv):
    b, rows, s = q.shape
    nh = rows // dq
    nkb, tk = v.shape[1], v.shape[3]
    return pl.pallas_call(
        _dense_body,
        out_shape=jax.ShapeDtypeStruct((b, nh * HEAD_DIM, s), BF16),
        grid=(b, nh, s // TQ_STEP),
        in_specs=[
            pl.BlockSpec((1, dq, TQ_STEP), lambda i, h, j: (i, h, j)),
            pl.BlockSpec((1, s, LANES), lambda i, h, j: (i, 0, h // q_per_kv)),
            pl.BlockSpec((1, nkb, HEAD_DIM, tk), lambda i, h, j: (i, 0, h // q_per_kv, 0)),
        ],
        out_specs=pl.BlockSpec((1, HEAD_DIM, TQ_STEP), lambda i, h, j: (i, h, j)),
        compiler_params=_cparams(("parallel", "arbitrary", "arbitrary")),
        name="dense_attention",
    )(q, k, v)


def _diff_body(sc_ref, q_ref, k_ref, v_ref, g_ref, bias_ref, o_ref):
    tq = TQ
    n_tiles = q_ref.shape[2] // tq
    nkb, dv, ks = v_ref.shape[1], v_ref.shape[2], v_ref.shape[3]
    hh = pl.program_id(1)
    lam = sc_ref[0]
    out_scale = sc_ref[1]
    slope = sc_ref[2 + hh]
    z32 = jnp.zeros((B_QK_DIM, tq), BF16)
    z48 = jnp.zeros((48, tq), BF16)
    zfeat = jnp.zeros((BF16_ROWS, tq), BF16)
    ones = jnp.ones((BF16_ROWS, ks), BF16)
    frow = lax.broadcasted_iota(jnp.int32, (BF16_ROWS, tq), 0)
    piece = jnp.where(frow % 3 == 0, LOG2E_PIECES[0], jnp.where(frow % 3 == 1, LOG2E_PIECES[1], LOG2E_PIECES[2]))
    cfeat = jnp.where(frow < 6, piece, 0.0)
    lane_pos = lax.broadcasted_iota(jnp.int32, (1, tq), 1)

    n_mixed = tq // ks

    def tile_index(job):
        return pl.program_id(2) * n_tiles + job // 2

    def block_index(job, ii):
        return lax.rem(tile_index(job) * n_mixed + ii, nkb)

    def side(job, ii):
        return jnp.where(block_index(job, ii) < tile_index(job) * n_mixed, 1.0, -1.0).astype(F32)

    def qrows(job):
        q = q_ref[0, :, (job // 2) * tq:(job // 2 + 1) * tq]
        if job % 2 == 0:
            return jnp.concatenate([q[:B_QK_DIM], z32], axis=0)
        return jnp.concatenate([z32, q[B_QK_DIM:]], axis=0)

    def scores(job, ii):
        i = block_index(job, ii)
        k = k_ref[0, pl.ds(pl.multiple_of(i * ks, ks), ks), :]
        feat = zfeat if ii < n_mixed else (cfeat * (side(job, ii) * slope)).astype(BF16)
        return jnp.dot(k, jnp.concatenate([qrows(job), feat, z48], axis=0), preferred_element_type=F32)

    def update(job, ii, s, carry):
        v_aug = jnp.concatenate([v_ref[0, block_index(job, ii)], ones], axis=0)
        if ii < n_mixed:
            return _online_update(s - bias_ref[0, ii], v_aug, carry)
        qpos = (tile_index(job) * tq + lane_pos).astype(F32)
        return _online_update(s, v_aug, carry, qpos * (-(side(job, ii) * slope * LOG2E)))

    res = []

    def finish(job, carry):
        res.append(_finish(carry, dv))
        if job % 2 == 1:
            o = res[job - 1] - lam * res[job]
            t = job // 2
            o_ref[0, :, t * tq:(t + 1) * tq] = (_rms_rows(o, g_ref[...]) * out_scale).astype(o_ref.dtype)

    _run_pipelined(2 * n_tiles, nkb, scores, update, lambda: _init_carry(dv, tq), finish)


def _diff_attention(scal, slopes, q, k, v, sub_g):
    b, rows, s = q.shape
    nkb, tk = v.shape[1], v.shape[3]
    n_mixed = TQ // tk
    jj = jnp.arange(tk)[None, :, None] + tk * jnp.arange(n_mixed)[:, None, None]
    dist = jnp.abs(jj - jnp.arange(TQ)[None, None, :]).astype(F32)
    bias = dist[None] * (LOG2E * jnp.asarray(slopes, F32))[:, None, None, None]
    return pl.pallas_call(
        _diff_body,
        out_shape=jax.ShapeDtypeStruct((b, rows, s), BF16),
        grid=(b, N_HEADS, s // TQ_STEP),
        in_specs=[
            pl.BlockSpec(memory_space=pltpu.SMEM),
            pl.BlockSpec((1, 2 * B_QK_DIM, TQ_STEP), lambda i, h, j: (i, h, j)),
            pl.BlockSpec((1, s, LANES), lambda i, h, j: (i, 0, h)),
            pl.BlockSpec((1, nkb, HEAD_DIM, tk), lambda i, h, j: (i, 0, h, 0)),
            pl.BlockSpec((HEAD_DIM, 1), lambda i, h, j: (0, 0)),
            pl.BlockSpec((1, n_mixed, tk, TQ), lambda i, h, j: (h, 0, 0, 0)),
        ],
        out_specs=pl.BlockSpec((1, HEAD_DIM, TQ_STEP), lambda i, h, j: (i, h, j)),
        compiler_params=_cparams(("parallel", "arbitrary", "arbitrary")),
        name="diff_attention",
    )(scal, q, k, v, sub_g, bias)


def _band_body(q_ref, k_ref, v_ref, o_ref, lse_ref, *, step_len, slopes, qb):
    l_len = k_ref.shape[1]
    win = min(qb + 2 * BAND_RADIUS, l_len)
    lane = lax.broadcasted_iota(jnp.int32, (qb, LANES), 1)
    row = lax.broadcasted_iota(jnp.int32, (qb, win), 0)
    colw = lax.broadcasted_iota(jnp.int32, (qb, win), 1)
    for sb in range(q_ref.shape[1] // qb):
        jb = pl.program_id(2) * (q_ref.shape[1] // qb) + sb
        start = jnp.clip(jb * qb - BAND_RADIUS, 0, l_len - win)
        start = pl.multiple_of(start, BAND_RADIUS)
        q = q_ref[0, sb * qb:(sb + 1) * qb, :]
        kwin = k_ref[0, pl.ds(start, win), :]
        vwin = v_ref[0, pl.ds(start, win), :]
        rel_i = jnp.abs((jb * qb - start) + row - colw)
        allowed = rel_i <= BAND_RADIUS
        rel = rel_i.astype(F32)
        for g in range(2):
            q2 = q[:, LANES * g:LANES * (g + 1)].astype(F32)
            k2 = kwin[:, LANES * g:LANES * (g + 1)]
            v2 = vwin[:, LANES * g:LANES * (g + 1)]
            out_g = jnp.zeros((qb, LANES), F32)
            lse_g = jnp.zeros((qb, LANES), F32)
            for hh in range(2):
                own = (lane >= HEAD_DIM * hh) & (lane < HEAD_DIM * (hh + 1))
                qm = jnp.where(own, q2, 0.0).astype(BF16)
                s = lax.dot_general(qm, k2, (((1,), (1,)), ((), ())), preferred_element_type=F32)
                s = s - (slopes[2 * g + hh] * step_len * LOG2E) * rel
                s = jnp.where(allowed, s, -jnp.inf)
                m = jnp.max(s, axis=1, keepdims=True)
                p = jnp.exp2(s - m)
                lsum = jnp.sum(p, axis=1, keepdims=True)
                o = jnp.dot(p.astype(BF16), v2, preferred_element_type=F32) / lsum
                lse = m + jnp.log2(lsum)
                out_g = jnp.where(own, o, out_g)
                lse_g = jnp.where(own, lse, lse_g)
            o_ref[0, sb * qb:(sb + 1) * qb, LANES * g:LANES * (g + 1)] = out_g
            lse_ref[0, sb * qb:(sb + 1) * qb, LANES * g:LANES * (g + 1)] = lse_g


def _band_attention(q, k, v, dil, slopes):
    b, l_len, cw = q.shape
    c = cw // dil
    qb = min(BAND_Q, l_len)
    qstep = min(BAND_QSTEP, l_len)
    body = functools.partial(_band_body, step_len=float(dil), slopes=slopes, qb=qb)
    return pl.pallas_call(
        body,
        out_shape=(jax.ShapeDtypeStruct((b, l_len, cw), F32),) * 2,
        grid=(b, dil, l_len // qstep),
        in_specs=[
            pl.BlockSpec((1, qstep, c), lambda i, r, j: (i, j, r)),
            pl.BlockSpec((1, l_len, c), lambda i, r, j: (i, 0, r)),
            pl.BlockSpec((1, l_len, c), lambda i, r, j: (i, 0, r)),
        ],
        out_specs=(pl.BlockSpec((1, qstep, c), lambda i, r, j: (i, j, r)),) * 2,
        compiler_params=_cparams(("parallel", "arbitrary", "arbitrary")),
        name="band_attention_d%d" % dil,
    )(q, k, v)


def _band_combine_body(*refs):
    n = len(DILATED_CONFIGS)
    in_refs, o_ref, scr_ref = refs[:2 * n], refs[2 * n], refs[2 * n + 1]
    tm, c = o_ref.shape[1], o_ref.shape[2]
    vals = []
    for idx, (_, dil) in enumerate(DILATED_CONFIGS):
        for which in range(2):
            src = in_refs[2 * idx + which]
            if dil == 1:
                vals.append(src[0])
                continue
            groups = []
            for g in range(c // LANES):
                slot = scr_ref.at[(2 * idx + which) * (c // LANES) + g]
                for r in range(dil):
                    lo = c * r + LANES * g
                    slot[pl.ds(r, tm // dil, stride=dil), :] = src[0, :, lo:lo + LANES]
                groups.append(slot[...])
            vals.append(jnp.concatenate(groups, axis=1))
    outs, lses = vals[0::2], vals[1::2]
    m = functools.reduce(jnp.maximum, lses)
    ws = [jnp.exp2(x - m) for x in lses]
    num = functools.reduce(lambda a, t: a + t, [w * o for w, o in zip(ws, outs)])
    den = functools.reduce(lambda a, t: a + t, ws)
    o_ref[0] = (num / den).astype(o_ref.dtype)


def _band_combine(pairs):
    b, s, c = pairs[0][0].shape
    tm = TOK_TILE
    specs = []
    for (_, dil) in DILATED_CONFIGS:
        specs += [pl.BlockSpec((1, tm // dil, dil * c), lambda i, j: (i, j, 0))] * 2
    flat = [a for p in pairs for a in p]
    return pl.pallas_call(
        _band_combine_body,
        out_shape=jax.ShapeDtypeStruct((b, s, c), BF16),
        grid=(b, s // tm),
        in_specs=specs,
        out_specs=pl.BlockSpec((1, tm, c), lambda i, j: (i, j, 0)),
        scratch_shapes=[pltpu.VMEM((2 * len(DILATED_CONFIGS) * (c // LANES), tm, LANES), F32)],
        compiler_params=_cparams(("parallel", "arbitrary")),
        name="band_combine",
    )(*flat)


def _outproj_body(x_ref, oa_ref, ob_ref, oc_ref, od_ref, w_ref, g_ref, x1_ref, h2_ref, edge_ref):
    d, tm = x_ref.shape[1], x_ref.shape[2]
    j = pl.program_id(1)
    mix = jnp.concatenate([oa_ref[0], ob_ref[0], oc_ref[0]], axis=0)
    w = w_ref[...]
    y = jnp.dot(w[:, :768], mix, preferred_element_type=F32)
    y = y + lax.dot_general(w[:, 768:], od_ref[0], (((1,), (1,)), ((), ())), preferred_element_type=F32)
    x1 = x_ref[0] + y
    h2 = _rms_rows(x1, g_ref[...])
    x1_ref[0] = x1.T
    h2t = h2.T
    h2_ref[0] = h2t.astype(BF16)
    edge_ref[0, pl.ds(2 * j, 1), :] = h2t[0:1].astype(BF16).astype(F32)
    edge_ref[0, pl.ds(2 * j + 1, 1), :] = h2t[tm - 1:tm].astype(BF16).astype(F32)


def _out_project(xt, oa, ob, oc, od, w_out_t, g2):
    b, d, s = xt.shape
    tm = TOK_TILE
    nt = s // tm
    fm = lambda rows: pl.BlockSpec((1, rows, tm), lambda i, j: (i, 0, j))
    tmj = pl.BlockSpec((1, tm, d), lambda i, j: (i, j, 0))
    return pl.pallas_call(
        _outproj_body,
        out_shape=(jax.ShapeDtypeStruct((b, s, d), F32), jax.ShapeDtypeStruct((b, s, d), BF16),
                   jax.ShapeDtypeStruct((b, 2 * nt, d), F32)),
        grid=(b, nt),
        in_specs=[fm(d), fm(256), fm(256), fm(256),
                  pl.BlockSpec((1, tm, D_WIDTH), lambda i, j: (i, j, 0)),
                  pl.BlockSpec((d, d), lambda i, j: (0, 0)),
                  pl.BlockSpec((d, 1), lambda i, j: (0, 0))],
        out_specs=(tmj, tmj, pl.BlockSpec((1, 2 * nt, d), lambda i, j: (i, 0, 0))),
        compiler_params=_cparams(("parallel", "arbitrary")),
        name="out_project",
    )(xt, oa, ob, oc, od, w_out_t, g2)


def _halo_body(h_ref, w_ref, o_ref):
    o_ref[...] = jnp.dot(h_ref[...], w_ref[...], preferred_element_type=F32)


def _halo_up(halo_h, w_up):
    nh, d = halo_h.shape
    ff2 = w_up.shape[1]
    n_blk = 4
    cols = ff2 // n_blk
    return pl.pallas_call(
        _halo_body,
        out_shape=jax.ShapeDtypeStruct((nh, ff2), F32),
        grid=(n_blk,),
        in_specs=[pl.BlockSpec((nh, d), lambda i: (0, 0)),
                  pl.BlockSpec((d, cols), lambda i: (0, i))],
        out_specs=pl.BlockSpec((nh, cols), lambda i: (0, i)),
        compiler_params=_cparams(("arbitrary",)),
        name="mlp_halo_up",
    )(halo_h, w_up)


def _mlp_body(x1_ref, h2_ref, wup_ref, wdn_ref, par_ref, halo_ref, o_ref, gbuf_ref, vbuf_ref, *, chunks):
    tf = h2_ref.shape[1]
    ff = wdn_ref.shape[0]
    nt = pl.num_programs(1)
    row = (pl.program_id(0) * nt + pl.program_id(1)) * 2
    h = h2_ref[0]

    def conv(buf_ref, u, c0, width):
        buf_ref[MLP_PAD:MLP_PAD + tf, :width] = u
        buf_ref[MLP_PAD - 1:MLP_PAD, :width] = halo_ref[pl.ds(row, 1), c0:c0 + width]
        buf_ref[MLP_PAD + tf:MLP_PAD + tf + 1, :width] = halo_ref[pl.ds(row + 1, 1), c0:c0 + width]
        prev = buf_ref[MLP_PAD - 1:MLP_PAD - 1 + tf, :width]
        nxt = buf_ref[MLP_PAD + 1:MLP_PAD + 1 + tf, :width]
        p = par_ref[:, c0:c0 + width]
        return p[0:1] * prev + p[1:2] * u + p[2:3] * nxt + p[3:4]

    y = x1_ref[0]
    for c0, width in chunks:
        ug = jnp.dot(h, wup_ref[:, c0:c0 + width], preferred_element_type=F32)
        uv = jnp.dot(h, wup_ref[:, ff + c0:ff + c0 + width], preferred_element_type=F32)
        gate = conv(gbuf_ref, ug, c0, width)
        val = conv(vbuf_ref, uv, ff + c0, width)
        act = gate / (1.0 + jnp.exp(-gate)) * val
        y = y + jnp.dot(act.astype(BF16), wdn_ref[c0:c0 + width, :], preferred_element_type=F32)
    o_ref[0] = y.T


def _mlp(x1, h2, w_up, w_down, conv_par, halo_up, chunks):
    b, s, d = x1.shape
    tf = TOK_TILE
    ff2 = w_up.shape[1]
    nh = halo_up.shape[0]
    whole = lambda i, j: (0, 0)
    body = functools.partial(_mlp_body, chunks=chunks)
    const = dict(pipeline_mode=pl.Buffered(1))
    wmax = max(w for _, w in chunks)
    return pl.pallas_call(
        body,
        out_shape=jax.ShapeDtypeStruct((b, d, s), F32),
        grid=(b, s // tf),
        in_specs=[pl.BlockSpec((1, tf, d), lambda i, j: (i, j, 0)),
                  pl.BlockSpec((1, tf, d), lambda i, j: (i, j, 0)),
                  pl.BlockSpec((d, ff2), whole, **const),
                  pl.BlockSpec((ff2 // 2, d), whole, **const),
                  pl.BlockSpec((8, ff2), whole, **const),
                  pl.BlockSpec((nh, ff2), whole, **const)],
        out_specs=pl.BlockSpec((1, d, tf), lambda i, j: (i, 0, j)),
        scratch_shapes=[pltpu.VMEM((tf + 2 * MLP_PAD, wmax), F32)] * 2,
        compiler_params=_cparams(("arbitrary", "arbitrary")),
        name="gated_conv_mlp",
    )(x1, h2, w_up, w_down, conv_par, halo_up)


def _halo_rows(edges, nt):
    b, _, d = edges.shape
    ez = jnp.concatenate([edges, jnp.zeros((b, 1, d), edges.dtype)], axis=1)
    idx = []
    for j in range(nt):
        idx += [2 * j - 1 if j > 0 else 2 * nt, 2 * j + 2 if j < nt - 1 else 2 * nt]
    rows = jnp.take(ez, jnp.asarray(idx, jnp.int32), axis=1).reshape(b * nt * 2, d)
    pad = (-rows.shape[0]) % BF16_ROWS
    return jnp.pad(rows, ((0, pad), (0, 0))).astype(BF16)


def _rope_tables(s):
    def freqs(dim):
        return 1.0 / (ROPE_THETA ** (jnp.arange(0, dim, 2, dtype=F32) / dim))
    t = jnp.arange(s)
    rows = (t // GRID_W).astype(F32)
    cols = (t % GRID_W).astype(F32)
    pos = t.astype(F32)
    fa = freqs(HEAD_DIM // 2)
    ar, ac = fa[:, None] * rows[None, :], fa[:, None] * cols[None, :]
    am = freqs(C_ROPE)[:, None] * pos[None, :]
    cosa = jnp.concatenate([jnp.cos(ar), jnp.cos(ar), jnp.cos(ac), jnp.cos(ac)], axis=0)
    sina = jnp.concatenate([-jnp.sin(ar), jnp.sin(ar), -jnp.sin(ac), jnp.sin(ac)], axis=0)
    cosm = jnp.concatenate([jnp.cos(am), jnp.cos(am)], axis=0)
    sinm = jnp.concatenate([-jnp.sin(am), jnp.sin(am)], axis=0)
    return cosa, sina, cosm, sinm


def kernel(x, norm1_g, w_in, a_qn_g, a_kn_g, b_qn_g, b_kn_g, b_lam_q1, b_lam_k1, b_lam_q2, b_lam_k2, b_sub_g,
           c_qa_g, c_kva_g, c_wqb, c_wkvb, c_qn_g, c_kn_g, d_qn_g, d_kn_g, w_out, norm2_g, w_up, conv_w, conv_b,
           w_down):
    b, s, d = x.shape
    depth = w_in.shape[0]
    ff = w_down.shape[1]
    nt = s // TOK_TILE
    assert s % TOK_TILE == 0 and TOK_TILE % KSTEP == 0 and TQ % KSTEP == 0 and s % TQ == 0 and s % GRID_W == 0
    assert s <= 1 << (POS_SPLIT_BITS + BF16_MANTISSA_BITS)
    for window, dil in DILATED_CONFIGS:
        assert window // (2 * dil) == BAND_RADIUS and (s // dil) % min(BAND_Q, s // dil) == 0
        assert (s // dil) % min(BAND_QSTEP, s // dil) == 0 and BAND_QSTEP % BAND_Q == 0
        assert TOK_TILE % (dil * BF16_ROWS) == 0

    tr = lambda w: jnp.swapaxes(w, 1, 2).astype(BF16)
    gains = jnp.concatenate([norm1_g, a_qn_g, a_kn_g, b_qn_g, b_kn_g, c_qa_g, c_kva_g, c_qn_g, c_kn_g,
                             d_qn_g, d_kn_g], axis=1)[:, :, None]
    conv_par = jnp.concatenate([conv_w, conv_b[:, None, :], jnp.zeros((depth, 4, 2 * ff), F32)], axis=1)
    slopes = [2.0 ** (-8.0 * i / (2 * N_HEADS)) for i in range(1, 2 * N_HEADS + 1)]
    slopes_b, slopes_d = slopes[:N_HEADS], tuple(slopes[N_HEADS:])
    lam_init = jnp.asarray([0.8 - 0.6 * math.exp(-0.3 * l) for l in range(depth)], F32)
    lam = (jnp.exp(jnp.sum(b_lam_q1 * b_lam_k1, axis=1)) - jnp.exp(jnp.sum(b_lam_q2 * b_lam_k2, axis=1)) + lam_init)
    scal = jnp.concatenate([lam[:, None], (1.0 - lam_init)[:, None],
                            jnp.broadcast_to(jnp.asarray(slopes_b, F32), (depth, N_HEADS)),
                            jnp.zeros((depth, 2), F32)], axis=1)
    layers = dict(w_in=tr(w_in), wqb=tr(c_wqb), wkvb=tr(c_wkvb), w_out=tr(w_out),
                  w_up=w_up.astype(BF16), w_down=w_down.astype(BF16),
                  gains=gains, g2=norm2_g[:, :, None], sub_g=b_sub_g[:, :, None], conv_par=conv_par, scal=scal)
    cosa, sina, cosm, sinm = _rope_tables(s)
    assert ff % LANES == 0
    chunks = tuple((c0, min(MLP_CHUNK, ff - c0)) for c0 in range(0, ff, MLP_CHUNK))

    def layer(xt, p):
        (qa, ka, va, qb, kb, vb, qc, kc, vc, dq1, dq4, dq16, dk1, dk4, dk16, dv1, dv4, dv16) = _project(
            xt, p["w_in"], p["wqb"], p["wkvb"], p["gains"], cosa, sina, cosm, sinm)
        oa = _dense_attention(qa, ka, va, HEAD_DIM, N_HEADS // A_KV_HEADS)
        ob = _diff_attention(p["scal"], slopes_b, qb, kb, vb, p["sub_g"])
        oc = _dense_attention(qc, kc, vc, C_QK, 1)
        branches = [_band_attention(q_, k_, v_, dil, slopes_d)
                    for (_, dil), q_, k_, v_ in zip(DILATED_CONFIGS, (dq1, dq4, dq16), (dk1, dk4, dk16), (dv1, dv4, dv16))]
        od = _band_combine(branches)
        x1, h2, edges = _out_project(xt, oa, ob, oc, od, p["w_out"], p["g2"])
        halo = _halo_up(_halo_rows(edges, nt), p["w_up"])
        return _mlp(x1, h2, p["w_up"], p["w_down"], p["conv_par"], halo, chunks), None

    xt = _transpose_last2(x, TOK_TILE, d)
    xt, _ = lax.scan(layer, xt, layers)
    return _transpose_last2(xt, d, TOK_TILE)
```

```python
import functools
import math

import jax
import jax.numpy as jnp
import numpy as np
from jax import lax
from jax.experimental import pallas as pl
from jax.experimental.pallas import tpu as pltpu

F32 = jnp.float32
BF16 = jnp.bfloat16

HEAD_DIM = 64
N_HEADS = 4
A_KV_HEADS = 2
B_QK_DIM = 32
C_Q_LORA = 256
C_KV_LORA = 128
C_NOPE = 64
C_ROPE = 32
C_QK = C_NOPE + C_ROPE
DILATED_CONFIGS = ((128, 1), (512, 4), (2048, 16))
D_WIDTH = N_HEADS * HEAD_DIM
GRID_W = 64
ROPE_THETA = 10000.0
NORM_EPS = 1e-6
LOG2E = 1.4426950408889634


def _bf16_pieces(x, n):
    out = []
    for _ in range(n):
        u = np.float32(x).view(np.uint32)
        u = np.uint32((int(u) + 0x7FFF + ((int(u) >> 16) & 1)) & 0xFFFF0000)
        piece = float(u.view(np.float32))
        out.append(piece)
        x = x - piece
    return tuple(out)


LOG2E_PIECES = _bf16_pieces(LOG2E, 3)
POS_SPLIT_BITS = 6
BF16_MANTISSA_BITS = 8

_SPLITS = (256, 128, 128, 256, 256, 256, C_Q_LORA, C_KV_LORA, C_ROPE, 256, 256, 256)
_OFF = [0]
for _n in _SPLITS:
    _OFF.append(_OFF[-1] + _n)
(O_AQ, O_AK, O_AV, O_BQ, O_BK, O_BV, O_CQL, O_CKVL, O_CKR, O_DQ, O_DK, O_DV, IN_WIDTH) = _OFF

_GAIN_SIZES = (("n1", 1024), ("aq", 64), ("ak", 64), ("bq", 32), ("bk", 32), ("cqa", 256), ("ckva", 128),
               ("cqn", 96), ("ckn", 96), ("dq", 64), ("dk", 64))
G_OFF = {}
_o = 0
for _k, _n in _GAIN_SIZES:
    G_OFF[_k] = (_o, _n)
    _o += _n
G_TOTAL = _o

LANES = 128
BF16_ROWS = 16
TOK_TILE = 512
MLP_CHUNK = 1024
MLP_PAD = 8
TQ = 512
TQ_STEP = 2048
KSTEP = 256
SCORES_AHEAD = 2
BAND_Q = 128
BAND_QSTEP = 512
BAND_RADIUS = 64
VMEM_LIMIT = 56 * 1024 * 1024


def _cparams(sem, flags=None):
    return pltpu.CompilerParams(dimension_semantics=sem, vmem_limit_bytes=VMEM_LIMIT, flags=flags)


def _rms_rows(v, g_col):
    ms = jnp.mean(v * v, axis=0, keepdims=True)
    return v * lax.rsqrt(ms + NORM_EPS) * g_col


def _swap_halves(x, n):
    parts = []
    for i in range(0, x.shape[0], 2 * n):
        parts.append(x[i + n:i + 2 * n])
        parts.append(x[i:i + n])
    return jnp.concatenate(parts, axis=0)


def _rope_rows(x, cos, sin_signed, n):
    return x * cos + _swap_halves(x, n) * sin_signed


def _transpose_body(x_ref, o_ref):
    o_ref[0] = x_ref[0].T


def _transpose_last2(x, tr, tc):
    b, r, c = x.shape
    return pl.pallas_call(
        _transpose_body,
        out_shape=jax.ShapeDtypeStruct((b, c, r), x.dtype),
        grid=(b, r // tr, c // tc),
        in_specs=[pl.BlockSpec((1, tr, tc), lambda i, j, k: (i, j, k))],
        out_specs=pl.BlockSpec((1, tc, tr), lambda i, j, k: (i, k, j)),
        compiler_params=_cparams(("parallel", "arbitrary", "arbitrary")),
        name="transpose_tokens",
    )(x)


def _store_dilated(scr_ref, val, out_refs):
    tm, c = val.shape
    for g in range(c // LANES):
        scr_ref[g] = val[:, LANES * g:LANES * (g + 1)]
    for (_, dil), out in zip(DILATED_CONFIGS, out_refs):
        if dil == 1:
            out[0] = val.astype(BF16)
            continue
        n = tm // dil
        for r in range(dil):
            for g in range(c // LANES):
                lo = c * r + LANES * g
                out[0, :, lo:lo + LANES] = scr_ref[g, pl.ds(r, n, stride=dil), :].astype(BF16)


def _store_key_blocks(v_ref, v):
    for c in range(v.shape[1] // KSTEP):
        v_ref[0, c] = v[:, c * KSTEP:(c + 1) * KSTEP]


def _proj_body(x_ref, w_ref, wqb_ref, wkvb_ref, g_ref, cosa_ref, sina_ref, cosm_ref, sinm_ref,
               qa_ref, ka_ref, va_ref, qb_ref, kb_ref, vb_ref, qc_ref, kc_ref, vc_ref,
               dq1_ref, dq4_ref, dq16_ref, dk1_ref, dk4_ref, dk16_ref, dv1_ref, dv4_ref, dv16_ref,
               scr_ref):
    tm = x_ref.shape[2]

    def gain(name):
        o, n = G_OFF[name]
        return g_ref[o:o + n, :]

    x = x_ref[0]
    h = _rms_rows(x, gain("n1")).astype(BF16)
    proj = jnp.dot(w_ref[...], h, preferred_element_type=F32)

    cosa, sina = cosa_ref[...], sina_ref[...]
    cosm, sinm = cosm_ref[...], sinm_ref[...]
    zeros64 = jnp.zeros((64, tm), F32)
    zeros32 = jnp.zeros((32, tm), F32)

    sa = HEAD_DIM ** -0.5 * LOG2E
    for hh in range(N_HEADS):
        q = proj[O_AQ + 64 * hh:O_AQ + 64 * hh + 64]
        q = _rope_rows(_rms_rows(q, gain("aq")), cosa, sina, 16) * sa
        qa_ref[0, 64 * hh:64 * hh + 64, :] = q.astype(BF16)
    for g in range(A_KV_HEADS):
        k = proj[O_AK + 64 * g:O_AK + 64 * g + 64]
        k = _rope_rows(_rms_rows(k, gain("ak")), cosa, sina, 16)
        kpad = jnp.concatenate([k, zeros64], axis=0)
        ka_ref[0, :, LANES * g:LANES * (g + 1)] = kpad.T.astype(BF16)
    _store_key_blocks(va_ref, proj[O_AV:O_AV + 128].astype(BF16))

    sb = B_QK_DIM ** -0.5 * LOG2E
    for i in range(2 * N_HEADS):
        q = proj[O_BQ + 32 * i:O_BQ + 32 * i + 32]
        qb_ref[0, 32 * i:32 * i + 32, :] = (_rms_rows(q, gain("bq")) * sb).astype(BF16)
    kpos = pl.program_id(1) * tm + lax.broadcasted_iota(jnp.int32, (BF16_ROWS, tm), 1)
    frow = lax.broadcasted_iota(jnp.int32, (BF16_ROWS, tm), 0)
    pos_hi = ((kpos >> POS_SPLIT_BITS) << POS_SPLIT_BITS).astype(F32)
    pos_lo = (kpos & ((1 << POS_SPLIT_BITS) - 1)).astype(F32)
    kfeat = jnp.where(frow < 3, pos_hi, jnp.where(frow < 6, pos_lo, 0.0))
    zeros48 = jnp.zeros((48, tm), F32)
    for hh in range(N_HEADS):
        k1 = _rms_rows(proj[O_BK + 64 * hh:O_BK + 64 * hh + 32], gain("bk"))
        k2 = _rms_rows(proj[O_BK + 64 * hh + 32:O_BK + 64 * hh + 64], gain("bk"))
        kpad = jnp.concatenate([k1, k2, kfeat, zeros48], axis=0)
        kb_ref[0, :, LANES * hh:LANES * (hh + 1)] = kpad.T.astype(BF16)
    _store_key_blocks(vb_ref, proj[O_BV:O_BV + 256].astype(BF16))

    sc = C_QK ** -0.5 * LOG2E
    cq = _rms_rows(proj[O_CQL:O_CQL + C_Q_LORA], gain("cqa")).astype(BF16)
    qc = jnp.dot(wqb_ref[...], cq, preferred_element_type=F32)
    ckv = _rms_rows(proj[O_CKVL:O_CKVL + C_KV_LORA], gain("ckva")).astype(BF16)
    kvc = jnp.dot(wkvb_ref[...], ckv, preferred_element_type=F32)
    kr = proj[O_CKR:O_CKR + C_ROPE]
    vcs = []
    for hh in range(N_HEADS):
        q = _rms_rows(qc[C_QK * hh:C_QK * (hh + 1)], gain("cqn"))
        q = jnp.concatenate([q[:C_NOPE], _rope_rows(q[C_NOPE:], cosm, sinm, 16)], axis=0) * sc
        qc_ref[0, C_QK * hh:C_QK * (hh + 1), :] = q.astype(BF16)
        kk = jnp.concatenate([kvc[128 * hh:128 * hh + C_NOPE], kr], axis=0)
        kk = _rms_rows(kk, gain("ckn"))
        kk = jnp.concatenate([kk[:C_NOPE], _rope_rows(kk[C_NOPE:], cosm, sinm, 16), zeros32], axis=0)
        kc_ref[0, :, LANES * hh:LANES * (hh + 1)] = kk.T.astype(BF16)
        vcs.append(kvc[128 * hh + C_NOPE:128 * (hh + 1)])
    _store_key_blocks(vc_ref, jnp.concatenate(vcs, axis=0).astype(BF16))

    sd = HEAD_DIM ** -0.5 * LOG2E
    qs, ks, vs = [], [], []
    for g in range(2):
        qg, kg = [], []
        for hh in (2 * g, 2 * g + 1):
            qg.append(_rms_rows(proj[O_DQ + 64 * hh:O_DQ + 64 * hh + 64], gain("dq")) * sd)
            kg.append(_rms_rows(proj[O_DK + 64 * hh:O_DK + 64 * hh + 64], gain("dk")))
        qs.append(jnp.concatenate(qg, axis=0).T)
        ks.append(jnp.concatenate(kg, axis=0).T)
        vs.append(proj[O_DV + 128 * g:O_DV + 128 * (g + 1)].T)
    _store_dilated(scr_ref, jnp.concatenate(qs, axis=1), (dq1_ref, dq4_ref, dq16_ref))
    _store_dilated(scr_ref, jnp.concatenate(ks, axis=1), (dk1_ref, dk4_ref, dk16_ref))
    _store_dilated(scr_ref, jnp.concatenate(vs, axis=1), (dv1_ref, dv4_ref, dv16_ref))


def _project(xt, w_in_t, wqb_t, wkvb_t, gains, cosa, sina, cosm, sinm):
    b, d, s = xt.shape
    tm = TOK_TILE
    nt = s // tm
    fm = lambda rows: jax.ShapeDtypeStruct((b, rows, s), BF16)
    tmj = lambda cols: jax.ShapeDtypeStruct((b, s, cols), BF16)
    vblk = lambda rows: jax.ShapeDtypeStruct((b, s // KSTEP, rows, KSTEP), BF16)
    dviews = tuple(jax.ShapeDtypeStruct((b, s // dil, dil * D_WIDTH), BF16) for _, dil in DILATED_CONFIGS)
    out_shape = (fm(256), tmj(256), vblk(128),
                 fm(256), tmj(512), vblk(256),
                 fm(384), tmj(512), vblk(256)) + dviews * 3
    fm_spec = lambda rows: pl.BlockSpec((1, rows, tm), lambda i, j: (i, 0, j))
    tm_spec = lambda cols: pl.BlockSpec((1, tm, cols), lambda i, j: (i, j, 0))
    v_spec = lambda rows: pl.BlockSpec((1, tm // KSTEP, rows, KSTEP), lambda i, j: (i, j, 0, 0))
    dspecs = tuple(pl.BlockSpec((1, tm // dil, dil * D_WIDTH), lambda i, j: (i, j, 0)) for _, dil in DILATED_CONFIGS)
    out_specs = (fm_spec(256), tm_spec(256), v_spec(128),
                 fm_spec(256), tm_spec(512), v_spec(256),
                 fm_spec(384), tm_spec(512), v_spec(256)) + dspecs * 3
    whole = lambda i, j: (0, 0)
    in_specs = [
        pl.BlockSpec((1, d, tm), lambda i, j: (i, 0, j)),
        pl.BlockSpec((IN_WIDTH, d), whole),
        pl.BlockSpec((N_HEADS * C_QK, C_Q_LORA), whole),
        pl.BlockSpec((N_HEADS * 128, C_KV_LORA), whole),
        pl.BlockSpec((G_TOTAL, 1), whole),
        pl.BlockSpec((64, tm), lambda i, j: (0, j)),
        pl.BlockSpec((64, tm), lambda i, j: (0, j)),
        pl.BlockSpec((32, tm), lambda i, j: (0, j)),
        pl.BlockSpec((32, tm), lambda i, j: (0, j)),
    ]
    return pl.pallas_call(
        _proj_body,
        out_shape=out_shape,
        grid=(b, nt),
        in_specs=in_specs,
        out_specs=out_specs,
        scratch_shapes=[pltpu.VMEM((D_WIDTH // LANES, tm, LANES), F32)],
        compiler_params=_cparams(("parallel", "arbitrary")),
        name="project_heads",
    )(xt, w_in_t, wqb_t, wkvb_t, gains, cosa, sina, cosm, sinm)


def _online_update(s, v_aug, carry, shift=None):
    m, acc = carry
    smax = jnp.max(s, axis=0, keepdims=True)
    if shift is None:
        m_new = jnp.maximum(m, smax)
        p = jnp.exp2(s - m_new).astype(BF16)
    else:
        m_new = jnp.maximum(m, smax + shift)
        p = jnp.exp2(s - (m_new - shift)).astype(BF16)
    alpha = jnp.exp2(m - m_new)
    acc = alpha * acc + jnp.dot(v_aug, p, preferred_element_type=F32)
    return m_new, acc


def _init_carry(dv, tq):
    return (jnp.full((1, tq), -jnp.inf, F32), jnp.zeros((dv + BF16_ROWS, tq), F32))


def _finish(carry, dv):
    _, acc = carry
    return acc[:dv] / acc[dv:dv + 1]


def _run_pipelined(n_jobs, n_blocks, scores_fn, update_fn, init_fn, finish_fn):
    steps = [(t, i) for t in range(n_jobs) for i in range(n_blocks)]
    pending = [scores_fn(*steps[n]) for n in range(min(SCORES_AHEAD, len(steps)))]
    carry = None
    for n, (t, i) in enumerate(steps):
        s = pending.pop(0)
        if n + SCORES_AHEAD < len(steps):
            pending.append(scores_fn(*steps[n + SCORES_AHEAD]))
        if i == 0:
            carry = init_fn()
        carry = update_fn(t, i, s, carry)
        if i == n_blocks - 1:
            finish_fn(t, carry)


def _dense_body(q_ref, k_ref, v_ref, o_ref):
    dq = q_ref.shape[1]
    nkb, dv, ks = v_ref.shape[1], v_ref.shape[2], v_ref.shape[3]
    tq = TQ
    ones = jnp.ones((BF16_ROWS, ks), BF16)
    zpad = jnp.zeros((LANES - dq, tq), BF16)
    qpads = [jnp.concatenate([q_ref[0, :, t * tq:(t + 1) * tq], zpad], axis=0) for t in range(q_ref.shape[2] // tq)]

    def scores(t, i):
        return jnp.dot(k_ref[0, i * ks:(i + 1) * ks, :], qpads[t], preferred_element_type=F32)

    def update(t, i, s, carry):
        return _online_update(s, jnp.concatenate([v_ref[0, i], ones], axis=0), carry)

    def finish(t, carry):
        o_ref[0, :, t * tq:(t + 1) * tq] = _finish(carry, dv).astype(o_ref.dtype)

    _run_pipelined(len(qpads), nkb, scores, update, lambda: _init_carry(dv, tq), finish)


def _dense_attention(q, k, v, dq, q_per_kv):
    b, rows, s = q.shape
    nh = rows // dq
    nkb, tk = v.shape[1], v.shape[3]
    return pl.pallas_call(
        _dense_body,
        out_shape=jax.ShapeDtypeStruct((b, nh * HEAD_DIM, s), BF16),
        grid=(b, nh, s // TQ_STEP),
        in_specs=[
            pl.BlockSpec((1, dq, TQ_STEP), lambda i, h, j: (i, h, j)),
            pl.BlockSpec((1, s, LANES), lambda i, h, j: (i, 0, h // q_per_kv)),
            pl.BlockSpec((1, nkb, HEAD_DIM, tk), lambda i, h, j: (i, 0, h // q_per_kv, 0)),
        ],
        out_specs=pl.BlockSpec((1, HEAD_DIM, TQ_STEP), lambda i, h, j: (i, h, j)),
        compiler_params=_cparams(("parallel", "arbitrary", "arbitrary")),
        name="dense_attention",
    )(q, k, v)


def _diff_body(sc_ref, q_ref, k_ref, v_ref, g_ref, bias_ref, o_ref):
    tq = TQ
    n_tiles = q_ref.shape[2] // tq
    nkb, dv, ks = v_ref.shape[1], v_ref.shape[2], v_ref.shape[3]
    hh = pl.program_id(1)
    lam = sc_ref[0]
    out_scale = sc_ref[1]
    slope = sc_ref[2 + hh]
    z32 = jnp.zeros((B_QK_DIM, tq), BF16)
    z48 = jnp.zeros((48, tq), BF16)
    zfeat = jnp.zeros((BF16_ROWS, tq), BF16)
    ones = jnp.ones((BF16_ROWS, ks), BF16)
    frow = lax.broadcasted_iota(jnp.int32, (BF16_ROWS, tq), 0)
    piece = jnp.where(frow % 3 == 0, LOG2E_PIECES[0], jnp.where(frow % 3 == 1, LOG2E_PIECES[1], LOG2E_PIECES[2]))
    cfeat = jnp.where(frow < 6, piece, 0.0)
    lane_pos = lax.broadcasted_iota(jnp.int32, (1, tq), 1)

    n_mixed = tq // ks

    def tile_index(job):
        return pl.program_id(2) * n_tiles + job // 2

    def block_index(job, ii):
        return lax.rem(tile_index(job) * n_mixed + ii, nkb)

    def side(job, ii):
        return jnp.where(block_index(job, ii) < tile_index(job) * n_mixed, 1.0, -1.0).astype(F32)

    def qrows(job):
        q = q_ref[0, :, (job // 2) * tq:(job // 2 + 1) * tq]
        if job % 2 == 0:
            return jnp.concatenate([q[:B_QK_DIM], z32], axis=0)
        return jnp.concatenate([z32, q[B_QK_DIM:]], axis=0)

    def scores(job, ii):
        i = block_index(job, ii)
        k = k_ref[0, pl.ds(pl.multiple_of(i * ks, ks), ks), :]
        feat = zfeat if ii < n_mixed else (cfeat * (side(job, ii) * slope)).astype(BF16)
        return jnp.dot(k, jnp.concatenate([qrows(job), feat, z48], axis=0), preferred_element_type=F32)

    def update(job, ii, s, carry):
        v_aug = jnp.concatenate([v_ref[0, block_index(job, ii)], ones], axis=0)
        if ii < n_mixed:
            return _online_update(s - bias_ref[0, ii], v_aug, carry)
        qpos = (tile_index(job) * tq + lane_pos).astype(F32)
        return _online_update(s, v_aug, carry, qpos * (-(side(job, ii) * slope * LOG2E)))

    res = []

    def finish(job, carry):
        res.append(_finish(carry, dv))
        if job % 2 == 1:
            o = res[job - 1] - lam * res[job]
            t = job // 2
            o_ref[0, :, t * tq:(t + 1) * tq] = (_rms_rows(o, g_ref[...]) * out_scale).astype(o_ref.dtype)

    _run_pipelined(2 * n_tiles, nkb, scores, update, lambda: _init_carry(dv, tq), finish)


def _diff_attention(scal, slopes, q, k, v, sub_g):
    b, rows, s = q.shape
    nkb, tk = v.shape[1], v.shape[3]
    n_mixed = TQ // tk
    jj = jnp.arange(tk)[None, :, None] + tk * jnp.arange(n_mixed)[:, None, None]
    dist = jnp.abs(jj - jnp.arange(TQ)[None, None, :]).astype(F32)
    bias = dist[None] * (LOG2E * jnp.asarray(slopes, F32))[:, None, None, None]
    return pl.pallas_call(
        _diff_body,
        out_shape=jax.ShapeDtypeStruct((b, rows, s), BF16),
        grid=(b, N_HEADS, s // TQ_STEP),
        in_specs=[
            pl.BlockSpec(memory_space=pltpu.SMEM),
            pl.BlockSpec((1, 2 * B_QK_DIM, TQ_STEP), lambda i, h, j: (i, h, j)),
            pl.BlockSpec((1, s, LANES), lambda i, h, j: (i, 0, h)),
            pl.BlockSpec((1, nkb, HEAD_DIM, tk), lambda i, h, j: (i, 0, h, 0)),
            pl.BlockSpec((HEAD_DIM, 1), lambda i, h, j: (0, 0)),
            pl.BlockSpec((1, n_mixed, tk, TQ), lambda i, h, j: (h, 0, 0, 0)),
        ],
        out_specs=pl.BlockSpec((1, HEAD_DIM, TQ_STEP), lambda i, h, j: (i, h, j)),
        compiler_params=_cparams(("parallel", "arbitrary", "arbitrary")),
        name="diff_attention",
    )(scal, q, k, v, sub_g, bias)


def _band_body(q_ref, k_ref, v_ref, o_ref, lse_ref, *, step_len, slopes, qb):
    l_len = k_ref.shape[1]
    win = min(qb + 2 * BAND_RADIUS, l_len)
    lane = lax.broadcasted_iota(jnp.int32, (qb, LANES), 1)
    row = lax.broadcasted_iota(jnp.int32, (qb, win), 0)
    colw = lax.broadcasted_iota(jnp.int32, (qb, win), 1)
    for sb in range(q_ref.shape[1] // qb):
        jb = pl.program_id(2) * (q_ref.shape[1] // qb) + sb
        start = jnp.clip(jb * qb - BAND_RADIUS, 0, l_len - win)
        start = pl.multiple_of(start, BAND_RADIUS)
        q = q_ref[0, sb * qb:(sb + 1) * qb, :]
        kwin = k_ref[0, pl.ds(start, win), :]
        vwin = v_ref[0, pl.ds(start, win), :]
        rel_i = jnp.abs((jb * qb - start) + row - colw)
        allowed = rel_i <= BAND_RADIUS
        rel = rel_i.astype(F32)
        for g in range(2):
            q2 = q[:, LANES * g:LANES * (g + 1)].astype(F32)
            k2 = kwin[:, LANES * g:LANES * (g + 1)]
            v2 = vwin[:, LANES * g:LANES * (g + 1)]
            out_g = jnp.zeros((qb, LANES), F32)
            lse_g = jnp.zeros((qb, LANES), F32)
            for hh in range(2):
                own = (lane >= HEAD_DIM * hh) & (lane < HEAD_DIM * (hh + 1))
                qm = jnp.where(own, q2, 0.0).astype(BF16)
                s = lax.dot_general(qm, k2, (((1,), (1,)), ((), ())), preferred_element_type=F32)
                s = s - (slopes[2 * g + hh] * step_len * LOG2E) * rel
                s = jnp.where(allowed, s, -jnp.inf)
                m = jnp.max(s, axis=1, keepdims=True)
                p = jnp.exp2(s - m)
                lsum = jnp.sum(p, axis=1, keepdims=True)
                o = jnp.dot(p.astype(BF16), v2, preferred_element_type=F32) / lsum
                lse = m + jnp.log2(lsum)
                out_g = jnp.where(own, o, out_g)
                lse_g = jnp.where(own, lse, lse_g)
            o_ref[0, sb * qb:(sb + 1) * qb, LANES * g:LANES * (g + 1)] = out_g
            lse_ref[0, sb * qb:(sb + 1) * qb, LANES * g:LANES * (g + 1)] = lse_g


def _band_attention(q, k, v, dil, slopes):
    b, l_len, cw = q.shape
    c = cw // dil
    qb = min(BAND_Q, l_len)
    qstep = min(BAND_QSTEP, l_len)
    body = functools.partial(_band_body, step_len=float(dil), slopes=slopes, qb=qb)
    return pl.pallas_call(
        body,
        out_shape=(jax.ShapeDtypeStruct((b, l_len, cw), F32),) * 2,
        grid=(b, dil, l_len // qstep),
        in_specs=[
            pl.BlockSpec((1, qstep, c), lambda i, r, j: (i, j, r)),
            pl.BlockSpec((1, l_len, c), lambda i, r, j: (i, 0, r)),
            pl.BlockSpec((1, l_len, c), lambda i, r, j: (i, 0, r)),
        ],
        out_specs=(pl.BlockSpec((1, qstep, c), lambda i, r, j: (i, j, r)),) * 2,
        compiler_params=_cparams(("parallel", "arbitrary", "arbitrary")),
        name="band_attention_d%d" % dil,
    )(q, k, v)


def _band_combine_body(*refs):
    n = len(DILATED_CONFIGS)
    in_refs, o_ref, scr_ref = refs[:2 * n], refs[2 * n], refs[2 * n + 1]
    tm, c = o_ref.shape[1], o_ref.shape[2]
    vals = []
    for idx, (_, dil) in enumerate(DILATED_CONFIGS):
        for which in range(2):
            src = in_refs[2 * idx + which]
            if dil == 1:
                vals.append(src[0])
                continue
            groups = []
            for g in range(c // LANES):
                slot = scr_ref.at[(2 * idx + which) * (c // LANES) + g]
                for r in range(dil):
                    lo = c * r + LANES * g
                    slot[pl.ds(r, tm // dil, stride=dil), :] = src[0, :, lo:lo + LANES]
                groups.append(slot[...])
            vals.append(jnp.concatenate(groups, axis=1))
    outs, lses = vals[0::2], vals[1::2]
    m = functools.reduce(jnp.maximum, lses)
    ws = [jnp.exp2(x - m) for x in lses]
    num = functools.reduce(lambda a, t: a + t, [w * o for w, o in zip(ws, outs)])
    den = functools.reduce(lambda a, t: a + t, ws)
    o_ref[0] = (num / den).astype(o_ref.dtype)


def _band_combine(pairs):
    b, s, c = pairs[0][0].shape
    tm = TOK_TILE
    specs = []
    for (_, dil) in DILATED_CONFIGS:
        specs += [pl.BlockSpec((1, tm // dil, dil * c), lambda i, j: (i, j, 0))] * 2
    flat = [a for p in pairs for a in p]
    return pl.pallas_call(
        _band_combine_body,
        out_shape=jax.ShapeDtypeStruct((b, s, c), BF16),
        grid=(b, s // tm),
        in_specs=specs,
        out_specs=pl.BlockSpec((1, tm, c), lambda i, j: (i, j, 0)),
        scratch_shapes=[pltpu.VMEM((2 * len(DILATED_CONFIGS) * (c // LANES), tm, LANES), F32)],
        compiler_params=_cparams(("parallel", "arbitrary")),
        name="band_combine",
    )(*flat)


def _outproj_body(x_ref, oa_ref, ob_ref, oc_ref, od_ref, w_ref, g_ref, x1_ref, h2_ref, edge_ref):
    d, tm = x_ref.shape[1], x_ref.shape[2]
    j = pl.program_id(1)
    mix = jnp.concatenate([oa_ref[0], ob_ref[0], oc_ref[0]], axis=0)
    w = w_ref[...]
    y = jnp.dot(w[:, :768], mix, preferred_element_type=F32)
    y = y + lax.dot_general(w[:, 768:], od_ref[0], (((1,), (1,)), ((), ())), preferred_element_type=F32)
    x1 = x_ref[0] + y
    h2 = _rms_rows(x1, g_ref[...])
    x1_ref[0] = x1.T
    h2t = h2.T
    h2_ref[0] = h2t.astype(BF16)
    edge_ref[0, pl.ds(2 * j, 1), :] = h2t[0:1].astype(BF16).astype(F32)
    edge_ref[0, pl.ds(2 * j + 1, 1), :] = h2t[tm - 1:tm].astype(BF16).astype(F32)


def _out_project(xt, oa, ob, oc, od, w_out_t, g2):
    b, d, s = xt.shape
    tm = TOK_TILE
    nt = s // tm
    fm = lambda rows: pl.BlockSpec((1, rows, tm), lambda i, j: (i, 0, j))
    tmj = pl.BlockSpec((1, tm, d), lambda i, j: (i, j, 0))
    return pl.pallas_call(
        _outproj_body,
        out_shape=(jax.ShapeDtypeStruct((b, s, d), F32), jax.ShapeDtypeStruct((b, s, d), BF16),
                   jax.ShapeDtypeStruct((b, 2 * nt, d), F32)),
        grid=(b, nt),
        in_specs=[fm(d), fm(256), fm(256), fm(256),
                  pl.BlockSpec((1, tm, D_WIDTH), lambda i, j: (i, j, 0)),
                  pl.BlockSpec((d, d), lambda i, j: (0, 0)),
                  pl.BlockSpec((d, 1), lambda i, j: (0, 0))],
        out_specs=(tmj, tmj, pl.BlockSpec((1, 2 * nt, d), lambda i, j: (i, 0, 0))),
        compiler_params=_cparams(("parallel", "arbitrary")),
        name="out_project",
    )(xt, oa, ob, oc, od, w_out_t, g2)


def _halo_body(h_ref, w_ref, o_ref):
    o_ref[...] = jnp.dot(h_ref[...], w_ref[...], preferred_element_type=F32)


def _halo_up(halo_h, w_up):
    nh, d = halo_h.shape
    ff2 = w_up.shape[1]
    n_blk = 4
    cols = ff2 // n_blk
    return pl.pallas_call(
        _halo_body,
        out_shape=jax.ShapeDtypeStruct((nh, ff2), F32),
        grid=(n_blk,),
        in_specs=[pl.BlockSpec((nh, d), lambda i: (0, 0)),
                  pl.BlockSpec((d, cols), lambda i: (0, i))],
        out_specs=pl.BlockSpec((nh, cols), lambda i: (0, i)),
        compiler_params=_cparams(("arbitrary",)),
        name="mlp_halo_up",
    )(halo_h, w_up)


def _mlp_body(x1_ref, h2_ref, wup_ref, wdn_ref, par_ref, halo_ref, o_ref, gbuf_ref, vbuf_ref, *, chunks):
    tf = h2_ref.shape[1]
    ff = wdn_ref.shape[0]
    nt = pl.num_programs(1)
    row = (pl.program_id(0) * nt + pl.program_id(1)) * 2
    h = h2_ref[0]

    def conv(buf_ref, u, c0, width):
        buf_ref[MLP_PAD:MLP_PAD + tf, :width] = u
        buf_ref[MLP_PAD - 1:MLP_PAD, :width] = halo_ref[pl.ds(row, 1), c0:c0 + width]
        buf_ref[MLP_PAD + tf:MLP_PAD + tf + 1, :width] = halo_ref[pl.ds(row + 1, 1), c0:c0 + width]
        prev = buf_ref[MLP_PAD - 1:MLP_PAD - 1 + tf, :width]
        nxt = buf_ref[MLP_PAD + 1:MLP_PAD + 1 + tf, :width]
        p = par_ref[:, c0:c0 + width]
        return p[0:1] * prev + p[1:2] * u + p[2:3] * nxt + p[3:4]

    y = x1_ref[0]
    for c0, width in chunks:
        ug = jnp.dot(h, wup_ref[:, c0:c0 + width], preferred_element_type=F32)
        uv = jnp.dot(h, wup_ref[:, ff + c0:ff + c0 + width], preferred_element_type=F32)
        gate = conv(gbuf_ref, ug, c0, width)
        val = conv(vbuf_ref, uv, ff + c0, width)
        act = gate / (1.0 + jnp.exp(-gate)) * val
        y = y + jnp.dot(act.astype(BF16), wdn_ref[c0:c0 + width, :], preferred_element_type=F32)
    o_ref[0] = y.T


def _mlp(x1, h2, w_up, w_down, conv_par, halo_up, chunks):
    b, s, d = x1.shape
    tf = TOK_TILE
    ff2 = w_up.shape[1]
    nh = halo_up.shape[0]
    whole = lambda i, j: (0, 0)
    body = functools.partial(_mlp_body, chunks=chunks)
    const = dict(pipeline_mode=pl.Buffered(1))
    wmax = max(w for _, w in chunks)
    return pl.pallas_call(
        body,
        out_shape=jax.ShapeDtypeStruct((b, d, s), F32),
        grid=(b, s // tf),
        in_specs=[pl.BlockSpec((1, tf, d), lambda i, j: (i, j, 0)),
                  pl.BlockSpec((1, tf, d), lambda i, j: (i, j, 0)),
                  pl.BlockSpec((d, ff2), whole, **const),
                  pl.BlockSpec((ff2 // 2, d), whole, **const),
                  pl.BlockSpec((8, ff2), whole, **const),
                  pl.BlockSpec((nh, ff2), whole, **const)],
        out_specs=pl.BlockSpec((1, d, tf), lambda i, j: (i, 0, j)),
        scratch_shapes=[pltpu.VMEM((tf + 2 * MLP_PAD, wmax), F32)] * 2,
        compiler_params=_cparams(("arbitrary", "arbitrary")),
        name="gated_conv_mlp",
    )(x1, h2, w_up, w_down, conv_par, halo_up)


def _halo_rows(edges, nt):
    b, _, d = edges.shape
    ez = jnp.concatenate([edges, jnp.zeros((b, 1, d), edges.dtype)], axis=1)
    idx = []
    for j in range(nt):
        idx += [2 * j - 1 if j > 0 else 2 * nt, 2 * j + 2 if j < nt - 1 else 2 * nt]
    rows = jnp.take(ez, jnp.asarray(idx, jnp.int32), axis=1).reshape(b * nt * 2, d)
    pad = (-rows.shape[0]) % BF16_ROWS
    return jnp.pad(rows, ((0, pad), (0, 0))).astype(BF16)


def _rope_tables(s):
    def freqs(dim):
        return 1.0 / (ROPE_THETA ** (jnp.arange(0, dim, 2, dtype=F32) / dim))
    t = jnp.arange(s)
    rows = (t // GRID_W).astype(F32)
    cols = (t % GRID_W).astype(F32)
    pos = t.astype(F32)
    fa = freqs(HEAD_DIM // 2)
    ar, ac = fa[:, None] * rows[None, :], fa[:, None] * cols[None, :]
    am = freqs(C_ROPE)[:, None] * pos[None, :]
    cosa = jnp.concatenate([jnp.cos(ar), jnp.cos(ar), jnp.cos(ac), jnp.cos(ac)], axis=0)
    sina = jnp.concatenate([-jnp.sin(ar), jnp.sin(ar), -jnp.sin(ac), jnp.sin(ac)], axis=0)
    cosm = jnp.concatenate([jnp.cos(am), jnp.cos(am)], axis=0)
    sinm = jnp.concatenate([-jnp.sin(am), jnp.sin(am)], axis=0)
    return cosa, sina, cosm, sinm


def kernel(x, norm1_g, w_in, a_qn_g, a_kn_g, b_qn_g, b_kn_g, b_lam_q1, b_lam_k1, b_lam_q2, b_lam_k2, b_sub_g,
           c_qa_g, c_kva_g, c_wqb, c_wkvb, c_qn_g, c_kn_g, d_qn_g, d_kn_g, w_out, norm2_g, w_up, conv_w, conv_b,
           w_down):
    b, s, d = x.shape
    depth = w_in.shape[0]
    ff = w_down.shape[1]
    nt = s // TOK_TILE
    assert s % TOK_TILE == 0 and TOK_TILE % KSTEP == 0 and TQ % KSTEP == 0 and s % TQ == 0 and s % GRID_W == 0
    assert s <= 1 << (POS_SPLIT_BITS + BF16_MANTISSA_BITS)
    for window, dil in DILATED_CONFIGS:
        assert window // (2 * dil) == BAND_RADIUS and (s // dil) % min(BAND_Q, s // dil) == 0
        assert (s // dil) % min(BAND_QSTEP, s // dil) == 0 and BAND_QSTEP % BAND_Q == 0
        assert TOK_TILE % (dil * BF16_ROWS) == 0

    tr = lambda w: jnp.swapaxes(w, 1, 2).astype(BF16)
    gains = jnp.concatenate([norm1_g, a_qn_g, a_kn_g, b_qn_g, b_kn_g, c_qa_g, c_kva_g, c_qn_g, c_kn_g,
                             d_qn_g, d_kn_g], axis=1)[:, :, None]
    conv_par = jnp.concatenate([conv_w, conv_b[:, None, :], jnp.zeros((depth, 4, 2 * ff), F32)], axis=1)
    slopes = [2.0 ** (-8.0 * i / (2 * N_HEADS)) for i in range(1, 2 * N_HEADS + 1)]
    slopes_b, slopes_d = slopes[:N_HEADS], tuple(slopes[N_HEADS:])
    lam_init = jnp.asarray([0.8 - 0.6 * math.exp(-0.3 * l) for l in range(depth)], F32)
    lam = (jnp.exp(jnp.sum(b_lam_q1 * b_lam_k1, axis=1)) - jnp.exp(jnp.sum(b_lam_q2 * b_lam_k2, axis=1)) + lam_init)
    scal = jnp.concatenate([lam[:, None], (1.0 - lam_init)[:, None],
                            jnp.broadcast_to(jnp.asarray(slopes_b, F32), (depth, N_HEADS)),
                            jnp.zeros((depth, 2), F32)], axis=1)
    layers = dict(w_in=tr(w_in), wqb=tr(c_wqb), wkvb=tr(c_wkvb), w_out=tr(w_out),
                  w_up=w_up.astype(BF16), w_down=w_down.astype(BF16),
                  gains=gains, g2=norm2_g[:, :, None], sub_g=b_sub_g[:, :, None], conv_par=conv_par, scal=scal)
    cosa, sina, cosm, sinm = _rope_tables(s)
    assert ff % LANES == 0
    chunks = tuple((c0, min(MLP_CHUNK, ff - c0)) for c0 in range(0, ff, MLP_CHUNK))

    def layer(xt, p):
        (qa, ka, va, qb, kb, vb, qc, kc, vc, dq1, dq4, dq16, dk1, dk4, dk16, dv1, dv4, dv16) = _project(
            xt, p["w_in"], p["wqb"], p["wkvb"], p["gains"], cosa, sina, cosm, sinm)
        oa = _dense_attention(qa, ka, va, HEAD_DIM, N_HEADS // A_KV_HEADS)
        ob = _diff_attention(p["scal"], slopes_b, qb, kb, vb, p["sub_g"])
        oc = _dense_attention(qc, kc, vc, C_QK, 1)
        branches = [_band_attention(q_, k_, v_, dil, slopes_d)
                    for (_, dil), q_, k_, v_ in zip(DILATED_CONFIGS, (dq1, dq4, dq16), (dk1, dk4, dk16), (dv1, dv4, dv16))]
        od = _band_combine(branches)
        x1, h2, edges = _out_project(xt, oa, ob, oc, od, p["w_out"], p["g2"])
        halo = _halo_up(_halo_rows(edges, nt), p["w_up"])
        return _mlp(x1, h2, p["w_up"], p["w_down"], p["conv_par"], halo, chunks), None

    xt = _transpose_last2(x, TOK_TILE, d)
    xt, _ = lax.scan(layer, xt, layers)
    return _transpose_last2(xt, d, TOK_TILE)
```

```python
import functools
import math

import jax
import jax.numpy as jnp
import numpy as np
from jax import lax
from jax.experimental import pallas as pl
from jax.experimental.pallas import tpu as pltpu

F32 = jnp.float32
BF16 = jnp.bfloat16

HEAD_DIM = 64
N_HEADS = 4
A_KV_HEADS = 2
B_QK_DIM = 32
C_Q_LORA = 256
C_KV_LORA = 128
C_NOPE = 64
C_ROPE = 32
C_QK = C_NOPE + C_ROPE
DILATED_CONFIGS = ((128, 1), (512, 4), (2048, 16))
D_WIDTH = N_HEADS * HEAD_DIM
GRID_W = 64
ROPE_THETA = 10000.0
NORM_EPS = 1e-6
LOG2E = 1.4426950408889634


def _bf16_pieces(x, n):
    out = []
    for _ in range(n):
        u = np.float32(x).view(np.uint32)
        u = np.uint32((int(u) + 0x7FFF + ((int(u) >> 16) & 1)) & 0xFFFF0000)
        piece = float(u.view(np.float32))
        out.append(piece)
        x = x - piece
    return tuple(out)


LOG2E_PIECES = _bf16_pieces(LOG2E, 3)
POS_SPLIT_BITS = 6
BF16_MANTISSA_BITS = 8

_SPLITS = (256, 128, 128, 256, 256, 256, C_Q_LORA, C_KV_LORA, C_ROPE, 256, 256, 256)
_OFF = [0]
for _n in _SPLITS:
    _OFF.append(_OFF[-1] + _n)
(O_AQ, O_AK, O_AV, O_BQ, O_BK, O_BV, O_CQL, O_CKVL, O_CKR, O_DQ, O_DK, O_DV, IN_WIDTH) = _OFF

_GAIN_SIZES = (("n1", 1024), ("aq", 64), ("ak", 64), ("bq", 32), ("bk", 32), ("cqa", 256), ("ckva", 128),
               ("cqn", 96), ("ckn", 96), ("dq", 64), ("dk", 64))
G_OFF = {}
_o = 0
for _k, _n in _GAIN_SIZES:
    G_OFF[_k] = (_o, _n)
    _o += _n
G_TOTAL = _o

LANES = 128
BF16_ROWS = 16
TOK_TILE = 512
MLP_CHUNK = 1024
MLP_PAD = 8
TQ = 512
TQ_STEP = 2048
KSTEP = 256
SCORES_AHEAD = 2
BAND_Q = 128
BAND_QSTEP = 1024
BAND_MASKED = 1e30
BAND_RADIUS = 64
VMEM_LIMIT = 56 * 1024 * 1024


def _cparams(sem, flags=None):
    return pltpu.CompilerParams(dimension_semantics=sem, vmem_limit_bytes=VMEM_LIMIT, flags=flags)


def _rms_rows(v, g_col):
    ms = jnp.mean(v * v, axis=0, keepdims=True)
    return v * lax.rsqrt(ms + NORM_EPS) * g_col


def _swap_halves(x, n):
    parts = []
    for i in range(0, x.shape[0], 2 * n):
        parts.append(x[i + n:i + 2 * n])
        parts.append(x[i:i + n])
    return jnp.concatenate(parts, axis=0)


def _rope_rows(x, cos, sin_signed, n):
    return x * cos + _swap_halves(x, n) * sin_signed


def _transpose_body(x_ref, o_ref):
    o_ref[0] = x_ref[0].T


def _transpose_last2(x, tr, tc):
    b, r, c = x.shape
    return pl.pallas_call(
        _transpose_body,
        out_shape=jax.ShapeDtypeStruct((b, c, r), x.dtype),
        grid=(b, r // tr, c // tc),
        in_specs=[pl.BlockSpec((1, tr, tc), lambda i, j, k: (i, j, k))],
        out_specs=pl.BlockSpec((1, tc, tr), lambda i, j, k: (i, k, j)),
        compiler_params=_cparams(("parallel", "arbitrary", "arbitrary")),
        name="transpose_tokens",
    )(x)


def _store_dilated(scr_ref, val, out_refs):
    tm, c = val.shape
    for g in range(c // LANES):
        scr_ref[g] = val[:, LANES * g:LANES * (g + 1)]
    for (_, dil), out in zip(DILATED_CONFIGS, out_refs):
        if dil == 1:
            out[0] = val.astype(BF16)
            continue
        n = tm // dil
        for r in range(dil):
            for g in range(c // LANES):
                lo = c * r + LANES * g
                out[0, :, lo:lo + LANES] = scr_ref[g, pl.ds(r, n, stride=dil), :].astype(BF16)


def _store_key_blocks(v_ref, v):
    for c in range(v.shape[1] // KSTEP):
        v_ref[0, c] = v[:, c * KSTEP:(c + 1) * KSTEP]


def _proj_body(x_ref, w_ref, wqb_ref, wkvb_ref, g_ref, cosa_ref, sina_ref, cosm_ref, sinm_ref,
               qa_ref, ka_ref, va_ref, qb_ref, kb_ref, vb_ref, qc_ref, kc_ref, vc_ref,
               dq1_ref, dq4_ref, dq16_ref, dk1_ref, dk4_ref, dk16_ref, dv1_ref, dv4_ref, dv16_ref,
               scr_ref):
    tm = x_ref.shape[2]

    def gain(name):
        o, n = G_OFF[name]
        return g_ref[o:o + n, :]

    x = x_ref[0]
    h = _rms_rows(x, gain("n1")).astype(BF16)

    def section(lo, hi):
        part = jnp.dot(w_ref[lo:hi, :], h, preferred_element_type=F32)
        return lambda o, n: part[o - lo:o - lo + n]

    cosa, sina = cosa_ref[...], sina_ref[...]
    cosm, sinm = cosm_ref[...], sinm_ref[...]
    zeros64 = jnp.zeros((64, tm), F32)
    zeros32 = jnp.zeros((32, tm), F32)

    def prep_a(proj):
        sa = HEAD_DIM ** -0.5 * LOG2E
        for hh in range(N_HEADS):
            q = _rope_rows(_rms_rows(proj(O_AQ + 64 * hh, 64), gain("aq")), cosa, sina, 16) * sa
            qa_ref[0, 64 * hh:64 * hh + 64, :] = q.astype(BF16)
        for g in range(A_KV_HEADS):
            k = _rope_rows(_rms_rows(proj(O_AK + 64 * g, 64), gain("ak")), cosa, sina, 16)
            kpad = jnp.concatenate([k, zeros64], axis=0)
            ka_ref[0, :, LANES * g:LANES * (g + 1)] = kpad.T.astype(BF16)
        _store_key_blocks(va_ref, proj(O_AV, 128).astype(BF16))

    def prep_b(proj):
        sb = B_QK_DIM ** -0.5 * LOG2E
        for i in range(2 * N_HEADS):
            qb_ref[0, 32 * i:32 * i + 32, :] = (_rms_rows(proj(O_BQ + 32 * i, 32), gain("bq")) * sb).astype(BF16)
        kpos = pl.program_id(1) * tm + lax.broadcasted_iota(jnp.int32, (BF16_ROWS, tm), 1)
        frow = lax.broadcasted_iota(jnp.int32, (BF16_ROWS, tm), 0)
        pos_hi = ((kpos >> POS_SPLIT_BITS) << POS_SPLIT_BITS).astype(F32)
        pos_lo = (kpos & ((1 << POS_SPLIT_BITS) - 1)).astype(F32)
        kfeat = jnp.where(frow < 3, pos_hi, jnp.where(frow < 6, pos_lo, 0.0))
        zeros48 = jnp.zeros((48, tm), F32)
        for hh in range(N_HEADS):
            k1 = _rms_rows(proj(O_BK + 64 * hh, 32), gain("bk"))
            k2 = _rms_rows(proj(O_BK + 64 * hh + 32, 32), gain("bk"))
            kpad = jnp.concatenate([k1, k2, kfeat, zeros48], axis=0)
            kb_ref[0, :, LANES * hh:LANES * (hh + 1)] = kpad.T.astype(BF16)
        _store_key_blocks(vb_ref, proj(O_BV, 256).astype(BF16))

    def prep_c(proj):
        sc = C_QK ** -0.5 * LOG2E
        cq = _rms_rows(proj(O_CQL, C_Q_LORA), gain("cqa")).astype(BF16)
        qc = jnp.dot(wqb_ref[...], cq, preferred_element_type=F32)
        ckv = _rms_rows(proj(O_CKVL, C_KV_LORA), gain("ckva")).astype(BF16)
        kvc = jnp.dot(wkvb_ref[...], ckv, preferred_element_type=F32)
        kr = proj(O_CKR, C_ROPE)
        vcs = []
        for hh in range(N_HEADS):
            q = _rms_rows(qc[C_QK * hh:C_QK * (hh + 1)], gain("cqn"))
            q = jnp.concatenate([q[:C_NOPE], _rope_rows(q[C_NOPE:], cosm, sinm, 16)], axis=0) * sc
            qc_ref[0, C_QK * hh:C_QK * (hh + 1), :] = q.astype(BF16)
            kk = jnp.concatenate([kvc[128 * hh:128 * hh + C_NOPE], kr], axis=0)
            kk = _rms_rows(kk, gain("ckn"))
            kk = jnp.concatenate([kk[:C_NOPE], _rope_rows(kk[C_NOPE:], cosm, sinm, 16), zeros32], axis=0)
            kc_ref[0, :, LANES * hh:LANES * (hh + 1)] = kk.T.astype(BF16)
            vcs.append(kvc[128 * hh + C_NOPE:128 * (hh + 1)])
        _store_key_blocks(vc_ref, jnp.concatenate(vcs, axis=0).astype(BF16))

    def prep_d(proj):
        sd = HEAD_DIM ** -0.5 * LOG2E
        qs, ks, vs = [], [], []
        for g in range(2):
            qg, kg = [], []
            for hh in (2 * g, 2 * g + 1):
                qg.append(_rms_rows(proj(O_DQ + 64 * hh, 64), gain("dq")) * sd)
                kg.append(_rms_rows(proj(O_DK + 64 * hh, 64), gain("dk")))
            qs.append(jnp.concatenate(qg, axis=0).T)
            ks.append(jnp.concatenate(kg, axis=0).T)
            vs.append(proj(O_DV + 128 * g, 128).T)
        _store_dilated(scr_ref.at[0], jnp.concatenate(qs, axis=1), (dq1_ref, dq4_ref, dq16_ref))
        _store_dilated(scr_ref.at[1], jnp.concatenate(ks, axis=1), (dk1_ref, dk4_ref, dk16_ref))
        _store_dilated(scr_ref.at[2], jnp.concatenate(vs, axis=1), (dv1_ref, dv4_ref, dv16_ref))

    sec_a = section(O_AQ, O_BQ)
    sec_b = section(O_BQ, O_CQL)
    prep_a(sec_a)
    sec_c = section(O_CQL, O_DQ)
    prep_b(sec_b)
    sec_d = section(O_DQ, IN_WIDTH)
    prep_c(sec_c)
    prep_d(sec_d)


def _project(xt, w_in_t, wqb_t, wkvb_t, gains, cosa, sina, cosm, sinm):
    b, d, s = xt.shape
    tm = TOK_TILE
    nt = s // tm
    fm = lambda rows: jax.ShapeDtypeStruct((b, rows, s), BF16)
    tmj = lambda cols: jax.ShapeDtypeStruct((b, s, cols), BF16)
    vblk = lambda rows: jax.ShapeDtypeStruct((b, s // KSTEP, rows, KSTEP), BF16)
    dviews = tuple(jax.ShapeDtypeStruct((b, s // dil, dil * D_WIDTH), BF16) for _, dil in DILATED_CONFIGS)
    out_shape = (fm(256), tmj(256), vblk(128),
                 fm(256), tmj(512), vblk(256),
                 fm(384), tmj(512), vblk(256)) + dviews * 3
    fm_spec = lambda rows: pl.BlockSpec((1, rows, tm), lambda i, j: (i, 0, j))
    tm_spec = lambda cols: pl.BlockSpec((1, tm, cols), lambda i, j: (i, j, 0))
    v_spec = lambda rows: pl.BlockSpec((1, tm // KSTEP, rows, KSTEP), lambda i, j: (i, j, 0, 0))
    dspecs = tuple(pl.BlockSpec((1, tm // dil, dil * D_WIDTH), lambda i, j: (i, j, 0)) for _, dil in DILATED_CONFIGS)
    out_specs = (fm_spec(256), tm_spec(256), v_spec(128),
                 fm_spec(256), tm_spec(512), v_spec(256),
                 fm_spec(384), tm_spec(512), v_spec(256)) + dspecs * 3
    whole = lambda i, j: (0, 0)
    in_specs = [
        pl.BlockSpec((1, d, tm), lambda i, j: (i, 0, j)),
        pl.BlockSpec((IN_WIDTH, d), whole),
        pl.BlockSpec((N_HEADS * C_QK, C_Q_LORA), whole),
        pl.BlockSpec((N_HEADS * 128, C_KV_LORA), whole),
        pl.BlockSpec((G_TOTAL, 1), whole),
        pl.BlockSpec((64, tm), lambda i, j: (0, j)),
        pl.BlockSpec((64, tm), lambda i, j: (0, j)),
        pl.BlockSpec((32, tm), lambda i, j: (0, j)),
        pl.BlockSpec((32, tm), lambda i, j: (0, j)),
    ]
    return pl.pallas_call(
        _proj_body,
        out_shape=out_shape,
        grid=(b, nt),
        in_specs=in_specs,
        out_specs=out_specs,
        scratch_shapes=[pltpu.VMEM((3, D_WIDTH // LANES, tm, LANES), F32)],
        compiler_params=_cparams(("parallel", "arbitrary")),
        name="project_heads",
    )(xt, w_in_t, wqb_t, wkvb_t, gains, cosa, sina, cosm, sinm)


def _online_update(s, v_aug, carry, shift=None):
    m, acc = carry
    smax = jnp.max(s, axis=0, keepdims=True)
    if shift is None:
        m_new = jnp.maximum(m, smax)
        p = jnp.exp2(s - m_new).astype(BF16)
    else:
        m_new = jnp.maximum(m, smax + shift)
        p = jnp.exp2(s - (m_new - shift)).astype(BF16)
    alpha = jnp.exp2(m - m_new)
    acc = alpha * acc + jnp.dot(v_aug, p, preferred_element_type=F32)
    return m_new, acc


def _init_carry(dv, tq):
    return (jnp.full((1, tq), -jnp.inf, F32), jnp.zeros((dv + BF16_ROWS, tq), F32))


def _finish(carry, dv):
    _, acc = carry
    return acc[:dv] / acc[dv:dv + 1]


def _run_pipelined(n_jobs, n_blocks, scores_fn, update_fn, init_fn, finish_fn):
    steps = [(t, i) for t in range(n_jobs) for i in range(n_blocks)]
    pending = [scores_fn(*steps[n]) for n in range(min(SCORES_AHEAD, len(steps)))]
    carry = None
    for n, (t, i) in enumerate(steps):
        s = pending.pop(0)
        if n + SCORES_AHEAD < len(steps):
            pending.append(scores_fn(*steps[n + SCORES_AHEAD]))
        if i == 0:
            carry = init_fn()
        carry = update_fn(t, i, s, carry)
        if i == n_blocks - 1:
            finish_fn(t, carry)


def _dense_body(q_ref, k_ref, v_ref, o_ref):
    dq = q_ref.shape[1]
    nkb, dv, ks = v_ref.shape[1], v_ref.shape[2], v_ref.shape[3]
    tq = TQ
    ones = jnp.ones((BF16_ROWS, ks), BF16)
    zpad = jnp.zeros((LANES - dq, tq), BF16)
    qpads = [jnp.concatenate([q_ref[0, :, t * tq:(t + 1) * tq], zpad], axis=0) for t in range(q_ref.shape[2] // tq)]

    def scores(t, i):
        return jnp.dot(k_ref[0, i * ks:(i + 1) * ks, :], qpads[t], preferred_element_type=F32)

    def update(t, i, s, carry):
        return _online_update(s, jnp.concatenate([v_ref[0, i], ones], axis=0), carry)

    def finish(t, carry):
        o_ref[0, :, t * tq:(t + 1) * tq] = _finish(carry, dv).astype(o_ref.dtype)

    _run_pipelined(len(qpads), nkb, scores, update, lambda: _init_carry(dv, tq), finish)


def _dense_attention(q, k, v, dq, q_per_kv):
    b, rows, s = q.shape
    nh = rows // dq
    nkb, tk = v.shape[1], v.shape[3]
    return pl.pallas_call(
        _dense_body,
        out_shape=jax.ShapeDtypeStruct((b, nh * HEAD_DIM, s), BF16),
        grid=(b, nh, s // TQ_STEP),
        in_specs=[
            pl.BlockSpec((1, dq, TQ_STEP), lambda i, h, j: (i, h, j)),
            pl.BlockSpec((1, s, LANES), lambda i, h, j: (i, 0, h // q_per_kv)),
            pl.BlockSpec((1, nkb, HEAD_DIM, tk), lambda i, h, j: (i, 0, h // q_per_kv, 0)),
        ],
        out_specs=pl.BlockSpec((1, HEAD_DIM, TQ_STEP), lambda i, h, j: (i, h, j)),
        compiler_params=_cparams(("parallel", "arbitrary", "arbitrary")),
        name="dense_attention",
    )(q, k, v)


def _diff_body(sc_ref, q_ref, k_ref, v_ref, g_ref, bias_ref, o_ref):
    tq = TQ
    n_tiles = q_ref.shape[2] // tq
    nkb, dv, ks = v_ref.shape[1], v_ref.shape[2], v_ref.shape[3]
    hh = pl.program_id(1)
    lam = sc_ref[0]
    out_scale = sc_ref[1]
    slope = sc_ref[2 + hh]
    z32 = jnp.zeros((B_QK_DIM, tq), BF16)
    z48 = jnp.zeros((48, tq), BF16)
    zfeat = jnp.zeros((BF16_ROWS, tq), BF16)
    ones = jnp.ones((BF16_ROWS, ks), BF16)
    frow = lax.broadcasted_iota(jnp.int32, (BF16_ROWS, tq), 0)
    piece = jnp.where(frow % 3 == 0, LOG2E_PIECES[0], jnp.where(frow % 3 == 1, LOG2E_PIECES[1], LOG2E_PIECES[2]))
    cfeat = jnp.where(frow < 6, piece, 0.0)
    lane_pos = lax.broadcasted_iota(jnp.int32, (1, tq), 1)

    n_mixed = tq // ks

    def tile_index(job):
        return pl.program_id(2) * n_tiles + job // 2

    def block_index(job, ii):
        return lax.rem(tile_index(job) * n_mixed + ii, nkb)

    def side(job, ii):
        return jnp.where(block_index(job, ii) < tile_index(job) * n_mixed, 1.0, -1.0).astype(F32)

    def qrows(job):
        q = q_ref[0, :, (job // 2) * tq:(job // 2 + 1) * tq]
        if job % 2 == 0:
            return jnp.concatenate([q[:B_QK_DIM], z32], axis=0)
        return jnp.concatenate([z32, q[B_QK_DIM:]], axis=0)

    def scores(job, ii):
        i = block_index(job, ii)
        k = k_ref[0, pl.ds(pl.multiple_of(i * ks, ks), ks), :]
        feat = zfeat if ii < n_mixed else (cfeat * (side(job, ii) * slope)).astype(BF16)
        return jnp.dot(k, jnp.concatenate([qrows(job), feat, z48], axis=0), preferred_element_type=F32)

    def update(job, ii, s, carry):
        v_aug = jnp.concatenate([v_ref[0, block_index(job, ii)], ones], axis=0)
        if ii < n_mixed:
            return _online_update(s - bias_ref[0, ii], v_aug, carry)
        qpos = (tile_index(job) * tq + lane_pos).astype(F32)
        return _online_update(s, v_aug, carry, qpos * (-(side(job, ii) * slope * LOG2E)))

    res = []

    def finish(job, carry):
        res.append(_finish(carry, dv))
        if job % 2 == 1:
            o = res[job - 1] - lam * res[job]
            t = job // 2
            o_ref[0, :, t * tq:(t + 1) * tq] = (_rms_rows(o, g_ref[...]) * out_scale).astype(o_ref.dtype)

    _run_pipelined(2 * n_tiles, nkb, scores, update, lambda: _init_carry(dv, tq), finish)


def _diff_attention(scal, slopes, q, k, v, sub_g):
    b, rows, s = q.shape
    nkb, tk = v.shape[1], v.shape[3]
    n_mixed = TQ // tk
    jj = jnp.arange(tk)[None, :, None] + tk * jnp.arange(n_mixed)[:, None, None]
    dist = jnp.abs(jj - jnp.arange(TQ)[None, None, :]).astype(F32)
    bias = dist[None] * (LOG2E * jnp.asarray(slopes, F32))[:, None, None, None]
    return pl.pallas_call(
        _diff_body,
        out_shape=jax.ShapeDtypeStruct((b, rows, s), BF16),
        grid=(b, N_HEADS, s // TQ_STEP),
        in_specs=[
            pl.BlockSpec(memory_space=pltpu.SMEM),
            pl.BlockSpec((1, 2 * B_QK_DIM, TQ_STEP), lambda i, h, j: (i, h, j)),
            pl.BlockSpec((1, s, LANES), lambda i, h, j: (i, 0, h)),
            pl.BlockSpec((1, nkb, HEAD_DIM, tk), lambda i, h, j: (i, 0, h, 0)),
            pl.BlockSpec((HEAD_DIM, 1), lambda i, h, j: (0, 0)),
            pl.BlockSpec((1, n_mixed, tk, TQ), lambda i, h, j: (h, 0, 0, 0)),
        ],
        out_specs=pl.BlockSpec((1, HEAD_DIM, TQ_STEP), lambda i, h, j: (i, h, j)),
        compiler_params=_cparams(("parallel", "arbitrary", "arbitrary")),
        name="diff_attention",
    )(scal, q, k, v, sub_g, bias)


def _band_body(q_ref, k_ref, v_ref, pen_ref, o_ref, lse_ref, *, qb):
    l_len = k_ref.shape[1]
    win = pen_ref.shape[3]
    lane = lax.broadcasted_iota(jnp.int32, (qb, LANES), 1)
    n_sub = q_ref.shape[1] // qb
    n_groups = q_ref.shape[2] // LANES
    for sb in range(n_sub):
        jb = pl.program_id(2) * n_sub + sb
        start = jnp.clip(jb * qb - BAND_RADIUS, 0, l_len - win)
        start = pl.multiple_of(start, BAND_RADIUS)
        variant = (jb * qb - start) // BAND_RADIUS
        q = q_ref[0, sb * qb:(sb + 1) * qb, :]
        kwin = k_ref[0, pl.ds(start, win), :]
        vwin = v_ref[0, pl.ds(start, win), :]
        for gc in range(n_groups):
            g = gc % 2
            q2 = q[:, LANES * gc:LANES * (gc + 1)].astype(F32)
            k2 = kwin[:, LANES * gc:LANES * (gc + 1)]
            v2 = vwin[:, LANES * gc:LANES * (gc + 1)]
            out_g = jnp.zeros((qb, LANES), F32)
            lse_g = jnp.zeros((qb, LANES), F32)
            for hh in range(2):
                own = (lane >= HEAD_DIM * hh) & (lane < HEAD_DIM * (hh + 1))
                qm = jnp.where(own, q2, 0.0).astype(BF16)
                s = lax.dot_general(qm, k2, (((1,), (1,)), ((), ())), preferred_element_type=F32)
                s = s - pen_ref[variant, 2 * g + hh]
                m = jnp.max(s, axis=1, keepdims=True)
                p = jnp.exp2(s - m)
                lsum = jnp.sum(p, axis=1, keepdims=True)
                o = jnp.dot(p.astype(BF16), v2, preferred_element_type=F32) / lsum
                lse = m + jnp.log2(lsum)
                out_g = jnp.where(own, o, out_g)
                lse_g = jnp.where(own, lse, lse_g)
            o_ref[0, sb * qb:(sb + 1) * qb, LANES * gc:LANES * (gc + 1)] = out_g.astype(o_ref.dtype)
            lse_ref[0, sb * qb:(sb + 1) * qb, LANES * gc:LANES * (gc + 1)] = lse_g


def _band_penalty(qb, win, dil, slopes):
    rows = jnp.arange(qb)[None, :, None] + BAND_RADIUS * jnp.arange(3)[:, None, None]
    rel = jnp.abs(rows - jnp.arange(win)[None, None, :])
    scale = jnp.asarray(slopes, F32) * (float(dil) * LOG2E)
    pen = rel.astype(F32)[:, None] * scale[None, :, None, None]
    return jnp.where((rel <= BAND_RADIUS)[:, None], pen, BAND_MASKED)


def _band_attention(q, k, v, dil, slopes):
    b, l_len, cw = q.shape
    c = cw // dil
    qb = min(BAND_Q, l_len)
    qstep = min(BAND_QSTEP, l_len)
    cps = min(dil, max(1, BAND_QSTEP // l_len))
    assert dil % cps == 0
    win = min(qb + 2 * BAND_RADIUS, l_len)
    pen = _band_penalty(qb, win, dil, slopes)
    body = functools.partial(_band_body, qb=qb)
    return pl.pallas_call(
        body,
        out_shape=(jax.ShapeDtypeStruct((b, l_len, cw), BF16), jax.ShapeDtypeStruct((b, l_len, cw), F32)),
        grid=(b, dil // cps, l_len // qstep),
        in_specs=[
            pl.BlockSpec((1, qstep, c * cps), lambda i, r, j: (i, j, r)),
            pl.BlockSpec((1, l_len, c * cps), lambda i, r, j: (i, 0, r)),
            pl.BlockSpec((1, l_len, c * cps), lambda i, r, j: (i, 0, r)),
            pl.BlockSpec(pen.shape, lambda i, r, j: (0, 0, 0, 0)),
        ],
        out_specs=(pl.BlockSpec((1, qstep, c * cps), lambda i, r, j: (i, j, r)),) * 2,
        compiler_params=_cparams(("parallel", "arbitrary", "arbitrary")),
        name="band_attention_d%d" % dil,
    )(q, k, v, pen)


def _band_combine_body(*refs):
    n = len(DILATED_CONFIGS)
    in_refs, o_ref, scr_ref = refs[:2 * n], refs[2 * n], refs[2 * n + 1]
    tm, c = o_ref.shape[1], o_ref.shape[2]
    vals = []
    for idx, (_, dil) in enumerate(DILATED_CONFIGS):
        for which in range(2):
            src = in_refs[2 * idx + which]
            if dil == 1:
                vals.append(src[0].astype(F32))
                continue
            groups = []
            for g in range(c // LANES):
                slot = scr_ref.at[(2 * idx + which) * (c // LANES) + g]
                for r in range(dil):
                    lo = c * r + LANES * g
                    slot[pl.ds(r, tm // dil, stride=dil), :] = src[0, :, lo:lo + LANES].astype(F32)
                groups.append(slot[...])
            vals.append(jnp.concatenate(groups, axis=1))
    outs, lses = vals[0::2], vals[1::2]
    m = functools.reduce(jnp.maximum, lses)
    ws = [jnp.exp2(x - m) for x in lses]
    num = functools.reduce(lambda a, t: a + t, [w * o for w, o in zip(ws, outs)])
    den = functools.reduce(lambda a, t: a + t, ws)
    o_ref[0] = (num / den).astype(o_ref.dtype)


def _band_combine(pairs):
    b, s, c = pairs[0][0].shape
    tm = TOK_TILE
    specs = []
    for (_, dil) in DILATED_CONFIGS:
        specs += [pl.BlockSpec((1, tm // dil, dil * c), lambda i, j: (i, j, 0))] * 2
    flat = [a for p in pairs for a in p]
    return pl.pallas_call(
        _band_combine_body,
        out_shape=jax.ShapeDtypeStruct((b, s, c), BF16),
        grid=(b, s // tm),
        in_specs=specs,
        out_specs=pl.BlockSpec((1, tm, c), lambda i, j: (i, j, 0)),
        scratch_shapes=[pltpu.VMEM((2 * len(DILATED_CONFIGS) * (c // LANES), tm, LANES), F32)],
        compiler_params=_cparams(("parallel", "arbitrary")),
        name="band_combine",
    )(*flat)


def _outproj_body(x_ref, oa_ref, ob_ref, oc_ref, od_ref, w_ref, g_ref, x1_ref, h2_ref, edge_ref):
    d, tm = x_ref.shape[1], x_ref.shape[2]
    j = pl.program_id(1)
    mix = jnp.concatenate([oa_ref[0], ob_ref[0], oc_ref[0]], axis=0)
    w = w_ref[...]
    y = jnp.dot(w[:, :768], mix, preferred_element_type=F32)
    y = y + lax.dot_general(w[:, 768:], od_ref[0], (((1,), (1,)), ((), ())), preferred_element_type=F32)
    x1 = x_ref[0] + y
    h2 = _rms_rows(x1, g_ref[...])
    x1_ref[0] = x1.T
    h2t = h2.T
    h2_ref[0] = h2t.astype(BF16)
    edge_ref[0, pl.ds(2 * j, 1), :] = h2t[0:1].astype(BF16).astype(F32)
    edge_ref[0, pl.ds(2 * j + 1, 1), :] = h2t[tm - 1:tm].astype(BF16).astype(F32)


def _out_project(xt, oa, ob, oc, od, w_out_t, g2):
    b, d, s = xt.shape
    tm = TOK_TILE
    nt = s // tm
    fm = lambda rows: pl.BlockSpec((1, rows, tm), lambda i, j: (i, 0, j))
    tmj = pl.BlockSpec((1, tm, d), lambda i, j: (i, j, 0))
    return pl.pallas_call(
        _outproj_body,
        out_shape=(jax.ShapeDtypeStruct((b, s, d), F32), jax.ShapeDtypeStruct((b, s, d), BF16),
                   jax.ShapeDtypeStruct((b, 2 * nt, d), F32)),
        grid=(b, nt),
        in_specs=[fm(d), fm(256), fm(256), fm(256),
                  pl.BlockSpec((1, tm, D_WIDTH), lambda i, j: (i, j, 0)),
                  pl.BlockSpec((d, d), lambda i, j: (0, 0)),
                  pl.BlockSpec((d, 1), lambda i, j: (0, 0))],
        out_specs=(tmj, tmj, pl.BlockSpec((1, 2 * nt, d), lambda i, j: (i, 0, 0))),
        compiler_params=_cparams(("parallel", "arbitrary")),
        name="out_project",
    )(xt, oa, ob, oc, od, w_out_t, g2)


def _halo_body(h_ref, w_ref, o_ref):
    o_ref[...] = jnp.dot(h_ref[...], w_ref[...], preferred_element_type=F32)


def _halo_up(halo_h, w_up):
    nh, d = halo_h.shape
    ff2 = w_up.shape[1]
    n_blk = 4
    cols = ff2 // n_blk
    return pl.pallas_call(
        _halo_body,
        out_shape=jax.ShapeDtypeStruct((nh, ff2), F32),
        grid=(n_blk,),
        in_specs=[pl.BlockSpec((nh, d), lambda i: (0, 0)),
                  pl.BlockSpec((d, cols), lambda i: (0, i))],
        out_specs=pl.BlockSpec((nh, cols), lambda i: (0, i)),
        compiler_params=_cparams(("arbitrary",)),
        name="mlp_halo_up",
    )(halo_h, w_up)


def _mlp_body(x1_ref, h2_ref, wup_ref, wdn_ref, par_ref, halo_ref, o_ref, gbuf_ref, vbuf_ref, *, chunks):
    tf = h2_ref.shape[1]
    ff = wdn_ref.shape[0]
    nt = pl.num_programs(1)
    row = (pl.program_id(0) * nt + pl.program_id(1)) * 2
    h = h2_ref[0]

    def conv(buf_ref, u, c0, width):
        buf_ref[MLP_PAD:MLP_PAD + tf, :width] = u
        buf_ref[MLP_PAD - 1:MLP_PAD, :width] = halo_ref[pl.ds(row, 1), c0:c0 + width]
        buf_ref[MLP_PAD + tf:MLP_PAD + tf + 1, :width] = halo_ref[pl.ds(row + 1, 1), c0:c0 + width]
        prev = buf_ref[MLP_PAD - 1:MLP_PAD - 1 + tf, :width]
        nxt = buf_ref[MLP_PAD + 1:MLP_PAD + 1 + tf, :width]
        p = par_ref[:, c0:c0 + width]
        return p[0:1] * prev + p[1:2] * u + p[2:3] * nxt + p[3:4]

    y = x1_ref[0]
    for c0, width in chunks:
        ug = jnp.dot(h, wup_ref[:, c0:c0 + width], preferred_element_type=F32)
        uv = jnp.dot(h, wup_ref[:, ff + c0:ff + c0 + width], preferred_element_type=F32)
        gate = conv(gbuf_ref, ug, c0, width)
        val = conv(vbuf_ref, uv, ff + c0, width)
        act = gate / (1.0 + jnp.exp(-gate)) * val
        y = y + jnp.dot(act.astype(BF16), wdn_ref[c0:c0 + width, :], preferred_element_type=F32)
    o_ref[0] = y.T


def _mlp(x1, h2, w_up, w_down, conv_par, halo_up, chunks):
    b, s, d = x1.shape
    tf = TOK_TILE
    ff2 = w_up.shape[1]
    nh = halo_up.shape[0]
    whole = lambda i, j: (0, 0)
    body = functools.partial(_mlp_body, chunks=chunks)
    const = dict(pipeline_mode=pl.Buffered(1))
    wmax = max(w for _, w in chunks)
    return pl.pallas_call(
        body,
        out_shape=jax.ShapeDtypeStruct((b, d, s), F32),
        grid=(b, s // tf),
        in_specs=[pl.BlockSpec((1, tf, d), lambda i, j: (i, j, 0)),
                  pl.BlockSpec((1, tf, d), lambda i, j: (i, j, 0)),
                  pl.BlockSpec((d, ff2), whole, **const),
                  pl.BlockSpec((ff2 // 2, d), whole, **const),
                  pl.BlockSpec((8, ff2), whole, **const),
                  pl.BlockSpec((nh, ff2), whole, **const)],
        out_specs=pl.BlockSpec((1, d, tf), lambda i, j: (i, 0, j)),
        scratch_shapes=[pltpu.VMEM((tf + 2 * MLP_PAD, wmax), F32)] * 2,
        compiler_params=_cparams(("arbitrary", "arbitrary")),
        name="gated_conv_mlp",
    )(x1, h2, w_up, w_down, conv_par, halo_up)


def _halo_rows(edges, nt):
    b, _, d = edges.shape
    ez = jnp.concatenate([edges, jnp.zeros((b, 1, d), edges.dtype)], axis=1)
    idx = []
    for j in range(nt):
        idx += [2 * j - 1 if j > 0 else 2 * nt, 2 * j + 2 if j < nt - 1 else 2 * nt]
    rows = jnp.take(ez, jnp.asarray(idx, jnp.int32), axis=1).reshape(b * nt * 2, d)
    pad = (-rows.shape[0]) % BF16_ROWS
    return jnp.pad(rows, ((0, pad), (0, 0))).astype(BF16)


def _rope_tables(s):
    def freqs(dim):
        return 1.0 / (ROPE_THETA ** (jnp.arange(0, dim, 2, dtype=F32) / dim))
    t = jnp.arange(s)
    rows = (t // GRID_W).astype(F32)
    cols = (t % GRID_W).astype(F32)
    pos = t.astype(F32)
    fa = freqs(HEAD_DIM // 2)
    ar, ac = fa[:, None] * rows[None, :], fa[:, None] * cols[None, :]
    am = freqs(C_ROPE)[:, None] * pos[None, :]
    cosa = jnp.concatenate([jnp.cos(ar), jnp.cos(ar), jnp.cos(ac), jnp.cos(ac)], axis=0)
    sina = jnp.concatenate([-jnp.sin(ar), jnp.sin(ar), -jnp.sin(ac), jnp.sin(ac)], axis=0)
    cosm = jnp.concatenate([jnp.cos(am), jnp.cos(am)], axis=0)
    sinm = jnp.concatenate([-jnp.sin(am), jnp.sin(am)], axis=0)
    return cosa, sina, cosm, sinm


def kernel(x, norm1_g, w_in, a_qn_g, a_kn_g, b_qn_g, b_kn_g, b_lam_q1, b_lam_k1, b_lam_q2, b_lam_k2, b_sub_g,
           c_qa_g, c_kva_g, c_wqb, c_wkvb, c_qn_g, c_kn_g, d_qn_g, d_kn_g, w_out, norm2_g, w_up, conv_w, conv_b,
           w_down):
    b, s, d = x.shape
    depth = w_in.shape[0]
    ff = w_down.shape[1]
    nt = s // TOK_TILE
    assert s % TOK_TILE == 0 and TOK_TILE % KSTEP == 0 and TQ % KSTEP == 0 and s % TQ == 0 and s % GRID_W == 0
    assert s <= 1 << (POS_SPLIT_BITS + BF16_MANTISSA_BITS) and s % TQ_STEP == 0
    for window, dil in DILATED_CONFIGS:
        assert window // (2 * dil) == BAND_RADIUS and (s // dil) % min(BAND_Q, s // dil) == 0
        assert (s // dil) % min(BAND_QSTEP, s // dil) == 0 and BAND_QSTEP % BAND_Q == 0
        assert TOK_TILE % (dil * BF16_ROWS) == 0

    tr = lambda w: jnp.swapaxes(w, 1, 2).astype(BF16)
    gains = jnp.concatenate([norm1_g, a_qn_g, a_kn_g, b_qn_g, b_kn_g, c_qa_g, c_kva_g, c_qn_g, c_kn_g,
                             d_qn_g, d_kn_g], axis=1)[:, :, None]
    conv_par = jnp.concatenate([conv_w, conv_b[:, None, :], jnp.zeros((depth, 4, 2 * ff), F32)], axis=1)
    slopes = [2.0 ** (-8.0 * i / (2 * N_HEADS)) for i in range(1, 2 * N_HEADS + 1)]
    slopes_b, slopes_d = slopes[:N_HEADS], tuple(slopes[N_HEADS:])
    lam_init = jnp.asarray([0.8 - 0.6 * math.exp(-0.3 * l) for l in range(depth)], F32)
    lam = (jnp.exp(jnp.sum(b_lam_q1 * b_lam_k1, axis=1)) - jnp.exp(jnp.sum(b_lam_q2 * b_lam_k2, axis=1)) + lam_init)
    scal = jnp.concatenate([lam[:, None], (1.0 - lam_init)[:, None],
                            jnp.broadcast_to(jnp.asarray(slopes_b, F32), (depth, N_HEADS)),
                            jnp.zeros((depth, 2), F32)], axis=1)
    layers = dict(w_in=tr(w_in), wqb=tr(c_wqb), wkvb=tr(c_wkvb), w_out=tr(w_out),
                  w_up=w_up.astype(BF16), w_down=w_down.astype(BF16),
                  gains=gains, g2=norm2_g[:, :, None], sub_g=b_sub_g[:, :, None], conv_par=conv_par, scal=scal)
    cosa, sina, cosm, sinm = _rope_tables(s)
    assert ff % LANES == 0
    chunks = tuple((c0, min(MLP_CHUNK, ff - c0)) for c0 in range(0, ff, MLP_CHUNK))

    def layer(xt, p):
        (qa, ka, va, qb, kb, vb, qc, kc, vc, dq1, dq4, dq16, dk1, dk4, dk16, dv1, dv4, dv16) = _project(
            xt, p["w_in"], p["wqb"], p["wkvb"], p["gains"], cosa, sina, cosm, sinm)
        oa = _dense_attention(qa, ka, va, HEAD_DIM, N_HEADS // A_KV_HEADS)
        ob = _diff_attention(p["scal"], slopes_b, qb, kb, vb, p["sub_g"])
        oc = _dense_attention(qc, kc, vc, C_QK, 1)
        branches = [_band_attention(q_, k_, v_, dil, slopes_d)
                    for (_, dil), q_, k_, v_ in zip(DILATED_CONFIGS, (dq1, dq4, dq16), (dk1, dk4, dk16), (dv1, dv4, dv16))]
        od = _band_combine(branches)
        x1, h2, edges = _out_project(xt, oa, ob, oc, od, p["w_out"], p["g2"])
        halo = _halo_up(_halo_rows(edges, nt), p["w_up"])
        return _mlp(x1, h2, p["w_up"], p["w_down"], p["conv_par"], halo, chunks), None

    xt = _transpose_last2(x, TOK_TILE, d)
    xt, _ = lax.scan(layer, xt, layers)
    return _transpose_last2(xt, d, TOK_TILE)
```

```python
import functools
import math

import jax
import jax.numpy as jnp
import numpy as np
from jax import lax
from jax.experimental import pallas as pl
from jax.experimental.pallas import tpu as pltpu

F32 = jnp.float32
BF16 = jnp.bfloat16

HEAD_DIM = 64
N_HEADS = 4
A_KV_HEADS = 2
B_QK_DIM = 32
C_Q_LORA = 256
C_KV_LORA = 128
C_NOPE = 64
C_ROPE = 32
C_QK = C_NOPE + C_ROPE
DILATED_CONFIGS = ((128, 1), (512, 4), (2048, 16))
D_WIDTH = N_HEADS * HEAD_DIM
GRID_W = 64
ROPE_THETA = 10000.0
NORM_EPS = 1e-6
LOG2E = 1.4426950408889634


def _bf16_pieces(x, n):
    out = []
    for _ in range(n):
        u = np.float32(x).view(np.uint32)
        u = np.uint32((int(u) + 0x7FFF + ((int(u) >> 16) & 1)) & 0xFFFF0000)
        piece = float(u.view(np.float32))
        out.append(piece)
        x = x - piece
    return tuple(out)


LOG2E_PIECES = _bf16_pieces(LOG2E, 3)
POS_SPLIT_BITS = 6
BF16_MANTISSA_BITS = 8

_SPLITS = (256, 128, 128, 256, 256, 256, C_Q_LORA, C_KV_LORA, C_ROPE, 256, 256, 256)
_OFF = [0]
for _n in _SPLITS:
    _OFF.append(_OFF[-1] + _n)
(O_AQ, O_AK, O_AV, O_BQ, O_BK, O_BV, O_CQL, O_CKVL, O_CKR, O_DQ, O_DK, O_DV, IN_WIDTH) = _OFF

_GAIN_SIZES = (("n1", 1024), ("aq", 64), ("ak", 64), ("bq", 32), ("bk", 32), ("cqa", 256), ("ckva", 128),
               ("cqn", 96), ("ckn", 96), ("dq", 64), ("dk", 64))
G_OFF = {}
_o = 0
for _k, _n in _GAIN_SIZES:
    G_OFF[_k] = (_o, _n)
    _o += _n
G_TOTAL = _o

LANES = 128
BF16_ROWS = 16
TOK_TILE = 512
MLP_CHUNK = 1024
MLP_PAD = 8
TQ = 512
TQ_STEP = 2048
KSTEP = 256
SCORES_AHEAD = 2
BAND_Q = 128
BAND_QSTEP = 1024
BAND_MASKED = 1e30
BAND_RADIUS = 64
VMEM_LIMIT = 56 * 1024 * 1024


def _cparams(sem, flags=None):
    return pltpu.CompilerParams(dimension_semantics=sem, vmem_limit_bytes=VMEM_LIMIT, flags=flags)


def _rms_rows(v, g_col):
    ms = jnp.mean(v * v, axis=0, keepdims=True)
    return v * lax.rsqrt(ms + NORM_EPS) * g_col


def _swap_halves(x, n):
    parts = []
    for i in range(0, x.shape[0], 2 * n):
        parts.append(x[i + n:i + 2 * n])
        parts.append(x[i:i + n])
    return jnp.concatenate(parts, axis=0)


def _rope_rows(x, cos, sin_signed, n):
    return x * cos + _swap_halves(x, n) * sin_signed


def _transpose_body(x_ref, o_ref):
    o_ref[0] = x_ref[0].T


def _transpose_last2(x, tr, tc):
    b, r, c = x.shape
    return pl.pallas_call(
        _transpose_body,
        out_shape=jax.ShapeDtypeStruct((b, c, r), x.dtype),
        grid=(b, r // tr, c // tc),
        in_specs=[pl.BlockSpec((1, tr, tc), lambda i, j, k: (i, j, k))],
        out_specs=pl.BlockSpec((1, tc, tr), lambda i, j, k: (i, k, j)),
        compiler_params=_cparams(("parallel", "arbitrary", "arbitrary")),
        name="transpose_tokens",
    )(x)


def _store_dilated(scr_ref, val, out_refs):
    tm, c = val.shape
    for g in range(c // LANES):
        scr_ref[g] = val[:, LANES * g:LANES * (g + 1)]
    for (_, dil), out in zip(DILATED_CONFIGS, out_refs):
        if dil == 1:
            out[0] = val.astype(BF16)
            continue
        n = tm // dil
        for r in range(dil):
            for g in range(c // LANES):
                lo = c * r + LANES * g
                out[0, :, lo:lo + LANES] = scr_ref[g, pl.ds(r, n, stride=dil), :].astype(BF16)


def _store_key_blocks(v_ref, v):
    for c in range(v.shape[1] // KSTEP):
        v_ref[0, c] = v[:, c * KSTEP:(c + 1) * KSTEP]


def _proj_body(x_ref, w_ref, wqb_ref, wkvb_ref, g_ref, cosa_ref, sina_ref, cosm_ref, sinm_ref,
               qa_ref, ka_ref, va_ref, qb_ref, kb_ref, vb_ref, qc_ref, kc_ref, vc_ref,
               dq1_ref, dq4_ref, dq16_ref, dk1_ref, dk4_ref, dk16_ref, dv1_ref, dv4_ref, dv16_ref,
               scr_ref):
    tm = x_ref.shape[1]

    def gain(name):
        o, n = G_OFF[name]
        return g_ref[o:o + n, :]

    x = x_ref[0].T
    h = _rms_rows(x, gain("n1")).astype(BF16)

    def section(lo, hi):
        part = jnp.dot(w_ref[lo:hi, :], h, preferred_element_type=F32)
        return lambda o, n: part[o - lo:o - lo + n]

    cosa, sina = cosa_ref[...], sina_ref[...]
    cosm, sinm = cosm_ref[...], sinm_ref[...]
    zeros64 = jnp.zeros((64, tm), F32)
    zeros32 = jnp.zeros((32, tm), F32)

    def prep_a(proj):
        sa = HEAD_DIM ** -0.5 * LOG2E
        for hh in range(N_HEADS):
            q = _rope_rows(_rms_rows(proj(O_AQ + 64 * hh, 64), gain("aq")), cosa, sina, 16) * sa
            qa_ref[0, 64 * hh:64 * hh + 64, :] = q.astype(BF16)
        for g in range(A_KV_HEADS):
            k = _rope_rows(_rms_rows(proj(O_AK + 64 * g, 64), gain("ak")), cosa, sina, 16)
            kpad = jnp.concatenate([k, zeros64], axis=0)
            ka_ref[0, :, LANES * g:LANES * (g + 1)] = kpad.T.astype(BF16)
        _store_key_blocks(va_ref, proj(O_AV, 128).astype(BF16))

    def prep_b(proj):
        sb = B_QK_DIM ** -0.5 * LOG2E
        for i in range(2 * N_HEADS):
            qb_ref[0, 32 * i:32 * i + 32, :] = (_rms_rows(proj(O_BQ + 32 * i, 32), gain("bq")) * sb).astype(BF16)
        kpos = pl.program_id(1) * tm + lax.broadcasted_iota(jnp.int32, (BF16_ROWS, tm), 1)
        frow = lax.broadcasted_iota(jnp.int32, (BF16_ROWS, tm), 0)
        pos_hi = ((kpos >> POS_SPLIT_BITS) << POS_SPLIT_BITS).astype(F32)
        pos_lo = (kpos & ((1 << POS_SPLIT_BITS) - 1)).astype(F32)
        kfeat = jnp.where(frow < 3, pos_hi, jnp.where(frow < 6, pos_lo, 0.0))
        zeros48 = jnp.zeros((48, tm), F32)
        for hh in range(N_HEADS):
            k1 = _rms_rows(proj(O_BK + 64 * hh, 32), gain("bk"))
            k2 = _rms_rows(proj(O_BK + 64 * hh + 32, 32), gain("bk"))
            kpad = jnp.concatenate([k1, k2, kfeat, zeros48], axis=0)
            kb_ref[0, :, LANES * hh:LANES * (hh + 1)] = kpad.T.astype(BF16)
        _store_key_blocks(vb_ref, proj(O_BV, 256).astype(BF16))

    def prep_c(proj):
        sc = C_QK ** -0.5 * LOG2E
        cq = _rms_rows(proj(O_CQL, C_Q_LORA), gain("cqa")).astype(BF16)
        qc = jnp.dot(wqb_ref[...], cq, preferred_element_type=F32)
        ckv = _rms_rows(proj(O_CKVL, C_KV_LORA), gain("ckva")).astype(BF16)
        kvc = jnp.dot(wkvb_ref[...], ckv, preferred_element_type=F32)
        kr = proj(O_CKR, C_ROPE)
        vcs = []
        for hh in range(N_HEADS):
            q = _rms_rows(qc[C_QK * hh:C_QK * (hh + 1)], gain("cqn"))
            q = jnp.concatenate([q[:C_NOPE], _rope_rows(q[C_NOPE:], cosm, sinm, 16)], axis=0) * sc
            qc_ref[0, C_QK * hh:C_QK * (hh + 1), :] = q.astype(BF16)
            kk = jnp.concatenate([kvc[128 * hh:128 * hh + C_NOPE], kr], axis=0)
            kk = _rms_rows(kk, gain("ckn"))
            kk = jnp.concatenate([kk[:C_NOPE], _rope_rows(kk[C_NOPE:], cosm, sinm, 16), zeros32], axis=0)
            kc_ref[0, :, LANES * hh:LANES * (hh + 1)] = kk.T.astype(BF16)
            vcs.append(kvc[128 * hh + C_NOPE:128 * (hh + 1)])
        _store_key_blocks(vc_ref, jnp.concatenate(vcs, axis=0).astype(BF16))

    def prep_d(proj):
        sd = HEAD_DIM ** -0.5 * LOG2E
        qs, ks, vs = [], [], []
        for g in range(2):
            qg, kg = [], []
            for hh in (2 * g, 2 * g + 1):
                qg.append(_rms_rows(proj(O_DQ + 64 * hh, 64), gain("dq")) * sd)
                kg.append(_rms_rows(proj(O_DK + 64 * hh, 64), gain("dk")))
            qs.append(jnp.concatenate(qg, axis=0).T)
            ks.append(jnp.concatenate(kg, axis=0).T)
            vs.append(proj(O_DV + 128 * g, 128).T)
        _store_dilated(scr_ref.at[0], jnp.concatenate(qs, axis=1), (dq1_ref, dq4_ref, dq16_ref))
        _store_dilated(scr_ref.at[1], jnp.concatenate(ks, axis=1), (dk1_ref, dk4_ref, dk16_ref))
        _store_dilated(scr_ref.at[2], jnp.concatenate(vs, axis=1), (dv1_ref, dv4_ref, dv16_ref))

    sec_a = section(O_AQ, O_BQ)
    sec_b = section(O_BQ, O_CQL)
    prep_a(sec_a)
    sec_c = section(O_CQL, O_DQ)
    prep_b(sec_b)
    sec_d = section(O_DQ, IN_WIDTH)
    prep_c(sec_c)
    prep_d(sec_d)


def _project(x, w_in_t, wqb_t, wkvb_t, gains, cosa, sina, cosm, sinm):
    b, s, d = x.shape
    tm = TOK_TILE
    nt = s // tm
    fm = lambda rows: jax.ShapeDtypeStruct((b, rows, s), BF16)
    tmj = lambda cols: jax.ShapeDtypeStruct((b, s, cols), BF16)
    vblk = lambda rows: jax.ShapeDtypeStruct((b, s // KSTEP, rows, KSTEP), BF16)
    dviews = tuple(jax.ShapeDtypeStruct((b, s // dil, dil * D_WIDTH), BF16) for _, dil in DILATED_CONFIGS)
    out_shape = (fm(256), tmj(256), vblk(128),
                 fm(256), tmj(512), vblk(256),
                 fm(384), tmj(512), vblk(256)) + dviews * 3
    fm_spec = lambda rows: pl.BlockSpec((1, rows, tm), lambda i, j: (i, 0, j))
    tm_spec = lambda cols: pl.BlockSpec((1, tm, cols), lambda i, j: (i, j, 0))
    v_spec = lambda rows: pl.BlockSpec((1, tm // KSTEP, rows, KSTEP), lambda i, j: (i, j, 0, 0))
    dspecs = tuple(pl.BlockSpec((1, tm // dil, dil * D_WIDTH), lambda i, j: (i, j, 0)) for _, dil in DILATED_CONFIGS)
    out_specs = (fm_spec(256), tm_spec(256), v_spec(128),
                 fm_spec(256), tm_spec(512), v_spec(256),
                 fm_spec(384), tm_spec(512), v_spec(256)) + dspecs * 3
    whole = lambda i, j: (0, 0)
    in_specs = [
        pl.BlockSpec((1, tm, d), lambda i, j: (i, j, 0)),
        pl.BlockSpec((IN_WIDTH, d), whole),
        pl.BlockSpec((N_HEADS * C_QK, C_Q_LORA), whole),
        pl.BlockSpec((N_HEADS * 128, C_KV_LORA), whole),
        pl.BlockSpec((G_TOTAL, 1), whole),
        pl.BlockSpec((64, tm), lambda i, j: (0, j)),
        pl.BlockSpec((64, tm), lambda i, j: (0, j)),
        pl.BlockSpec((32, tm), lambda i, j: (0, j)),
        pl.BlockSpec((32, tm), lambda i, j: (0, j)),
    ]
    return pl.pallas_call(
        _proj_body,
        out_shape=out_shape,
        grid=(b, nt),
        in_specs=in_specs,
        out_specs=out_specs,
        scratch_shapes=[pltpu.VMEM((3, D_WIDTH // LANES, tm, LANES), F32)],
        compiler_params=_cparams(("parallel", "arbitrary")),
        name="project_heads",
    )(x, w_in_t, wqb_t, wkvb_t, gains, cosa, sina, cosm, sinm)


def _online_update(s, v_aug, carry, shift=None):
    m, acc = carry
    smax = jnp.max(s, axis=0, keepdims=True)
    if shift is None:
        m_new = jnp.maximum(m, smax)
        p = jnp.exp2(s - m_new).astype(BF16)
    else:
        m_new = jnp.maximum(m, smax + shift)
        p = jnp.exp2(s - (m_new - shift)).astype(BF16)
    alpha = jnp.exp2(m - m_new)
    acc = alpha * acc + jnp.dot(v_aug, p, preferred_element_type=F32)
    return m_new, acc


def _init_carry(dv, tq):
    return (jnp.full((1, tq), -jnp.inf, F32), jnp.zeros((dv + BF16_ROWS, tq), F32))


def _finish(carry, dv):
    _, acc = carry
    return acc[:dv] / acc[dv:dv + 1]


def _run_pipelined(n_jobs, n_blocks, scores_fn, update_fn, init_fn, finish_fn):
    steps = [(t, i) for t in range(n_jobs) for i in range(n_blocks)]
    pending = [scores_fn(*steps[n]) for n in range(min(SCORES_AHEAD, len(steps)))]
    carry = None
    for n, (t, i) in enumerate(steps):
        s = pending.pop(0)
        if n + SCORES_AHEAD < len(steps):
            pending.append(scores_fn(*steps[n + SCORES_AHEAD]))
        if i == 0:
            carry = init_fn()
        carry = update_fn(t, i, s, carry)
        if i == n_blocks - 1:
            finish_fn(t, carry)


def _dense_body(q_ref, k_ref, v_ref, o_ref):
    dq = q_ref.shape[1]
    nkb, dv, ks = v_ref.shape[1], v_ref.shape[2], v_ref.shape[3]
    tq = TQ
    ones = jnp.ones((BF16_ROWS, ks), BF16)
    zpad = jnp.zeros((LANES - dq, tq), BF16)
    qpads = [jnp.concatenate([q_ref[0, :, t * tq:(t + 1) * tq], zpad], axis=0) for t in range(q_ref.shape[2] // tq)]

    def scores(t, i):
        return jnp.dot(k_ref[0, i * ks:(i + 1) * ks, :], qpads[t], preferred_element_type=F32)

    def update(t, i, s, carry):
        return _online_update(s, jnp.concatenate([v_ref[0, i], ones], axis=0), carry)

    def finish(t, carry):
        o_ref[0, :, t * tq:(t + 1) * tq] = _finish(carry, dv).astype(o_ref.dtype)

    _run_pipelined(len(qpads), nkb, scores, update, lambda: _init_carry(dv, tq), finish)


def _dense_attention(q, k, v, dq, q_per_kv):
    b, rows, s = q.shape
    nh = rows // dq
    nkb, tk = v.shape[1], v.shape[3]
    return pl.pallas_call(
        _dense_body,
        out_shape=jax.ShapeDtypeStruct((b, nh * HEAD_DIM, s), BF16),
        grid=(b, nh, s // TQ_STEP),
        in_specs=[
            pl.BlockSpec((1, dq, TQ_STEP), lambda i, h, j: (i, h, j)),
            pl.BlockSpec((1, s, LANES), lambda i, h, j: (i, 0, h // q_per_kv)),
            pl.BlockSpec((1, nkb, HEAD_DIM, tk), lambda i, h, j: (i, 0, h // q_per_kv, 0)),
        ],
        out_specs=pl.BlockSpec((1, HEAD_DIM, TQ_STEP), lambda i, h, j: (i, h, j)),
        compiler_params=_cparams(("parallel", "arbitrary", "arbitrary")),
        name="dense_attention",
    )(q, k, v)


def _diff_body(sc_ref, q_ref, k_ref, v_ref, g_ref, bias_ref, o_ref):
    tq = TQ
    n_tiles = q_ref.shape[2] // tq
    nkb, dv, ks = v_ref.shape[1], v_ref.shape[2], v_ref.shape[3]
    hh = pl.program_id(1)
    lam = sc_ref[0]
    out_scale = sc_ref[1]
    slope = sc_ref[2 + hh]
    z32 = jnp.zeros((B_QK_DIM, tq), BF16)
    z48 = jnp.zeros((48, tq), BF16)
    zfeat = jnp.zeros((BF16_ROWS, tq), BF16)
    ones = jnp.ones((BF16_ROWS, ks), BF16)
    frow = lax.broadcasted_iota(jnp.int32, (BF16_ROWS, tq), 0)
    piece = jnp.where(frow % 3 == 0, LOG2E_PIECES[0], jnp.where(frow % 3 == 1, LOG2E_PIECES[1], LOG2E_PIECES[2]))
    cfeat = jnp.where(frow < 6, piece, 0.0)
    lane_pos = lax.broadcasted_iota(jnp.int32, (1, tq), 1)

    n_mixed = tq // ks

    def tile_index(job):
        return pl.program_id(2) * n_tiles + job // 2

    def block_index(job, ii):
        return lax.rem(tile_index(job) * n_mixed + ii, nkb)

    def side(job, ii):
        return jnp.where(block_index(job, ii) < tile_index(job) * n_mixed, 1.0, -1.0).astype(F32)

    def qrows(job):
        q = q_ref[0, :, (job // 2) * tq:(job // 2 + 1) * tq]
        if job % 2 == 0:
            return jnp.concatenate([q[:B_QK_DIM], z32], axis=0)
        return jnp.concatenate([z32, q[B_QK_DIM:]], axis=0)

    def scores(job, ii):
        i = block_index(job, ii)
        k = k_ref[0, pl.ds(pl.multiple_of(i * ks, ks), ks), :]
        feat = zfeat if ii < n_mixed else (cfeat * (side(job, ii) * slope)).astype(BF16)
        return jnp.dot(k, jnp.concatenate([qrows(job), feat, z48], axis=0), preferred_element_type=F32)

    def update(job, ii, s, carry):
        v_aug = jnp.concatenate([v_ref[0, block_index(job, ii)], ones], axis=0)
        if ii < n_mixed:
            return _online_update(s - bias_ref[0, ii], v_aug, carry)
        qpos = (tile_index(job) * tq + lane_pos).astype(F32)
        return _online_update(s, v_aug, carry, qpos * (-(side(job, ii) * slope * LOG2E)))

    res = []

    def finish(job, carry):
        res.append(_finish(carry, dv))
        if job % 2 == 1:
            o = res[job - 1] - lam * res[job]
            t = job // 2
            o_ref[0, :, t * tq:(t + 1) * tq] = (_rms_rows(o, g_ref[...]) * out_scale).astype(o_ref.dtype)

    _run_pipelined(2 * n_tiles, nkb, scores, update, lambda: _init_carry(dv, tq), finish)


def _diff_attention(scal, slopes, q, k, v, sub_g):
    b, rows, s = q.shape
    nkb, tk = v.shape[1], v.shape[3]
    n_mixed = TQ // tk
    jj = jnp.arange(tk)[None, :, None] + tk * jnp.arange(n_mixed)[:, None, None]
    dist = jnp.abs(jj - jnp.arange(TQ)[None, None, :]).astype(F32)
    bias = dist[None] * (LOG2E * jnp.asarray(slopes, F32))[:, None, None, None]
    return pl.pallas_call(
        _diff_body,
        out_shape=jax.ShapeDtypeStruct((b, rows, s), BF16),
        grid=(b, N_HEADS, s // TQ_STEP),
        in_specs=[
            pl.BlockSpec(memory_space=pltpu.SMEM),
            pl.BlockSpec((1, 2 * B_QK_DIM, TQ_STEP), lambda i, h, j: (i, h, j)),
            pl.BlockSpec((1, s, LANES), lambda i, h, j: (i, 0, h)),
            pl.BlockSpec((1, nkb, HEAD_DIM, tk), lambda i, h, j: (i, 0, h, 0)),
            pl.BlockSpec((HEAD_DIM, 1), lambda i, h, j: (0, 0)),
            pl.BlockSpec((1, n_mixed, tk, TQ), lambda i, h, j: (h, 0, 0, 0)),
        ],
        out_specs=pl.BlockSpec((1, HEAD_DIM, TQ_STEP), lambda i, h, j: (i, h, j)),
        compiler_params=_cparams(("parallel", "arbitrary", "arbitrary")),
        name="diff_attention",
    )(scal, q, k, v, sub_g, bias)


def _band_body(q_ref, k_ref, v_ref, pen_ref, o_ref, lse_ref, *, qb):
    l_len = k_ref.shape[1]
    win = pen_ref.shape[3]
    lane = lax.broadcasted_iota(jnp.int32, (qb, LANES), 1)
    n_sub = q_ref.shape[1] // qb
    n_groups = q_ref.shape[2] // LANES
    for sb in range(n_sub):
        jb = pl.program_id(2) * n_sub + sb
        start = jnp.clip(jb * qb - BAND_RADIUS, 0, l_len - win)
        start = pl.multiple_of(start, BAND_RADIUS)
        variant = (jb * qb - start) // BAND_RADIUS
        q = q_ref[0, sb * qb:(sb + 1) * qb, :]
        kwin = k_ref[0, pl.ds(start, win), :]
        vwin = v_ref[0, pl.ds(start, win), :]
        for gc in range(n_groups):
            g = gc % 2
            q2 = q[:, LANES * gc:LANES * (gc + 1)].astype(F32)
            k2 = kwin[:, LANES * gc:LANES * (gc + 1)]
            v2 = vwin[:, LANES * gc:LANES * (gc + 1)]
            out_g = jnp.zeros((qb, LANES), F32)
            lse_g = jnp.zeros((qb, LANES), F32)
            for hh in range(2):
                own = (lane >= HEAD_DIM * hh) & (lane < HEAD_DIM * (hh + 1))
                qm = jnp.where(own, q2, 0.0).astype(BF16)
                s = lax.dot_general(qm, k2, (((1,), (1,)), ((), ())), preferred_element_type=F32)
                s = s - pen_ref[variant, 2 * g + hh]
                m = jnp.max(s, axis=1, keepdims=True)
                p = jnp.exp2(s - m)
                lsum = jnp.sum(p, axis=1, keepdims=True)
                o = jnp.dot(p.astype(BF16), v2, preferred_element_type=F32) / lsum
                lse = m + jnp.log2(lsum)
                out_g = jnp.where(own, o, out_g)
                lse_g = jnp.where(own, lse, lse_g)
            o_ref[0, sb * qb:(sb + 1) * qb, LANES * gc:LANES * (gc + 1)] = out_g.astype(o_ref.dtype)
            lse_ref[0, sb * qb:(sb + 1) * qb, LANES * gc:LANES * (gc + 1)] = lse_g


def _band_penalty(qb, win, dil, slopes):
    rows = jnp.arange(qb)[None, :, None] + BAND_RADIUS * jnp.arange(3)[:, None, None]
    rel = jnp.abs(rows - jnp.arange(win)[None, None, :])
    scale = jnp.asarray(slopes, F32) * (float(dil) * LOG2E)
    pen = rel.astype(F32)[:, None] * scale[None, :, None, None]
    return jnp.where((rel <= BAND_RADIUS)[:, None], pen, BAND_MASKED)


def _band_attention(q, k, v, dil, slopes):
    b, l_len, cw = q.shape
    c = cw // dil
    qb = min(BAND_Q, l_len)
    qstep = min(BAND_QSTEP, l_len)
    cps = min(dil, max(1, BAND_QSTEP // l_len))
    assert dil % cps == 0
    win = min(qb + 2 * BAND_RADIUS, l_len)
    pen = _band_penalty(qb, win, dil, slopes)
    body = functools.partial(_band_body, qb=qb)
    return pl.pallas_call(
        body,
        out_shape=(jax.ShapeDtypeStruct((b, l_len, cw), BF16), jax.ShapeDtypeStruct((b, l_len, cw), F32)),
        grid=(b, dil // cps, l_len // qstep),
        in_specs=[
            pl.BlockSpec((1, qstep, c * cps), lambda i, r, j: (i, j, r)),
            pl.BlockSpec((1, l_len, c * cps), lambda i, r, j: (i, 0, r)),
            pl.BlockSpec((1, l_len, c * cps), lambda i, r, j: (i, 0, r)),
            pl.BlockSpec(pen.shape, lambda i, r, j: (0, 0, 0, 0)),
        ],
        out_specs=(pl.BlockSpec((1, qstep, c * cps), lambda i, r, j: (i, j, r)),) * 2,
        compiler_params=_cparams(("parallel", "arbitrary", "arbitrary")),
        name="band_attention_d%d" % dil,
    )(q, k, v, pen)


def _band_combine_body(*refs):
    n = len(DILATED_CONFIGS)
    in_refs, o_ref, scr_ref = refs[:2 * n], refs[2 * n], refs[2 * n + 1]
    tm, c = o_ref.shape[1], o_ref.shape[2]
    vals = []
    for idx, (_, dil) in enumerate(DILATED_CONFIGS):
        for which in range(2):
            src = in_refs[2 * idx + which]
            if dil == 1:
                vals.append(src[0].astype(F32))
                continue
            groups = []
            for g in range(c // LANES):
                slot = scr_ref.at[(2 * idx + which) * (c // LANES) + g]
                for r in range(dil):
                    lo = c * r + LANES * g
                    slot[pl.ds(r, tm // dil, stride=dil), :] = src[0, :, lo:lo + LANES].astype(F32)
                groups.append(slot[...])
            vals.append(jnp.concatenate(groups, axis=1))
    outs, lses = vals[0::2], vals[1::2]
    m = functools.reduce(jnp.maximum, lses)
    ws = [jnp.exp2(x - m) for x in lses]
    num = functools.reduce(lambda a, t: a + t, [w * o for w, o in zip(ws, outs)])
    den = functools.reduce(lambda a, t: a + t, ws)
    o_ref[0] = (num / den).astype(o_ref.dtype)


def _band_combine(pairs):
    b, s, c = pairs[0][0].shape
    tm = TOK_TILE
    specs = []
    for (_, dil) in DILATED_CONFIGS:
        specs += [pl.BlockSpec((1, tm // dil, dil * c), lambda i, j: (i, j, 0))] * 2
    flat = [a for p in pairs for a in p]
    return pl.pallas_call(
        _band_combine_body,
        out_shape=jax.ShapeDtypeStruct((b, s, c), BF16),
        grid=(b, s // tm),
        in_specs=specs,
        out_specs=pl.BlockSpec((1, tm, c), lambda i, j: (i, j, 0)),
        scratch_shapes=[pltpu.VMEM((2 * len(DILATED_CONFIGS) * (c // LANES), tm, LANES), F32)],
        compiler_params=_cparams(("parallel", "arbitrary")),
        name="band_combine",
    )(*flat)


def _outproj_body(x_ref, oa_ref, ob_ref, oc_ref, od_ref, w_ref, g_ref, x1_ref, h2_ref, edge_ref):
    tm = x_ref.shape[1]
    j = pl.program_id(1)
    mix = jnp.concatenate([oa_ref[0], ob_ref[0], oc_ref[0]], axis=0)
    w = w_ref[...]
    y = jnp.dot(w[:, :768], mix, preferred_element_type=F32)
    y = y + lax.dot_general(w[:, 768:], od_ref[0], (((1,), (1,)), ((), ())), preferred_element_type=F32)
    x1 = x_ref[0] + y.T
    x1_ref[0] = x1
    ms = jnp.mean(x1 * x1, axis=1, keepdims=True)
    h2 = (x1 * lax.rsqrt(ms + NORM_EPS) * g_ref[...]).astype(BF16)
    h2_ref[0] = h2
    edge_ref[0, pl.ds(2 * j, 1), :] = h2[0:1].astype(F32)
    edge_ref[0, pl.ds(2 * j + 1, 1), :] = h2[tm - 1:tm].astype(F32)


def _out_project(x, oa, ob, oc, od, w_out_t, g2):
    b, s, d = x.shape
    tm = TOK_TILE
    nt = s // tm
    fm = lambda rows: pl.BlockSpec((1, rows, tm), lambda i, j: (i, 0, j))
    tmj = pl.BlockSpec((1, tm, d), lambda i, j: (i, j, 0))
    return pl.pallas_call(
        _outproj_body,
        out_shape=(jax.ShapeDtypeStruct((b, s, d), F32), jax.ShapeDtypeStruct((b, s, d), BF16),
                   jax.ShapeDtypeStruct((b, 2 * nt, d), F32)),
        grid=(b, nt),
        in_specs=[tmj, fm(256), fm(256), fm(256),
                  pl.BlockSpec((1, tm, D_WIDTH), lambda i, j: (i, j, 0)),
                  pl.BlockSpec((d, d), lambda i, j: (0, 0)),
                  pl.BlockSpec((1, d), lambda i, j: (0, 0))],
        out_specs=(tmj, tmj, pl.BlockSpec((1, 2 * nt, d), lambda i, j: (i, 0, 0))),
        compiler_params=_cparams(("parallel", "arbitrary")),
        name="out_project",
    )(x, oa, ob, oc, od, w_out_t, g2)


def _halo_body(h_ref, w_ref, o_ref):
    o_ref[...] = jnp.dot(h_ref[...], w_ref[...], preferred_element_type=F32)


def _halo_up(halo_h, w_up):
    nh, d = halo_h.shape
    ff2 = w_up.shape[1]
    n_blk = 4
    cols = ff2 // n_blk
    return pl.pallas_call(
        _halo_body,
        out_shape=jax.ShapeDtypeStruct((nh, ff2), F32),
        grid=(n_blk,),
        in_specs=[pl.BlockSpec((nh, d), lambda i: (0, 0)),
                  pl.BlockSpec((d, cols), lambda i: (0, i))],
        out_specs=pl.BlockSpec((nh, cols), lambda i: (0, i)),
        compiler_params=_cparams(("arbitrary",)),
        name="mlp_halo_up",
    )(halo_h, w_up)


def _mlp_body(x1_ref, h2_ref, wup_ref, wdn_ref, par_ref, halo_ref, o_ref, gbuf_ref, vbuf_ref, *, chunks):
    tf = h2_ref.shape[1]
    ff = wdn_ref.shape[0]
    nt = pl.num_programs(1)
    row = (pl.program_id(0) * nt + pl.program_id(1)) * 2
    h = h2_ref[0]

    def conv(buf_ref, u, c0, width):
        buf_ref[MLP_PAD:MLP_PAD + tf, :width] = u
        buf_ref[MLP_PAD - 1:MLP_PAD, :width] = halo_ref[pl.ds(row, 1), c0:c0 + width]
        buf_ref[MLP_PAD + tf:MLP_PAD + tf + 1, :width] = halo_ref[pl.ds(row + 1, 1), c0:c0 + width]
        prev = buf_ref[MLP_PAD - 1:MLP_PAD - 1 + tf, :width]
        nxt = buf_ref[MLP_PAD + 1:MLP_PAD + 1 + tf, :width]
        p = par_ref[:, c0:c0 + width]
        return p[0:1] * prev + p[1:2] * u + p[2:3] * nxt + p[3:4]

    def up(c0, width):
        return (jnp.dot(h, wup_ref[:, c0:c0 + width], preferred_element_type=F32),
                jnp.dot(h, wup_ref[:, ff + c0:ff + c0 + width], preferred_element_type=F32))

    y = x1_ref[0]
    nxt_up = up(*chunks[0])
    for n, (c0, width) in enumerate(chunks):
        ug, uv = nxt_up
        if n + 1 < len(chunks):
            nxt_up = up(*chunks[n + 1])
        gate = conv(gbuf_ref, ug, c0, width)
        val = conv(vbuf_ref, uv, ff + c0, width)
        act = gate / (1.0 + jnp.exp(-gate)) * val
        y = y + jnp.dot(act.astype(BF16), wdn_ref[c0:c0 + width, :], preferred_element_type=F32)
    o_ref[0] = y


def _mlp(x1, h2, w_up, w_down, conv_par, halo_up, chunks):
    b, s, d = x1.shape
    tf = TOK_TILE
    ff2 = w_up.shape[1]
    nh = halo_up.shape[0]
    whole = lambda i, j: (0, 0)
    body = functools.partial(_mlp_body, chunks=chunks)
    const = dict(pipeline_mode=pl.Buffered(1))
    wmax = max(w for _, w in chunks)
    return pl.pallas_call(
        body,
        out_shape=jax.ShapeDtypeStruct((b, s, d), F32),
        grid=(b, s // tf),
        in_specs=[pl.BlockSpec((1, tf, d), lambda i, j: (i, j, 0)),
                  pl.BlockSpec((1, tf, d), lambda i, j: (i, j, 0)),
                  pl.BlockSpec((d, ff2), whole, **const),
                  pl.BlockSpec((ff2 // 2, d), whole, **const),
                  pl.BlockSpec((8, ff2), whole, **const),
                  pl.BlockSpec((nh, ff2), whole, **const)],
        out_specs=pl.BlockSpec((1, tf, d), lambda i, j: (i, j, 0)),
        scratch_shapes=[pltpu.VMEM((tf + 2 * MLP_PAD, wmax), F32)] * 2,
        compiler_params=_cparams(("arbitrary", "arbitrary")),
        name="gated_conv_mlp",
    )(x1, h2, w_up, w_down, conv_par, halo_up)


def _halo_rows(edges, nt):
    b, _, d = edges.shape
    ez = jnp.concatenate([edges, jnp.zeros((b, 1, d), edges.dtype)], axis=1)
    idx = []
    for j in range(nt):
        idx += [2 * j - 1 if j > 0 else 2 * nt, 2 * j + 2 if j < nt - 1 else 2 * nt]
    rows = jnp.take(ez, jnp.asarray(idx, jnp.int32), axis=1).reshape(b * nt * 2, d)
    pad = (-rows.shape[0]) % BF16_ROWS
    return jnp.pad(rows, ((0, pad), (0, 0))).astype(BF16)


def _rope_tables(s):
    def freqs(dim):
        return 1.0 / (ROPE_THETA ** (jnp.arange(0, dim, 2, dtype=F32) / dim))
    t = jnp.arange(s)
    rows = (t // GRID_W).astype(F32)
    cols = (t % GRID_W).astype(F32)
    pos = t.astype(F32)
    fa = freqs(HEAD_DIM // 2)
    ar, ac = fa[:, None] * rows[None, :], fa[:, None] * cols[None, :]
    am = freqs(C_ROPE)[:, None] * pos[None, :]
    cosa = jnp.concatenate([jnp.cos(ar), jnp.cos(ar), jnp.cos(ac), jnp.cos(ac)], axis=0)
    sina = jnp.concatenate([-jnp.sin(ar), jnp.sin(ar), -jnp.sin(ac), jnp.sin(ac)], axis=0)
    cosm = jnp.concatenate([jnp.cos(am), jnp.cos(am)], axis=0)
    sinm = jnp.concatenate([-jnp.sin(am), jnp.sin(am)], axis=0)
    return cosa, sina, cosm, sinm


def kernel(x, norm1_g, w_in, a_qn_g, a_kn_g, b_qn_g, b_kn_g, b_lam_q1, b_lam_k1, b_lam_q2, b_lam_k2, b_sub_g,
           c_qa_g, c_kva_g, c_wqb, c_wkvb, c_qn_g, c_kn_g, d_qn_g, d_kn_g, w_out, norm2_g, w_up, conv_w, conv_b,
           w_down):
    b, s, d = x.shape
    depth = w_in.shape[0]
    ff = w_down.shape[1]
    nt = s // TOK_TILE
    assert s % TOK_TILE == 0 and TOK_TILE % KSTEP == 0 and TQ % KSTEP == 0 and s % TQ == 0 and s % GRID_W == 0
    assert s <= 1 << (POS_SPLIT_BITS + BF16_MANTISSA_BITS) and s % TQ_STEP == 0
    for window, dil in DILATED_CONFIGS:
        assert window // (2 * dil) == BAND_RADIUS and (s // dil) % min(BAND_Q, s // dil) == 0
        assert (s // dil) % min(BAND_QSTEP, s // dil) == 0 and BAND_QSTEP % BAND_Q == 0
        assert TOK_TILE % (dil * BF16_ROWS) == 0

    tr = lambda w: jnp.swapaxes(w, 1, 2).astype(BF16)
    gains = jnp.concatenate([norm1_g, a_qn_g, a_kn_g, b_qn_g, b_kn_g, c_qa_g, c_kva_g, c_qn_g, c_kn_g,
                             d_qn_g, d_kn_g], axis=1)[:, :, None]
    conv_par = jnp.concatenate([conv_w, conv_b[:, None, :], jnp.zeros((depth, 4, 2 * ff), F32)], axis=1)
    slopes = [2.0 ** (-8.0 * i / (2 * N_HEADS)) for i in range(1, 2 * N_HEADS + 1)]
    slopes_b, slopes_d = slopes[:N_HEADS], tuple(slopes[N_HEADS:])
    lam_init = jnp.asarray([0.8 - 0.6 * math.exp(-0.3 * l) for l in range(depth)], F32)
    lam = (jnp.exp(jnp.sum(b_lam_q1 * b_lam_k1, axis=1)) - jnp.exp(jnp.sum(b_lam_q2 * b_lam_k2, axis=1)) + lam_init)
    scal = jnp.concatenate([lam[:, None], (1.0 - lam_init)[:, None],
                            jnp.broadcast_to(jnp.asarray(slopes_b, F32), (depth, N_HEADS)),
                            jnp.zeros((depth, 2), F32)], axis=1)
    layers = dict(w_in=tr(w_in), wqb=tr(c_wqb), wkvb=tr(c_wkvb), w_out=tr(w_out),
                  w_up=w_up.astype(BF16), w_down=w_down.astype(BF16),
                  gains=gains, g2=norm2_g[:, None, :], sub_g=b_sub_g[:, :, None], conv_par=conv_par, scal=scal)
    cosa, sina, cosm, sinm = _rope_tables(s)
    assert ff % LANES == 0
    chunks = tuple((c0, min(MLP_CHUNK, ff - c0)) for c0 in range(0, ff, MLP_CHUNK))

    def layer(xs, p):
        (qa, ka, va, qb, kb, vb, qc, kc, vc, dq1, dq4, dq16, dk1, dk4, dk16, dv1, dv4, dv16) = _project(
            xs, p["w_in"], p["wqb"], p["wkvb"], p["gains"], cosa, sina, cosm, sinm)
        oa = _dense_attention(qa, ka, va, HEAD_DIM, N_HEADS // A_KV_HEADS)
        ob = _diff_attention(p["scal"], slopes_b, qb, kb, vb, p["sub_g"])
        oc = _dense_attention(qc, kc, vc, C_QK, 1)
        branches = [_band_attention(q_, k_, v_, dil, slopes_d)
                    for (_, dil), q_, k_, v_ in zip(DILATED_CONFIGS, (dq1, dq4, dq16), (dk1, dk4, dk16), (dv1, dv4, dv16))]
        od = _band_combine(branches)
        x1, h2, edges = _out_project(xs, oa, ob, oc, od, p["w_out"], p["g2"])
        halo = _halo_up(_halo_rows(edges, nt), p["w_up"])
        return _mlp(x1, h2, p["w_up"], p["w_down"], p["conv_par"], halo, chunks), None

    out, _ = lax.scan(layer, x, layers)
    return out
```

```python
import functools
import math

import jax
import jax.numpy as jnp
import numpy as np
from jax import lax
from jax.experimental import pallas as pl
from jax.experimental.pallas import tpu as pltpu

F32 = jnp.float32
BF16 = jnp.bfloat16

HEAD_DIM = 64
N_HEADS = 4
A_KV_HEADS = 2
B_QK_DIM = 32
C_Q_LORA = 256
C_KV_LORA = 128
C_NOPE = 64
C_ROPE = 32
C_QK = C_NOPE + C_ROPE
DILATED_CONFIGS = ((128, 1), (512, 4), (2048, 16))
D_WIDTH = N_HEADS * HEAD_DIM
GRID_W = 64
ROPE_THETA = 10000.0
NORM_EPS = 1e-6
LOG2E = 1.4426950408889634


def _bf16_pieces(x, n):
    out = []
    for _ in range(n):
        u = np.float32(x).view(np.uint32)
        u = np.uint32((int(u) + 0x7FFF + ((int(u) >> 16) & 1)) & 0xFFFF0000)
        piece = float(u.view(np.float32))
        out.append(piece)
        x = x - piece
    return tuple(out)


LOG2E_PIECES = _bf16_pieces(LOG2E, 3)
POS_SPLIT_BITS = 6
BF16_MANTISSA_BITS = 8

_SPLITS = (256, 128, 128, 256, 256, 256, C_Q_LORA, C_KV_LORA, C_ROPE, 256, 256, 256)
_OFF = [0]
for _n in _SPLITS:
    _OFF.append(_OFF[-1] + _n)
(O_AQ, O_AK, O_AV, O_BQ, O_BK, O_BV, O_CQL, O_CKVL, O_CKR, O_DQ, O_DK, O_DV, IN_WIDTH) = _OFF

_GAIN_SIZES = (("n1", 1024), ("aq", 64), ("ak", 64), ("bq", 32), ("bk", 32), ("cqa", 256), ("ckva", 128),
               ("cqn", 96), ("ckn", 96), ("dq", 64), ("dk", 64))
G_OFF = {}
_o = 0
for _k, _n in _GAIN_SIZES:
    G_OFF[_k] = (_o, _n)
    _o += _n
G_TOTAL = _o

LANES = 128
BF16_ROWS = 16
TOK_TILE = 512
MLP_CHUNK = 1024
MLP_PAD = 8
TQ = 512
TQ_STEP = 2048
KSTEP = 256
SCORES_AHEAD = 2
BAND_Q = 128
BAND_QSTEP = 1024
BAND_MASKED = 1e30
BAND_RADIUS = 64
VMEM_LIMIT = 56 * 1024 * 1024


def _cparams(sem, flags=None):
    return pltpu.CompilerParams(dimension_semantics=sem, vmem_limit_bytes=VMEM_LIMIT, flags=flags)


def _rms_rows(v, g_col):
    ms = jnp.mean(v * v, axis=0, keepdims=True)
    return v * lax.rsqrt(ms + NORM_EPS) * g_col


def _swap_halves(x, n):
    parts = []
    for i in range(0, x.shape[0], 2 * n):
        parts.append(x[i + n:i + 2 * n])
        parts.append(x[i:i + n])
    return jnp.concatenate(parts, axis=0)


def _rope_rows(x, cos, sin_signed, n):
    return x * cos + _swap_halves(x, n) * sin_signed


def _store_dilated(scr_ref, val, out_refs):
    tm, c = val.shape
    for g in range(c // LANES):
        scr_ref[g] = val[:, LANES * g:LANES * (g + 1)]
    for (_, dil), out in zip(DILATED_CONFIGS, out_refs):
        if dil == 1:
            out[0] = val.astype(BF16)
            continue
        n = tm // dil
        for r in range(dil):
            for g in range(c // LANES):
                lo = c * r + LANES * g
                out[0, :, lo:lo + LANES] = scr_ref[g, pl.ds(r, n, stride=dil), :].astype(BF16)


def _store_key_blocks(v_ref, v):
    for c in range(v.shape[1] // KSTEP):
        v_ref[0, c] = v[:, c * KSTEP:(c + 1) * KSTEP]


def _proj_body(x_ref, w_ref, wqb_ref, wkvb_ref, g_ref, cosa_ref, sina_ref, cosm_ref, sinm_ref,
               qa_ref, ka_ref, va_ref, qb_ref, kb_ref, vb_ref, qc_ref, kc_ref, vc_ref,
               dq1_ref, dq4_ref, dq16_ref, dk1_ref, dk4_ref, dk16_ref, dv1_ref, dv4_ref, dv16_ref,
               scr_ref):
    tm = x_ref.shape[1]

    def gain(name):
        o, n = G_OFF[name]
        return g_ref[o:o + n, :]

    x = x_ref[0].T
    h = _rms_rows(x, gain("n1")).astype(BF16)

    def section(lo, hi):
        part = jnp.dot(w_ref[lo:hi, :], h, preferred_element_type=F32)
        return lambda o, n: part[o - lo:o - lo + n]

    cosa, sina = cosa_ref[...], sina_ref[...]
    cosm, sinm = cosm_ref[...], sinm_ref[...]
    zeros64 = jnp.zeros((64, tm), F32)
    zeros32 = jnp.zeros((32, tm), F32)

    def prep_a(proj):
        sa = HEAD_DIM ** -0.5 * LOG2E
        for hh in range(N_HEADS):
            q = _rope_rows(_rms_rows(proj(O_AQ + 64 * hh, 64), gain("aq")), cosa, sina, 16) * sa
            qa_ref[0, 64 * hh:64 * hh + 64, :] = q.astype(BF16)
        for g in range(A_KV_HEADS):
            k = _rope_rows(_rms_rows(proj(O_AK + 64 * g, 64), gain("ak")), cosa, sina, 16)
            kpad = jnp.concatenate([k, zeros64], axis=0)
            ka_ref[0, :, LANES * g:LANES * (g + 1)] = kpad.T.astype(BF16)
        _store_key_blocks(va_ref, proj(O_AV, 128).astype(BF16))

    def prep_b(proj):
        sb = B_QK_DIM ** -0.5 * LOG2E
        for i in range(2 * N_HEADS):
            qb_ref[0, 32 * i:32 * i + 32, :] = (_rms_rows(proj(O_BQ + 32 * i, 32), gain("bq")) * sb).astype(BF16)
        kpos = pl.program_id(1) * tm + lax.broadcasted_iota(jnp.int32, (BF16_ROWS, tm), 1)
        frow = lax.broadcasted_iota(jnp.int32, (BF16_ROWS, tm), 0)
        pos_hi = ((kpos >> POS_SPLIT_BITS) << POS_SPLIT_BITS).astype(F32)
        pos_lo = (kpos & ((1 << POS_SPLIT_BITS) - 1)).astype(F32)
        kfeat = jnp.where(frow < 3, pos_hi, jnp.where(frow < 6, pos_lo, 0.0))
        zeros48 = jnp.zeros((48, tm), F32)
        for hh in range(N_HEADS):
            k1 = _rms_rows(proj(O_BK + 64 * hh, 32), gain("bk"))
            k2 = _rms_rows(proj(O_BK + 64 * hh + 32, 32), gain("bk"))
            kpad = jnp.concatenate([k1, k2, kfeat, zeros48], axis=0)
            kb_ref[0, :, LANES * hh:LANES * (hh + 1)] = kpad.T.astype(BF16)
        _store_key_blocks(vb_ref, proj(O_BV, 256).astype(BF16))

    def prep_c(proj):
        sc = C_QK ** -0.5 * LOG2E
        cq = _rms_rows(proj(O_CQL, C_Q_LORA), gain("cqa")).astype(BF16)
        qc = jnp.dot(wqb_ref[...], cq, preferred_element_type=F32)
        ckv = _rms_rows(proj(O_CKVL, C_KV_LORA), gain("ckva")).astype(BF16)
        kvc = jnp.dot(wkvb_ref[...], ckv, preferred_element_type=F32)
        kr = proj(O_CKR, C_ROPE)
        vcs = []
        for hh in range(N_HEADS):
            q = _rms_rows(qc[C_QK * hh:C_QK * (hh + 1)], gain("cqn"))
            q = jnp.concatenate([q[:C_NOPE], _rope_rows(q[C_NOPE:], cosm, sinm, 16)], axis=0) * sc
            qc_ref[0, C_QK * hh:C_QK * (hh + 1), :] = q.astype(BF16)
            kk = jnp.concatenate([kvc[128 * hh:128 * hh + C_NOPE], kr], axis=0)
            kk = _rms_rows(kk, gain("ckn"))
            kk = jnp.concatenate([kk[:C_NOPE], _rope_rows(kk[C_NOPE:], cosm, sinm, 16), zeros32], axis=0)
            kc_ref[0, :, LANES * hh:LANES * (hh + 1)] = kk.T.astype(BF16)
            vcs.append(kvc[128 * hh + C_NOPE:128 * (hh + 1)])
        _store_key_blocks(vc_ref, jnp.concatenate(vcs, axis=0).astype(BF16))

    def prep_d(proj):
        sd = HEAD_DIM ** -0.5 * LOG2E
        qs, ks, vs = [], [], []
        for g in range(2):
            qg, kg = [], []
            for hh in (2 * g, 2 * g + 1):
                qg.append(_rms_rows(proj(O_DQ + 64 * hh, 64), gain("dq")) * sd)
                kg.append(_rms_rows(proj(O_DK + 64 * hh, 64), gain("dk")))
            qs.append(jnp.concatenate(qg, axis=0).T)
            ks.append(jnp.concatenate(kg, axis=0).T)
            vs.append(proj(O_DV + 128 * g, 128).T)
        _store_dilated(scr_ref.at[0], jnp.concatenate(qs, axis=1), (dq1_ref, dq4_ref, dq16_ref))
        _store_dilated(scr_ref.at[1], jnp.concatenate(ks, axis=1), (dk1_ref, dk4_ref, dk16_ref))
        _store_dilated(scr_ref.at[2], jnp.concatenate(vs, axis=1), (dv1_ref, dv4_ref, dv16_ref))

    sec_a = section(O_AQ, O_BQ)
    sec_b = section(O_BQ, O_CQL)
    prep_a(sec_a)
    sec_c = section(O_CQL, O_DQ)
    prep_b(sec_b)
    sec_d = section(O_DQ, IN_WIDTH)
    prep_c(sec_c)
    prep_d(sec_d)


def _project(x, w_in_t, wqb_t, wkvb_t, gains, cosa, sina, cosm, sinm):
    b, s, d = x.shape
    tm = TOK_TILE
    nt = s // tm
    fm = lambda rows: jax.ShapeDtypeStruct((b, rows, s), BF16)
    tmj = lambda cols: jax.ShapeDtypeStruct((b, s, cols), BF16)
    vblk = lambda rows: jax.ShapeDtypeStruct((b, s // KSTEP, rows, KSTEP), BF16)
    dviews = tuple(jax.ShapeDtypeStruct((b, s // dil, dil * D_WIDTH), BF16) for _, dil in DILATED_CONFIGS)
    out_shape = (fm(256), tmj(256), vblk(128),
                 fm(256), tmj(512), vblk(256),
                 fm(384), tmj(512), vblk(256)) + dviews * 3
    fm_spec = lambda rows: pl.BlockSpec((1, rows, tm), lambda i, j: (i, 0, j))
    tm_spec = lambda cols: pl.BlockSpec((1, tm, cols), lambda i, j: (i, j, 0))
    v_spec = lambda rows: pl.BlockSpec((1, tm // KSTEP, rows, KSTEP), lambda i, j: (i, j, 0, 0))
    dspecs = tuple(pl.BlockSpec((1, tm // dil, dil * D_WIDTH), lambda i, j: (i, j, 0)) for _, dil in DILATED_CONFIGS)
    out_specs = (fm_spec(256), tm_spec(256), v_spec(128),
                 fm_spec(256), tm_spec(512), v_spec(256),
                 fm_spec(384), tm_spec(512), v_spec(256)) + dspecs * 3
    whole = lambda i, j: (0, 0)
    in_specs = [
        pl.BlockSpec((1, tm, d), lambda i, j: (i, j, 0)),
        pl.BlockSpec((IN_WIDTH, d), whole),
        pl.BlockSpec((N_HEADS * C_QK, C_Q_LORA), whole),
        pl.BlockSpec((N_HEADS * 128, C_KV_LORA), whole),
        pl.BlockSpec((G_TOTAL, 1), whole),
        pl.BlockSpec((64, tm), lambda i, j: (0, j)),
        pl.BlockSpec((64, tm), lambda i, j: (0, j)),
        pl.BlockSpec((32, tm), lambda i, j: (0, j)),
        pl.BlockSpec((32, tm), lambda i, j: (0, j)),
    ]
    return pl.pallas_call(
        _proj_body,
        out_shape=out_shape,
        grid=(b, nt),
        in_specs=in_specs,
        out_specs=out_specs,
        scratch_shapes=[pltpu.VMEM((3, D_WIDTH // LANES, tm, LANES), F32)],
        compiler_params=_cparams(("parallel", "arbitrary")),
        name="project_heads",
    )(x, w_in_t, wqb_t, wkvb_t, gains, cosa, sina, cosm, sinm)


def _online_update(s, v_aug, carry, shift=None):
    m, acc = carry
    smax = jnp.max(s, axis=0, keepdims=True)
    if shift is None:
        m_new = jnp.maximum(m, smax)
        p = jnp.exp2(s - m_new).astype(BF16)
    else:
        m_new = jnp.maximum(m, smax + shift)
        p = jnp.exp2(s - (m_new - shift)).astype(BF16)
    alpha = jnp.exp2(m - m_new)
    acc = alpha * acc + jnp.dot(v_aug, p, preferred_element_type=F32)
    return m_new, acc


def _init_carry(dv, tq):
    return (jnp.full((1, tq), -jnp.inf, F32), jnp.zeros((dv + BF16_ROWS, tq), F32))


def _finish(carry, dv):
    _, acc = carry
    return acc[:dv] / acc[dv:dv + 1]


def _run_pipelined(n_jobs, n_blocks, scores_fn, update_fn, init_fn, finish_fn):
    steps = [(t, i) for t in range(n_jobs) for i in range(n_blocks)]
    pending = [scores_fn(*steps[n]) for n in range(min(SCORES_AHEAD, len(steps)))]
    carry = None
    for n, (t, i) in enumerate(steps):
        s = pending.pop(0)
        if n + SCORES_AHEAD < len(steps):
            pending.append(scores_fn(*steps[n + SCORES_AHEAD]))
        if i == 0:
            carry = init_fn()
        carry = update_fn(t, i, s, carry)
        if i == n_blocks - 1:
            finish_fn(t, carry)


def _dense_body(q_ref, k_ref, v_ref, o_ref):
    dq = q_ref.shape[1]
    nkb, dv, ks = v_ref.shape[1], v_ref.shape[2], v_ref.shape[3]
    tq = TQ
    ones = jnp.ones((BF16_ROWS, ks), BF16)
    zpad = jnp.zeros((LANES - dq, tq), BF16)
    qpads = [jnp.concatenate([q_ref[0, :, t * tq:(t + 1) * tq], zpad], axis=0) for t in range(q_ref.shape[2] // tq)]

    def scores(t, i):
        return jnp.dot(k_ref[0, i * ks:(i + 1) * ks, :], qpads[t], preferred_element_type=F32)

    def update(t, i, s, carry):
        return _online_update(s, jnp.concatenate([v_ref[0, i], ones], axis=0), carry)

    def finish(t, carry):
        o_ref[0, :, t * tq:(t + 1) * tq] = _finish(carry, dv).astype(o_ref.dtype)

    _run_pipelined(len(qpads), nkb, scores, update, lambda: _init_carry(dv, tq), finish)


def _dense_attention(q, k, v, dq, q_per_kv):
    b, rows, s = q.shape
    nh = rows // dq
    nkb, tk = v.shape[1], v.shape[3]
    return pl.pallas_call(
        _dense_body,
        out_shape=jax.ShapeDtypeStruct((b, nh * HEAD_DIM, s), BF16),
        grid=(b, nh, s // TQ_STEP),
        in_specs=[
            pl.BlockSpec((1, dq, TQ_STEP), lambda i, h, j: (i, h, j)),
            pl.BlockSpec((1, s, LANES), lambda i, h, j: (i, 0, h // q_per_kv)),
            pl.BlockSpec((1, nkb, HEAD_DIM, tk), lambda i, h, j: (i, 0, h // q_per_kv, 0)),
        ],
        out_specs=pl.BlockSpec((1, HEAD_DIM, TQ_STEP), lambda i, h, j: (i, h, j)),
        compiler_params=_cparams(("parallel", "arbitrary", "arbitrary")),
        name="dense_attention",
    )(q, k, v)


def _diff_body(sc_ref, q_ref, k_ref, v_ref, g_ref, bias_ref, o_ref):
    tq = TQ
    n_tiles = q_ref.shape[2] // tq
    nkb, dv, ks = v_ref.shape[1], v_ref.shape[2], v_ref.shape[3]
    hh = pl.program_id(1)
    lam = sc_ref[0]
    out_scale = sc_ref[1]
    slope = sc_ref[2 + hh]
    z32 = jnp.zeros((B_QK_DIM, tq), BF16)
    z48 = jnp.zeros((48, tq), BF16)
    zfeat = jnp.zeros((BF16_ROWS, tq), BF16)
    ones = jnp.ones((BF16_ROWS, ks), BF16)
    frow = lax.broadcasted_iota(jnp.int32, (BF16_ROWS, tq), 0)
    piece = jnp.where(frow % 3 == 0, LOG2E_PIECES[0], jnp.where(frow % 3 == 1, LOG2E_PIECES[1], LOG2E_PIECES[2]))
    cfeat = jnp.where(frow < 6, piece, 0.0)
    lane_pos = lax.broadcasted_iota(jnp.int32, (1, tq), 1)

    n_mixed = tq // ks

    def tile_index(job):
        return pl.program_id(2) * n_tiles + job // 2

    def block_index(job, ii):
        return lax.rem(tile_index(job) * n_mixed + ii, nkb)

    def side(job, ii):
        return jnp.where(block_index(job, ii) < tile_index(job) * n_mixed, 1.0, -1.0).astype(F32)

    def qrows(job):
        q = q_ref[0, :, (job // 2) * tq:(job // 2 + 1) * tq]
        if job % 2 == 0:
            return jnp.concatenate([q[:B_QK_DIM], z32], axis=0)
        return jnp.concatenate([z32, q[B_QK_DIM:]], axis=0)

    def scores(job, ii):
        i = block_index(job, ii)
        k = k_ref[0, pl.ds(pl.multiple_of(i * ks, ks), ks), :]
        feat = zfeat if ii < n_mixed else (cfeat * (side(job, ii) * slope)).astype(BF16)
        return jnp.dot(k, jnp.concatenate([qrows(job), feat, z48], axis=0), preferred_element_type=F32)

    def update(job, ii, s, carry):
        v_aug = jnp.concatenate([v_ref[0, block_index(job, ii)], ones], axis=0)
        if ii < n_mixed:
            return _online_update(s - bias_ref[0, ii], v_aug, carry)
        qpos = (tile_index(job) * tq + lane_pos).astype(F32)
        return _online_update(s, v_aug, carry, qpos * (-(side(job, ii) * slope * LOG2E)))

    res = []

    def finish(job, carry):
        res.append(_finish(carry, dv))
        if job % 2 == 1:
            o = res[job - 1] - lam * res[job]
            t = job // 2
            o_ref[0, :, t * tq:(t + 1) * tq] = (_rms_rows(o, g_ref[...]) * out_scale).astype(o_ref.dtype)

    _run_pipelined(2 * n_tiles, nkb, scores, update, lambda: _init_carry(dv, tq), finish)


def _diff_attention(scal, slopes, q, k, v, sub_g):
    b, rows, s = q.shape
    nkb, tk = v.shape[1], v.shape[3]
    n_mixed = TQ // tk
    jj = jnp.arange(tk)[None, :, None] + tk * jnp.arange(n_mixed)[:, None, None]
    dist = jnp.abs(jj - jnp.arange(TQ)[None, None, :]).astype(F32)
    bias = dist[None] * (LOG2E * jnp.asarray(slopes, F32))[:, None, None, None]
    return pl.pallas_call(
        _diff_body,
        out_shape=jax.ShapeDtypeStruct((b, rows, s), BF16),
        grid=(b, N_HEADS, s // TQ_STEP),
        in_specs=[
            pl.BlockSpec(memory_space=pltpu.SMEM),
            pl.BlockSpec((1, 2 * B_QK_DIM, TQ_STEP), lambda i, h, j: (i, h, j)),
            pl.BlockSpec((1, s, LANES), lambda i, h, j: (i, 0, h)),
            pl.BlockSpec((1, nkb, HEAD_DIM, tk), lambda i, h, j: (i, 0, h, 0)),
            pl.BlockSpec((HEAD_DIM, 1), lambda i, h, j: (0, 0)),
            pl.BlockSpec((1, n_mixed, tk, TQ), lambda i, h, j: (h, 0, 0, 0)),
        ],
        out_specs=pl.BlockSpec((1, HEAD_DIM, TQ_STEP), lambda i, h, j: (i, h, j)),
        compiler_params=_cparams(("parallel", "arbitrary", "arbitrary")),
        name="diff_attention",
    )(scal, q, k, v, sub_g, bias)


def _band_body(q_ref, k_ref, v_ref, pen_ref, o_ref, lse_ref, *, qb):
    l_len = k_ref.shape[1]
    win = pen_ref.shape[3]
    lane = lax.broadcasted_iota(jnp.int32, (qb, LANES), 1)
    n_sub = q_ref.shape[1] // qb
    n_groups = q_ref.shape[2] // LANES
    for sb in range(n_sub):
        jb = pl.program_id(2) * n_sub + sb
        start = jnp.clip(jb * qb - BAND_RADIUS, 0, l_len - win)
        start = pl.multiple_of(start, BAND_RADIUS)
        variant = (jb * qb - start) // BAND_RADIUS
        q = q_ref[0, sb * qb:(sb + 1) * qb, :]
        kwin = k_ref[0, pl.ds(start, win), :]
        vwin = v_ref[0, pl.ds(start, win), :]
        for gc in range(n_groups):
            g = gc % 2
            q2 = q[:, LANES * gc:LANES * (gc + 1)].astype(F32)
            k2 = kwin[:, LANES * gc:LANES * (gc + 1)]
            v2 = vwin[:, LANES * gc:LANES * (gc + 1)]
            out_g = jnp.zeros((qb, LANES), F32)
            lse_g = jnp.zeros((qb, LANES), F32)
            for hh in range(2):
                own = (lane >= HEAD_DIM * hh) & (lane < HEAD_DIM * (hh + 1))
                qm = jnp.where(own, q2, 0.0).astype(BF16)
                s = lax.dot_general(qm, k2, (((1,), (1,)), ((), ())), preferred_element_type=F32)
                s = s - pen_ref[variant, 2 * g + hh]
                m = jnp.max(s, axis=1, keepdims=True)
                p = jnp.exp2(s - m)
                lsum = jnp.sum(p, axis=1, keepdims=True)
                o = jnp.dot(p.astype(BF16), v2, preferred_element_type=F32) / lsum
                lse = m + jnp.log2(lsum)
                out_g = jnp.where(own, o, out_g)
                lse_g = jnp.where(own, lse, lse_g)
            o_ref[0, sb * qb:(sb + 1) * qb, LANES * gc:LANES * (gc + 1)] = out_g.astype(o_ref.dtype)
            lse_ref[0, sb * qb:(sb + 1) * qb, LANES * gc:LANES * (gc + 1)] = lse_g


def _band_penalty(qb, win, dil, slopes):
    rows = jnp.arange(qb)[None, :, None] + BAND_RADIUS * jnp.arange(3)[:, None, None]
    rel = jnp.abs(rows - jnp.arange(win)[None, None, :])
    scale = jnp.asarray(slopes, F32) * (float(dil) * LOG2E)
    pen = rel.astype(F32)[:, None] * scale[None, :, None, None]
    return jnp.where((rel <= BAND_RADIUS)[:, None], pen, BAND_MASKED)


def _band_attention(q, k, v, dil, slopes):
    b, l_len, cw = q.shape
    c = cw // dil
    qb = min(BAND_Q, l_len)
    qstep = min(BAND_QSTEP, l_len)
    cps = min(dil, max(1, BAND_QSTEP // l_len))
    assert dil % cps == 0
    win = min(qb + 2 * BAND_RADIUS, l_len)
    pen = _band_penalty(qb, win, dil, slopes)
    body = functools.partial(_band_body, qb=qb)
    return pl.pallas_call(
        body,
        out_shape=(jax.ShapeDtypeStruct((b, l_len, cw), BF16), jax.ShapeDtypeStruct((b, l_len, cw), F32)),
        grid=(b, dil // cps, l_len // qstep),
        in_specs=[
            pl.BlockSpec((1, qstep, c * cps), lambda i, r, j: (i, j, r)),
            pl.BlockSpec((1, l_len, c * cps), lambda i, r, j: (i, 0, r)),
            pl.BlockSpec((1, l_len, c * cps), lambda i, r, j: (i, 0, r)),
            pl.BlockSpec(pen.shape, lambda i, r, j: (0, 0, 0, 0)),
        ],
        out_specs=(pl.BlockSpec((1, qstep, c * cps), lambda i, r, j: (i, j, r)),) * 2,
        compiler_params=_cparams(("parallel", "arbitrary", "arbitrary")),
        name="band_attention_d%d" % dil,
    )(q, k, v, pen)


def _band_combine(in_refs, scr_ref, tm, c):
    vals = []
    for idx, (_, dil) in enumerate(DILATED_CONFIGS):
        for which in range(2):
            src = in_refs[2 * idx + which]
            if dil == 1:
                vals.append(src[0].astype(F32))
                continue
            groups = []
            for g in range(c // LANES):
                slot = scr_ref.at[(2 * idx + which) * (c // LANES) + g]
                for r in range(dil):
                    lo = c * r + LANES * g
                    slot[pl.ds(r, tm // dil, stride=dil), :] = src[0, :, lo:lo + LANES].astype(F32)
                groups.append(slot[...])
            vals.append(jnp.concatenate(groups, axis=1))
    outs, lses = vals[0::2], vals[1::2]
    m = functools.reduce(jnp.maximum, lses)
    ws = [jnp.exp2(x - m) for x in lses]
    num = functools.reduce(lambda a, t: a + t, [w * o for w, o in zip(ws, outs)])
    den = functools.reduce(lambda a, t: a + t, ws)
    return num / den


def _outproj_body(*refs):
    n_d = 2 * len(DILATED_CONFIGS)
    x_ref, oa_ref, ob_ref, oc_ref = refs[:4]
    d_refs = refs[4:4 + n_d]
    w_ref, g_ref, x1_ref, h2_ref, edge_ref, scr_ref = refs[4 + n_d:]
    tm = x_ref.shape[1]
    j = pl.program_id(1)
    mix = jnp.concatenate([oa_ref[0], ob_ref[0], oc_ref[0]], axis=0)
    od = _band_combine(d_refs, scr_ref, tm, D_WIDTH).astype(BF16)
    w = w_ref[...]
    y = jnp.dot(w[:, :768], mix, preferred_element_type=F32)
    y = y + lax.dot_general(w[:, 768:], od, (((1,), (1,)), ((), ())), preferred_element_type=F32)
    x1 = x_ref[0] + y.T
    x1_ref[0] = x1
    ms = jnp.mean(x1 * x1, axis=1, keepdims=True)
    h2 = (x1 * lax.rsqrt(ms + NORM_EPS) * g_ref[...]).astype(BF16)
    h2_ref[0] = h2
    edge_ref[0, pl.ds(2 * j, 1), :] = h2[0:1].astype(F32)
    edge_ref[0, pl.ds(2 * j + 1, 1), :] = h2[tm - 1:tm].astype(F32)


def _out_project(x, oa, ob, oc, branches, w_out_t, g2):
    b, s, d = x.shape
    tm = TOK_TILE
    nt = s // tm
    fm = lambda rows: pl.BlockSpec((1, rows, tm), lambda i, j: (i, 0, j))
    tmj = pl.BlockSpec((1, tm, d), lambda i, j: (i, j, 0))
    d_specs = []
    for (_, dil) in DILATED_CONFIGS:
        d_specs += [pl.BlockSpec((1, tm // dil, dil * D_WIDTH), lambda i, j: (i, j, 0))] * 2
    flat = [a for pair in branches for a in pair]
    return pl.pallas_call(
        _outproj_body,
        out_shape=(jax.ShapeDtypeStruct((b, s, d), F32), jax.ShapeDtypeStruct((b, s, d), BF16),
                   jax.ShapeDtypeStruct((b, 2 * nt, d), F32)),
        grid=(b, nt),
        in_specs=[tmj, fm(256), fm(256), fm(256)] + d_specs + [
            pl.BlockSpec((d, d), lambda i, j: (0, 0)),
            pl.BlockSpec((1, d), lambda i, j: (0, 0))],
        out_specs=(tmj, tmj, pl.BlockSpec((1, 2 * nt, d), lambda i, j: (i, 0, 0))),
        scratch_shapes=[pltpu.VMEM((2 * len(DILATED_CONFIGS) * (D_WIDTH // LANES), tm, LANES), F32)],
        compiler_params=_cparams(("parallel", "arbitrary")),
        name="out_project",
    )(x, oa, ob, oc, *flat, w_out_t, g2)


def _halo_body(h_ref, w_ref, o_ref):
    o_ref[...] = jnp.dot(h_ref[...], w_ref[...], preferred_element_type=F32)


def _halo_up(halo_h, w_up):
    nh, d = halo_h.shape
    ff2 = w_up.shape[1]
    n_blk = 4
    cols = ff2 // n_blk
    return pl.pallas_call(
        _halo_body,
        out_shape=jax.ShapeDtypeStruct((nh, ff2), F32),
        grid=(n_blk,),
        in_specs=[pl.BlockSpec((nh, d), lambda i: (0, 0)),
                  pl.BlockSpec((d, cols), lambda i: (0, i))],
        out_specs=pl.BlockSpec((nh, cols), lambda i: (0, i)),
        compiler_params=_cparams(("arbitrary",)),
        name="mlp_halo_up",
    )(halo_h, w_up)


def _mlp_body(x1_ref, h2_ref, wup_ref, wdn_ref, par_ref, halo_ref, o_ref, gbuf_ref, vbuf_ref, *, chunks):
    tf = h2_ref.shape[1]
    ff = wdn_ref.shape[0]
    nt = pl.num_programs(1)
    row = (pl.program_id(0) * nt + pl.program_id(1)) * 2
    h = h2_ref[0]

    def conv(buf_ref, u, c0, width):
        buf_ref[MLP_PAD:MLP_PAD + tf, :width] = u
        buf_ref[MLP_PAD - 1:MLP_PAD, :width] = halo_ref[pl.ds(row, 1), c0:c0 + width]
        buf_ref[MLP_PAD + tf:MLP_PAD + tf + 1, :width] = halo_ref[pl.ds(row + 1, 1), c0:c0 + width]
        prev = buf_ref[MLP_PAD - 1:MLP_PAD - 1 + tf, :width]
        nxt = buf_ref[MLP_PAD + 1:MLP_PAD + 1 + tf, :width]
        p = par_ref[:, c0:c0 + width]
        return p[0:1] * prev + p[1:2] * u + p[2:3] * nxt + p[3:4]

    def up(c0, width):
        return (jnp.dot(h, wup_ref[:, c0:c0 + width], preferred_element_type=F32),
                jnp.dot(h, wup_ref[:, ff + c0:ff + c0 + width], preferred_element_type=F32))

    y = x1_ref[0]
    nxt_up = up(*chunks[0])
    for n, (c0, width) in enumerate(chunks):
        ug, uv = nxt_up
        if n + 1 < len(chunks):
            nxt_up = up(*chunks[n + 1])
        gate = conv(gbuf_ref, ug, c0, width)
        val = conv(vbuf_ref, uv, ff + c0, width)
        act = gate / (1.0 + jnp.exp(-gate)) * val
        y = y + jnp.dot(act.astype(BF16), wdn_ref[c0:c0 + width, :], preferred_element_type=F32)
    o_ref[0] = y


def _mlp(x1, h2, w_up, w_down, conv_par, halo_up, chunks):
    b, s, d = x1.shape
    tf = TOK_TILE
    ff2 = w_up.shape[1]
    nh = halo_up.shape[0]
    whole = lambda i, j: (0, 0)
    body = functools.partial(_mlp_body, chunks=chunks)
    const = dict(pipeline_mode=pl.Buffered(1))
    wmax = max(w for _, w in chunks)
    return pl.pallas_call(
        body,
        out_shape=jax.ShapeDtypeStruct((b, s, d), F32),
        grid=(b, s // tf),
        in_specs=[pl.BlockSpec((1, tf, d), lambda i, j: (i, j, 0)),
                  pl.BlockSpec((1, tf, d), lambda i, j: (i, j, 0)),
                  pl.BlockSpec((d, ff2), whole, **const),
                  pl.BlockSpec((ff2 // 2, d), whole, **const),
                  pl.BlockSpec((8, ff2), whole, **const),
                  pl.BlockSpec((nh, ff2), whole, **const)],
        out_specs=pl.BlockSpec((1, tf, d), lambda i, j: (i, j, 0)),
        scratch_shapes=[pltpu.VMEM((tf + 2 * MLP_PAD, wmax), F32)] * 2,
        compiler_params=_cparams(("arbitrary", "arbitrary")),
        name="gated_conv_mlp",
    )(x1, h2, w_up, w_down, conv_par, halo_up)


def _halo_rows(edges, nt):
    b, _, d = edges.shape
    ez = jnp.concatenate([edges, jnp.zeros((b, 1, d), edges.dtype)], axis=1)
    idx = []
    for j in range(nt):
        idx += [2 * j - 1 if j > 0 else 2 * nt, 2 * j + 2 if j < nt - 1 else 2 * nt]
    rows = jnp.take(ez, jnp.asarray(idx, jnp.int32), axis=1).reshape(b * nt * 2, d)
    pad = (-rows.shape[0]) % BF16_ROWS
    return jnp.pad(rows, ((0, pad), (0, 0))).astype(BF16)


def _rope_tables(s):
    def freqs(dim):
        return 1.0 / (ROPE_THETA ** (jnp.arange(0, dim, 2, dtype=F32) / dim))
    t = jnp.arange(s)
    rows = (t // GRID_W).astype(F32)
    cols = (t % GRID_W).astype(F32)
    pos = t.astype(F32)
    fa = freqs(HEAD_DIM // 2)
    ar, ac = fa[:, None] * rows[None, :], fa[:, None] * cols[None, :]
    am = freqs(C_ROPE)[:, None] * pos[None, :]
    cosa = jnp.concatenate([jnp.cos(ar), jnp.cos(ar), jnp.cos(ac), jnp.cos(ac)], axis=0)
    sina = jnp.concatenate([-jnp.sin(ar), jnp.sin(ar), -jnp.sin(ac), jnp.sin(ac)], axis=0)
    cosm = jnp.concatenate([jnp.cos(am), jnp.cos(am)], axis=0)
    sinm = jnp.concatenate([-jnp.sin(am), jnp.sin(am)], axis=0)
    return cosa, sina, cosm, sinm


def kernel(x, norm1_g, w_in, a_qn_g, a_kn_g, b_qn_g, b_kn_g, b_lam_q1, b_lam_k1, b_lam_q2, b_lam_k2, b_sub_g,
           c_qa_g, c_kva_g, c_wqb, c_wkvb, c_qn_g, c_kn_g, d_qn_g, d_kn_g, w_out, norm2_g, w_up, conv_w, conv_b,
           w_down):
    b, s, d = x.shape
    depth = w_in.shape[0]
    ff = w_down.shape[1]
    nt = s // TOK_TILE
    assert s % TOK_TILE == 0 and TOK_TILE % KSTEP == 0 and TQ % KSTEP == 0 and s % TQ == 0 and s % GRID_W == 0
    assert s <= 1 << (POS_SPLIT_BITS + BF16_MANTISSA_BITS) and s % TQ_STEP == 0
    for window, dil in DILATED_CONFIGS:
        assert window // (2 * dil) == BAND_RADIUS and (s // dil) % min(BAND_Q, s // dil) == 0
        assert (s // dil) % min(BAND_QSTEP, s // dil) == 0 and BAND_QSTEP % BAND_Q == 0
        assert TOK_TILE % (dil * BF16_ROWS) == 0

    tr = lambda w: jnp.swapaxes(w, 1, 2).astype(BF16)
    gains = jnp.concatenate([norm1_g, a_qn_g, a_kn_g, b_qn_g, b_kn_g, c_qa_g, c_kva_g, c_qn_g, c_kn_g,
                             d_qn_g, d_kn_g], axis=1)[:, :, None]
    conv_par = jnp.concatenate([conv_w, conv_b[:, None, :], jnp.zeros((depth, 4, 2 * ff), F32)], axis=1)
    slopes = [2.0 ** (-8.0 * i / (2 * N_HEADS)) for i in range(1, 2 * N_HEADS + 1)]
    slopes_b, slopes_d = slopes[:N_HEADS], tuple(slopes[N_HEADS:])
    lam_init = jnp.asarray([0.8 - 0.6 * math.exp(-0.3 * l) for l in range(depth)], F32)
    lam = (jnp.exp(jnp.sum(b_lam_q1 * b_lam_k1, axis=1)) - jnp.exp(jnp.sum(b_lam_q2 * b_lam_k2, axis=1)) + lam_init)
    scal = jnp.concatenate([lam[:, None], (1.0 - lam_init)[:, None],
                            jnp.broadcast_to(jnp.asarray(slopes_b, F32), (depth, N_HEADS)),
                            jnp.zeros((depth, 2), F32)], axis=1)
    layers = dict(w_in=tr(w_in), wqb=tr(c_wqb), wkvb=tr(c_wkvb), w_out=tr(w_out),
                  w_up=w_up.astype(BF16), w_down=w_down.astype(BF16),
                  gains=gains, g2=norm2_g[:, None, :], sub_g=b_sub_g[:, :, None], conv_par=conv_par, scal=scal)
    cosa, sina, cosm, sinm = _rope_tables(s)
    assert ff % LANES == 0
    chunks = tuple((c0, min(MLP_CHUNK, ff - c0)) for c0 in range(0, ff, MLP_CHUNK))

    def layer(xs, p):
        (qa, ka, va, qb, kb, vb, qc, kc, vc, dq1, dq4, dq16, dk1, dk4, dk16, dv1, dv4, dv16) = _project(
            xs, p["w_in"], p["wqb"], p["wkvb"], p["gains"], cosa, sina, cosm, sinm)
        oa = _dense_attention(qa, ka, va, HEAD_DIM, N_HEADS // A_KV_HEADS)
        ob = _diff_attention(p["scal"], slopes_b, qb, kb, vb, p["sub_g"])
        oc = _dense_attention(qc, kc, vc, C_QK, 1)
        branches = [_band_attention(q_, k_, v_, dil, slopes_d)
                    for (_, dil), q_, k_, v_ in zip(DILATED_CONFIGS, (dq1, dq4, dq16), (dk1, dk4, dk16), (dv1, dv4, dv16))]
        x1, h2, edges = _out_project(xs, oa, ob, oc, branches, p["w_out"], p["g2"])
        halo = _halo_up(_halo_rows(edges, nt), p["w_up"])
        return _mlp(x1, h2, p["w_up"], p["w_down"], p["conv_par"], halo, chunks), None

    out, _ = lax.scan(layer, x, layers)
    return out
```

```python
import functools
import math

import jax
import jax.numpy as jnp
import numpy as np
from jax import lax
from jax.experimental import pallas as pl
from jax.experimental.pallas import tpu as pltpu

F32 = jnp.float32
BF16 = jnp.bfloat16

HEAD_DIM = 64
N_HEADS = 4
A_KV_HEADS = 2
B_QK_DIM = 32
C_Q_LORA = 256
C_KV_LORA = 128
C_NOPE = 64
C_ROPE = 32
C_QK = C_NOPE + C_ROPE
DILATED_CONFIGS = ((128, 1), (512, 4), (2048, 16))
D_WIDTH = N_HEADS * HEAD_DIM
GRID_W = 64
ROPE_THETA = 10000.0
NORM_EPS = 1e-6
LOG2E = 1.4426950408889634


def _bf16_pieces(x, n):
    out = []
    for _ in range(n):
        u = np.float32(x).view(np.uint32)
        u = np.uint32((int(u) + 0x7FFF + ((int(u) >> 16) & 1)) & 0xFFFF0000)
        piece = float(u.view(np.float32))
        out.append(piece)
        x = x - piece
    return tuple(out)


LOG2E_PIECES = _bf16_pieces(LOG2E, 3)
POS_SPLIT_BITS = 6
BF16_MANTISSA_BITS = 8

_SPLITS = (256, 128, 128, 256, 256, 256, C_Q_LORA, C_KV_LORA, C_ROPE, 256, 256, 256)
_OFF = [0]
for _n in _SPLITS:
    _OFF.append(_OFF[-1] + _n)
(O_AQ, O_AK, O_AV, O_BQ, O_BK, O_BV, O_CQL, O_CKVL, O_CKR, O_DQ, O_DK, O_DV, IN_WIDTH) = _OFF

_GAIN_SIZES = (("n1", 1024), ("aq", 64), ("ak", 64), ("bq", 32), ("bk", 32), ("cqa", 256), ("ckva", 128),
               ("cqn", 96), ("ckn", 96), ("dq", 64), ("dk", 64))
G_OFF = {}
_o = 0
for _k, _n in _GAIN_SIZES:
    G_OFF[_k] = (_o, _n)
    _o += _n
G_TOTAL = _o

LANES = 128
BF16_ROWS = 16
TOK_TILE = 512
MLP_CHUNK = 1024
MLP_PAD = 8
TQ = 512
TQ_STEP = 2048
KSTEP = 256
SCORES_AHEAD = 2
BAND_Q = 128
BAND_QSTEP = 1024
BAND_MASKED = 1e30
BAND_RADIUS = 64
VMEM_LIMIT = 56 * 1024 * 1024


def _cparams(sem, flags=None):
    return pltpu.CompilerParams(dimension_semantics=sem, vmem_limit_bytes=VMEM_LIMIT, flags=flags)


def _rms_rows(v, g_col):
    ms = jnp.mean(v * v, axis=0, keepdims=True)
    return v * lax.rsqrt(ms + NORM_EPS) * g_col


def _swap_halves(x, n):
    parts = []
    for i in range(0, x.shape[0], 2 * n):
        parts.append(x[i + n:i + 2 * n])
        parts.append(x[i:i + n])
    return jnp.concatenate(parts, axis=0)


def _rope_rows(x, cos, sin_signed, n):
    return x * cos + _swap_halves(x, n) * sin_signed


def _store_dilated(scr_ref, val, out_refs):
    tm, c = val.shape
    for g in range(c // LANES):
        scr_ref[g] = val[:, LANES * g:LANES * (g + 1)]
    for (_, dil), out in zip(DILATED_CONFIGS, out_refs):
        if dil == 1:
            out[0] = val.astype(BF16)
            continue
        n = tm // dil
        for r in range(dil):
            for g in range(c // LANES):
                lo = c * r + LANES * g
                out[0, :, lo:lo + LANES] = scr_ref[g, pl.ds(r, n, stride=dil), :].astype(BF16)


def _store_key_blocks(v_ref, v):
    for c in range(v.shape[1] // KSTEP):
        v_ref[0, c] = v[:, c * KSTEP:(c + 1) * KSTEP]


def _proj_body(x_ref, w_ref, wqb_ref, wkvb_ref, g_ref, cosa_ref, sina_ref, cosm_ref, sinm_ref,
               qa_ref, ka_ref, va_ref, qb_ref, kb_ref, vb_ref, qc_ref, kc_ref, vc_ref,
               dq1_ref, dq4_ref, dq16_ref, dk1_ref, dk4_ref, dk16_ref, dv1_ref, dv4_ref, dv16_ref,
               scr_ref):
    tm = x_ref.shape[1]

    def gain(name):
        o, n = G_OFF[name]
        return g_ref[o:o + n, :]

    x = x_ref[0].T
    h = (x * gain("n1")).astype(BF16)
    r = lax.rsqrt(jnp.mean(x * x, axis=0, keepdims=True) + NORM_EPS)

    def section(lo, hi):
        part = jnp.dot(w_ref[lo:hi, :], h, preferred_element_type=F32) * r
        return lambda o, n: part[o - lo:o - lo + n]

    cosa, sina = cosa_ref[...], sina_ref[...]
    cosm, sinm = cosm_ref[...], sinm_ref[...]
    zeros64 = jnp.zeros((64, tm), F32)
    zeros32 = jnp.zeros((32, tm), F32)

    def prep_a(proj):
        sa = HEAD_DIM ** -0.5 * LOG2E
        for hh in range(N_HEADS):
            q = _rope_rows(_rms_rows(proj(O_AQ + 64 * hh, 64), gain("aq")), cosa, sina, 16) * sa
            qa_ref[0, 64 * hh:64 * hh + 64, :] = q.astype(BF16)
        for g in range(A_KV_HEADS):
            k = _rope_rows(_rms_rows(proj(O_AK + 64 * g, 64), gain("ak")), cosa, sina, 16)
            kpad = jnp.concatenate([k, zeros64], axis=0)
            ka_ref[0, :, LANES * g:LANES * (g + 1)] = kpad.T.astype(BF16)
        _store_key_blocks(va_ref, proj(O_AV, 128).astype(BF16))

    def prep_b(proj):
        sb = B_QK_DIM ** -0.5 * LOG2E
        for i in range(2 * N_HEADS):
            qb_ref[0, 32 * i:32 * i + 32, :] = (_rms_rows(proj(O_BQ + 32 * i, 32), gain("bq")) * sb).astype(BF16)
        kpos = pl.program_id(1) * tm + lax.broadcasted_iota(jnp.int32, (BF16_ROWS, tm), 1)
        frow = lax.broadcasted_iota(jnp.int32, (BF16_ROWS, tm), 0)
        pos_hi = ((kpos >> POS_SPLIT_BITS) << POS_SPLIT_BITS).astype(F32)
        pos_lo = (kpos & ((1 << POS_SPLIT_BITS) - 1)).astype(F32)
        kfeat = jnp.where(frow < 3, pos_hi, jnp.where(frow < 6, pos_lo, 0.0))
        zeros48 = jnp.zeros((48, tm), F32)
        for hh in range(N_HEADS):
            k1 = _rms_rows(proj(O_BK + 64 * hh, 32), gain("bk"))
            k2 = _rms_rows(proj(O_BK + 64 * hh + 32, 32), gain("bk"))
            kpad = jnp.concatenate([k1, k2, kfeat, zeros48], axis=0)
            kb_ref[0, :, LANES * hh:LANES * (hh + 1)] = kpad.T.astype(BF16)
        _store_key_blocks(vb_ref, proj(O_BV, 256).astype(BF16))

    def prep_c(proj):
        sc = C_QK ** -0.5 * LOG2E
        cq = _rms_rows(proj(O_CQL, C_Q_LORA), gain("cqa")).astype(BF16)
        qc = jnp.dot(wqb_ref[...], cq, preferred_element_type=F32)
        ckv = _rms_rows(proj(O_CKVL, C_KV_LORA), gain("ckva")).astype(BF16)
        kvc = jnp.dot(wkvb_ref[...], ckv, preferred_element_type=F32)
        kr = proj(O_CKR, C_ROPE)
        vcs = []
        for hh in range(N_HEADS):
            q = _rms_rows(qc[C_QK * hh:C_QK * (hh + 1)], gain("cqn"))
            q = jnp.concatenate([q[:C_NOPE], _rope_rows(q[C_NOPE:], cosm, sinm, 16)], axis=0) * sc
            qc_ref[0, C_QK * hh:C_QK * (hh + 1), :] = q.astype(BF16)
            kk = jnp.concatenate([kvc[128 * hh:128 * hh + C_NOPE], kr], axis=0)
            kk = _rms_rows(kk, gain("ckn"))
            kk = jnp.concatenate([kk[:C_NOPE], _rope_rows(kk[C_NOPE:], cosm, sinm, 16), zeros32], axis=0)
            kc_ref[0, :, LANES * hh:LANES * (hh + 1)] = kk.T.astype(BF16)
            vcs.append(kvc[128 * hh + C_NOPE:128 * (hh + 1)])
        _store_key_blocks(vc_ref, jnp.concatenate(vcs, axis=0).astype(BF16))

    def prep_d(proj):
        sd = HEAD_DIM ** -0.5 * LOG2E
        qs, ks, vs = [], [], []
        for g in range(2):
            qg, kg = [], []
            for hh in (2 * g, 2 * g + 1):
                qg.append(_rms_rows(proj(O_DQ + 64 * hh, 64), gain("dq")) * sd)
                kg.append(_rms_rows(proj(O_DK + 64 * hh, 64), gain("dk")))
            qs.append(jnp.concatenate(qg, axis=0).T)
            ks.append(jnp.concatenate(kg, axis=0).T)
            vs.append(proj(O_DV + 128 * g, 128).T)
        _store_dilated(scr_ref.at[0], jnp.concatenate(qs, axis=1), (dq1_ref, dq4_ref, dq16_ref))
        _store_dilated(scr_ref.at[1], jnp.concatenate(ks, axis=1), (dk1_ref, dk4_ref, dk16_ref))
        _store_dilated(scr_ref.at[2], jnp.concatenate(vs, axis=1), (dv1_ref, dv4_ref, dv16_ref))

    sec_c = section(O_CQL, O_DQ)
    sec_d = section(O_DQ, IN_WIDTH)
    prep_c(sec_c)
    sec_b = section(O_BQ, O_CQL)
    prep_d(sec_d)
    sec_a = section(O_AQ, O_BQ)
    prep_b(sec_b)
    prep_a(sec_a)


def _project(x, w_in_t, wqb_t, wkvb_t, gains, cosa, sina, cosm, sinm):
    b, s, d = x.shape
    tm = TOK_TILE
    nt = s // tm
    fm = lambda rows: jax.ShapeDtypeStruct((b, rows, s), BF16)
    tmj = lambda cols: jax.ShapeDtypeStruct((b, s, cols), BF16)
    vblk = lambda rows: jax.ShapeDtypeStruct((b, s // KSTEP, rows, KSTEP), BF16)
    dviews = tuple(jax.ShapeDtypeStruct((b, s // dil, dil * D_WIDTH), BF16) for _, dil in DILATED_CONFIGS)
    out_shape = (fm(256), tmj(256), vblk(128),
                 fm(256), tmj(512), vblk(256),
                 fm(384), tmj(512), vblk(256)) + dviews * 3
    fm_spec = lambda rows: pl.BlockSpec((1, rows, tm), lambda i, j: (i, 0, j))
    tm_spec = lambda cols: pl.BlockSpec((1, tm, cols), lambda i, j: (i, j, 0))
    v_spec = lambda rows: pl.BlockSpec((1, tm // KSTEP, rows, KSTEP), lambda i, j: (i, j, 0, 0))
    dspecs = tuple(pl.BlockSpec((1, tm // dil, dil * D_WIDTH), lambda i, j: (i, j, 0)) for _, dil in DILATED_CONFIGS)
    out_specs = (fm_spec(256), tm_spec(256), v_spec(128),
                 fm_spec(256), tm_spec(512), v_spec(256),
                 fm_spec(384), tm_spec(512), v_spec(256)) + dspecs * 3
    whole = lambda i, j: (0, 0)
    in_specs = [
        pl.BlockSpec((1, tm, d), lambda i, j: (i, j, 0)),
        pl.BlockSpec((IN_WIDTH, d), whole),
        pl.BlockSpec((N_HEADS * C_QK, C_Q_LORA), whole),
        pl.BlockSpec((N_HEADS * 128, C_KV_LORA), whole),
        pl.BlockSpec((G_TOTAL, 1), whole),
        pl.BlockSpec((64, tm), lambda i, j: (0, j)),
        pl.BlockSpec((64, tm), lambda i, j: (0, j)),
        pl.BlockSpec((32, tm), lambda i, j: (0, j)),
        pl.BlockSpec((32, tm), lambda i, j: (0, j)),
    ]
    return pl.pallas_call(
        _proj_body,
        out_shape=out_shape,
        grid=(b, nt),
        in_specs=in_specs,
        out_specs=out_specs,
        scratch_shapes=[pltpu.VMEM((3, D_WIDTH // LANES, tm, LANES), F32)],
        compiler_params=_cparams(("parallel", "arbitrary")),
        name="project_heads",
    )(x, w_in_t, wqb_t, wkvb_t, gains, cosa, sina, cosm, sinm)


def _online_update(s, v_aug, carry, shift=None):
    m, acc = carry
    smax = jnp.max(s, axis=0, keepdims=True)
    if shift is None:
        m_new = jnp.maximum(m, smax)
        p = jnp.exp2(s - m_new).astype(BF16)
    else:
        m_new = jnp.maximum(m, smax + shift)
        p = jnp.exp2(s - (m_new - shift)).astype(BF16)
    alpha = jnp.exp2(m - m_new)
    acc = alpha * acc + jnp.dot(v_aug, p, preferred_element_type=F32)
    return m_new, acc


def _init_carry(dv, tq):
    return (jnp.full((1, tq), -jnp.inf, F32), jnp.zeros((dv + BF16_ROWS, tq), F32))


def _finish(carry, dv):
    _, acc = carry
    return acc[:dv] / acc[dv:dv + 1]


def _run_pipelined(n_jobs, n_blocks, scores_fn, update_fn, init_fn, finish_fn):
    steps = [(t, i) for t in range(n_jobs) for i in range(n_blocks)]
    pending = [scores_fn(*steps[n]) for n in range(min(SCORES_AHEAD, len(steps)))]
    carry = None
    for n, (t, i) in enumerate(steps):
        s = pending.pop(0)
        if n + SCORES_AHEAD < len(steps):
            pending.append(scores_fn(*steps[n + SCORES_AHEAD]))
        if i == 0:
            carry = init_fn()
        carry = update_fn(t, i, s, carry)
        if i == n_blocks - 1:
            finish_fn(t, carry)


def _dense_body(q_ref, k_ref, v_ref, o_ref):
    dq = q_ref.shape[1]
    nkb, dv, ks = v_ref.shape[1], v_ref.shape[2], v_ref.shape[3]
    tq = TQ
    ones = jnp.ones((BF16_ROWS, ks), BF16)
    zpad = jnp.zeros((LANES - dq, tq), BF16)
    qpads = [jnp.concatenate([q_ref[0, :, t * tq:(t + 1) * tq], zpad], axis=0) for t in range(q_ref.shape[2] // tq)]

    def scores(t, i):
        return jnp.dot(k_ref[0, i * ks:(i + 1) * ks, :], qpads[t], preferred_element_type=F32)

    def update(t, i, s, carry):
        return _online_update(s, jnp.concatenate([v_ref[0, i], ones], axis=0), carry)

    def finish(t, carry):
        o_ref[0, :, t * tq:(t + 1) * tq] = _finish(carry, dv).astype(o_ref.dtype)

    _run_pipelined(len(qpads), nkb, scores, update, lambda: _init_carry(dv, tq), finish)


def _dense_attention(q, k, v, dq, q_per_kv):
    b, rows, s = q.shape
    nh = rows // dq
    nkb, tk = v.shape[1], v.shape[3]
    return pl.pallas_call(
        _dense_body,
        out_shape=jax.ShapeDtypeStruct((b, nh * HEAD_DIM, s), BF16),
        grid=(b, nh, s // TQ_STEP),
        in_specs=[
            pl.BlockSpec((1, dq, TQ_STEP), lambda i, h, j: (i, h, j)),
            pl.BlockSpec((1, s, LANES), lambda i, h, j: (i, 0, h // q_per_kv)),
            pl.BlockSpec((1, nkb, HEAD_DIM, tk), lambda i, h, j: (i, 0, h // q_per_kv, 0)),
        ],
        out_specs=pl.BlockSpec((1, HEAD_DIM, TQ_STEP), lambda i, h, j: (i, h, j)),
        compiler_params=_cparams(("parallel", "arbitrary", "arbitrary")),
        name="dense_attention",
    )(q, k, v)


def _diff_body(sc_ref, q_ref, k_ref, v_ref, g_ref, bias_ref, o_ref):
    tq = TQ
    n_tiles = q_ref.shape[2] // tq
    nkb, dv, ks = v_ref.shape[1], v_ref.shape[2], v_ref.shape[3]
    hh = pl.program_id(1)
    lam = sc_ref[0]
    out_scale = sc_ref[1]
    slope = sc_ref[2 + hh]
    z32 = jnp.zeros((B_QK_DIM, tq), BF16)
    z48 = jnp.zeros((48, tq), BF16)
    zfeat = jnp.zeros((BF16_ROWS, tq), BF16)
    ones = jnp.ones((BF16_ROWS, ks), BF16)
    frow = lax.broadcasted_iota(jnp.int32, (BF16_ROWS, tq), 0)
    piece = jnp.where(frow % 3 == 0, LOG2E_PIECES[0], jnp.where(frow % 3 == 1, LOG2E_PIECES[1], LOG2E_PIECES[2]))
    cfeat = jnp.where(frow < 6, piece, 0.0)
    lane_pos = lax.broadcasted_iota(jnp.int32, (1, tq), 1)

    n_mixed = tq // ks

    def tile_index(job):
        return pl.program_id(2) * n_tiles + job // 2

    def block_index(job, ii):
        return lax.rem(tile_index(job) * n_mixed + ii, nkb)

    def side(job, ii):
        return jnp.where(block_index(job, ii) < tile_index(job) * n_mixed, 1.0, -1.0).astype(F32)

    def qrows(job):
        q = q_ref[0, :, (job // 2) * tq:(job // 2 + 1) * tq]
        if job % 2 == 0:
            return jnp.concatenate([q[:B_QK_DIM], z32], axis=0)
        return jnp.concatenate([z32, q[B_QK_DIM:]], axis=0)

    def scores(job, ii):
        i = block_index(job, ii)
        k = k_ref[0, pl.ds(pl.multiple_of(i * ks, ks), ks), :]
        feat = zfeat if ii < n_mixed else (cfeat * (side(job, ii) * slope)).astype(BF16)
        return jnp.dot(k, jnp.concatenate([qrows(job), feat, z48], axis=0), preferred_element_type=F32)

    def update(job, ii, s, carry):
        v_aug = jnp.concatenate([v_ref[0, block_index(job, ii)], ones], axis=0)
        if ii < n_mixed:
            return _online_update(s - bias_ref[0, ii], v_aug, carry)
        qpos = (tile_index(job) * tq + lane_pos).astype(F32)
        return _online_update(s, v_aug, carry, qpos * (-(side(job, ii) * slope * LOG2E)))

    res = []

    def finish(job, carry):
        res.append(_finish(carry, dv))
        if job % 2 == 1:
            o = res[job - 1] - lam * res[job]
            t = job // 2
            o_ref[0, :, t * tq:(t + 1) * tq] = (_rms_rows(o, g_ref[...]) * out_scale).astype(o_ref.dtype)

    _run_pipelined(2 * n_tiles, nkb, scores, update, lambda: _init_carry(dv, tq), finish)


def _diff_attention(scal, slopes, q, k, v, sub_g):
    b, rows, s = q.shape
    nkb, tk = v.shape[1], v.shape[3]
    n_mixed = TQ // tk
    jj = jnp.arange(tk)[None, :, None] + tk * jnp.arange(n_mixed)[:, None, None]
    dist = jnp.abs(jj - jnp.arange(TQ)[None, None, :]).astype(F32)
    bias = dist[None] * (LOG2E * jnp.asarray(slopes, F32))[:, None, None, None]
    return pl.pallas_call(
        _diff_body,
        out_shape=jax.ShapeDtypeStruct((b, rows, s), BF16),
        grid=(b, N_HEADS, s // TQ_STEP),
        in_specs=[
            pl.BlockSpec(memory_space=pltpu.SMEM),
            pl.BlockSpec((1, 2 * B_QK_DIM, TQ_STEP), lambda i, h, j: (i, h, j)),
            pl.BlockSpec((1, s, LANES), lambda i, h, j: (i, 0, h)),
            pl.BlockSpec((1, nkb, HEAD_DIM, tk), lambda i, h, j: (i, 0, h, 0)),
            pl.BlockSpec((HEAD_DIM, 1), lambda i, h, j: (0, 0)),
            pl.BlockSpec((1, n_mixed, tk, TQ), lambda i, h, j: (h, 0, 0, 0)),
        ],
        out_specs=pl.BlockSpec((1, HEAD_DIM, TQ_STEP), lambda i, h, j: (i, h, j)),
        compiler_params=_cparams(("parallel", "arbitrary", "arbitrary")),
        name="diff_attention",
    )(scal, q, k, v, sub_g, bias)


def _band_body(q_ref, k_ref, v_ref, pen_ref, o_ref, lse_ref, *, qb):
    l_len = k_ref.shape[1]
    win = pen_ref.shape[3]
    lane = lax.broadcasted_iota(jnp.int32, (qb, LANES), 1)
    n_sub = q_ref.shape[1] // qb
    n_groups = q_ref.shape[2] // LANES
    for sb in range(n_sub):
        jb = pl.program_id(2) * n_sub + sb
        start = jnp.clip(jb * qb - BAND_RADIUS, 0, l_len - win)
        start = pl.multiple_of(start, BAND_RADIUS)
        variant = (jb * qb - start) // BAND_RADIUS
        q = q_ref[0, sb * qb:(sb + 1) * qb, :]
        kwin = k_ref[0, pl.ds(start, win), :]
        vwin = v_ref[0, pl.ds(start, win), :]
        for gc in range(n_groups):
            g = gc % 2
            q2 = q[:, LANES * gc:LANES * (gc + 1)].astype(F32)
            k2 = kwin[:, LANES * gc:LANES * (gc + 1)]
            v2 = vwin[:, LANES * gc:LANES * (gc + 1)]
            out_g = jnp.zeros((qb, LANES), F32)
            lse_g = jnp.zeros((qb, LANES), F32)
            for hh in range(2):
                own = (lane >= HEAD_DIM * hh) & (lane < HEAD_DIM * (hh + 1))
                qm = jnp.where(own, q2, 0.0).astype(BF16)
                s = lax.dot_general(qm, k2, (((1,), (1,)), ((), ())), preferred_element_type=F32)
                s = s - pen_ref[variant, 2 * g + hh]
                m = jnp.max(s, axis=1, keepdims=True)
                p = jnp.exp2(s - m)
                lsum = jnp.sum(p, axis=1, keepdims=True)
                o = jnp.dot(p.astype(BF16), v2, preferred_element_type=F32) / lsum
                lse = m + jnp.log2(lsum)
                out_g = jnp.where(own, o, out_g)
                lse_g = jnp.where(own, lse, lse_g)
            o_ref[0, sb * qb:(sb + 1) * qb, LANES * gc:LANES * (gc + 1)] = out_g.astype(o_ref.dtype)
            lse_ref[0, sb * qb:(sb + 1) * qb, LANES * gc:LANES * (gc + 1)] = lse_g


def _band_penalty(qb, win, dil, slopes):
    rows = jnp.arange(qb)[None, :, None] + BAND_RADIUS * jnp.arange(3)[:, None, None]
    rel = jnp.abs(rows - jnp.arange(win)[None, None, :])
    scale = jnp.asarray(slopes, F32) * (float(dil) * LOG2E)
    pen = rel.astype(F32)[:, None] * scale[None, :, None, None]
    return jnp.where((rel <= BAND_RADIUS)[:, None], pen, BAND_MASKED)


def _band_attention(q, k, v, dil, slopes):
    b, l_len, cw = q.shape
    c = cw // dil
    qb = min(BAND_Q, l_len)
    qstep = min(BAND_QSTEP, l_len)
    cps = min(dil, max(1, BAND_QSTEP // l_len))
    assert dil % cps == 0
    win = min(qb + 2 * BAND_RADIUS, l_len)
    pen = _band_penalty(qb, win, dil, slopes)
    body = functools.partial(_band_body, qb=qb)
    return pl.pallas_call(
        body,
        out_shape=(jax.ShapeDtypeStruct((b, l_len, cw), BF16), jax.ShapeDtypeStruct((b, l_len, cw), F32)),
        grid=(b, dil // cps, l_len // qstep),
        in_specs=[
            pl.BlockSpec((1, qstep, c * cps), lambda i, r, j: (i, j, r)),
            pl.BlockSpec((1, l_len, c * cps), lambda i, r, j: (i, 0, r)),
            pl.BlockSpec((1, l_len, c * cps), lambda i, r, j: (i, 0, r)),
            pl.BlockSpec(pen.shape, lambda i, r, j: (0, 0, 0, 0)),
        ],
        out_specs=(pl.BlockSpec((1, qstep, c * cps), lambda i, r, j: (i, j, r)),) * 2,
        compiler_params=_cparams(("parallel", "arbitrary", "arbitrary")),
        name="band_attention_d%d" % dil,
    )(q, k, v, pen)


def _band_combine(in_refs, scr_ref, tm, c):
    vals = []
    for idx, (_, dil) in enumerate(DILATED_CONFIGS):
        for which in range(2):
            src = in_refs[2 * idx + which]
            if dil == 1:
                vals.append(src[0].astype(F32))
                continue
            groups = []
            for g in range(c // LANES):
                slot = scr_ref.at[(2 * idx + which) * (c // LANES) + g]
                for r in range(dil):
                    lo = c * r + LANES * g
                    slot[pl.ds(r, tm // dil, stride=dil), :] = src[0, :, lo:lo + LANES].astype(F32)
                groups.append(slot[...])
            vals.append(jnp.concatenate(groups, axis=1))
    outs, lses = vals[0::2], vals[1::2]
    m = functools.reduce(jnp.maximum, lses)
    ws = [jnp.exp2(x - m) for x in lses]
    num = functools.reduce(lambda a, t: a + t, [w * o for w, o in zip(ws, outs)])
    den = functools.reduce(lambda a, t: a + t, ws)
    return num / den


def _outproj_body(*refs):
    n_d = 2 * len(DILATED_CONFIGS)
    x_ref, oa_ref, ob_ref, oc_ref = refs[:4]
    d_refs = refs[4:4 + n_d]
    w_ref, g_ref, x1_ref, h2_ref, edge_ref, scr_ref = refs[4 + n_d:]
    tm = x_ref.shape[1]
    j = pl.program_id(1)
    mix = jnp.concatenate([oa_ref[0], ob_ref[0], oc_ref[0]], axis=0)
    od = _band_combine(d_refs, scr_ref, tm, D_WIDTH).astype(BF16)
    w = w_ref[...]
    y = jnp.dot(w[:, :768], mix, preferred_element_type=F32)
    y = y + lax.dot_general(w[:, 768:], od, (((1,), (1,)), ((), ())), preferred_element_type=F32)
    x1 = x_ref[0] + y.T
    x1_ref[0] = x1
    ms = jnp.mean(x1 * x1, axis=1, keepdims=True)
    h2 = (x1 * lax.rsqrt(ms + NORM_EPS) * g_ref[...]).astype(BF16)
    h2_ref[0] = h2
    edge_ref[0, pl.ds(2 * j, 1), :] = h2[0:1].astype(F32)
    edge_ref[0, pl.ds(2 * j + 1, 1), :] = h2[tm - 1:tm].astype(F32)


def _out_project(x, oa, ob, oc, branches, w_out_t, g2):
    b, s, d = x.shape
    tm = TOK_TILE
    nt = s // tm
    fm = lambda rows: pl.BlockSpec((1, rows, tm), lambda i, j: (i, 0, j))
    tmj = pl.BlockSpec((1, tm, d), lambda i, j: (i, j, 0))
    d_specs = []
    for (_, dil) in DILATED_CONFIGS:
        d_specs += [pl.BlockSpec((1, tm // dil, dil * D_WIDTH), lambda i, j: (i, j, 0))] * 2
    flat = [a for pair in branches for a in pair]
    return pl.pallas_call(
        _outproj_body,
        out_shape=(jax.ShapeDtypeStruct((b, s, d), F32), jax.ShapeDtypeStruct((b, s, d), BF16),
                   jax.ShapeDtypeStruct((b, 2 * nt, d), F32)),
        grid=(b, nt),
        in_specs=[tmj, fm(256), fm(256), fm(256)] + d_specs + [
            pl.BlockSpec((d, d), lambda i, j: (0, 0)),
            pl.BlockSpec((1, d), lambda i, j: (0, 0))],
        out_specs=(tmj, tmj, pl.BlockSpec((1, 2 * nt, d), lambda i, j: (i, 0, 0))),
        scratch_shapes=[pltpu.VMEM((2 * len(DILATED_CONFIGS) * (D_WIDTH // LANES), tm, LANES), F32)],
        compiler_params=_cparams(("parallel", "arbitrary")),
        name="out_project",
    )(x, oa, ob, oc, *flat, w_out_t, g2)


def _halo_body(h_ref, w_ref, o_ref):
    o_ref[...] = jnp.dot(h_ref[...], w_ref[...], preferred_element_type=F32)


def _halo_up(halo_h, w_up):
    nh, d = halo_h.shape
    ff2 = w_up.shape[1]
    n_blk = 4
    cols = ff2 // n_blk
    return pl.pallas_call(
        _halo_body,
        out_shape=jax.ShapeDtypeStruct((nh, ff2), F32),
        grid=(n_blk,),
        in_specs=[pl.BlockSpec((nh, d), lambda i: (0, 0)),
                  pl.BlockSpec((d, cols), lambda i: (0, i))],
        out_specs=pl.BlockSpec((nh, cols), lambda i: (0, i)),
        compiler_params=_cparams(("arbitrary",)),
        name="mlp_halo_up",
    )(halo_h, w_up)


def _mlp_body(x1_ref, h2_ref, wup_ref, wdn_ref, par_ref, halo_ref, o_ref, gbuf_ref, vbuf_ref, *, chunks):
    tf = h2_ref.shape[1]
    ff = wdn_ref.shape[0]
    nt = pl.num_programs(1)
    row = (pl.program_id(0) * nt + pl.program_id(1)) * 2
    h = h2_ref[0]

    def conv(buf_ref, u, c0, width):
        buf_ref[MLP_PAD:MLP_PAD + tf, :width] = u
        buf_ref[MLP_PAD - 1:MLP_PAD, :width] = halo_ref[pl.ds(row, 1), c0:c0 + width]
        buf_ref[MLP_PAD + tf:MLP_PAD + tf + 1, :width] = halo_ref[pl.ds(row + 1, 1), c0:c0 + width]
        prev = buf_ref[MLP_PAD - 1:MLP_PAD - 1 + tf, :width]
        nxt = buf_ref[MLP_PAD + 1:MLP_PAD + 1 + tf, :width]
        p = par_ref[:, c0:c0 + width]
        return p[0:1] * prev + p[1:2] * u + p[2:3] * nxt + p[3:4]

    def up(c0, width):
        return (jnp.dot(h, wup_ref[:, c0:c0 + width], preferred_element_type=F32),
                jnp.dot(h, wup_ref[:, ff + c0:ff + c0 + width], preferred_element_type=F32))

    y = x1_ref[0]
    nxt_up = up(*chunks[0])
    for n, (c0, width) in enumerate(chunks):
        ug, uv = nxt_up
        if n + 1 < len(chunks):
            nxt_up = up(*chunks[n + 1])
        gate = conv(gbuf_ref, ug, c0, width)
        val = conv(vbuf_ref, uv, ff + c0, width)
        act = gate / (1.0 + jnp.exp(-gate)) * val
        y = y + jnp.dot(act.astype(BF16), wdn_ref[c0:c0 + width, :], preferred_element_type=F32)
    o_ref[0] = y


def _mlp(x1, h2, w_up, w_down, conv_par, halo_up, chunks):
    b, s, d = x1.shape
    tf = TOK_TILE
    ff2 = w_up.shape[1]
    nh = halo_up.shape[0]
    whole = lambda i, j: (0, 0)
    body = functools.partial(_mlp_body, chunks=chunks)
    const = dict(pipeline_mode=pl.Buffered(1))
    wmax = max(w for _, w in chunks)
    return pl.pallas_call(
        body,
        out_shape=jax.ShapeDtypeStruct((b, s, d), F32),
        grid=(b, s // tf),
        in_specs=[pl.BlockSpec((1, tf, d), lambda i, j: (i, j, 0)),
                  pl.BlockSpec((1, tf, d), lambda i, j: (i, j, 0)),
                  pl.BlockSpec((d, ff2), whole, **const),
                  pl.BlockSpec((ff2 // 2, d), whole, **const),
                  pl.BlockSpec((8, ff2), whole, **const),
                  pl.BlockSpec((nh, ff2), whole, **const)],
        out_specs=pl.BlockSpec((1, tf, d), lambda i, j: (i, j, 0)),
        scratch_shapes=[pltpu.VMEM((tf + 2 * MLP_PAD, wmax), F32)] * 2,
        compiler_params=_cparams(("arbitrary", "arbitrary")),
        name="gated_conv_mlp",
    )(x1, h2, w_up, w_down, conv_par, halo_up)


def _halo_rows(edges, nt):
    b, _, d = edges.shape
    ez = jnp.concatenate([edges, jnp.zeros((b, 1, d), edges.dtype)], axis=1)
    idx = []
    for j in range(nt):
        idx += [2 * j - 1 if j > 0 else 2 * nt, 2 * j + 2 if j < nt - 1 else 2 * nt]
    rows = jnp.take(ez, jnp.asarray(idx, jnp.int32), axis=1).reshape(b * nt * 2, d)
    pad = (-rows.shape[0]) % BF16_ROWS
    return jnp.pad(rows, ((0, pad), (0, 0))).astype(BF16)


def _rope_tables(s):
    def freqs(dim):
        return 1.0 / (ROPE_THETA ** (jnp.arange(0, dim, 2, dtype=F32) / dim))
    t = jnp.arange(s)
    rows = (t // GRID_W).astype(F32)
    cols = (t % GRID_W).astype(F32)
    pos = t.astype(F32)
    fa = freqs(HEAD_DIM // 2)
    ar, ac = fa[:, None] * rows[None, :], fa[:, None] * cols[None, :]
    am = freqs(C_ROPE)[:, None] * pos[None, :]
    cosa = jnp.concatenate([jnp.cos(ar), jnp.cos(ar), jnp.cos(ac), jnp.cos(ac)], axis=0)
    sina = jnp.concatenate([-jnp.sin(ar), jnp.sin(ar), -jnp.sin(ac), jnp.sin(ac)], axis=0)
    cosm = jnp.concatenate([jnp.cos(am), jnp.cos(am)], axis=0)
    sinm = jnp.concatenate([-jnp.sin(am), jnp.sin(am)], axis=0)
    return cosa, sina, cosm, sinm


def kernel(x, norm1_g, w_in, a_qn_g, a_kn_g, b_qn_g, b_kn_g, b_lam_q1, b_lam_k1, b_lam_q2, b_lam_k2, b_sub_g,
           c_qa_g, c_kva_g, c_wqb, c_wkvb, c_qn_g, c_kn_g, d_qn_g, d_kn_g, w_out, norm2_g, w_up, conv_w, conv_b,
           w_down):
    b, s, d = x.shape
    depth = w_in.shape[0]
    ff = w_down.shape[1]
    nt = s // TOK_TILE
    assert s % TOK_TILE == 0 and TOK_TILE % KSTEP == 0 and TQ % KSTEP == 0 and s % TQ == 0 and s % GRID_W == 0
    assert s <= 1 << (POS_SPLIT_BITS + BF16_MANTISSA_BITS) and s % TQ_STEP == 0
    for window, dil in DILATED_CONFIGS:
        assert window // (2 * dil) == BAND_RADIUS and (s // dil) % min(BAND_Q, s // dil) == 0
        assert (s // dil) % min(BAND_QSTEP, s // dil) == 0 and BAND_QSTEP % BAND_Q == 0
        assert TOK_TILE % (dil * BF16_ROWS) == 0

    tr = lambda w: jnp.swapaxes(w, 1, 2).astype(BF16)
    gains = jnp.concatenate([norm1_g, a_qn_g, a_kn_g, b_qn_g, b_kn_g, c_qa_g, c_kva_g, c_qn_g, c_kn_g,
                             d_qn_g, d_kn_g], axis=1)[:, :, None]
    conv_par = jnp.concatenate([conv_w, conv_b[:, None, :], jnp.zeros((depth, 4, 2 * ff), F32)], axis=1)
    slopes = [2.0 ** (-8.0 * i / (2 * N_HEADS)) for i in range(1, 2 * N_HEADS + 1)]
    slopes_b, slopes_d = slopes[:N_HEADS], tuple(slopes[N_HEADS:])
    lam_init = jnp.asarray([0.8 - 0.6 * math.exp(-0.3 * l) for l in range(depth)], F32)
    lam = (jnp.exp(jnp.sum(b_lam_q1 * b_lam_k1, axis=1)) - jnp.exp(jnp.sum(b_lam_q2 * b_lam_k2, axis=1)) + lam_init)
    scal = jnp.concatenate([lam[:, None], (1.0 - lam_init)[:, None],
                            jnp.broadcast_to(jnp.asarray(slopes_b, F32), (depth, N_HEADS)),
                            jnp.zeros((depth, 2), F32)], axis=1)
    layers = dict(w_in=tr(w_in), wqb=tr(c_wqb), wkvb=tr(c_wkvb), w_out=tr(w_out),
                  w_up=w_up.astype(BF16), w_down=w_down.astype(BF16),
                  gains=gains, g2=norm2_g[:, None, :], sub_g=b_sub_g[:, :, None], conv_par=conv_par, scal=scal)
    cosa, sina, cosm, sinm = _rope_tables(s)
    assert ff % LANES == 0
    chunks = tuple((c0, min(MLP_CHUNK, ff - c0)) for c0 in range(0, ff, MLP_CHUNK))

    def layer(xs, p):
        (qa, ka, va, qb, kb, vb, qc, kc, vc, dq1, dq4, dq16, dk1, dk4, dk16, dv1, dv4, dv16) = _project(
            xs, p["w_in"], p["wqb"], p["wkvb"], p["gains"], cosa, sina, cosm, sinm)
        oa = _dense_attention(qa, ka, va, HEAD_DIM, N_HEADS // A_KV_HEADS)
        ob = _diff_attention(p["scal"], slopes_b, qb, kb, vb, p["sub_g"])
        oc = _dense_attention(qc, kc, vc, C_QK, 1)
        branches = [_band_attention(q_, k_, v_, dil, slopes_d)
                    for (_, dil), q_, k_, v_ in zip(DILATED_CONFIGS, (dq1, dq4, dq16), (dk1, dk4, dk16), (dv1, dv4, dv16))]
        x1, h2, edges = _out_project(xs, oa, ob, oc, branches, p["w_out"], p["g2"])
        halo = _halo_up(_halo_rows(edges, nt), p["w_up"])
        return _mlp(x1, h2, p["w_up"], p["w_down"], p["conv_par"], halo, chunks), None

    out, _ = layer(x, jax.tree.map(lambda a: a[0], layers))
    if depth > 1:
        out, _ = lax.scan(layer, out, jax.tree.map(lambda a: a[1:], layers))
    return out
```

```python
import functools
import math

import jax
import jax.numpy as jnp
import numpy as np
from jax import lax
from jax.experimental import pallas as pl
from jax.experimental.pallas import tpu as pltpu

F32 = jnp.float32
BF16 = jnp.bfloat16

HEAD_DIM = 64
N_HEADS = 4
A_KV_HEADS = 2
B_QK_DIM = 32
C_Q_LORA = 256
C_KV_LORA = 128
C_NOPE = 64
C_ROPE = 32
C_QK = C_NOPE + C_ROPE
DILATED_CONFIGS = ((128, 1), (512, 4), (2048, 16))
D_WIDTH = N_HEADS * HEAD_DIM
GRID_W = 64
ROPE_THETA = 10000.0
NORM_EPS = 1e-6
LOG2E = 1.4426950408889634


def _bf16_pieces(x, n):
    out = []
    for _ in range(n):
        u = np.float32(x).view(np.uint32)
        u = np.uint32((int(u) + 0x7FFF + ((int(u) >> 16) & 1)) & 0xFFFF0000)
        piece = float(u.view(np.float32))
        out.append(piece)
        x = x - piece
    return tuple(out)


LOG2E_PIECES = _bf16_pieces(LOG2E, 3)
POS_SPLIT_BITS = 6
BF16_MANTISSA_BITS = 8

_SPLITS = (256, 128, 128, 256, 256, 256, C_Q_LORA, C_KV_LORA, C_ROPE, 256, 256, 256)
_OFF = [0]
for _n in _SPLITS:
    _OFF.append(_OFF[-1] + _n)
(O_AQ, O_AK, O_AV, O_BQ, O_BK, O_BV, O_CQL, O_CKVL, O_CKR, O_DQ, O_DK, O_DV, IN_WIDTH) = _OFF

_GAIN_SIZES = (("n1", 1024), ("aq", 64), ("ak", 64), ("bq", 32), ("bk", 32), ("cqa", 256), ("ckva", 128),
               ("cqn", 96), ("ckn", 96), ("dq", 64), ("dk", 64))
G_OFF = {}
_o = 0
for _k, _n in _GAIN_SIZES:
    G_OFF[_k] = (_o, _n)
    _o += _n
G_TOTAL = _o

LANES = 128
BF16_ROWS = 16
TOK_TILE = 512
MLP_CHUNK = 1024
MLP_PAD = 8
TQ = 512
TQ_STEP = 4096
DENSE_TQ_STEP = 4096
KSTEP = 256
SCORES_AHEAD = 2
BAND_Q = 128
BAND_QSTEP = 1024
BAND_MASKED = 1e30
BAND_RADIUS = 64
VMEM_LIMIT = 56 * 1024 * 1024


def _cparams(sem, flags=None):
    return pltpu.CompilerParams(dimension_semantics=sem, vmem_limit_bytes=VMEM_LIMIT, flags=flags)


def _rms_rows(v, g_col):
    ms = jnp.mean(v * v, axis=0, keepdims=True)
    return v * lax.rsqrt(ms + NORM_EPS) * g_col


def _swap_halves(x, n):
    parts = []
    for i in range(0, x.shape[0], 2 * n):
        parts.append(x[i + n:i + 2 * n])
        parts.append(x[i:i + n])
    return jnp.concatenate(parts, axis=0)


def _rope_rows(x, cos, sin_signed, n):
    return x * cos + _swap_halves(x, n) * sin_signed


def _store_dilated(scr_ref, val, out_refs):
    tm, c = val.shape
    for g in range(c // LANES):
        scr_ref[g] = val[:, LANES * g:LANES * (g + 1)]
    for (_, dil), out in zip(DILATED_CONFIGS, out_refs):
        if dil == 1:
            out[0] = val.astype(BF16)
            continue
        n = tm // dil
        for r in range(dil):
            for g in range(c // LANES):
                lo = c * r + LANES * g
                out[0, :, lo:lo + LANES] = scr_ref[g, pl.ds(r, n, stride=dil), :].astype(BF16)


def _store_key_blocks(v_ref, v):
    for c in range(v.shape[1] // KSTEP):
        v_ref[0, c] = v[:, c * KSTEP:(c + 1) * KSTEP]


def _proj_body(x_ref, w_ref, wqb_ref, wkvb_ref, g_ref, cosa_ref, sina_ref, cosm_ref, sinm_ref,
               qa_ref, ka_ref, va_ref, qb_ref, kb_ref, vb_ref, qc_ref, kc_ref, vc_ref,
               dq1_ref, dq4_ref, dq16_ref, dk1_ref, dk4_ref, dk16_ref, dv1_ref, dv4_ref, dv16_ref,
               scr_ref):
    tm = x_ref.shape[1]

    def gain(name):
        o, n = G_OFF[name]
        return g_ref[o:o + n, :]

    x = x_ref[0].T
    h = (x * gain("n1")).astype(BF16)
    r = lax.rsqrt(jnp.mean(x * x, axis=0, keepdims=True) + NORM_EPS)

    def section(lo, hi):
        part = jnp.dot(w_ref[lo:hi, :], h, preferred_element_type=F32) * r
        return lambda o, n: part[o - lo:o - lo + n]

    cosa, sina = cosa_ref[...], sina_ref[...]
    cosm, sinm = cosm_ref[...], sinm_ref[...]
    zeros64 = jnp.zeros((64, tm), F32)
    zeros32 = jnp.zeros((32, tm), F32)

    def prep_a(proj):
        sa = HEAD_DIM ** -0.5 * LOG2E
        for hh in range(N_HEADS):
            q = _rope_rows(_rms_rows(proj(O_AQ + 64 * hh, 64), gain("aq")), cosa, sina, 16) * sa
            qa_ref[0, 64 * hh:64 * hh + 64, :] = q.astype(BF16)
        for g in range(A_KV_HEADS):
            k = _rope_rows(_rms_rows(proj(O_AK + 64 * g, 64), gain("ak")), cosa, sina, 16)
            kpad = jnp.concatenate([k, zeros64], axis=0)
            ka_ref[0, :, LANES * g:LANES * (g + 1)] = kpad.T.astype(BF16)
        _store_key_blocks(va_ref, proj(O_AV, 128).astype(BF16))

    def prep_b(proj):
        sb = B_QK_DIM ** -0.5 * LOG2E
        for i in range(2 * N_HEADS):
            qb_ref[0, 32 * i:32 * i + 32, :] = (_rms_rows(proj(O_BQ + 32 * i, 32), gain("bq")) * sb).astype(BF16)
        kpos = pl.program_id(1) * tm + lax.broadcasted_iota(jnp.int32, (BF16_ROWS, tm), 1)
        frow = lax.broadcasted_iota(jnp.int32, (BF16_ROWS, tm), 0)
        pos_hi = ((kpos >> POS_SPLIT_BITS) << POS_SPLIT_BITS).astype(F32)
        pos_lo = (kpos & ((1 << POS_SPLIT_BITS) - 1)).astype(F32)
        kfeat = jnp.where(frow < 3, pos_hi, jnp.where(frow < 6, pos_lo, 0.0))
        zeros48 = jnp.zeros((48, tm), F32)
        for hh in range(N_HEADS):
            k1 = _rms_rows(proj(O_BK + 64 * hh, 32), gain("bk"))
            k2 = _rms_rows(proj(O_BK + 64 * hh + 32, 32), gain("bk"))
            kpad = jnp.concatenate([k1, k2, kfeat, zeros48], axis=0)
            kb_ref[0, :, LANES * hh:LANES * (hh + 1)] = kpad.T.astype(BF16)
        _store_key_blocks(vb_ref, proj(O_BV, 256).astype(BF16))

    def prep_c(proj):
        sc = C_QK ** -0.5 * LOG2E
        cq = _rms_rows(proj(O_CQL, C_Q_LORA), gain("cqa")).astype(BF16)
        qc = jnp.dot(wqb_ref[...], cq, preferred_element_type=F32)
        ckv = _rms_rows(proj(O_CKVL, C_KV_LORA), gain("ckva")).astype(BF16)
        kvc = jnp.dot(wkvb_ref[...], ckv, preferred_element_type=F32)
        kr = proj(O_CKR, C_ROPE)
        vcs = []
        for hh in range(N_HEADS):
            q = _rms_rows(qc[C_QK * hh:C_QK * (hh + 1)], gain("cqn"))
            q = jnp.concatenate([q[:C_NOPE], _rope_rows(q[C_NOPE:], cosm, sinm, 16)], axis=0) * sc
            qc_ref[0, C_QK * hh:C_QK * (hh + 1), :] = q.astype(BF16)
            kk = jnp.concatenate([kvc[128 * hh:128 * hh + C_NOPE], kr], axis=0)
            kk = _rms_rows(kk, gain("ckn"))
            kk = jnp.concatenate([kk[:C_NOPE], _rope_rows(kk[C_NOPE:], cosm, sinm, 16), zeros32], axis=0)
            kc_ref[0, :, LANES * hh:LANES * (hh + 1)] = kk.T.astype(BF16)
            vcs.append(kvc[128 * hh + C_NOPE:128 * (hh + 1)])
        _store_key_blocks(vc_ref, jnp.concatenate(vcs, axis=0).astype(BF16))

    def prep_d(proj):
        sd = HEAD_DIM ** -0.5 * LOG2E
        qs, ks, vs = [], [], []
        for g in range(2):
            qg, kg = [], []
            for hh in (2 * g, 2 * g + 1):
                qg.append(_rms_rows(proj(O_DQ + 64 * hh, 64), gain("dq")) * sd)
                kg.append(_rms_rows(proj(O_DK + 64 * hh, 64), gain("dk")))
            qs.append(jnp.concatenate(qg, axis=0).T)
            ks.append(jnp.concatenate(kg, axis=0).T)
            vs.append(proj(O_DV + 128 * g, 128).T)
        _store_dilated(scr_ref.at[0], jnp.concatenate(qs, axis=1), (dq1_ref, dq4_ref, dq16_ref))
        _store_dilated(scr_ref.at[1], jnp.concatenate(ks, axis=1), (dk1_ref, dk4_ref, dk16_ref))
        _store_dilated(scr_ref.at[2], jnp.concatenate(vs, axis=1), (dv1_ref, dv4_ref, dv16_ref))

    sec_c = section(O_CQL, O_DQ)
    sec_d = section(O_DQ, IN_WIDTH)
    prep_c(sec_c)
    sec_b = section(O_BQ, O_CQL)
    prep_d(sec_d)
    sec_a = section(O_AQ, O_BQ)
    prep_b(sec_b)
    prep_a(sec_a)


def _project(x, w_in_t, wqb_t, wkvb_t, gains, cosa, sina, cosm, sinm):
    b, s, d = x.shape
    tm = TOK_TILE
    nt = s // tm
    fm = lambda rows: jax.ShapeDtypeStruct((b, rows, s), BF16)
    tmj = lambda cols: jax.ShapeDtypeStruct((b, s, cols), BF16)
    vblk = lambda rows: jax.ShapeDtypeStruct((b, s // KSTEP, rows, KSTEP), BF16)
    dviews = tuple(jax.ShapeDtypeStruct((b, s // dil, dil * D_WIDTH), BF16) for _, dil in DILATED_CONFIGS)
    out_shape = (fm(256), tmj(256), vblk(128),
                 fm(256), tmj(512), vblk(256),
                 fm(384), tmj(512), vblk(256)) + dviews * 3
    fm_spec = lambda rows: pl.BlockSpec((1, rows, tm), lambda i, j: (i, 0, j))
    tm_spec = lambda cols: pl.BlockSpec((1, tm, cols), lambda i, j: (i, j, 0))
    v_spec = lambda rows: pl.BlockSpec((1, tm // KSTEP, rows, KSTEP), lambda i, j: (i, j, 0, 0))
    dspecs = tuple(pl.BlockSpec((1, tm // dil, dil * D_WIDTH), lambda i, j: (i, j, 0)) for _, dil in DILATED_CONFIGS)
    out_specs = (fm_spec(256), tm_spec(256), v_spec(128),
                 fm_spec(256), tm_spec(512), v_spec(256),
                 fm_spec(384), tm_spec(512), v_spec(256)) + dspecs * 3
    whole = lambda i, j: (0, 0)
    in_specs = [
        pl.BlockSpec((1, tm, d), lambda i, j: (i, j, 0)),
        pl.BlockSpec((IN_WIDTH, d), whole),
        pl.BlockSpec((N_HEADS * C_QK, C_Q_LORA), whole),
        pl.BlockSpec((N_HEADS * 128, C_KV_LORA), whole),
        pl.BlockSpec((G_TOTAL, 1), whole),
        pl.BlockSpec((64, tm), lambda i, j: (0, j)),
        pl.BlockSpec((64, tm), lambda i, j: (0, j)),
        pl.BlockSpec((32, tm), lambda i, j: (0, j)),
        pl.BlockSpec((32, tm), lambda i, j: (0, j)),
    ]
    return pl.pallas_call(
        _proj_body,
        out_shape=out_shape,
        grid=(b, nt),
        in_specs=in_specs,
        out_specs=out_specs,
        scratch_shapes=[pltpu.VMEM((3, D_WIDTH // LANES, tm, LANES), F32)],
        compiler_params=_cparams(("parallel", "arbitrary")),
        name="project_heads",
    )(x, w_in_t, wqb_t, wkvb_t, gains, cosa, sina, cosm, sinm)


def _online_update(s, v_aug, carry, shift=None):
    m, acc = carry
    smax = jnp.max(s, axis=0, keepdims=True)
    if shift is None:
        m_new = jnp.maximum(m, smax)
        p = jnp.exp2(s - m_new).astype(BF16)
    else:
        m_new = jnp.maximum(m, smax + shift)
        p = jnp.exp2(s - (m_new - shift)).astype(BF16)
    alpha = jnp.exp2(m - m_new)
    acc = alpha * acc + jnp.dot(v_aug, p, preferred_element_type=F32)
    return m_new, acc


def _init_carry(dv, tq):
    return (jnp.full((1, tq), -jnp.inf, F32), jnp.zeros((dv + BF16_ROWS, tq), F32))


def _finish(carry, dv):
    _, acc = carry
    return acc[:dv] / acc[dv:dv + 1]


def _run_pipelined(n_jobs, n_blocks, scores_fn, update_fn, init_fn, finish_fn):
    steps = [(t, i) for t in range(n_jobs) for i in range(n_blocks)]
    pending = [scores_fn(*steps[n]) for n in range(min(SCORES_AHEAD, len(steps)))]
    carry = None
    for n, (t, i) in enumerate(steps):
        s = pending.pop(0)
        if n + SCORES_AHEAD < len(steps):
            pending.append(scores_fn(*steps[n + SCORES_AHEAD]))
        if i == 0:
            carry = init_fn()
        carry = update_fn(t, i, s, carry)
        if i == n_blocks - 1:
            finish_fn(t, carry)


def _dense_body(q_ref, k_ref, v_ref, o_ref):
    dq = q_ref.shape[1]
    nkb, dv, ks = v_ref.shape[1], v_ref.shape[2], v_ref.shape[3]
    tq = TQ
    ones = jnp.ones((BF16_ROWS, ks), BF16)
    zpad = jnp.zeros((LANES - dq, tq), BF16)
    qpads = [jnp.concatenate([q_ref[0, :, t * tq:(t + 1) * tq], zpad], axis=0) for t in range(q_ref.shape[2] // tq)]

    def scores(t, i):
        return jnp.dot(k_ref[0, i * ks:(i + 1) * ks, :], qpads[t], preferred_element_type=F32)

    def update(t, i, s, carry):
        return _online_update(s, jnp.concatenate([v_ref[0, i], ones], axis=0), carry)

    def finish(t, carry):
        o_ref[0, :, t * tq:(t + 1) * tq] = _finish(carry, dv).astype(o_ref.dtype)

    _run_pipelined(len(qpads), nkb, scores, update, lambda: _init_carry(dv, tq), finish)


def _dense_attention(q, k, v, dq, q_per_kv):
    b, rows, s = q.shape
    nh = rows // dq
    nkb, tk = v.shape[1], v.shape[3]
    return pl.pallas_call(
        _dense_body,
        out_shape=jax.ShapeDtypeStruct((b, nh * HEAD_DIM, s), BF16),
        grid=(b, nh, s // DENSE_TQ_STEP),
        in_specs=[
            pl.BlockSpec((1, dq, DENSE_TQ_STEP), lambda i, h, j: (i, h, j)),
            pl.BlockSpec((1, s, LANES), lambda i, h, j: (i, 0, h // q_per_kv)),
            pl.BlockSpec((1, nkb, HEAD_DIM, tk), lambda i, h, j: (i, 0, h // q_per_kv, 0)),
        ],
        out_specs=pl.BlockSpec((1, HEAD_DIM, DENSE_TQ_STEP), lambda i, h, j: (i, h, j)),
        compiler_params=_cparams(("parallel", "arbitrary", "arbitrary")),
        name="dense_attention",
    )(q, k, v)


def _diff_body(sc_ref, q_ref, k_ref, v_ref, g_ref, bias_ref, o_ref):
    tq = TQ
    n_tiles = q_ref.shape[2] // tq
    nkb, dv, ks = v_ref.shape[1], v_ref.shape[2], v_ref.shape[3]
    hh = pl.program_id(1)
    lam = sc_ref[0]
    out_scale = sc_ref[1]
    slope = sc_ref[2 + hh]
    z32 = jnp.zeros((B_QK_DIM, tq), BF16)
    z48 = jnp.zeros((48, tq), BF16)
    zfeat = jnp.zeros((BF16_ROWS, tq), BF16)
    ones = jnp.ones((BF16_ROWS, ks), BF16)
    frow = lax.broadcasted_iota(jnp.int32, (BF16_ROWS, tq), 0)
    piece = jnp.where(frow % 3 == 0, LOG2E_PIECES[0], jnp.where(frow % 3 == 1, LOG2E_PIECES[1], LOG2E_PIECES[2]))
    cfeat = jnp.where(frow < 6, piece, 0.0)
    lane_pos = lax.broadcasted_iota(jnp.int32, (1, tq), 1)

    n_mixed = tq // ks

    def tile_index(job):
        return pl.program_id(2) * n_tiles + job // 2

    def block_index(job, ii):
        return lax.rem(tile_index(job) * n_mixed + ii, nkb)

    def side(job, ii):
        return jnp.where(block_index(job, ii) < tile_index(job) * n_mixed, 1.0, -1.0).astype(F32)

    def qrows(job):
        q = q_ref[0, :, (job // 2) * tq:(job // 2 + 1) * tq]
        if job % 2 == 0:
            return jnp.concatenate([q[:B_QK_DIM], z32], axis=0)
        return jnp.concatenate([z32, q[B_QK_DIM:]], axis=0)

    def scores(job, ii):
        i = block_index(job, ii)
        k = k_ref[0, pl.ds(pl.multiple_of(i * ks, ks), ks), :]
        feat = zfeat if ii < n_mixed else (cfeat * (side(job, ii) * slope)).astype(BF16)
        return jnp.dot(k, jnp.concatenate([qrows(job), feat, z48], axis=0), preferred_element_type=F32)

    def update(job, ii, s, carry):
        v_aug = jnp.concatenate([v_ref[0, block_index(job, ii)], ones], axis=0)
        if ii < n_mixed:
            return _online_update(s - bias_ref[0, ii], v_aug, carry)
        qpos = (tile_index(job) * tq + lane_pos).astype(F32)
        return _online_update(s, v_aug, carry, qpos * (-(side(job, ii) * slope * LOG2E)))

    res = []

    def finish(job, carry):
        res.append(_finish(carry, dv))
        if job % 2 == 1:
            o = res[job - 1] - lam * res[job]
            t = job // 2
            o_ref[0, :, t * tq:(t + 1) * tq] = (_rms_rows(o, g_ref[...]) * out_scale).astype(o_ref.dtype)

    _run_pipelined(2 * n_tiles, nkb, scores, update, lambda: _init_carry(dv, tq), finish)


def _diff_attention(scal, slopes, q, k, v, sub_g):
    b, rows, s = q.shape
    nkb, tk = v.shape[1], v.shape[3]
    n_mixed = TQ // tk
    jj = jnp.arange(tk)[None, :, None] + tk * jnp.arange(n_mixed)[:, None, None]
    dist = jnp.abs(jj - jnp.arange(TQ)[None, None, :]).astype(F32)
    bias = dist[None] * (LOG2E * jnp.asarray(slopes, F32))[:, None, None, None]
    return pl.pallas_call(
        _diff_body,
        out_shape=jax.ShapeDtypeStruct((b, rows, s), BF16),
        grid=(b, N_HEADS, s // TQ_STEP),
        in_specs=[
            pl.BlockSpec(memory_space=pltpu.SMEM),
            pl.BlockSpec((1, 2 * B_QK_DIM, TQ_STEP), lambda i, h, j: (i, h, j)),
            pl.BlockSpec((1, s, LANES), lambda i, h, j: (i, 0, h)),
            pl.BlockSpec((1, nkb, HEAD_DIM, tk), lambda i, h, j: (i, 0, h, 0)),
            pl.BlockSpec((HEAD_DIM, 1), lambda i, h, j: (0, 0)),
            pl.BlockSpec((1, n_mixed, tk, TQ), lambda i, h, j: (h, 0, 0, 0)),
        ],
        out_specs=pl.BlockSpec((1, HEAD_DIM, TQ_STEP), lambda i, h, j: (i, h, j)),
        compiler_params=_cparams(("parallel", "arbitrary", "arbitrary")),
        name="diff_attention",
    )(scal, q, k, v, sub_g, bias)


def _band_body(q_ref, k_ref, v_ref, pen_ref, o_ref, lse_ref, *, qb):
    l_len = k_ref.shape[1]
    win = pen_ref.shape[3]
    lane = lax.broadcasted_iota(jnp.int32, (qb, LANES), 1)
    n_sub = q_ref.shape[1] // qb
    n_groups = q_ref.shape[2] // LANES
    for sb in range(n_sub):
        jb = pl.program_id(2) * n_sub + sb
        start = jnp.clip(jb * qb - BAND_RADIUS, 0, l_len - win)
        start = pl.multiple_of(start, BAND_RADIUS)
        variant = (jb * qb - start) // BAND_RADIUS
        q = q_ref[0, sb * qb:(sb + 1) * qb, :]
        kwin = k_ref[0, pl.ds(start, win), :]
        vwin = v_ref[0, pl.ds(start, win), :]
        for gc in range(n_groups):
            g = gc % 2
            q2 = q[:, LANES * gc:LANES * (gc + 1)].astype(F32)
            k2 = kwin[:, LANES * gc:LANES * (gc + 1)]
            v2 = vwin[:, LANES * gc:LANES * (gc + 1)]
            out_g = jnp.zeros((qb, LANES), F32)
            lse_g = jnp.zeros((qb, LANES), F32)
            for hh in range(2):
                own = (lane >= HEAD_DIM * hh) & (lane < HEAD_DIM * (hh + 1))
                qm = jnp.where(own, q2, 0.0).astype(BF16)
                s = lax.dot_general(qm, k2, (((1,), (1,)), ((), ())), preferred_element_type=F32)
                s = s - pen_ref[variant, 2 * g + hh]
                m = jnp.max(s, axis=1, keepdims=True)
                p = jnp.exp2(s - m)
                lsum = jnp.sum(p, axis=1, keepdims=True)
                o = jnp.dot(p.astype(BF16), v2, preferred_element_type=F32) / lsum
                lse = m + jnp.log2(lsum)
                out_g = jnp.where(own, o, out_g)
                lse_g = jnp.where(own, lse, lse_g)
            o_ref[0, sb * qb:(sb + 1) * qb, LANES * gc:LANES * (gc + 1)] = out_g.astype(o_ref.dtype)
            lse_ref[0, sb * qb:(sb + 1) * qb, LANES * gc:LANES * (gc + 1)] = lse_g


def _band_penalty(qb, win, dil, slopes):
    rows = jnp.arange(qb)[None, :, None] + BAND_RADIUS * jnp.arange(3)[:, None, None]
    rel = jnp.abs(rows - jnp.arange(win)[None, None, :])
    scale = jnp.asarray(slopes, F32) * (float(dil) * LOG2E)
    pen = rel.astype(F32)[:, None] * scale[None, :, None, None]
    return jnp.where((rel <= BAND_RADIUS)[:, None], pen, BAND_MASKED)


def _band_attention(q, k, v, dil, slopes):
    b, l_len, cw = q.shape
    c = cw // dil
    qb = min(BAND_Q, l_len)
    qstep = min(BAND_QSTEP, l_len)
    cps = min(dil, max(1, BAND_QSTEP // l_len))
    assert dil % cps == 0
    win = min(qb + 2 * BAND_RADIUS, l_len)
    pen = _band_penalty(qb, win, dil, slopes)
    body = functools.partial(_band_body, qb=qb)
    return pl.pallas_call(
        body,
        out_shape=(jax.ShapeDtypeStruct((b, l_len, cw), BF16), jax.ShapeDtypeStruct((b, l_len, cw), F32)),
        grid=(b, dil // cps, l_len // qstep),
        in_specs=[
            pl.BlockSpec((1, qstep, c * cps), lambda i, r, j: (i, j, r)),
            pl.BlockSpec((1, l_len, c * cps), lambda i, r, j: (i, 0, r)),
            pl.BlockSpec((1, l_len, c * cps), lambda i, r, j: (i, 0, r)),
            pl.BlockSpec(pen.shape, lambda i, r, j: (0, 0, 0, 0)),
        ],
        out_specs=(pl.BlockSpec((1, qstep, c * cps), lambda i, r, j: (i, j, r)),) * 2,
        compiler_params=_cparams(("parallel", "arbitrary", "arbitrary")),
        name="band_attention_d%d" % dil,
    )(q, k, v, pen)


def _band_combine(in_refs, scr_ref, tm, c):
    vals = []
    for idx, (_, dil) in enumerate(DILATED_CONFIGS):
        for which in range(2):
            src = in_refs[2 * idx + which]
            if dil == 1:
                vals.append(src[0].astype(F32))
                continue
            groups = []
            for g in range(c // LANES):
                slot = scr_ref.at[(2 * idx + which) * (c // LANES) + g]
                for r in range(dil):
                    lo = c * r + LANES * g
                    slot[pl.ds(r, tm // dil, stride=dil), :] = src[0, :, lo:lo + LANES].astype(F32)
                groups.append(slot[...])
            vals.append(jnp.concatenate(groups, axis=1))
    outs, lses = vals[0::2], vals[1::2]
    m = functools.reduce(jnp.maximum, lses)
    ws = [jnp.exp2(x - m) for x in lses]
    num = functools.reduce(lambda a, t: a + t, [w * o for w, o in zip(ws, outs)])
    den = functools.reduce(lambda a, t: a + t, ws)
    return num / den


def _outproj_body(*refs):
    n_d = 2 * len(DILATED_CONFIGS)
    x_ref, oa_ref, ob_ref, oc_ref = refs[:4]
    d_refs = refs[4:4 + n_d]
    w_ref, g_ref, x1_ref, h2_ref, edge_ref, scr_ref = refs[4 + n_d:]
    tm = x_ref.shape[1]
    j = pl.program_id(1)
    mix = jnp.concatenate([oa_ref[0], ob_ref[0], oc_ref[0]], axis=0)
    od = _band_combine(d_refs, scr_ref, tm, D_WIDTH).astype(BF16)
    w = w_ref[...]
    y = jnp.dot(w[:, :768], mix, preferred_element_type=F32)
    y = y + lax.dot_general(w[:, 768:], od, (((1,), (1,)), ((), ())), preferred_element_type=F32)
    x1 = x_ref[0] + y.T
    x1_ref[0] = x1
    ms = jnp.mean(x1 * x1, axis=1, keepdims=True)
    h2 = (x1 * lax.rsqrt(ms + NORM_EPS) * g_ref[...]).astype(BF16)
    h2_ref[0] = h2
    edge_ref[0, pl.ds(2 * j, 1), :] = h2[0:1].astype(F32)
    edge_ref[0, pl.ds(2 * j + 1, 1), :] = h2[tm - 1:tm].astype(F32)


def _out_project(x, oa, ob, oc, branches, w_out_t, g2):
    b, s, d = x.shape
    tm = TOK_TILE
    nt = s // tm
    fm = lambda rows: pl.BlockSpec((1, rows, tm), lambda i, j: (i, 0, j))
    tmj = pl.BlockSpec((1, tm, d), lambda i, j: (i, j, 0))
    d_specs = []
    for (_, dil) in DILATED_CONFIGS:
        d_specs += [pl.BlockSpec((1, tm // dil, dil * D_WIDTH), lambda i, j: (i, j, 0))] * 2
    flat = [a for pair in branches for a in pair]
    return pl.pallas_call(
        _outproj_body,
        out_shape=(jax.ShapeDtypeStruct((b, s, d), F32), jax.ShapeDtypeStruct((b, s, d), BF16),
                   jax.ShapeDtypeStruct((b, 2 * nt, d), F32)),
        grid=(b, nt),
        in_specs=[tmj, fm(256), fm(256), fm(256)] + d_specs + [
            pl.BlockSpec((d, d), lambda i, j: (0, 0)),
            pl.BlockSpec((1, d), lambda i, j: (0, 0))],
        out_specs=(tmj, tmj, pl.BlockSpec((1, 2 * nt, d), lambda i, j: (i, 0, 0))),
        scratch_shapes=[pltpu.VMEM((2 * len(DILATED_CONFIGS) * (D_WIDTH // LANES), tm, LANES), F32)],
        compiler_params=_cparams(("parallel", "arbitrary")),
        name="out_project",
    )(x, oa, ob, oc, *flat, w_out_t, g2)


def _halo_body(h_ref, w_ref, o_ref):
    o_ref[...] = jnp.dot(h_ref[...], w_ref[...], preferred_element_type=F32)


def _halo_up(halo_h, w_up):
    nh, d = halo_h.shape
    ff2 = w_up.shape[1]
    n_blk = 4
    cols = ff2 // n_blk
    return pl.pallas_call(
        _halo_body,
        out_shape=jax.ShapeDtypeStruct((nh, ff2), F32),
        grid=(n_blk,),
        in_specs=[pl.BlockSpec((nh, d), lambda i: (0, 0)),
                  pl.BlockSpec((d, cols), lambda i: (0, i))],
        out_specs=pl.BlockSpec((nh, cols), lambda i: (0, i)),
        compiler_params=_cparams(("arbitrary",)),
        name="mlp_halo_up",
    )(halo_h, w_up)


def _mlp_body(x1_ref, h2_ref, wup_ref, wdn_ref, par_ref, halo_ref, o_ref, gbuf_ref, vbuf_ref, *, chunks):
    tf = h2_ref.shape[1]
    ff = wdn_ref.shape[0]
    nt = pl.num_programs(1)
    row = (pl.program_id(0) * nt + pl.program_id(1)) * 2
    h = h2_ref[0]

    def conv(buf_ref, u, c0, width):
        buf_ref[MLP_PAD:MLP_PAD + tf, :width] = u
        buf_ref[MLP_PAD - 1:MLP_PAD, :width] = halo_ref[pl.ds(row, 1), c0:c0 + width]
        buf_ref[MLP_PAD + tf:MLP_PAD + tf + 1, :width] = halo_ref[pl.ds(row + 1, 1), c0:c0 + width]
        prev = buf_ref[MLP_PAD - 1:MLP_PAD - 1 + tf, :width]
        nxt = buf_ref[MLP_PAD + 1:MLP_PAD + 1 + tf, :width]
        p = par_ref[:, c0:c0 + width]
        return p[0:1] * prev + p[1:2] * u + p[2:3] * nxt + p[3:4]

    def up(c0, width):
        return (jnp.dot(h, wup_ref[:, c0:c0 + width], preferred_element_type=F32),
                jnp.dot(h, wup_ref[:, ff + c0:ff + c0 + width], preferred_element_type=F32))

    y = x1_ref[0]
    nxt_up = up(*chunks[0])
    for n, (c0, width) in enumerate(chunks):
        ug, uv = nxt_up
        if n + 1 < len(chunks):
            nxt_up = up(*chunks[n + 1])
        gate = conv(gbuf_ref, ug, c0, width)
        val = conv(vbuf_ref, uv, ff + c0, width)
        act = gate / (1.0 + jnp.exp(-gate)) * val
        y = y + jnp.dot(act.astype(BF16), wdn_ref[c0:c0 + width, :], preferred_element_type=F32)
    o_ref[0] = y


def _mlp(x1, h2, w_up, w_down, conv_par, halo_up, chunks):
    b, s, d = x1.shape
    tf = TOK_TILE
    ff2 = w_up.shape[1]
    nh = halo_up.shape[0]
    whole = lambda i, j: (0, 0)
    body = functools.partial(_mlp_body, chunks=chunks)
    const = dict(pipeline_mode=pl.Buffered(1))
    wmax = max(w for _, w in chunks)
    return pl.pallas_call(
        body,
        out_shape=jax.ShapeDtypeStruct((b, s, d), F32),
        grid=(b, s // tf),
        in_specs=[pl.BlockSpec((1, tf, d), lambda i, j: (i, j, 0)),
                  pl.BlockSpec((1, tf, d), lambda i, j: (i, j, 0)),
                  pl.BlockSpec((d, ff2), whole, **const),
                  pl.BlockSpec((ff2 // 2, d), whole, **const),
                  pl.BlockSpec((8, ff2), whole, **const),
                  pl.BlockSpec((nh, ff2), whole, **const)],
        out_specs=pl.BlockSpec((1, tf, d), lambda i, j: (i, j, 0)),
        scratch_shapes=[pltpu.VMEM((tf + 2 * MLP_PAD, wmax), F32)] * 2,
        compiler_params=_cparams(("arbitrary", "arbitrary")),
        name="gated_conv_mlp",
    )(x1, h2, w_up, w_down, conv_par, halo_up)


def _halo_rows(edges, nt):
    b, _, d = edges.shape
    ez = jnp.concatenate([edges, jnp.zeros((b, 1, d), edges.dtype)], axis=1)
    idx = []
    for j in range(nt):
        idx += [2 * j - 1 if j > 0 else 2 * nt, 2 * j + 2 if j < nt - 1 else 2 * nt]
    rows = jnp.take(ez, jnp.asarray(idx, jnp.int32), axis=1).reshape(b * nt * 2, d)
    pad = (-rows.shape[0]) % BF16_ROWS
    return jnp.pad(rows, ((0, pad), (0, 0))).astype(BF16)


def _rope_tables(s):
    def freqs(dim):
        return 1.0 / (ROPE_THETA ** (jnp.arange(0, dim, 2, dtype=F32) / dim))
    t = jnp.arange(s)
    rows = (t // GRID_W).astype(F32)
    cols = (t % GRID_W).astype(F32)
    pos = t.astype(F32)
    fa = freqs(HEAD_DIM // 2)
    ar, ac = fa[:, None] * rows[None, :], fa[:, None] * cols[None, :]
    am = freqs(C_ROPE)[:, None] * pos[None, :]
    cosa = jnp.concatenate([jnp.cos(ar), jnp.cos(ar), jnp.cos(ac), jnp.cos(ac)], axis=0)
    sina = jnp.concatenate([-jnp.sin(ar), jnp.sin(ar), -jnp.sin(ac), jnp.sin(ac)], axis=0)
    cosm = jnp.concatenate([jnp.cos(am), jnp.cos(am)], axis=0)
    sinm = jnp.concatenate([-jnp.sin(am), jnp.sin(am)], axis=0)
    return cosa, sina, cosm, sinm


def kernel(x, norm1_g, w_in, a_qn_g, a_kn_g, b_qn_g, b_kn_g, b_lam_q1, b_lam_k1, b_lam_q2, b_lam_k2, b_sub_g,
           c_qa_g, c_kva_g, c_wqb, c_wkvb, c_qn_g, c_kn_g, d_qn_g, d_kn_g, w_out, norm2_g, w_up, conv_w, conv_b,
           w_down):
    b, s, d = x.shape
    depth = w_in.shape[0]
    ff = w_down.shape[1]
    nt = s // TOK_TILE
    assert s % TOK_TILE == 0 and TOK_TILE % KSTEP == 0 and TQ % KSTEP == 0 and s % TQ == 0 and s % GRID_W == 0
    assert s <= 1 << (POS_SPLIT_BITS + BF16_MANTISSA_BITS) and s % TQ_STEP == 0 and s % DENSE_TQ_STEP == 0
    for window, dil in DILATED_CONFIGS:
        assert window // (2 * dil) == BAND_RADIUS and (s // dil) % min(BAND_Q, s // dil) == 0
        assert (s // dil) % min(BAND_QSTEP, s // dil) == 0 and BAND_QSTEP % BAND_Q == 0
        assert TOK_TILE % (dil * BF16_ROWS) == 0

    tr = lambda w: jnp.swapaxes(w, 1, 2).astype(BF16)
    gains = jnp.concatenate([norm1_g, a_qn_g, a_kn_g, b_qn_g, b_kn_g, c_qa_g, c_kva_g, c_qn_g, c_kn_g,
                             d_qn_g, d_kn_g], axis=1)[:, :, None]
    conv_par = jnp.concatenate([conv_w, conv_b[:, None, :], jnp.zeros((depth, 4, 2 * ff), F32)], axis=1)
    slopes = [2.0 ** (-8.0 * i / (2 * N_HEADS)) for i in range(1, 2 * N_HEADS + 1)]
    slopes_b, slopes_d = slopes[:N_HEADS], tuple(slopes[N_HEADS:])
    lam_init = jnp.asarray([0.8 - 0.6 * math.exp(-0.3 * l) for l in range(depth)], F32)
    lam = (jnp.exp(jnp.sum(b_lam_q1 * b_lam_k1, axis=1)) - jnp.exp(jnp.sum(b_lam_q2 * b_lam_k2, axis=1)) + lam_init)
    scal = jnp.concatenate([lam[:, None], (1.0 - lam_init)[:, None],
                            jnp.broadcast_to(jnp.asarray(slopes_b, F32), (depth, N_HEADS)),
                            jnp.zeros((depth, 2), F32)], axis=1)
    layers = dict(w_in=tr(w_in), wqb=tr(c_wqb), wkvb=tr(c_wkvb), w_out=tr(w_out),
                  w_up=w_up.astype(BF16), w_down=w_down.astype(BF16),
                  gains=gains, g2=norm2_g[:, None, :], sub_g=b_sub_g[:, :, None], conv_par=conv_par, scal=scal)
    cosa, sina, cosm, sinm = _rope_tables(s)
    assert ff % LANES == 0
    chunks = tuple((c0, min(MLP_CHUNK, ff - c0)) for c0 in range(0, ff, MLP_CHUNK))

    def layer(xs, p):
        (qa, ka, va, qb, kb, vb, qc, kc, vc, dq1, dq4, dq16, dk1, dk4, dk16, dv1, dv4, dv16) = _project(
            xs, p["w_in"], p["wqb"], p["wkvb"], p["gains"], cosa, sina, cosm, sinm)
        oa = _dense_attention(qa, ka, va, HEAD_DIM, N_HEADS // A_KV_HEADS)
        ob = _diff_attention(p["scal"], slopes_b, qb, kb, vb, p["sub_g"])
        oc = _dense_attention(qc, kc, vc, C_QK, 1)
        branches = [_band_attention(q_, k_, v_, dil, slopes_d)
                    for (_, dil), q_, k_, v_ in zip(DILATED_CONFIGS, (dq1, dq4, dq16), (dk1, dk4, dk16), (dv1, dv4, dv16))]
        x1, h2, edges = _out_project(xs, oa, ob, oc, branches, p["w_out"], p["g2"])
        halo = _halo_up(_halo_rows(edges, nt), p["w_up"])
        return _mlp(x1, h2, p["w_up"], p["w_down"], p["conv_par"], halo, chunks), None

    out, _ = layer(x, jax.tree.map(lambda a: a[0], layers))
    if depth > 1:
        out, _ = lax.scan(layer, out, jax.tree.map(lambda a: a[1:], layers))
    return out
```

```python
import functools
import math

import jax
import jax.numpy as jnp
import numpy as np
from jax import lax
from jax.experimental import pallas as pl
from jax.experimental.pallas import tpu as pltpu

F32 = jnp.float32
BF16 = jnp.bfloat16

HEAD_DIM = 64
N_HEADS = 4
A_KV_HEADS = 2
B_QK_DIM = 32
C_Q_LORA = 256
C_KV_LORA = 128
C_NOPE = 64
C_ROPE = 32
C_QK = C_NOPE + C_ROPE
DILATED_CONFIGS = ((128, 1), (512, 4), (2048, 16))
D_WIDTH = N_HEADS * HEAD_DIM
GRID_W = 64
ROPE_THETA = 10000.0
NORM_EPS = 1e-6
LOG2E = 1.4426950408889634


def _bf16_pieces(x, n):
    out = []
    for _ in range(n):
        u = np.float32(x).view(np.uint32)
        u = np.uint32((int(u) + 0x7FFF + ((int(u) >> 16) & 1)) & 0xFFFF0000)
        piece = float(u.view(np.float32))
        out.append(piece)
        x = x - piece
    return tuple(out)


LOG2E_PIECES = _bf16_pieces(LOG2E, 3)
POS_SPLIT_BITS = 6
BF16_MANTISSA_BITS = 8

_SPLITS = (256, 128, 128, 256, 256, 256, C_Q_LORA, C_KV_LORA, C_ROPE, 256, 256, 256)
_OFF = [0]
for _n in _SPLITS:
    _OFF.append(_OFF[-1] + _n)
(O_AQ, O_AK, O_AV, O_BQ, O_BK, O_BV, O_CQL, O_CKVL, O_CKR, O_DQ, O_DK, O_DV, IN_WIDTH) = _OFF

_GAIN_SIZES = (("n1", 1024), ("aq", 64), ("ak", 64), ("bq", 32), ("bk", 32), ("cqa", 256), ("ckva", 128),
               ("cqn", 96), ("ckn", 96), ("dq", 64), ("dk", 64))
G_OFF = {}
_o = 0
for _k, _n in _GAIN_SIZES:
    G_OFF[_k] = (_o, _n)
    _o += _n
G_TOTAL = _o

LANES = 128
BF16_ROWS = 16
TOK_TILE = 512
MLP_CHUNK = 1024
MLP_PAD = 8
TQ = 512
TQ_STEP = 2048
DENSE_TQ_STEP = 4096
KSTEP = 256
SCORES_AHEAD = 2
BAND_Q = 128
BAND_QSTEP = 1024
BAND_MASKED = 1e30
BAND_RADIUS = 64
VMEM_LIMIT = 56 * 1024 * 1024


def _cparams(sem, flags=None):
    return pltpu.CompilerParams(dimension_semantics=sem, vmem_limit_bytes=VMEM_LIMIT, flags=flags)


def _rms_rows(v, g_col):
    ms = jnp.mean(v * v, axis=0, keepdims=True)
    return v * lax.rsqrt(ms + NORM_EPS) * g_col


def _swap_halves(x, n):
    parts = []
    for i in range(0, x.shape[0], 2 * n):
        parts.append(x[i + n:i + 2 * n])
        parts.append(x[i:i + n])
    return jnp.concatenate(parts, axis=0)


def _rope_rows(x, cos, sin_signed, n):
    return x * cos + _swap_halves(x, n) * sin_signed


def _store_dilated(scr_ref, val, out_refs):
    tm, c = val.shape
    for g in range(c // LANES):
        scr_ref[g] = val[:, LANES * g:LANES * (g + 1)]
    for (_, dil), out in zip(DILATED_CONFIGS, out_refs):
        if dil == 1:
            out[0] = val.astype(BF16)
            continue
        n = tm // dil
        for r in range(dil):
            for g in range(c // LANES):
                lo = c * r + LANES * g
                out[0, :, lo:lo + LANES] = scr_ref[g, pl.ds(r, n, stride=dil), :].astype(BF16)


def _store_key_blocks(v_ref, v):
    for c in range(v.shape[1] // KSTEP):
        v_ref[0, c] = v[:, c * KSTEP:(c + 1) * KSTEP]


def _proj_body(x_ref, w_ref, wqb_ref, wkvb_ref, g_ref, cosa_ref, sina_ref, cosm_ref, sinm_ref,
               qa_ref, ka_ref, va_ref, qb_ref, kb_ref, vb_ref, qc_ref, kc_ref, vc_ref,
               dq1_ref, dq4_ref, dq16_ref, dk1_ref, dk4_ref, dk16_ref, dv1_ref, dv4_ref, dv16_ref,
               scr_ref):
    tm = x_ref.shape[1]

    def gain(name):
        o, n = G_OFF[name]
        return g_ref[o:o + n, :]

    x = x_ref[0].T
    h = (x * gain("n1")).astype(BF16)
    r = lax.rsqrt(jnp.mean(x * x, axis=0, keepdims=True) + NORM_EPS)

    def section(lo, hi):
        part = jnp.dot(w_ref[lo:hi, :], h, preferred_element_type=F32) * r
        return lambda o, n: part[o - lo:o - lo + n]

    cosa, sina = cosa_ref[...], sina_ref[...]
    cosm, sinm = cosm_ref[...], sinm_ref[...]
    zeros64 = jnp.zeros((64, tm), F32)
    zeros32 = jnp.zeros((32, tm), F32)

    def prep_a(proj):
        sa = HEAD_DIM ** -0.5 * LOG2E
        for hh in range(N_HEADS):
            q = _rope_rows(_rms_rows(proj(O_AQ + 64 * hh, 64), gain("aq")), cosa, sina, 16) * sa
            qa_ref[0, 64 * hh:64 * hh + 64, :] = q.astype(BF16)
        for g in range(A_KV_HEADS):
            k = _rope_rows(_rms_rows(proj(O_AK + 64 * g, 64), gain("ak")), cosa, sina, 16)
            kpad = jnp.concatenate([k, zeros64], axis=0)
            ka_ref[0, :, LANES * g:LANES * (g + 1)] = kpad.T.astype(BF16)
        _store_key_blocks(va_ref, proj(O_AV, 128).astype(BF16))

    def prep_b(proj):
        sb = B_QK_DIM ** -0.5 * LOG2E
        for i in range(2 * N_HEADS):
            qb_ref[0, 32 * i:32 * i + 32, :] = (_rms_rows(proj(O_BQ + 32 * i, 32), gain("bq")) * sb).astype(BF16)
        kpos = pl.program_id(1) * tm + lax.broadcasted_iota(jnp.int32, (BF16_ROWS, tm), 1)
        frow = lax.broadcasted_iota(jnp.int32, (BF16_ROWS, tm), 0)
        pos_hi = ((kpos >> POS_SPLIT_BITS) << POS_SPLIT_BITS).astype(F32)
        pos_lo = (kpos & ((1 << POS_SPLIT_BITS) - 1)).astype(F32)
        kfeat = jnp.where(frow < 3, pos_hi, jnp.where(frow < 6, pos_lo, 0.0))
        zeros48 = jnp.zeros((48, tm), F32)
        for hh in range(N_HEADS):
            k1 = _rms_rows(proj(O_BK + 64 * hh, 32), gain("bk"))
            k2 = _rms_rows(proj(O_BK + 64 * hh + 32, 32), gain("bk"))
            kpad = jnp.concatenate([k1, k2, kfeat, zeros48], axis=0)
            kb_ref[0, :, LANES * hh:LANES * (hh + 1)] = kpad.T.astype(BF16)
        _store_key_blocks(vb_ref, proj(O_BV, 256).astype(BF16))

    def prep_c(proj):
        sc = C_QK ** -0.5 * LOG2E
        cq = _rms_rows(proj(O_CQL, C_Q_LORA), gain("cqa")).astype(BF16)
        qc = jnp.dot(wqb_ref[...], cq, preferred_element_type=F32)
        ckv = _rms_rows(proj(O_CKVL, C_KV_LORA), gain("ckva")).astype(BF16)
        kvc = jnp.dot(wkvb_ref[...], ckv, preferred_element_type=F32)
        kr = proj(O_CKR, C_ROPE)
        vcs = []
        for hh in range(N_HEADS):
            q = _rms_rows(qc[C_QK * hh:C_QK * (hh + 1)], gain("cqn"))
            q = jnp.concatenate([q[:C_NOPE], _rope_rows(q[C_NOPE:], cosm, sinm, 16)], axis=0) * sc
            qc_ref[0, C_QK * hh:C_QK * (hh + 1), :] = q.astype(BF16)
            kk = jnp.concatenate([kvc[128 * hh:128 * hh + C_NOPE], kr], axis=0)
            kk = _rms_rows(kk, gain("ckn"))
            kk = jnp.concatenate([kk[:C_NOPE], _rope_rows(kk[C_NOPE:], cosm, sinm, 16), zeros32], axis=0)
            kc_ref[0, :, LANES * hh:LANES * (hh + 1)] = kk.T.astype(BF16)
            vcs.append(kvc[128 * hh + C_NOPE:128 * (hh + 1)])
        _store_key_blocks(vc_ref, jnp.concatenate(vcs, axis=0).astype(BF16))

    def prep_d(proj):
        sd = HEAD_DIM ** -0.5 * LOG2E
        qs, ks, vs = [], [], []
        for g in range(2):
            qg, kg = [], []
            for hh in (2 * g, 2 * g + 1):
                qg.append(_rms_rows(proj(O_DQ + 64 * hh, 64), gain("dq")) * sd)
                kg.append(_rms_rows(proj(O_DK + 64 * hh, 64), gain("dk")))
            qs.append(jnp.concatenate(qg, axis=0).T)
            ks.append(jnp.concatenate(kg, axis=0).T)
            vs.append(proj(O_DV + 128 * g, 128).T)
        _store_dilated(scr_ref.at[0], jnp.concatenate(qs, axis=1), (dq1_ref, dq4_ref, dq16_ref))
        _store_dilated(scr_ref.at[1], jnp.concatenate(ks, axis=1), (dk1_ref, dk4_ref, dk16_ref))
        _store_dilated(scr_ref.at[2], jnp.concatenate(vs, axis=1), (dv1_ref, dv4_ref, dv16_ref))

    sec_c = section(O_CQL, O_DQ)
    sec_d = section(O_DQ, IN_WIDTH)
    prep_c(sec_c)
    sec_b = section(O_BQ, O_CQL)
    prep_d(sec_d)
    sec_a = section(O_AQ, O_BQ)
    prep_b(sec_b)
    prep_a(sec_a)


def _project(x, w_in_t, wqb_t, wkvb_t, gains, cosa, sina, cosm, sinm):
    b, s, d = x.shape
    tm = TOK_TILE
    nt = s // tm
    fm = lambda rows: jax.ShapeDtypeStruct((b, rows, s), BF16)
    tmj = lambda cols: jax.ShapeDtypeStruct((b, s, cols), BF16)
    vblk = lambda rows: jax.ShapeDtypeStruct((b, s // KSTEP, rows, KSTEP), BF16)
    dviews = tuple(jax.ShapeDtypeStruct((b, s // dil, dil * D_WIDTH), BF16) for _, dil in DILATED_CONFIGS)
    out_shape = (fm(256), tmj(256), vblk(128),
                 fm(256), tmj(512), vblk(256),
                 fm(384), tmj(512), vblk(256)) + dviews * 3
    fm_spec = lambda rows: pl.BlockSpec((1, rows, tm), lambda i, j: (i, 0, j))
    tm_spec = lambda cols: pl.BlockSpec((1, tm, cols), lambda i, j: (i, j, 0))
    v_spec = lambda rows: pl.BlockSpec((1, tm // KSTEP, rows, KSTEP), lambda i, j: (i, j, 0, 0))
    dspecs = tuple(pl.BlockSpec((1, tm // dil, dil * D_WIDTH), lambda i, j: (i, j, 0)) for _, dil in DILATED_CONFIGS)
    out_specs = (fm_spec(256), tm_spec(256), v_spec(128),
                 fm_spec(256), tm_spec(512), v_spec(256),
                 fm_spec(384), tm_spec(512), v_spec(256)) + dspecs * 3
    whole = lambda i, j: (0, 0)
    in_specs = [
        pl.BlockSpec((1, tm, d), lambda i, j: (i, j, 0)),
        pl.BlockSpec((IN_WIDTH, d), whole),
        pl.BlockSpec((N_HEADS * C_QK, C_Q_LORA), whole),
        pl.BlockSpec((N_HEADS * 128, C_KV_LORA), whole),
        pl.BlockSpec((G_TOTAL, 1), whole),
        pl.BlockSpec((64, tm), lambda i, j: (0, j)),
        pl.BlockSpec((64, tm), lambda i, j: (0, j)),
        pl.BlockSpec((32, tm), lambda i, j: (0, j)),
        pl.BlockSpec((32, tm), lambda i, j: (0, j)),
    ]
    return pl.pallas_call(
        _proj_body,
        out_shape=out_shape,
        grid=(b, nt),
        in_specs=in_specs,
        out_specs=out_specs,
        scratch_shapes=[pltpu.VMEM((3, D_WIDTH // LANES, tm, LANES), F32)],
        compiler_params=_cparams(("parallel", "arbitrary")),
        name="project_heads",
    )(x, w_in_t, wqb_t, wkvb_t, gains, cosa, sina, cosm, sinm)


def _online_update(s, v_aug, carry, shift=None):
    m, acc = carry
    smax = jnp.max(s, axis=0, keepdims=True)
    if shift is None:
        m_new = jnp.maximum(m, smax)
        p = jnp.exp2(s - m_new).astype(BF16)
    else:
        m_new = jnp.maximum(m, smax + shift)
        p = jnp.exp2(s - (m_new - shift)).astype(BF16)
    alpha = jnp.exp2(m - m_new)
    acc = alpha * acc + jnp.dot(v_aug, p, preferred_element_type=F32)
    return m_new, acc


def _init_carry(dv, tq):
    return (jnp.full((1, tq), -jnp.inf, F32), jnp.zeros((dv + BF16_ROWS, tq), F32))


def _finish(carry, dv):
    _, acc = carry
    return acc[:dv] / acc[dv:dv + 1]


def _run_pipelined(n_jobs, n_blocks, scores_fn, update_fn, init_fn, finish_fn):
    steps = [(t, i) for t in range(n_jobs) for i in range(n_blocks)]
    pending = [scores_fn(*steps[n]) for n in range(min(SCORES_AHEAD, len(steps)))]
    carry = None
    for n, (t, i) in enumerate(steps):
        s = pending.pop(0)
        if n + SCORES_AHEAD < len(steps):
            pending.append(scores_fn(*steps[n + SCORES_AHEAD]))
        if i == 0:
            carry = init_fn()
        carry = update_fn(t, i, s, carry)
        if i == n_blocks - 1:
            finish_fn(t, carry)


def _dense_body(q_ref, k_ref, v_ref, o_ref):
    dq = q_ref.shape[1]
    nkb, dv, ks = v_ref.shape[1], v_ref.shape[2], v_ref.shape[3]
    tq = TQ
    ones = jnp.ones((BF16_ROWS, ks), BF16)
    zpad = jnp.zeros((LANES - dq, tq), BF16)
    qpads = [jnp.concatenate([q_ref[0, :, t * tq:(t + 1) * tq], zpad], axis=0) for t in range(q_ref.shape[2] // tq)]

    def scores(t, i):
        return jnp.dot(k_ref[0, i * ks:(i + 1) * ks, :], qpads[t], preferred_element_type=F32)

    def update(t, i, s, carry):
        return _online_update(s, jnp.concatenate([v_ref[0, i], ones], axis=0), carry)

    def finish(t, carry):
        o_ref[0, :, t * tq:(t + 1) * tq] = _finish(carry, dv).astype(o_ref.dtype)

    _run_pipelined(len(qpads), nkb, scores, update, lambda: _init_carry(dv, tq), finish)


def _dense_attention(q, k, v, dq, q_per_kv):
    b, rows, s = q.shape
    nh = rows // dq
    nkb, tk = v.shape[1], v.shape[3]
    return pl.pallas_call(
        _dense_body,
        out_shape=jax.ShapeDtypeStruct((b, nh * HEAD_DIM, s), BF16),
        grid=(b, nh, s // DENSE_TQ_STEP),
        in_specs=[
            pl.BlockSpec((1, dq, DENSE_TQ_STEP), lambda i, h, j: (i, h, j)),
            pl.BlockSpec((1, s, LANES), lambda i, h, j: (i, 0, h // q_per_kv)),
            pl.BlockSpec((1, nkb, HEAD_DIM, tk), lambda i, h, j: (i, 0, h // q_per_kv, 0)),
        ],
        out_specs=pl.BlockSpec((1, HEAD_DIM, DENSE_TQ_STEP), lambda i, h, j: (i, h, j)),
        compiler_params=_cparams(("parallel", "arbitrary", "arbitrary")),
        name="dense_attention",
    )(q, k, v)


def _diff_body(sc_ref, q_ref, k_ref, v_ref, g_ref, bias_ref, o_ref):
    tq = TQ
    n_tiles = q_ref.shape[2] // tq
    nkb, dv, ks = v_ref.shape[1], v_ref.shape[2], v_ref.shape[3]
    hh = pl.program_id(1)
    lam = sc_ref[0]
    out_scale = sc_ref[1]
    slope = sc_ref[2 + hh]
    z32 = jnp.zeros((B_QK_DIM, tq), BF16)
    z48 = jnp.zeros((48, tq), BF16)
    zfeat = jnp.zeros((BF16_ROWS, tq), BF16)
    ones = jnp.ones((BF16_ROWS, ks), BF16)
    frow = lax.broadcasted_iota(jnp.int32, (BF16_ROWS, tq), 0)
    piece = jnp.where(frow % 3 == 0, LOG2E_PIECES[0], jnp.where(frow % 3 == 1, LOG2E_PIECES[1], LOG2E_PIECES[2]))
    cfeat = jnp.where(frow < 6, piece, 0.0)
    lane_pos = lax.broadcasted_iota(jnp.int32, (1, tq), 1)

    n_mixed = tq // ks

    def tile_index(job):
        return pl.program_id(2) * n_tiles + job // 2

    def block_index(job, ii):
        return lax.rem(tile_index(job) * n_mixed + ii, nkb)

    def side(job, ii):
        return jnp.where(block_index(job, ii) < tile_index(job) * n_mixed, 1.0, -1.0).astype(F32)

    def qrows(job):
        q = q_ref[0, :, (job // 2) * tq:(job // 2 + 1) * tq]
        if job % 2 == 0:
            return jnp.concatenate([q[:B_QK_DIM], z32], axis=0)
        return jnp.concatenate([z32, q[B_QK_DIM:]], axis=0)

    def scores(job, ii):
        i = block_index(job, ii)
        k = k_ref[0, pl.ds(pl.multiple_of(i * ks, ks), ks), :]
        feat = zfeat if ii < n_mixed else (cfeat * (side(job, ii) * slope)).astype(BF16)
        return jnp.dot(k, jnp.concatenate([qrows(job), feat, z48], axis=0), preferred_element_type=F32)

    def update(job, ii, s, carry):
        v_aug = jnp.concatenate([v_ref[0, block_index(job, ii)], ones], axis=0)
        if ii < n_mixed:
            return _online_update(s - bias_ref[0, ii], v_aug, carry)
        qpos = (tile_index(job) * tq + lane_pos).astype(F32)
        return _online_update(s, v_aug, carry, qpos * (-(side(job, ii) * slope * LOG2E)))

    res = []

    def finish(job, carry):
        res.append(_finish(carry, dv))
        if job % 2 == 1:
            o = res[job - 1] - lam * res[job]
            t = job // 2
            o_ref[0, :, t * tq:(t + 1) * tq] = (_rms_rows(o, g_ref[...]) * out_scale).astype(o_ref.dtype)

    _run_pipelined(2 * n_tiles, nkb, scores, update, lambda: _init_carry(dv, tq), finish)


def _diff_attention(scal, slopes, q, k, v, sub_g):
    b, rows, s = q.shape
    nkb, tk = v.shape[1], v.shape[3]
    n_mixed = TQ // tk
    jj = jnp.arange(tk)[None, :, None] + tk * jnp.arange(n_mixed)[:, None, None]
    dist = jnp.abs(jj - jnp.arange(TQ)[None, None, :]).astype(F32)
    bias = dist[None] * (LOG2E * jnp.asarray(slopes, F32))[:, None, None, None]
    return pl.pallas_call(
        _diff_body,
        out_shape=jax.ShapeDtypeStruct((b, rows, s), BF16),
        grid=(b, N_HEADS, s // TQ_STEP),
        in_specs=[
            pl.BlockSpec(memory_space=pltpu.SMEM),
            pl.BlockSpec((1, 2 * B_QK_DIM, TQ_STEP), lambda i, h, j: (i, h, j)),
            pl.BlockSpec((1, s, LANES), lambda i, h, j: (i, 0, h)),
            pl.BlockSpec((1, nkb, HEAD_DIM, tk), lambda i, h, j: (i, 0, h, 0)),
            pl.BlockSpec((HEAD_DIM, 1), lambda i, h, j: (0, 0)),
            pl.BlockSpec((1, n_mixed, tk, TQ), lambda i, h, j: (h, 0, 0, 0)),
        ],
        out_specs=pl.BlockSpec((1, HEAD_DIM, TQ_STEP), lambda i, h, j: (i, h, j)),
        compiler_params=_cparams(("parallel", "arbitrary", "arbitrary")),
        name="diff_attention",
    )(scal, q, k, v, sub_g, bias)


def _band_body(q_ref, k_ref, v_ref, pen_ref, o_ref, lse_ref, *, qb):
    l_len = k_ref.shape[1]
    win = pen_ref.shape[3]
    lane = lax.broadcasted_iota(jnp.int32, (qb, LANES), 1)
    n_sub = q_ref.shape[1] // qb
    n_groups = q_ref.shape[2] // LANES
    for sb in range(n_sub):
        jb = pl.program_id(2) * n_sub + sb
        start = jnp.clip(jb * qb - BAND_RADIUS, 0, l_len - win)
        start = pl.multiple_of(start, BAND_RADIUS)
        variant = (jb * qb - start) // BAND_RADIUS
        q = q_ref[0, sb * qb:(sb + 1) * qb, :]
        kwin = k_ref[0, pl.ds(start, win), :]
        vwin = v_ref[0, pl.ds(start, win), :]
        for gc in range(n_groups):
            g = gc % 2
            q2 = q[:, LANES * gc:LANES * (gc + 1)].astype(F32)
            k2 = kwin[:, LANES * gc:LANES * (gc + 1)]
            v2 = vwin[:, LANES * gc:LANES * (gc + 1)]
            out_g = jnp.zeros((qb, LANES), F32)
            lse_g = jnp.zeros((qb, LANES), F32)
            for hh in range(2):
                own = (lane >= HEAD_DIM * hh) & (lane < HEAD_DIM * (hh + 1))
                qm = jnp.where(own, q2, 0.0).astype(BF16)
                s = lax.dot_general(qm, k2, (((1,), (1,)), ((), ())), preferred_element_type=F32)
                s = s - pen_ref[variant, 2 * g + hh]
                m = jnp.max(s, axis=1, keepdims=True)
                p = jnp.exp2(s - m)
                lsum = jnp.sum(p, axis=1, keepdims=True)
                o = jnp.dot(p.astype(BF16), v2, preferred_element_type=F32) / lsum
                lse = m + jnp.log2(lsum)
                out_g = jnp.where(own, o, out_g)
                lse_g = jnp.where(own, lse, lse_g)
            o_ref[0, sb * qb:(sb + 1) * qb, LANES * gc:LANES * (gc + 1)] = out_g.astype(o_ref.dtype)
            lse_ref[0, sb * qb:(sb + 1) * qb, LANES * gc:LANES * (gc + 1)] = lse_g


def _band_penalty(qb, win, dil, slopes):
    rows = jnp.arange(qb)[None, :, None] + BAND_RADIUS * jnp.arange(3)[:, None, None]
    rel = jnp.abs(rows - jnp.arange(win)[None, None, :])
    scale = jnp.asarray(slopes, F32) * (float(dil) * LOG2E)
    pen = rel.astype(F32)[:, None] * scale[None, :, None, None]
    return jnp.where((rel <= BAND_RADIUS)[:, None], pen, BAND_MASKED)


def _band_attention(q, k, v, dil, slopes):
    b, l_len, cw = q.shape
    c = cw // dil
    qb = min(BAND_Q, l_len)
    qstep = min(BAND_QSTEP, l_len)
    cps = min(dil, max(1, BAND_QSTEP // l_len))
    assert dil % cps == 0
    win = min(qb + 2 * BAND_RADIUS, l_len)
    pen = _band_penalty(qb, win, dil, slopes)
    body = functools.partial(_band_body, qb=qb)
    return pl.pallas_call(
        body,
        out_shape=(jax.ShapeDtypeStruct((b, l_len, cw), BF16), jax.ShapeDtypeStruct((b, l_len, cw), F32)),
        grid=(b, dil // cps, l_len // qstep),
        in_specs=[
            pl.BlockSpec((1, qstep, c * cps), lambda i, r, j: (i, j, r)),
            pl.BlockSpec((1, l_len, c * cps), lambda i, r, j: (i, 0, r)),
            pl.BlockSpec((1, l_len, c * cps), lambda i, r, j: (i, 0, r)),
            pl.BlockSpec(pen.shape, lambda i, r, j: (0, 0, 0, 0)),
        ],
        out_specs=(pl.BlockSpec((1, qstep, c * cps), lambda i, r, j: (i, j, r)),) * 2,
        compiler_params=_cparams(("parallel", "arbitrary", "arbitrary")),
        name="band_attention_d%d" % dil,
    )(q, k, v, pen)


def _band_combine(in_refs, scr_ref, tm, c):
    vals = []
    for idx, (_, dil) in enumerate(DILATED_CONFIGS):
        for which in range(2):
            src = in_refs[2 * idx + which]
            if dil == 1:
                vals.append(src[0].astype(F32))
                continue
            groups = []
            for g in range(c // LANES):
                slot = scr_ref.at[(2 * idx + which) * (c // LANES) + g]
                for r in range(dil):
                    lo = c * r + LANES * g
                    slot[pl.ds(r, tm // dil, stride=dil), :] = src[0, :, lo:lo + LANES].astype(F32)
                groups.append(slot[...])
            vals.append(jnp.concatenate(groups, axis=1))
    outs, lses = vals[0::2], vals[1::2]
    m = functools.reduce(jnp.maximum, lses)
    ws = [jnp.exp2(x - m) for x in lses]
    num = functools.reduce(lambda a, t: a + t, [w * o for w, o in zip(ws, outs)])
    den = functools.reduce(lambda a, t: a + t, ws)
    return num / den


def _outproj_body(*refs):
    n_d = 2 * len(DILATED_CONFIGS)
    x_ref, oa_ref, ob_ref, oc_ref = refs[:4]
    d_refs = refs[4:4 + n_d]
    w_ref, g_ref, x1_ref, h2_ref, edge_ref, scr_ref = refs[4 + n_d:]
    tm = x_ref.shape[1]
    j = pl.program_id(1)
    mix = jnp.concatenate([oa_ref[0], ob_ref[0], oc_ref[0]], axis=0)
    od = _band_combine(d_refs, scr_ref, tm, D_WIDTH).astype(BF16)
    w = w_ref[...]
    y = jnp.dot(w[:, :768], mix, preferred_element_type=F32)
    y = y + lax.dot_general(w[:, 768:], od, (((1,), (1,)), ((), ())), preferred_element_type=F32)
    x1 = x_ref[0] + y.T
    x1_ref[0] = x1
    ms = jnp.mean(x1 * x1, axis=1, keepdims=True)
    h2 = (x1 * lax.rsqrt(ms + NORM_EPS) * g_ref[...]).astype(BF16)
    h2_ref[0] = h2
    edge_ref[0, pl.ds(2 * j, 1), :] = h2[0:1].astype(F32)
    edge_ref[0, pl.ds(2 * j + 1, 1), :] = h2[tm - 1:tm].astype(F32)


def _out_project(x, oa, ob, oc, branches, w_out_t, g2):
    b, s, d = x.shape
    tm = TOK_TILE
    nt = s // tm
    fm = lambda rows: pl.BlockSpec((1, rows, tm), lambda i, j: (i, 0, j))
    tmj = pl.BlockSpec((1, tm, d), lambda i, j: (i, j, 0))
    d_specs = []
    for (_, dil) in DILATED_CONFIGS:
        d_specs += [pl.BlockSpec((1, tm // dil, dil * D_WIDTH), lambda i, j: (i, j, 0))] * 2
    flat = [a for pair in branches for a in pair]
    return pl.pallas_call(
        _outproj_body,
        out_shape=(jax.ShapeDtypeStruct((b, s, d), F32), jax.ShapeDtypeStruct((b, s, d), BF16),
                   jax.ShapeDtypeStruct((b, 2 * nt, d), F32)),
        grid=(b, nt),
        in_specs=[tmj, fm(256), fm(256), fm(256)] + d_specs + [
            pl.BlockSpec((d, d), lambda i, j: (0, 0)),
            pl.BlockSpec((1, d), lambda i, j: (0, 0))],
        out_specs=(tmj, tmj, pl.BlockSpec((1, 2 * nt, d), lambda i, j: (i, 0, 0))),
        scratch_shapes=[pltpu.VMEM((2 * len(DILATED_CONFIGS) * (D_WIDTH // LANES), tm, LANES), F32)],
        compiler_params=_cparams(("parallel", "arbitrary")),
        name="out_project",
    )(x, oa, ob, oc, *flat, w_out_t, g2)


def _halo_body(h_ref, w_ref, o_ref):
    o_ref[...] = jnp.dot(h_ref[...], w_ref[...], preferred_element_type=F32)


def _halo_up(halo_h, w_up):
    nh, d = halo_h.shape
    ff2 = w_up.shape[1]
    n_blk = 4
    cols = ff2 // n_blk
    return pl.pallas_call(
        _halo_body,
        out_shape=jax.ShapeDtypeStruct((nh, ff2), F32),
        grid=(n_blk,),
        in_specs=[pl.BlockSpec((nh, d), lambda i: (0, 0)),
                  pl.BlockSpec((d, cols), lambda i: (0, i))],
        out_specs=pl.BlockSpec((nh, cols), lambda i: (0, i)),
        compiler_params=_cparams(("arbitrary",)),
        name="mlp_halo_up",
    )(halo_h, w_up)


def _mlp_body(x1_ref, h2_ref, wup_ref, wdn_ref, par_ref, halo_ref, o_ref, gbuf_ref, vbuf_ref, *, chunks):
    tf = h2_ref.shape[1]
    ff = wdn_ref.shape[0]
    nt = pl.num_programs(1)
    row = (pl.program_id(0) * nt + pl.program_id(1)) * 2
    h = h2_ref[0]

    def conv(buf_ref, u, c0, width):
        buf_ref[MLP_PAD:MLP_PAD + tf, :width] = u
        buf_ref[MLP_PAD - 1:MLP_PAD, :width] = halo_ref[pl.ds(row, 1), c0:c0 + width]
        buf_ref[MLP_PAD + tf:MLP_PAD + tf + 1, :width] = halo_ref[pl.ds(row + 1, 1), c0:c0 + width]
        prev = buf_ref[MLP_PAD - 1:MLP_PAD - 1 + tf, :width]
        nxt = buf_ref[MLP_PAD + 1:MLP_PAD + 1 + tf, :width]
        p = par_ref[:, c0:c0 + width]
        return p[0:1] * prev + p[1:2] * u + p[2:3] * nxt + p[3:4]

    def up(c0, width):
        return (jnp.dot(h, wup_ref[:, c0:c0 + width], preferred_element_type=F32),
                jnp.dot(h, wup_ref[:, ff + c0:ff + c0 + width], preferred_element_type=F32))

    y = x1_ref[0]
    nxt_up = up(*chunks[0])
    for n, (c0, width) in enumerate(chunks):
        ug, uv = nxt_up
        if n + 1 < len(chunks):
            nxt_up = up(*chunks[n + 1])
        gate = conv(gbuf_ref, ug, c0, width)
        val = conv(vbuf_ref, uv, ff + c0, width)
        act = gate / (1.0 + jnp.exp(-gate)) * val
        y = y + jnp.dot(act.astype(BF16), wdn_ref[c0:c0 + width, :], preferred_element_type=F32)
    o_ref[0] = y


def _mlp(x1, h2, w_up, w_down, conv_par, halo_up, chunks):
    b, s, d = x1.shape
    tf = TOK_TILE
    ff2 = w_up.shape[1]
    nh = halo_up.shape[0]
    whole = lambda i, j: (0, 0)
    body = functools.partial(_mlp_body, chunks=chunks)
    const = dict(pipeline_mode=pl.Buffered(1))
    wmax = max(w for _, w in chunks)
    return pl.pallas_call(
        body,
        out_shape=jax.ShapeDtypeStruct((b, s, d), F32),
        grid=(b, s // tf),
        in_specs=[pl.BlockSpec((1, tf, d), lambda i, j: (i, j, 0)),
                  pl.BlockSpec((1, tf, d), lambda i, j: (i, j, 0)),
                  pl.BlockSpec((d, ff2), whole, **const),
                  pl.BlockSpec((ff2 // 2, d), whole, **const),
                  pl.BlockSpec((8, ff2), whole, **const),
                  pl.BlockSpec((nh, ff2), whole, **const)],
        out_specs=pl.BlockSpec((1, tf, d), lambda i, j: (i, j, 0)),
        scratch_shapes=[pltpu.VMEM((tf + 2 * MLP_PAD, wmax), F32)] * 2,
        compiler_params=_cparams(("arbitrary", "arbitrary")),
        name="gated_conv_mlp",
    )(x1, h2, w_up, w_down, conv_par, halo_up)


def _halo_rows(edges, nt):
    b, _, d = edges.shape
    ez = jnp.concatenate([edges, jnp.zeros((b, 1, d), edges.dtype)], axis=1)
    idx = []
    for j in range(nt):
        idx += [2 * j - 1 if j > 0 else 2 * nt, 2 * j + 2 if j < nt - 1 else 2 * nt]
    rows = jnp.take(ez, jnp.asarray(idx, jnp.int32), axis=1).reshape(b * nt * 2, d)
    pad = (-rows.shape[0]) % BF16_ROWS
    return jnp.pad(rows, ((0, pad), (0, 0))).astype(BF16)


def _rope_tables(s):
    def freqs(dim):
        return 1.0 / (ROPE_THETA ** (jnp.arange(0, dim, 2, dtype=F32) / dim))
    t = jnp.arange(s)
    rows = (t // GRID_W).astype(F32)
    cols = (t % GRID_W).astype(F32)
    pos = t.astype(F32)
    fa = freqs(HEAD_DIM // 2)
    ar, ac = fa[:, None] * rows[None, :], fa[:, None] * cols[None, :]
    am = freqs(C_ROPE)[:, None] * pos[None, :]
    cosa = jnp.concatenate([jnp.cos(ar), jnp.cos(ar), jnp.cos(ac), jnp.cos(ac)], axis=0)
    sina = jnp.concatenate([-jnp.sin(ar), jnp.sin(ar), -jnp.sin(ac), jnp.sin(ac)], axis=0)
    cosm = jnp.concatenate([jnp.cos(am), jnp.cos(am)], axis=0)
    sinm = jnp.concatenate([-jnp.sin(am), jnp.sin(am)], axis=0)
    return cosa, sina, cosm, sinm


def kernel(x, norm1_g, w_in, a_qn_g, a_kn_g, b_qn_g, b_kn_g, b_lam_q1, b_lam_k1, b_lam_q2, b_lam_k2, b_sub_g,
           c_qa_g, c_kva_g, c_wqb, c_wkvb, c_qn_g, c_kn_g, d_qn_g, d_kn_g, w_out, norm2_g, w_up, conv_w, conv_b,
           w_down):
    b, s, d = x.shape
    depth = w_in.shape[0]
    ff = w_down.shape[1]
    nt = s // TOK_TILE
    assert s % TOK_TILE == 0 and TOK_TILE % KSTEP == 0 and TQ % KSTEP == 0 and s % TQ == 0 and s % GRID_W == 0
    assert s <= 1 << (POS_SPLIT_BITS + BF16_MANTISSA_BITS) and s % TQ_STEP == 0 and s % DENSE_TQ_STEP == 0
    for window, dil in DILATED_CONFIGS:
        assert window // (2 * dil) == BAND_RADIUS and (s // dil) % min(BAND_Q, s // dil) == 0
        assert (s // dil) % min(BAND_QSTEP, s // dil) == 0 and BAND_QSTEP % BAND_Q == 0
        assert TOK_TILE % (dil * BF16_ROWS) == 0

    tr = lambda w: jnp.swapaxes(w, 1, 2).astype(BF16)
    gains = jnp.concatenate([norm1_g, a_qn_g, a_kn_g, b_qn_g, b_kn_g, c_qa_g, c_kva_g, c_qn_g, c_kn_g,
                             d_qn_g, d_kn_g], axis=1)[:, :, None]
    conv_par = jnp.concatenate([conv_w, conv_b[:, None, :], jnp.zeros((depth, 4, 2 * ff), F32)], axis=1)
    slopes = [2.0 ** (-8.0 * i / (2 * N_HEADS)) for i in range(1, 2 * N_HEADS + 1)]
    slopes_b, slopes_d = slopes[:N_HEADS], tuple(slopes[N_HEADS:])
    lam_init = jnp.asarray([0.8 - 0.6 * math.exp(-0.3 * l) for l in range(depth)], F32)
    lam = (jnp.exp(jnp.sum(b_lam_q1 * b_lam_k1, axis=1)) - jnp.exp(jnp.sum(b_lam_q2 * b_lam_k2, axis=1)) + lam_init)
    scal = jnp.concatenate([lam[:, None], (1.0 - lam_init)[:, None],
                            jnp.broadcast_to(jnp.asarray(slopes_b, F32), (depth, N_HEADS)),
                            jnp.zeros((depth, 2), F32)], axis=1)
    layers = dict(w_in=tr(w_in), wqb=tr(c_wqb), wkvb=tr(c_wkvb), w_out=tr(w_out),
                  w_up=w_up.astype(BF16), w_down=w_down.astype(BF16),
                  gains=gains, g2=norm2_g[:, None, :], sub_g=b_sub_g[:, :, None], conv_par=conv_par, scal=scal)
    cosa, sina, cosm, sinm = _rope_tables(s)
    assert ff % LANES == 0
    chunks = tuple((c0, min(MLP_CHUNK, ff - c0)) for c0 in range(0, ff, MLP_CHUNK))

    def layer(xs, p):
        (qa, ka, va, qb, kb, vb, qc, kc, vc, dq1, dq4, dq16, dk1, dk4, dk16, dv1, dv4, dv16) = _project(
            xs, p["w_in"], p["wqb"], p["wkvb"], p["gains"], cosa, sina, cosm, sinm)
        oa = _dense_attention(qa, ka, va, HEAD_DIM, N_HEADS // A_KV_HEADS)
        ob = _diff_attention(p["scal"], slopes_b, qb, kb, vb, p["sub_g"])
        oc = _dense_attention(qc, kc, vc, C_QK, 1)
        branches = [_band_attention(q_, k_, v_, dil, slopes_d)
                    for (_, dil), q_, k_, v_ in zip(DILATED_CONFIGS, (dq1, dq4, dq16), (dk1, dk4, dk16), (dv1, dv4, dv16))]
        x1, h2, edges = _out_project(xs, oa, ob, oc, branches, p["w_out"], p["g2"])
        halo = _halo_up(_halo_rows(edges, nt), p["w_up"])
        return _mlp(x1, h2, p["w_up"], p["w_down"], p["conv_par"], halo, chunks), None

    out, _ = layer(x, jax.tree.map(lambda a: a[0], layers))
    if depth > 1:
        out, _ = lax.scan(layer, out, jax.tree.map(lambda a: a[1:], layers))
    return out
```

```python
import functools
import math

import jax
import jax.numpy as jnp
import numpy as np
from jax import lax
from jax.experimental import pallas as pl
from jax.experimental.pallas import tpu as pltpu

F32 = jnp.float32
BF16 = jnp.bfloat16

HEAD_DIM = 64
N_HEADS = 4
A_KV_HEADS = 2
B_QK_DIM = 32
C_Q_LORA = 256
C_KV_LORA = 128
C_NOPE = 64
C_ROPE = 32
C_QK = C_NOPE + C_ROPE
DILATED_CONFIGS = ((128, 1), (512, 4), (2048, 16))
D_WIDTH = N_HEADS * HEAD_DIM
GRID_W = 64
ROPE_THETA = 10000.0
NORM_EPS = 1e-6
LOG2E = 1.4426950408889634


def _bf16_pieces(x, n):
    out = []
    for _ in range(n):
        u = np.float32(x).view(np.uint32)
        u = np.uint32((int(u) + 0x7FFF + ((int(u) >> 16) & 1)) & 0xFFFF0000)
        piece = float(u.view(np.float32))
        out.append(piece)
        x = x - piece
    return tuple(out)


LOG2E_PIECES = _bf16_pieces(LOG2E, 3)
POS_SPLIT_BITS = 6
BF16_MANTISSA_BITS = 8

_SPLITS = (256, 128, 128, 256, 256, 256, C_Q_LORA, C_KV_LORA, C_ROPE, 256, 256, 256)
_OFF = [0]
for _n in _SPLITS:
    _OFF.append(_OFF[-1] + _n)
(O_AQ, O_AK, O_AV, O_BQ, O_BK, O_BV, O_CQL, O_CKVL, O_CKR, O_DQ, O_DK, O_DV, IN_WIDTH) = _OFF

_GAIN_SIZES = (("n1", 1024), ("aq", 64), ("ak", 64), ("bq", 32), ("bk", 32), ("cqa", 256), ("ckva", 128),
               ("cqn", 96), ("ckn", 96), ("dq", 64), ("dk", 64))
G_OFF = {}
_o = 0
for _k, _n in _GAIN_SIZES:
    G_OFF[_k] = (_o, _n)
    _o += _n
G_TOTAL = _o

LANES = 128
BF16_ROWS = 16
TOK_TILE = 512
MLP_CHUNK = 1536
MLP_PAD = 8
TQ = 512
TQ_STEP = 2048
DENSE_TQ_STEP = 4096
KSTEP = 256
SCORES_AHEAD = 2
BAND_Q = 128
BAND_QSTEP = 2048
BAND_MASKED = 1e30
BAND_RADIUS = 64
V7X_VMEM_BYTES = 64 * 1024 * 1024
VMEM_LIMIT = V7X_VMEM_BYTES * 7 // 8


def _cparams(sem):
    return pltpu.CompilerParams(dimension_semantics=sem, vmem_limit_bytes=VMEM_LIMIT)


def _rms_rows(v, g_col):
    ms = jnp.mean(v * v, axis=0, keepdims=True)
    return v * lax.rsqrt(ms + NORM_EPS) * g_col


def _swap_halves(x, n):
    parts = []
    for i in range(0, x.shape[0], 2 * n):
        parts.append(x[i + n:i + 2 * n])
        parts.append(x[i:i + n])
    return jnp.concatenate(parts, axis=0)


def _rope_rows(x, cos, sin_signed, n):
    return x * cos + _swap_halves(x, n) * sin_signed


def _store_dilated(scr_ref, val, out_refs):
    tm, c = val.shape
    for g in range(c // LANES):
        scr_ref[g] = val[:, LANES * g:LANES * (g + 1)]
    for (_, dil), out in zip(DILATED_CONFIGS, out_refs):
        if dil == 1:
            out[0] = val.astype(BF16)
            continue
        n = tm // dil
        for r in range(dil):
            for g in range(c // LANES):
                lo = c * r + LANES * g
                out[0, :, lo:lo + LANES] = scr_ref[g, pl.ds(r, n, stride=dil), :].astype(BF16)


def _store_key_blocks(v_ref, v):
    for c in range(v.shape[1] // KSTEP):
        v_ref[0, c] = v[:, c * KSTEP:(c + 1) * KSTEP]


def _proj_body(x_ref, w_ref, wqb_ref, wkvb_ref, g_ref, cosa_ref, sina_ref, cosm_ref, sinm_ref,
               qa_ref, ka_ref, va_ref, qb_ref, kb_ref, vb_ref, qc_ref, kc_ref, vc_ref,
               dq1_ref, dq4_ref, dq16_ref, dk1_ref, dk4_ref, dk16_ref, dv1_ref, dv4_ref, dv16_ref,
               scr_ref):
    tm = x_ref.shape[1]

    def gain(name):
        o, n = G_OFF[name]
        return g_ref[o:o + n, :]

    x = x_ref[0].T
    h = (x * gain("n1")).astype(BF16)
    r = lax.rsqrt(jnp.mean(x * x, axis=0, keepdims=True) + NORM_EPS)

    def section(lo, hi):
        part = jnp.dot(w_ref[lo:hi, :], h, preferred_element_type=F32) * r
        return lambda o, n: part[o - lo:o - lo + n]

    cosa, sina = cosa_ref[...], sina_ref[...]
    cosm, sinm = cosm_ref[...], sinm_ref[...]
    zeros64 = jnp.zeros((64, tm), F32)
    zeros32 = jnp.zeros((32, tm), F32)

    def prep_a(proj):
        sa = HEAD_DIM ** -0.5 * LOG2E
        for hh in range(N_HEADS):
            q = _rope_rows(_rms_rows(proj(O_AQ + 64 * hh, 64), gain("aq")), cosa, sina, 16) * sa
            qa_ref[0, 64 * hh:64 * hh + 64, :] = q.astype(BF16)
        for g in range(A_KV_HEADS):
            k = _rope_rows(_rms_rows(proj(O_AK + 64 * g, 64), gain("ak")), cosa, sina, 16)
            kpad = jnp.concatenate([k, zeros64], axis=0)
            ka_ref[0, :, LANES * g:LANES * (g + 1)] = kpad.T.astype(BF16)
        _store_key_blocks(va_ref, proj(O_AV, 128).astype(BF16))

    def prep_b(proj):
        sb = B_QK_DIM ** -0.5 * LOG2E
        for i in range(2 * N_HEADS):
            qb_ref[0, 32 * i:32 * i + 32, :] = (_rms_rows(proj(O_BQ + 32 * i, 32), gain("bq")) * sb).astype(BF16)
        kpos = pl.program_id(1) * tm + lax.broadcasted_iota(jnp.int32, (BF16_ROWS, tm), 1)
        frow = lax.broadcasted_iota(jnp.int32, (BF16_ROWS, tm), 0)
        pos_hi = ((kpos >> POS_SPLIT_BITS) << POS_SPLIT_BITS).astype(F32)
        pos_lo = (kpos & ((1 << POS_SPLIT_BITS) - 1)).astype(F32)
        kfeat = jnp.where(frow < 3, pos_hi, jnp.where(frow < 6, pos_lo, 0.0))
        zeros48 = jnp.zeros((48, tm), F32)
        for hh in range(N_HEADS):
            k1 = _rms_rows(proj(O_BK + 64 * hh, 32), gain("bk"))
            k2 = _rms_rows(proj(O_BK + 64 * hh + 32, 32), gain("bk"))
            kpad = jnp.concatenate([k1, k2, kfeat, zeros48], axis=0)
            kb_ref[0, :, LANES * hh:LANES * (hh + 1)] = kpad.T.astype(BF16)
        _store_key_blocks(vb_ref, proj(O_BV, 256).astype(BF16))

    def prep_c(proj):
        sc = C_QK ** -0.5 * LOG2E
        cq = _rms_rows(proj(O_CQL, C_Q_LORA), gain("cqa")).astype(BF16)
        qc = jnp.dot(wqb_ref[...], cq, preferred_element_type=F32)
        ckv = _rms_rows(proj(O_CKVL, C_KV_LORA), gain("ckva")).astype(BF16)
        kvc = jnp.dot(wkvb_ref[...], ckv, preferred_element_type=F32)
        kr = proj(O_CKR, C_ROPE)
        vcs = []
        for hh in range(N_HEADS):
            q = _rms_rows(qc[C_QK * hh:C_QK * (hh + 1)], gain("cqn"))
            q = jnp.concatenate([q[:C_NOPE], _rope_rows(q[C_NOPE:], cosm, sinm, 16)], axis=0) * sc
            qc_ref[0, C_QK * hh:C_QK * (hh + 1), :] = q.astype(BF16)
            kk = jnp.concatenate([kvc[128 * hh:128 * hh + C_NOPE], kr], axis=0)
            kk = _rms_rows(kk, gain("ckn"))
            kk = jnp.concatenate([kk[:C_NOPE], _rope_rows(kk[C_NOPE:], cosm, sinm, 16), zeros32], axis=0)
            kc_ref[0, :, LANES * hh:LANES * (hh + 1)] = kk.T.astype(BF16)
            vcs.append(kvc[128 * hh + C_NOPE:128 * (hh + 1)])
        _store_key_blocks(vc_ref, jnp.concatenate(vcs, axis=0).astype(BF16))

    def prep_d(proj):
        sd = HEAD_DIM ** -0.5 * LOG2E
        qs, ks, vs = [], [], []
        for g in range(2):
            qg, kg = [], []
            for hh in (2 * g, 2 * g + 1):
                qg.append(_rms_rows(proj(O_DQ + 64 * hh, 64), gain("dq")) * sd)
                kg.append(_rms_rows(proj(O_DK + 64 * hh, 64), gain("dk")))
            qs.append(jnp.concatenate(qg, axis=0).T)
            ks.append(jnp.concatenate(kg, axis=0).T)
            vs.append(proj(O_DV + 128 * g, 128).T)
        _store_dilated(scr_ref.at[0], jnp.concatenate(qs, axis=1), (dq1_ref, dq4_ref, dq16_ref))
        _store_dilated(scr_ref.at[1], jnp.concatenate(ks, axis=1), (dk1_ref, dk4_ref, dk16_ref))
        _store_dilated(scr_ref.at[2], jnp.concatenate(vs, axis=1), (dv1_ref, dv4_ref, dv16_ref))

    sec_c = section(O_CQL, O_DQ)
    sec_d = section(O_DQ, IN_WIDTH)
    prep_c(sec_c)
    sec_b = section(O_BQ, O_CQL)
    prep_d(sec_d)
    sec_a = section(O_AQ, O_BQ)
    prep_b(sec_b)
    prep_a(sec_a)


def _project(x, w_in_t, wqb_t, wkvb_t, gains, cosa, sina, cosm, sinm):
    b, s, d = x.shape
    tm = TOK_TILE
    nt = s // tm
    fm = lambda rows: jax.ShapeDtypeStruct((b, rows, s), BF16)
    tmj = lambda cols: jax.ShapeDtypeStruct((b, s, cols), BF16)
    vblk = lambda rows: jax.ShapeDtypeStruct((b, s // KSTEP, rows, KSTEP), BF16)
    dviews = tuple(jax.ShapeDtypeStruct((b, s // dil, dil * D_WIDTH), BF16) for _, dil in DILATED_CONFIGS)
    out_shape = (fm(256), tmj(256), vblk(128),
                 fm(256), tmj(512), vblk(256),
                 fm(384), tmj(512), vblk(256)) + dviews * 3
    fm_spec = lambda rows: pl.BlockSpec((1, rows, tm), lambda i, j: (i, 0, j))
    tm_spec = lambda cols: pl.BlockSpec((1, tm, cols), lambda i, j: (i, j, 0))
    v_spec = lambda rows: pl.BlockSpec((1, tm // KSTEP, rows, KSTEP), lambda i, j: (i, j, 0, 0))
    dspecs = tuple(pl.BlockSpec((1, tm // dil, dil * D_WIDTH), lambda i, j: (i, j, 0)) for _, dil in DILATED_CONFIGS)
    out_specs = (fm_spec(256), tm_spec(256), v_spec(128),
                 fm_spec(256), tm_spec(512), v_spec(256),
                 fm_spec(384), tm_spec(512), v_spec(256)) + dspecs * 3
    whole = lambda i, j: (0, 0)
    in_specs = [
        pl.BlockSpec((1, tm, d), lambda i, j: (i, j, 0)),
        pl.BlockSpec((IN_WIDTH, d), whole),
        pl.BlockSpec((N_HEADS * C_QK, C_Q_LORA), whole),
        pl.BlockSpec((N_HEADS * 128, C_KV_LORA), whole),
        pl.BlockSpec((G_TOTAL, 1), whole),
        pl.BlockSpec((64, tm), lambda i, j: (0, j)),
        pl.BlockSpec((64, tm), lambda i, j: (0, j)),
        pl.BlockSpec((32, tm), lambda i, j: (0, j)),
        pl.BlockSpec((32, tm), lambda i, j: (0, j)),
    ]
    return pl.pallas_call(
        _proj_body,
        out_shape=out_shape,
        grid=(b, nt),
        in_specs=in_specs,
        out_specs=out_specs,
        scratch_shapes=[pltpu.VMEM((3, D_WIDTH // LANES, tm, LANES), F32)],
        compiler_params=_cparams(("parallel", "arbitrary")),
        name="project_heads",
    )(x, w_in_t, wqb_t, wkvb_t, gains, cosa, sina, cosm, sinm)


def _online_update(s, v_aug, carry, shift=None):
    m, acc = carry
    smax = jnp.max(s, axis=0, keepdims=True)
    if shift is None:
        m_new = jnp.maximum(m, smax)
        p = jnp.exp2(s - m_new).astype(BF16)
    else:
        m_new = jnp.maximum(m, smax + shift)
        p = jnp.exp2(s - (m_new - shift)).astype(BF16)
    alpha = jnp.exp2(m - m_new)
    acc = alpha * acc + jnp.dot(v_aug, p, preferred_element_type=F32)
    return m_new, acc


def _init_carry(dv, tq):
    return (jnp.full((1, tq), -jnp.inf, F32), jnp.zeros((dv + BF16_ROWS, tq), F32))


def _finish(carry, dv):
    _, acc = carry
    return acc[:dv] / acc[dv:dv + 1]


def _run_pipelined(n_jobs, n_blocks, scores_fn, update_fn, init_fn, finish_fn):
    steps = [(t, i) for t in range(n_jobs) for i in range(n_blocks)]
    pending = [scores_fn(*steps[n]) for n in range(min(SCORES_AHEAD, len(steps)))]
    carry = None
    for n, (t, i) in enumerate(steps):
        s = pending.pop(0)
        if n + SCORES_AHEAD < len(steps):
            pending.append(scores_fn(*steps[n + SCORES_AHEAD]))
        if i == 0:
            carry = init_fn()
        carry = update_fn(t, i, s, carry)
        if i == n_blocks - 1:
            finish_fn(t, carry)


def _dense_body(q_ref, k_ref, v_ref, o_ref):
    dq = q_ref.shape[1]
    nkb, dv, ks = v_ref.shape[1], v_ref.shape[2], v_ref.shape[3]
    tq = TQ
    ones = jnp.ones((BF16_ROWS, ks), BF16)
    zpad = jnp.zeros((LANES - dq, tq), BF16)
    qpads = [jnp.concatenate([q_ref[0, :, t * tq:(t + 1) * tq], zpad], axis=0) for t in range(q_ref.shape[2] // tq)]

    def scores(t, i):
        return jnp.dot(k_ref[0, i * ks:(i + 1) * ks, :], qpads[t], preferred_element_type=F32)

    def update(t, i, s, carry):
        return _online_update(s, jnp.concatenate([v_ref[0, i], ones], axis=0), carry)

    def finish(t, carry):
        o_ref[0, :, t * tq:(t + 1) * tq] = _finish(carry, dv).astype(o_ref.dtype)

    _run_pipelined(len(qpads), nkb, scores, update, lambda: _init_carry(dv, tq), finish)


def _dense_attention(q, k, v, dq, q_per_kv):
    b, rows, s = q.shape
    nh = rows // dq
    nkb, tk = v.shape[1], v.shape[3]
    step = min(DENSE_TQ_STEP, s)
    assert s % step == 0 and step % TQ == 0
    return pl.pallas_call(
        _dense_body,
        out_shape=jax.ShapeDtypeStruct((b, nh * HEAD_DIM, s), BF16),
        grid=(b, nh, s // step),
        in_specs=[
            pl.BlockSpec((1, dq, step), lambda i, h, j: (i, h, j)),
            pl.BlockSpec((1, s, LANES), lambda i, h, j: (i, 0, h // q_per_kv)),
            pl.BlockSpec((1, nkb, HEAD_DIM, tk), lambda i, h, j: (i, 0, h // q_per_kv, 0)),
        ],
        out_specs=pl.BlockSpec((1, HEAD_DIM, step), lambda i, h, j: (i, h, j)),
        compiler_params=_cparams(("parallel", "arbitrary", "arbitrary")),
        name="dense_attention",
    )(q, k, v)


def _diff_body(sc_ref, q_ref, k_ref, v_ref, g_ref, bias_ref, o_ref):
    tq = TQ
    n_tiles = q_ref.shape[2] // tq
    nkb, dv, ks = v_ref.shape[1], v_ref.shape[2], v_ref.shape[3]
    hh = pl.program_id(1)
    lam = sc_ref[0]
    out_scale = sc_ref[1]
    slope = sc_ref[2 + hh]
    z32 = jnp.zeros((B_QK_DIM, tq), BF16)
    z48 = jnp.zeros((48, tq), BF16)
    zfeat = jnp.zeros((BF16_ROWS, tq), BF16)
    ones = jnp.ones((BF16_ROWS, ks), BF16)
    frow = lax.broadcasted_iota(jnp.int32, (BF16_ROWS, tq), 0)
    piece = jnp.where(frow % 3 == 0, LOG2E_PIECES[0], jnp.where(frow % 3 == 1, LOG2E_PIECES[1], LOG2E_PIECES[2]))
    cfeat = jnp.where(frow < 6, piece, 0.0)
    lane_pos = lax.broadcasted_iota(jnp.int32, (1, tq), 1)

    n_mixed = tq // ks

    def tile_index(job):
        return pl.program_id(2) * n_tiles + job // 2

    def block_index(job, ii):
        return lax.rem(tile_index(job) * n_mixed + ii, nkb)

    def side(job, ii):
        return jnp.where(block_index(job, ii) < tile_index(job) * n_mixed, 1.0, -1.0).astype(F32)

    def qrows(job):
        q = q_ref[0, :, (job // 2) * tq:(job // 2 + 1) * tq]
        if job % 2 == 0:
            return jnp.concatenate([q[:B_QK_DIM], z32], axis=0)
        return jnp.concatenate([z32, q[B_QK_DIM:]], axis=0)

    def scores(job, ii):
        i = block_index(job, ii)
        k = k_ref[0, pl.ds(pl.multiple_of(i * ks, ks), ks), :]
        feat = zfeat if ii < n_mixed else (cfeat * (side(job, ii) * slope)).astype(BF16)
        return jnp.dot(k, jnp.concatenate([qrows(job), feat, z48], axis=0), preferred_element_type=F32)

    def update(job, ii, s, carry):
        v_aug = jnp.concatenate([v_ref[0, block_index(job, ii)], ones], axis=0)
        if ii < n_mixed:
            return _online_update(s - bias_ref[0, ii], v_aug, carry)
        qpos = (tile_index(job) * tq + lane_pos).astype(F32)
        return _online_update(s, v_aug, carry, qpos * (-(side(job, ii) * slope * LOG2E)))

    res = []

    def finish(job, carry):
        res.append(_finish(carry, dv))
        if job % 2 == 1:
            o = res[job - 1] - lam * res[job]
            t = job // 2
            o_ref[0, :, t * tq:(t + 1) * tq] = (_rms_rows(o, g_ref[...]) * out_scale).astype(o_ref.dtype)

    _run_pipelined(2 * n_tiles, nkb, scores, update, lambda: _init_carry(dv, tq), finish)


def _diff_attention(scal, slopes, q, k, v, sub_g):
    b, rows, s = q.shape
    nkb, tk = v.shape[1], v.shape[3]
    n_mixed = TQ // tk
    jj = jnp.arange(tk)[None, :, None] + tk * jnp.arange(n_mixed)[:, None, None]
    dist = jnp.abs(jj - jnp.arange(TQ)[None, None, :]).astype(F32)
    bias = dist[None] * (LOG2E * jnp.asarray(slopes, F32))[:, None, None, None]
    step = min(TQ_STEP, s)
    assert s % step == 0 and step % TQ == 0
    return pl.pallas_call(
        _diff_body,
        out_shape=jax.ShapeDtypeStruct((b, rows, s), BF16),
        grid=(b, N_HEADS, s // step),
        in_specs=[
            pl.BlockSpec(memory_space=pltpu.SMEM),
            pl.BlockSpec((1, 2 * B_QK_DIM, step), lambda i, h, j: (i, h, j)),
            pl.BlockSpec((1, s, LANES), lambda i, h, j: (i, 0, h)),
            pl.BlockSpec((1, nkb, HEAD_DIM, tk), lambda i, h, j: (i, 0, h, 0)),
            pl.BlockSpec((HEAD_DIM, 1), lambda i, h, j: (0, 0)),
            pl.BlockSpec((1, n_mixed, tk, TQ), lambda i, h, j: (h, 0, 0, 0)),
        ],
        out_specs=pl.BlockSpec((1, HEAD_DIM, step), lambda i, h, j: (i, h, j)),
        compiler_params=_cparams(("parallel", "arbitrary", "arbitrary")),
        name="diff_attention",
    )(scal, q, k, v, sub_g, bias)


def _band_body(q_ref, k_ref, v_ref, pen_ref, o_ref, lse_ref, *, qb):
    l_len = k_ref.shape[1]
    win = pen_ref.shape[3]
    lane = lax.broadcasted_iota(jnp.int32, (qb, LANES), 1)
    n_sub = q_ref.shape[1] // qb
    n_groups = q_ref.shape[2] // LANES
    for sb in range(n_sub):
        jb = pl.program_id(2) * n_sub + sb
        start = jnp.clip(jb * qb - BAND_RADIUS, 0, l_len - win)
        start = pl.multiple_of(start, BAND_RADIUS)
        variant = (jb * qb - start) // BAND_RADIUS
        q = q_ref[0, sb * qb:(sb + 1) * qb, :]
        kwin = k_ref[0, pl.ds(start, win), :]
        vwin = v_ref[0, pl.ds(start, win), :]
        for gc in range(n_groups):
            g = gc % 2
            q2 = q[:, LANES * gc:LANES * (gc + 1)].astype(F32)
            k2 = kwin[:, LANES * gc:LANES * (gc + 1)]
            v2 = vwin[:, LANES * gc:LANES * (gc + 1)]
            out_g = jnp.zeros((qb, LANES), F32)
            lse_g = jnp.zeros((qb, LANES), F32)
            for hh in range(2):
                own = (lane >= HEAD_DIM * hh) & (lane < HEAD_DIM * (hh + 1))
                qm = jnp.where(own, q2, 0.0).astype(BF16)
                s = lax.dot_general(qm, k2, (((1,), (1,)), ((), ())), preferred_element_type=F32)
                s = s - pen_ref[variant, 2 * g + hh]
                m = jnp.max(s, axis=1, keepdims=True)
                p = jnp.exp2(s - m)
                lsum = jnp.sum(p, axis=1, keepdims=True)
                o = jnp.dot(p.astype(BF16), v2, preferred_element_type=F32) / lsum
                lse = m + jnp.log2(lsum)
                out_g = jnp.where(own, o, out_g)
                lse_g = jnp.where(own, lse, lse_g)
            o_ref[0, sb * qb:(sb + 1) * qb, LANES * gc:LANES * (gc + 1)] = out_g.astype(o_ref.dtype)
            lse_ref[0, sb * qb:(sb + 1) * qb, LANES * gc:LANES * (gc + 1)] = lse_g


def _band_penalty(qb, win, dil, slopes):
    rows = jnp.arange(qb)[None, :, None] + BAND_RADIUS * jnp.arange(3)[:, None, None]
    rel = jnp.abs(rows - jnp.arange(win)[None, None, :])
    scale = jnp.asarray(slopes, F32) * (float(dil) * LOG2E)
    pen = rel.astype(F32)[:, None] * scale[None, :, None, None]
    return jnp.where((rel <= BAND_RADIUS)[:, None], pen, BAND_MASKED)


def _band_attention(q, k, v, dil, slopes):
    b, l_len, cw = q.shape
    c = cw // dil
    qb = min(BAND_Q, l_len)
    qstep = min(BAND_QSTEP, l_len)
    cps = min(dil, max(1, BAND_QSTEP // l_len))
    assert dil % cps == 0
    win = min(qb + 2 * BAND_RADIUS, l_len)
    pen = _band_penalty(qb, win, dil, slopes)
    body = functools.partial(_band_body, qb=qb)
    return pl.pallas_call(
        body,
        out_shape=(jax.ShapeDtypeStruct((b, l_len, cw), BF16), jax.ShapeDtypeStruct((b, l_len, cw), F32)),
        grid=(b, dil // cps, l_len // qstep),
        in_specs=[
            pl.BlockSpec((1, qstep, c * cps), lambda i, r, j: (i, j, r)),
            pl.BlockSpec((1, l_len, c * cps), lambda i, r, j: (i, 0, r)),
            pl.BlockSpec((1, l_len, c * cps), lambda i, r, j: (i, 0, r)),
            pl.BlockSpec(pen.shape, lambda i, r, j: (0, 0, 0, 0)),
        ],
        out_specs=(pl.BlockSpec((1, qstep, c * cps), lambda i, r, j: (i, j, r)),) * 2,
        compiler_params=_cparams(("parallel", "arbitrary", "arbitrary")),
        name="band_attention_d%d" % dil,
    )(q, k, v, pen)


def _band_combine(in_refs, scr_ref, tm, c):
    vals = []
    for idx, (_, dil) in enumerate(DILATED_CONFIGS):
        for which in range(2):
            src = in_refs[2 * idx + which]
            if dil == 1:
                vals.append(src[0].astype(F32))
                continue
            groups = []
            for g in range(c // LANES):
                slot = scr_ref.at[(2 * idx + which) * (c // LANES) + g]
                for r in range(dil):
                    lo = c * r + LANES * g
                    slot[pl.ds(r, tm // dil, stride=dil), :] = src[0, :, lo:lo + LANES].astype(F32)
                groups.append(slot[...])
            vals.append(jnp.concatenate(groups, axis=1))
    outs, lses = vals[0::2], vals[1::2]
    m = functools.reduce(jnp.maximum, lses)
    ws = [jnp.exp2(x - m) for x in lses]
    num = functools.reduce(lambda a, t: a + t, [w * o for w, o in zip(ws, outs)])
    den = functools.reduce(lambda a, t: a + t, ws)
    return num / den


def _outproj_body(*refs):
    n_d = 2 * len(DILATED_CONFIGS)
    x_ref, oa_ref, ob_ref, oc_ref = refs[:4]
    d_refs = refs[4:4 + n_d]
    w_ref, g_ref, x1_ref, h2_ref, edge_ref, scr_ref = refs[4 + n_d:]
    tm = x_ref.shape[1]
    j = pl.program_id(1)
    mix = jnp.concatenate([oa_ref[0], ob_ref[0], oc_ref[0]], axis=0)
    od = _band_combine(d_refs, scr_ref, tm, D_WIDTH).astype(BF16)
    w = w_ref[...]
    y = jnp.dot(w[:, :768], mix, preferred_element_type=F32)
    y = y + lax.dot_general(w[:, 768:], od, (((1,), (1,)), ((), ())), preferred_element_type=F32)
    x1 = x_ref[0] + y.T
    x1_ref[0] = x1
    ms = jnp.mean(x1 * x1, axis=1, keepdims=True)
    h2 = (x1 * lax.rsqrt(ms + NORM_EPS) * g_ref[...]).astype(BF16)
    h2_ref[0] = h2
    edge_ref[0, pl.ds(2 * j, 1), :] = h2[0:1].astype(F32)
    edge_ref[0, pl.ds(2 * j + 1, 1), :] = h2[tm - 1:tm].astype(F32)


def _out_project(x, oa, ob, oc, branches, w_out_t, g2):
    b, s, d = x.shape
    tm = TOK_TILE
    nt = s // tm
    fm = lambda rows: pl.BlockSpec((1, rows, tm), lambda i, j: (i, 0, j))
    tmj = pl.BlockSpec((1, tm, d), lambda i, j: (i, j, 0))
    d_specs = []
    for (_, dil) in DILATED_CONFIGS:
        d_specs += [pl.BlockSpec((1, tm // dil, dil * D_WIDTH), lambda i, j: (i, j, 0))] * 2
    flat = [a for pair in branches for a in pair]
    return pl.pallas_call(
        _outproj_body,
        out_shape=(jax.ShapeDtypeStruct((b, s, d), F32), jax.ShapeDtypeStruct((b, s, d), BF16),
                   jax.ShapeDtypeStruct((b, 2 * nt, d), F32)),
        grid=(b, nt),
        in_specs=[tmj, fm(256), fm(256), fm(256)] + d_specs + [
            pl.BlockSpec((d, d), lambda i, j: (0, 0)),
            pl.BlockSpec((1, d), lambda i, j: (0, 0))],
        out_specs=(tmj, tmj, pl.BlockSpec((1, 2 * nt, d), lambda i, j: (i, 0, 0))),
        scratch_shapes=[pltpu.VMEM((2 * len(DILATED_CONFIGS) * (D_WIDTH // LANES), tm, LANES), F32)],
        compiler_params=_cparams(("parallel", "arbitrary")),
        name="out_project",
    )(x, oa, ob, oc, *flat, w_out_t, g2)


def _halo_body(h_ref, w_ref, o_ref):
    o_ref[...] = jnp.dot(h_ref[...], w_ref[...], preferred_element_type=F32)


def _halo_up(halo_h, w_up):
    nh, d = halo_h.shape
    ff2 = w_up.shape[1]
    n_blk = 4
    cols = ff2 // n_blk
    return pl.pallas_call(
        _halo_body,
        out_shape=jax.ShapeDtypeStruct((nh, ff2), F32),
        grid=(n_blk,),
        in_specs=[pl.BlockSpec((nh, d), lambda i: (0, 0)),
                  pl.BlockSpec((d, cols), lambda i: (0, i))],
        out_specs=pl.BlockSpec((nh, cols), lambda i: (0, i)),
        compiler_params=_cparams(("arbitrary",)),
        name="mlp_halo_up",
    )(halo_h, w_up)


def _mlp_body(x1_ref, h2_ref, wup_ref, wdn_ref, par_ref, halo_ref, o_ref, gbuf_ref, vbuf_ref, *, chunks):
    tf = h2_ref.shape[1]
    ff = wdn_ref.shape[0]
    nt = pl.num_programs(1)
    row = (pl.program_id(0) * nt + pl.program_id(1)) * 2
    h = h2_ref[0]

    def conv(buf_ref, u, c0, width):
        buf_ref[MLP_PAD:MLP_PAD + tf, :width] = u
        buf_ref[MLP_PAD - 1:MLP_PAD, :width] = halo_ref[pl.ds(row, 1), c0:c0 + width]
        buf_ref[MLP_PAD + tf:MLP_PAD + tf + 1, :width] = halo_ref[pl.ds(row + 1, 1), c0:c0 + width]
        prev = buf_ref[MLP_PAD - 1:MLP_PAD - 1 + tf, :width]
        nxt = buf_ref[MLP_PAD + 1:MLP_PAD + 1 + tf, :width]
        p = par_ref[:, c0:c0 + width]
        return p[0:1] * prev + p[1:2] * u + p[2:3] * nxt + p[3:4]

    def up(c0, width):
        return (jnp.dot(h, wup_ref[:, c0:c0 + width], preferred_element_type=F32),
                jnp.dot(h, wup_ref[:, ff + c0:ff + c0 + width], preferred_element_type=F32))

    y = x1_ref[0]
    nxt_up = up(*chunks[0])
    for n, (c0, width) in enumerate(chunks):
        ug, uv = nxt_up
        if n + 1 < len(chunks):
            nxt_up = up(*chunks[n + 1])
        gate = conv(gbuf_ref, ug, c0, width)
        val = conv(vbuf_ref, uv, ff + c0, width)
        act = gate / (1.0 + jnp.exp(-gate)) * val
        y = y + jnp.dot(act.astype(BF16), wdn_ref[c0:c0 + width, :], preferred_element_type=F32)
    o_ref[0] = y


def _mlp(x1, h2, w_up, w_down, conv_par, halo_up, chunks):
    b, s, d = x1.shape
    tf = TOK_TILE
    ff2 = w_up.shape[1]
    nh = halo_up.shape[0]
    whole = lambda i, j: (0, 0)
    body = functools.partial(_mlp_body, chunks=chunks)
    const = dict(pipeline_mode=pl.Buffered(1))
    wmax = max(w for _, w in chunks)
    return pl.pallas_call(
        body,
        out_shape=jax.ShapeDtypeStruct((b, s, d), F32),
        grid=(b, s // tf),
        in_specs=[pl.BlockSpec((1, tf, d), lambda i, j: (i, j, 0)),
                  pl.BlockSpec((1, tf, d), lambda i, j: (i, j, 0)),
                  pl.BlockSpec((d, ff2), whole, **const),
                  pl.BlockSpec((ff2 // 2, d), whole, **const),
                  pl.BlockSpec((8, ff2), whole, **const),
                  pl.BlockSpec((nh, ff2), whole, **const)],
        out_specs=pl.BlockSpec((1, tf, d), lambda i, j: (i, j, 0)),
        scratch_shapes=[pltpu.VMEM((tf + 2 * MLP_PAD, wmax), F32)] * 2,
        compiler_params=_cparams(("arbitrary", "arbitrary")),
        name="gated_conv_mlp",
    )(x1, h2, w_up, w_down, conv_par, halo_up)


def _halo_rows(edges, nt):
    b, _, d = edges.shape
    ez = jnp.concatenate([edges, jnp.zeros((b, 1, d), edges.dtype)], axis=1)
    idx = []
    for j in range(nt):
        idx += [2 * j - 1 if j > 0 else 2 * nt, 2 * j + 2 if j < nt - 1 else 2 * nt]
    rows = jnp.take(ez, jnp.asarray(idx, jnp.int32), axis=1).reshape(b * nt * 2, d)
    pad = (-rows.shape[0]) % BF16_ROWS
    return jnp.pad(rows, ((0, pad), (0, 0))).astype(BF16)


def _rope_tables(s):
    def freqs(dim):
        return 1.0 / (ROPE_THETA ** (jnp.arange(0, dim, 2, dtype=F32) / dim))
    t = jnp.arange(s)
    rows = (t // GRID_W).astype(F32)
    cols = (t % GRID_W).astype(F32)
    pos = t.astype(F32)
    fa = freqs(HEAD_DIM // 2)
    ar, ac = fa[:, None] * rows[None, :], fa[:, None] * cols[None, :]
    am = freqs(C_ROPE)[:, None] * pos[None, :]
    cosa = jnp.concatenate([jnp.cos(ar), jnp.cos(ar), jnp.cos(ac), jnp.cos(ac)], axis=0)
    sina = jnp.concatenate([-jnp.sin(ar), jnp.sin(ar), -jnp.sin(ac), jnp.sin(ac)], axis=0)
    cosm = jnp.concatenate([jnp.cos(am), jnp.cos(am)], axis=0)
    sinm = jnp.concatenate([-jnp.sin(am), jnp.sin(am)], axis=0)
    return cosa, sina, cosm, sinm


def kernel(x, norm1_g, w_in, a_qn_g, a_kn_g, b_qn_g, b_kn_g, b_lam_q1, b_lam_k1, b_lam_q2, b_lam_k2, b_sub_g,
           c_qa_g, c_kva_g, c_wqb, c_wkvb, c_qn_g, c_kn_g, d_qn_g, d_kn_g, w_out, norm2_g, w_up, conv_w, conv_b,
           w_down):
    b, s, d = x.shape
    depth = w_in.shape[0]
    ff = w_down.shape[1]
    nt = s // TOK_TILE
    assert s % TOK_TILE == 0 and TOK_TILE % KSTEP == 0 and TQ % KSTEP == 0 and s % TQ == 0 and s % GRID_W == 0
    assert s <= 1 << (POS_SPLIT_BITS + BF16_MANTISSA_BITS)
    for window, dil in DILATED_CONFIGS:
        assert window // (2 * dil) == BAND_RADIUS and (s // dil) % min(BAND_Q, s // dil) == 0
        assert (s // dil) % min(BAND_QSTEP, s // dil) == 0 and BAND_QSTEP % BAND_Q == 0
        assert TOK_TILE % (dil * BF16_ROWS) == 0

    tr = lambda w: jnp.swapaxes(w, 1, 2).astype(BF16)
    gains = jnp.concatenate([norm1_g, a_qn_g, a_kn_g, b_qn_g, b_kn_g, c_qa_g, c_kva_g, c_qn_g, c_kn_g,
                             d_qn_g, d_kn_g], axis=1)[:, :, None]
    conv_par = jnp.concatenate([conv_w, conv_b[:, None, :], jnp.zeros((depth, 4, 2 * ff), F32)], axis=1)
    slopes = [2.0 ** (-8.0 * i / (2 * N_HEADS)) for i in range(1, 2 * N_HEADS + 1)]
    slopes_b, slopes_d = slopes[:N_HEADS], tuple(slopes[N_HEADS:])
    lam_init = jnp.asarray([0.8 - 0.6 * math.exp(-0.3 * l) for l in range(depth)], F32)
    lam = (jnp.exp(jnp.sum(b_lam_q1 * b_lam_k1, axis=1)) - jnp.exp(jnp.sum(b_lam_q2 * b_lam_k2, axis=1)) + lam_init)
    scal = jnp.concatenate([lam[:, None], (1.0 - lam_init)[:, None],
                            jnp.broadcast_to(jnp.asarray(slopes_b, F32), (depth, N_HEADS)),
                            jnp.zeros((depth, 2), F32)], axis=1)
    layers = dict(w_in=tr(w_in), wqb=tr(c_wqb), wkvb=tr(c_wkvb), w_out=tr(w_out),
                  w_up=w_up.astype(BF16), w_down=w_down.astype(BF16),
                  gains=gains, g2=norm2_g[:, None, :], sub_g=b_sub_g[:, :, None], conv_par=conv_par, scal=scal)
    cosa, sina, cosm, sinm = _rope_tables(s)
    assert ff % LANES == 0
    chunks = tuple((c0, min(MLP_CHUNK, ff - c0)) for c0 in range(0, ff, MLP_CHUNK))

    def layer(xs, p):
        (qa, ka, va, qb, kb, vb, qc, kc, vc, dq1, dq4, dq16, dk1, dk4, dk16, dv1, dv4, dv16) = _project(
            xs, p["w_in"], p["wqb"], p["wkvb"], p["gains"], cosa, sina, cosm, sinm)
        oa = _dense_attention(qa, ka, va, HEAD_DIM, N_HEADS // A_KV_HEADS)
        ob = _diff_attention(p["scal"], slopes_b, qb, kb, vb, p["sub_g"])
        oc = _dense_attention(qc, kc, vc, C_QK, 1)
        branches = [_band_attention(q_, k_, v_, dil, slopes_d)
                    for (_, dil), q_, k_, v_ in zip(DILATED_CONFIGS, (dq1, dq4, dq16), (dk1, dk4, dk16), (dv1, dv4, dv16))]
        x1, h2, edges = _out_project(xs, oa, ob, oc, branches, p["w_out"], p["g2"])
        halo = _halo_up(_halo_rows(edges, nt), p["w_up"])
        return _mlp(x1, h2, p["w_up"], p["w_down"], p["conv_par"], halo, chunks), None

    out, _ = layer(x, jax.tree.map(lambda a: a[0], layers))
    if depth > 1:
        out, _ = lax.scan(layer, out, jax.tree.map(lambda a: a[1:], layers))
    return out
```

```python
import functools
import math

import jax
import jax.numpy as jnp
import numpy as np
from jax import lax
from jax.experimental import pallas as pl
from jax.experimental.pallas import tpu as pltpu

F32 = jnp.float32
BF16 = jnp.bfloat16

HEAD_DIM = 64
N_HEADS = 4
A_KV_HEADS = 2
B_QK_DIM = 32
C_Q_LORA = 256
C_KV_LORA = 128
C_NOPE = 64
C_ROPE = 32
C_QK = C_NOPE + C_ROPE
DILATED_CONFIGS = ((128, 1), (512, 4), (2048, 16))
D_WIDTH = N_HEADS * HEAD_DIM
GRID_W = 64
ROPE_THETA = 10000.0
NORM_EPS = 1e-6
LOG2E = 1.4426950408889634


def _bf16_pieces(x, n):
    out = []
    for _ in range(n):
        u = np.float32(x).view(np.uint32)
        u = np.uint32((int(u) + 0x7FFF + ((int(u) >> 16) & 1)) & 0xFFFF0000)
        piece = float(u.view(np.float32))
        out.append(piece)
        x = x - piece
    return tuple(out)


LOG2E_PIECES = _bf16_pieces(LOG2E, 3)
POS_SPLIT_BITS = 6
BF16_MANTISSA_BITS = 8

_SPLITS = (256, 128, 128, 256, 256, 256, C_Q_LORA, C_KV_LORA, C_ROPE, 256, 256, 256)
_OFF = [0]
for _n in _SPLITS:
    _OFF.append(_OFF[-1] + _n)
(O_AQ, O_AK, O_AV, O_BQ, O_BK, O_BV, O_CQL, O_CKVL, O_CKR, O_DQ, O_DK, O_DV, IN_WIDTH) = _OFF

_GAIN_SIZES = (("n1", 1024), ("aq", 64), ("ak", 64), ("bq", 32), ("bk", 32), ("cqa", 256), ("ckva", 128),
               ("cqn", 96), ("ckn", 96), ("dq", 64), ("dk", 64))
G_OFF = {}
_o = 0
for _k, _n in _GAIN_SIZES:
    G_OFF[_k] = (_o, _n)
    _o += _n
G_TOTAL = _o

LANES = 128
BF16_ROWS = 16
TOK_TILE = 512
MLP_CHUNK = 1536
MLP_PAD = 8
TQ = 512
TQ_STEP = 2048
DENSE_TQ_STEP = 4096
KSTEP = 256
SCORES_AHEAD = 2
BAND_Q = 128
BAND_QSTEP = 2048
BAND_MASKED = 1e30
BAND_RADIUS = 64
V7X_VMEM_BYTES = 64 * 1024 * 1024
VMEM_LIMIT = V7X_VMEM_BYTES * 7 // 8


def _cparams(sem):
    return pltpu.CompilerParams(dimension_semantics=sem, vmem_limit_bytes=VMEM_LIMIT)


def _rms_rows(v, g_col):
    ms = jnp.mean(v * v, axis=0, keepdims=True)
    return v * lax.rsqrt(ms + NORM_EPS) * g_col


def _swap_halves(x, n):
    parts = []
    for i in range(0, x.shape[0], 2 * n):
        parts.append(x[i + n:i + 2 * n])
        parts.append(x[i:i + n])
    return jnp.concatenate(parts, axis=0)


def _rope_rows(x, cos, sin_signed, n):
    return x * cos + _swap_halves(x, n) * sin_signed


def _store_dilated(scr_ref, val, out_refs):
    tm, c = val.shape
    for g in range(c // LANES):
        scr_ref[g] = val[:, LANES * g:LANES * (g + 1)]
    for (_, dil), out in zip(DILATED_CONFIGS, out_refs):
        if dil == 1:
            out[0] = val.astype(BF16)
            continue
        n = tm // dil
        for r in range(dil):
            for g in range(c // LANES):
                lo = c * r + LANES * g
                out[0, :, lo:lo + LANES] = scr_ref[g, pl.ds(r, n, stride=dil), :].astype(BF16)


def _store_key_blocks(v_ref, v):
    for c in range(v.shape[1] // KSTEP):
        v_ref[0, c] = v[:, c * KSTEP:(c + 1) * KSTEP]


def _proj_body(x_ref, w_ref, wqb_ref, wkvb_ref, g_ref, cosa_ref, sina_ref, cosm_ref, sinm_ref,
               qa_ref, ka_ref, va_ref, qb_ref, kb_ref, vb_ref, qc_ref, kc_ref, vc_ref,
               dq1_ref, dq4_ref, dq16_ref, dk1_ref, dk4_ref, dk16_ref, dv1_ref, dv4_ref, dv16_ref,
               scr_ref):
    tm = x_ref.shape[1]

    def gain(name):
        o, n = G_OFF[name]
        return g_ref[o:o + n, :]

    x = x_ref[0].T
    h = (x * gain("n1")).astype(BF16)
    r = lax.rsqrt(jnp.mean(x * x, axis=0, keepdims=True) + NORM_EPS)

    def section(lo, hi):
        part = jnp.dot(w_ref[lo:hi, :], h, preferred_element_type=F32) * r
        return lambda o, n: part[o - lo:o - lo + n]

    cosa, sina = cosa_ref[...], sina_ref[...]
    cosm, sinm = cosm_ref[...], sinm_ref[...]
    zeros64 = jnp.zeros((64, tm), F32)
    zeros32 = jnp.zeros((32, tm), F32)

    def prep_a(proj):
        sa = HEAD_DIM ** -0.5 * LOG2E
        for hh in range(N_HEADS):
            q = _rope_rows(_rms_rows(proj(O_AQ + 64 * hh, 64), gain("aq")), cosa, sina, 16) * sa
            qa_ref[0, 64 * hh:64 * hh + 64, :] = q.astype(BF16)
        for g in range(A_KV_HEADS):
            k = _rope_rows(_rms_rows(proj(O_AK + 64 * g, 64), gain("ak")), cosa, sina, 16)
            kpad = jnp.concatenate([k, zeros64], axis=0)
            ka_ref[0, :, LANES * g:LANES * (g + 1)] = kpad.T.astype(BF16)
        _store_key_blocks(va_ref, proj(O_AV, 128).astype(BF16))

    def prep_b(proj):
        sb = B_QK_DIM ** -0.5 * LOG2E
        for i in range(2 * N_HEADS):
            qb_ref[0, 32 * i:32 * i + 32, :] = (_rms_rows(proj(O_BQ + 32 * i, 32), gain("bq")) * sb).astype(BF16)
        kpos = pl.program_id(1) * tm + lax.broadcasted_iota(jnp.int32, (BF16_ROWS, tm), 1)
        frow = lax.broadcasted_iota(jnp.int32, (BF16_ROWS, tm), 0)
        pos_hi = ((kpos >> POS_SPLIT_BITS) << POS_SPLIT_BITS).astype(F32)
        pos_lo = (kpos & ((1 << POS_SPLIT_BITS) - 1)).astype(F32)
        kfeat = jnp.where(frow < 3, pos_hi, jnp.where(frow < 6, pos_lo, 0.0))
        zeros48 = jnp.zeros((48, tm), F32)
        for hh in range(N_HEADS):
            k1 = _rms_rows(proj(O_BK + 64 * hh, 32), gain("bk"))
            k2 = _rms_rows(proj(O_BK + 64 * hh + 32, 32), gain("bk"))
            kpad = jnp.concatenate([k1, k2, kfeat, zeros48], axis=0)
            kb_ref[0, :, LANES * hh:LANES * (hh + 1)] = kpad.T.astype(BF16)
        _store_key_blocks(vb_ref, proj(O_BV, 256).astype(BF16))

    def prep_c(proj):
        sc = C_QK ** -0.5 * LOG2E
        cq = _rms_rows(proj(O_CQL, C_Q_LORA), gain("cqa")).astype(BF16)
        qc = jnp.dot(wqb_ref[...], cq, preferred_element_type=F32)
        ckv = _rms_rows(proj(O_CKVL, C_KV_LORA), gain("ckva")).astype(BF16)
        kvc = jnp.dot(wkvb_ref[...], ckv, preferred_element_type=F32)
        kr = proj(O_CKR, C_ROPE)
        vcs = []
        for hh in range(N_HEADS):
            q = _rms_rows(qc[C_QK * hh:C_QK * (hh + 1)], gain("cqn"))
            q = jnp.concatenate([q[:C_NOPE], _rope_rows(q[C_NOPE:], cosm, sinm, 16)], axis=0) * sc
            qc_ref[0, C_QK * hh:C_QK * (hh + 1), :] = q.astype(BF16)
            kk = jnp.concatenate([kvc[128 * hh:128 * hh + C_NOPE], kr], axis=0)
            kk = _rms_rows(kk, gain("ckn"))
            kk = jnp.concatenate([kk[:C_NOPE], _rope_rows(kk[C_NOPE:], cosm, sinm, 16), zeros32], axis=0)
            kc_ref[0, :, LANES * hh:LANES * (hh + 1)] = kk.T.astype(BF16)
            vcs.append(kvc[128 * hh + C_NOPE:128 * (hh + 1)])
        _store_key_blocks(vc_ref, jnp.concatenate(vcs, axis=0).astype(BF16))

    def prep_d(proj):
        sd = HEAD_DIM ** -0.5 * LOG2E
        qs, ks, vs = [], [], []
        for g in range(2):
            qg, kg = [], []
            for hh in (2 * g, 2 * g + 1):
                qg.append(_rms_rows(proj(O_DQ + 64 * hh, 64), gain("dq")) * sd)
                kg.append(_rms_rows(proj(O_DK + 64 * hh, 64), gain("dk")))
            qs.append(jnp.concatenate(qg, axis=0).T)
            ks.append(jnp.concatenate(kg, axis=0).T)
            vs.append(proj(O_DV + 128 * g, 128).T)
        _store_dilated(scr_ref.at[0], jnp.concatenate(qs, axis=1), (dq1_ref, dq4_ref, dq16_ref))
        _store_dilated(scr_ref.at[1], jnp.concatenate(ks, axis=1), (dk1_ref, dk4_ref, dk16_ref))
        _store_dilated(scr_ref.at[2], jnp.concatenate(vs, axis=1), (dv1_ref, dv4_ref, dv16_ref))

    sec_c = section(O_CQL, O_DQ)
    sec_d = section(O_DQ, IN_WIDTH)
    prep_c(sec_c)
    sec_b = section(O_BQ, O_CQL)
    prep_d(sec_d)
    sec_a = section(O_AQ, O_BQ)
    prep_b(sec_b)
    prep_a(sec_a)


def _project(x, layer, w_in_t, wqb_t, wkvb_t, gains, cosa, sina, cosm, sinm):
    b, s, d = x.shape
    tm = TOK_TILE
    nt = s // tm
    fm = lambda rows: jax.ShapeDtypeStruct((b, rows, s), BF16)
    tmj = lambda cols: jax.ShapeDtypeStruct((b, s, cols), BF16)
    vblk = lambda rows: jax.ShapeDtypeStruct((b, s // KSTEP, rows, KSTEP), BF16)
    dviews = tuple(jax.ShapeDtypeStruct((b, s // dil, dil * D_WIDTH), BF16) for _, dil in DILATED_CONFIGS)
    out_shape = (fm(256), tmj(256), vblk(128),
                 fm(256), tmj(512), vblk(256),
                 fm(384), tmj(512), vblk(256)) + dviews * 3
    fm_spec = lambda rows: pl.BlockSpec((1, rows, tm), lambda i, j: (i, 0, j))
    tm_spec = lambda cols: pl.BlockSpec((1, tm, cols), lambda i, j: (i, j, 0))
    v_spec = lambda rows: pl.BlockSpec((1, tm // KSTEP, rows, KSTEP), lambda i, j: (i, j, 0, 0))
    dspecs = tuple(pl.BlockSpec((1, tm // dil, dil * D_WIDTH), lambda i, j: (i, j, 0)) for _, dil in DILATED_CONFIGS)
    out_specs = (fm_spec(256), tm_spec(256), v_spec(128),
                 fm_spec(256), tm_spec(512), v_spec(256),
                 fm_spec(384), tm_spec(512), v_spec(256)) + dspecs * 3
    whole = lambda i, j: (0, 0)
    in_specs = [
        pl.BlockSpec((1, tm, d), lambda i, j: (i, j, 0)),
        pl.BlockSpec((None, IN_WIDTH, d), lambda i, j: (layer, 0, 0)),
        pl.BlockSpec((N_HEADS * C_QK, C_Q_LORA), whole),
        pl.BlockSpec((N_HEADS * 128, C_KV_LORA), whole),
        pl.BlockSpec((G_TOTAL, 1), whole),
        pl.BlockSpec((64, tm), lambda i, j: (0, j)),
        pl.BlockSpec((64, tm), lambda i, j: (0, j)),
        pl.BlockSpec((32, tm), lambda i, j: (0, j)),
        pl.BlockSpec((32, tm), lambda i, j: (0, j)),
    ]
    return pl.pallas_call(
        _proj_body,
        out_shape=out_shape,
        grid=(b, nt),
        in_specs=in_specs,
        out_specs=out_specs,
        scratch_shapes=[pltpu.VMEM((3, D_WIDTH // LANES, tm, LANES), F32)],
        compiler_params=_cparams(("parallel", "arbitrary")),
        name="project_heads",
    )(x, w_in_t, wqb_t, wkvb_t, gains, cosa, sina, cosm, sinm)


def _online_update(s, v_aug, carry, shift=None):
    m, acc = carry
    smax = jnp.max(s, axis=0, keepdims=True)
    if shift is None:
        m_new = jnp.maximum(m, smax)
        p = jnp.exp2(s - m_new).astype(BF16)
    else:
        m_new = jnp.maximum(m, smax + shift)
        p = jnp.exp2(s - (m_new - shift)).astype(BF16)
    alpha = jnp.exp2(m - m_new)
    acc = alpha * acc + jnp.dot(v_aug, p, preferred_element_type=F32)
    return m_new, acc


def _init_carry(dv, tq):
    return (jnp.full((1, tq), -jnp.inf, F32), jnp.zeros((dv + BF16_ROWS, tq), F32))


def _finish(carry, dv):
    _, acc = carry
    return acc[:dv] / acc[dv:dv + 1]


def _run_pipelined(n_jobs, n_blocks, scores_fn, update_fn, init_fn, finish_fn):
    steps = [(t, i) for t in range(n_jobs) for i in range(n_blocks)]
    pending = [scores_fn(*steps[n]) for n in range(min(SCORES_AHEAD, len(steps)))]
    carry = None
    for n, (t, i) in enumerate(steps):
        s = pending.pop(0)
        if n + SCORES_AHEAD < len(steps):
            pending.append(scores_fn(*steps[n + SCORES_AHEAD]))
        if i == 0:
            carry = init_fn()
        carry = update_fn(t, i, s, carry)
        if i == n_blocks - 1:
            finish_fn(t, carry)


def _dense_body(q_ref, k_ref, v_ref, o_ref):
    dq = q_ref.shape[1]
    nkb, dv, ks = v_ref.shape[1], v_ref.shape[2], v_ref.shape[3]
    tq = TQ
    ones = jnp.ones((BF16_ROWS, ks), BF16)
    zpad = jnp.zeros((LANES - dq, tq), BF16)
    qpads = [jnp.concatenate([q_ref[0, :, t * tq:(t + 1) * tq], zpad], axis=0) for t in range(q_ref.shape[2] // tq)]

    def scores(t, i):
        return jnp.dot(k_ref[0, i * ks:(i + 1) * ks, :], qpads[t], preferred_element_type=F32)

    def update(t, i, s, carry):
        return _online_update(s, jnp.concatenate([v_ref[0, i], ones], axis=0), carry)

    def finish(t, carry):
        o_ref[0, :, t * tq:(t + 1) * tq] = _finish(carry, dv).astype(o_ref.dtype)

    _run_pipelined(len(qpads), nkb, scores, update, lambda: _init_carry(dv, tq), finish)


def _dense_attention(q, k, v, dq, q_per_kv):
    b, rows, s = q.shape
    nh = rows // dq
    nkb, tk = v.shape[1], v.shape[3]
    step = min(DENSE_TQ_STEP, s)
    assert s % step == 0 and step % TQ == 0
    return pl.pallas_call(
        _dense_body,
        out_shape=jax.ShapeDtypeStruct((b, nh * HEAD_DIM, s), BF16),
        grid=(b, nh, s // step),
        in_specs=[
            pl.BlockSpec((1, dq, step), lambda i, h, j: (i, h, j)),
            pl.BlockSpec((1, s, LANES), lambda i, h, j: (i, 0, h // q_per_kv)),
            pl.BlockSpec((1, nkb, HEAD_DIM, tk), lambda i, h, j: (i, 0, h // q_per_kv, 0)),
        ],
        out_specs=pl.BlockSpec((1, HEAD_DIM, step), lambda i, h, j: (i, h, j)),
        compiler_params=_cparams(("parallel", "arbitrary", "arbitrary")),
        name="dense_attention",
    )(q, k, v)


def _diff_body(sc_ref, q_ref, k_ref, v_ref, g_ref, bias_ref, o_ref):
    tq = TQ
    n_tiles = q_ref.shape[2] // tq
    nkb, dv, ks = v_ref.shape[1], v_ref.shape[2], v_ref.shape[3]
    hh = pl.program_id(1)
    lam = sc_ref[0]
    out_scale = sc_ref[1]
    slope = sc_ref[2 + hh]
    z32 = jnp.zeros((B_QK_DIM, tq), BF16)
    z48 = jnp.zeros((48, tq), BF16)
    zfeat = jnp.zeros((BF16_ROWS, tq), BF16)
    ones = jnp.ones((BF16_ROWS, ks), BF16)
    frow = lax.broadcasted_iota(jnp.int32, (BF16_ROWS, tq), 0)
    piece = jnp.where(frow % 3 == 0, LOG2E_PIECES[0], jnp.where(frow % 3 == 1, LOG2E_PIECES[1], LOG2E_PIECES[2]))
    cfeat = jnp.where(frow < 6, piece, 0.0)
    lane_pos = lax.broadcasted_iota(jnp.int32, (1, tq), 1)

    n_mixed = tq // ks

    def tile_index(job):
        return pl.program_id(2) * n_tiles + job // 2

    def block_index(job, ii):
        return lax.rem(tile_index(job) * n_mixed + ii, nkb)

    def side(job, ii):
        return jnp.where(block_index(job, ii) < tile_index(job) * n_mixed, 1.0, -1.0).astype(F32)

    def qrows(job):
        q = q_ref[0, :, (job // 2) * tq:(job // 2 + 1) * tq]
        if job % 2 == 0:
            return jnp.concatenate([q[:B_QK_DIM], z32], axis=0)
        return jnp.concatenate([z32, q[B_QK_DIM:]], axis=0)

    def scores(job, ii):
        i = block_index(job, ii)
        k = k_ref[0, pl.ds(pl.multiple_of(i * ks, ks), ks), :]
        feat = zfeat if ii < n_mixed else (cfeat * (side(job, ii) * slope)).astype(BF16)
        return jnp.dot(k, jnp.concatenate([qrows(job), feat, z48], axis=0), preferred_element_type=F32)

    def update(job, ii, s, carry):
        v_aug = jnp.concatenate([v_ref[0, block_index(job, ii)], ones], axis=0)
        if ii < n_mixed:
            return _online_update(s - bias_ref[0, ii], v_aug, carry)
        qpos = (tile_index(job) * tq + lane_pos).astype(F32)
        return _online_update(s, v_aug, carry, qpos * (-(side(job, ii) * slope * LOG2E)))

    res = []

    def finish(job, carry):
        res.append(_finish(carry, dv))
        if job % 2 == 1:
            o = res[job - 1] - lam * res[job]
            t = job // 2
            o_ref[0, :, t * tq:(t + 1) * tq] = (_rms_rows(o, g_ref[...]) * out_scale).astype(o_ref.dtype)

    _run_pipelined(2 * n_tiles, nkb, scores, update, lambda: _init_carry(dv, tq), finish)


def _diff_attention(scal, slopes, q, k, v, sub_g):
    b, rows, s = q.shape
    nkb, tk = v.shape[1], v.shape[3]
    n_mixed = TQ // tk
    jj = jnp.arange(tk)[None, :, None] + tk * jnp.arange(n_mixed)[:, None, None]
    dist = jnp.abs(jj - jnp.arange(TQ)[None, None, :]).astype(F32)
    bias = dist[None] * (LOG2E * jnp.asarray(slopes, F32))[:, None, None, None]
    step = min(TQ_STEP, s)
    assert s % step == 0 and step % TQ == 0
    return pl.pallas_call(
        _diff_body,
        out_shape=jax.ShapeDtypeStruct((b, rows, s), BF16),
        grid=(b, N_HEADS, s // step),
        in_specs=[
            pl.BlockSpec(memory_space=pltpu.SMEM),
            pl.BlockSpec((1, 2 * B_QK_DIM, step), lambda i, h, j: (i, h, j)),
            pl.BlockSpec((1, s, LANES), lambda i, h, j: (i, 0, h)),
            pl.BlockSpec((1, nkb, HEAD_DIM, tk), lambda i, h, j: (i, 0, h, 0)),
            pl.BlockSpec((HEAD_DIM, 1), lambda i, h, j: (0, 0)),
            pl.BlockSpec((1, n_mixed, tk, TQ), lambda i, h, j: (h, 0, 0, 0)),
        ],
        out_specs=pl.BlockSpec((1, HEAD_DIM, step), lambda i, h, j: (i, h, j)),
        compiler_params=_cparams(("parallel", "arbitrary", "arbitrary")),
        name="diff_attention",
    )(scal, q, k, v, sub_g, bias)


def _band_body(q_ref, k_ref, v_ref, pen_ref, o_ref, lse_ref, *, qb):
    l_len = k_ref.shape[1]
    win = pen_ref.shape[3]
    lane = lax.broadcasted_iota(jnp.int32, (qb, LANES), 1)
    n_sub = q_ref.shape[1] // qb
    n_groups = q_ref.shape[2] // LANES
    for sb in range(n_sub):
        jb = pl.program_id(2) * n_sub + sb
        start = jnp.clip(jb * qb - BAND_RADIUS, 0, l_len - win)
        start = pl.multiple_of(start, BAND_RADIUS)
        variant = (jb * qb - start) // BAND_RADIUS
        q = q_ref[0, sb * qb:(sb + 1) * qb, :]
        kwin = k_ref[0, pl.ds(start, win), :]
        vwin = v_ref[0, pl.ds(start, win), :]
        for gc in range(n_groups):
            g = gc % 2
            q2 = q[:, LANES * gc:LANES * (gc + 1)].astype(F32)
            k2 = kwin[:, LANES * gc:LANES * (gc + 1)]
            v2 = vwin[:, LANES * gc:LANES * (gc + 1)]
            out_g = jnp.zeros((qb, LANES), F32)
            lse_g = jnp.zeros((qb, LANES), F32)
            for hh in range(2):
                own = (lane >= HEAD_DIM * hh) & (lane < HEAD_DIM * (hh + 1))
                qm = jnp.where(own, q2, 0.0).astype(BF16)
                s = lax.dot_general(qm, k2, (((1,), (1,)), ((), ())), preferred_element_type=F32)
                s = s - pen_ref[variant, 2 * g + hh]
                m = jnp.max(s, axis=1, keepdims=True)
                p = jnp.exp2(s - m)
                lsum = jnp.sum(p, axis=1, keepdims=True)
                o = jnp.dot(p.astype(BF16), v2, preferred_element_type=F32) / lsum
                lse = m + jnp.log2(lsum)
                out_g = jnp.where(own, o, out_g)
                lse_g = jnp.where(own, lse, lse_g)
            o_ref[0, sb * qb:(sb + 1) * qb, LANES * gc:LANES * (gc + 1)] = out_g.astype(o_ref.dtype)
            lse_ref[0, sb * qb:(sb + 1) * qb, LANES * gc:LANES * (gc + 1)] = lse_g


def _band_penalty(qb, win, dil, slopes):
    rows = jnp.arange(qb)[None, :, None] + BAND_RADIUS * jnp.arange(3)[:, None, None]
    rel = jnp.abs(rows - jnp.arange(win)[None, None, :])
    scale = jnp.asarray(slopes, F32) * (float(dil) * LOG2E)
    pen = rel.astype(F32)[:, None] * scale[None, :, None, None]
    return jnp.where((rel <= BAND_RADIUS)[:, None], pen, BAND_MASKED)


def _band_attention(q, k, v, dil, slopes):
    b, l_len, cw = q.shape
    c = cw // dil
    qb = min(BAND_Q, l_len)
    qstep = min(BAND_QSTEP, l_len)
    cps = min(dil, max(1, BAND_QSTEP // l_len))
    assert dil % cps == 0
    win = min(qb + 2 * BAND_RADIUS, l_len)
    pen = _band_penalty(qb, win, dil, slopes)
    body = functools.partial(_band_body, qb=qb)
    return pl.pallas_call(
        body,
        out_shape=(jax.ShapeDtypeStruct((b, l_len, cw), BF16), jax.ShapeDtypeStruct((b, l_len, cw), F32)),
        grid=(b, dil // cps, l_len // qstep),
        in_specs=[
            pl.BlockSpec((1, qstep, c * cps), lambda i, r, j: (i, j, r)),
            pl.BlockSpec((1, l_len, c * cps), lambda i, r, j: (i, 0, r)),
            pl.BlockSpec((1, l_len, c * cps), lambda i, r, j: (i, 0, r)),
            pl.BlockSpec(pen.shape, lambda i, r, j: (0, 0, 0, 0)),
        ],
        out_specs=(pl.BlockSpec((1, qstep, c * cps), lambda i, r, j: (i, j, r)),) * 2,
        compiler_params=_cparams(("parallel", "arbitrary", "arbitrary")),
        name="band_attention_d%d" % dil,
    )(q, k, v, pen)


def _band_combine(in_refs, scr_ref, tm, c):
    vals = []
    for idx, (_, dil) in enumerate(DILATED_CONFIGS):
        for which in range(2):
            src = in_refs[2 * idx + which]
            if dil == 1:
                vals.append(src[0].astype(F32))
                continue
            groups = []
            for g in range(c // LANES):
                slot = scr_ref.at[(2 * idx + which) * (c // LANES) + g]
                for r in range(dil):
                    lo = c * r + LANES * g
                    slot[pl.ds(r, tm // dil, stride=dil), :] = src[0, :, lo:lo + LANES].astype(F32)
                groups.append(slot[...])
            vals.append(jnp.concatenate(groups, axis=1))
    outs, lses = vals[0::2], vals[1::2]
    m = functools.reduce(jnp.maximum, lses)
    ws = [jnp.exp2(x - m) for x in lses]
    num = functools.reduce(lambda a, t: a + t, [w * o for w, o in zip(ws, outs)])
    den = functools.reduce(lambda a, t: a + t, ws)
    return num / den


def _outproj_body(*refs):
    n_d = 2 * len(DILATED_CONFIGS)
    x_ref, oa_ref, ob_ref, oc_ref = refs[:4]
    d_refs = refs[4:4 + n_d]
    w_ref, g_ref, x1_ref, h2_ref, edge_ref, scr_ref = refs[4 + n_d:]
    tm = x_ref.shape[1]
    j = pl.program_id(1)
    mix = jnp.concatenate([oa_ref[0], ob_ref[0], oc_ref[0]], axis=0)
    od = _band_combine(d_refs, scr_ref, tm, D_WIDTH).astype(BF16)
    w = w_ref[...]
    y = jnp.dot(w[:, :768], mix, preferred_element_type=F32)
    y = y + lax.dot_general(w[:, 768:], od, (((1,), (1,)), ((), ())), preferred_element_type=F32)
    x1 = x_ref[0] + y.T
    x1_ref[0] = x1
    ms = jnp.mean(x1 * x1, axis=1, keepdims=True)
    h2 = (x1 * lax.rsqrt(ms + NORM_EPS) * g_ref[...]).astype(BF16)
    h2_ref[0] = h2
    edge_ref[0, pl.ds(2 * j, 1), :] = h2[0:1].astype(F32)
    edge_ref[0, pl.ds(2 * j + 1, 1), :] = h2[tm - 1:tm].astype(F32)


def _out_project(x, oa, ob, oc, branches, layer, w_out_t, g2):
    b, s, d = x.shape
    tm = TOK_TILE
    nt = s // tm
    fm = lambda rows: pl.BlockSpec((1, rows, tm), lambda i, j: (i, 0, j))
    tmj = pl.BlockSpec((1, tm, d), lambda i, j: (i, j, 0))
    d_specs = []
    for (_, dil) in DILATED_CONFIGS:
        d_specs += [pl.BlockSpec((1, tm // dil, dil * D_WIDTH), lambda i, j: (i, j, 0))] * 2
    flat = [a for pair in branches for a in pair]
    return pl.pallas_call(
        _outproj_body,
        out_shape=(jax.ShapeDtypeStruct((b, s, d), F32), jax.ShapeDtypeStruct((b, s, d), BF16),
                   jax.ShapeDtypeStruct((b, 2 * nt, d), F32)),
        grid=(b, nt),
        in_specs=[tmj, fm(256), fm(256), fm(256)] + d_specs + [
            pl.BlockSpec((None, d, d), lambda i, j: (layer, 0, 0)),
            pl.BlockSpec((1, d), lambda i, j: (0, 0))],
        out_specs=(tmj, tmj, pl.BlockSpec((1, 2 * nt, d), lambda i, j: (i, 0, 0))),
        scratch_shapes=[pltpu.VMEM((2 * len(DILATED_CONFIGS) * (D_WIDTH // LANES), tm, LANES), F32)],
        compiler_params=_cparams(("parallel", "arbitrary")),
        name="out_project",
    )(x, oa, ob, oc, *flat, w_out_t, g2)


def _halo_body(h_ref, w_ref, o_ref):
    o_ref[...] = jnp.dot(h_ref[...], w_ref[...], preferred_element_type=F32)


def _halo_up(halo_h, layer, w_up):
    nh, d = halo_h.shape
    ff2 = w_up.shape[2]
    n_blk = 4
    cols = ff2 // n_blk
    return pl.pallas_call(
        _halo_body,
        out_shape=jax.ShapeDtypeStruct((nh, ff2), F32),
        grid=(n_blk,),
        in_specs=[pl.BlockSpec((nh, d), lambda i: (0, 0)),
                  pl.BlockSpec((None, d, cols), lambda i: (layer, 0, i))],
        out_specs=pl.BlockSpec((nh, cols), lambda i: (0, i)),
        compiler_params=_cparams(("arbitrary",)),
        name="mlp_halo_up",
    )(halo_h, w_up)


def _mlp_body(x1_ref, h2_ref, wup_ref, wdn_ref, par_ref, halo_ref, o_ref, gbuf_ref, vbuf_ref, *, chunks):
    tf = h2_ref.shape[1]
    ff = wdn_ref.shape[0]
    nt = pl.num_programs(1)
    row = (pl.program_id(0) * nt + pl.program_id(1)) * 2
    h = h2_ref[0]

    def conv(buf_ref, u, c0, width):
        buf_ref[MLP_PAD:MLP_PAD + tf, :width] = u
        buf_ref[MLP_PAD - 1:MLP_PAD, :width] = halo_ref[pl.ds(row, 1), c0:c0 + width]
        buf_ref[MLP_PAD + tf:MLP_PAD + tf + 1, :width] = halo_ref[pl.ds(row + 1, 1), c0:c0 + width]
        prev = buf_ref[MLP_PAD - 1:MLP_PAD - 1 + tf, :width]
        nxt = buf_ref[MLP_PAD + 1:MLP_PAD + 1 + tf, :width]
        p = par_ref[:, c0:c0 + width]
        return p[0:1] * prev + p[1:2] * u + p[2:3] * nxt + p[3:4]

    def up(c0, width):
        return (jnp.dot(h, wup_ref[:, c0:c0 + width], preferred_element_type=F32),
                jnp.dot(h, wup_ref[:, ff + c0:ff + c0 + width], preferred_element_type=F32))

    y = x1_ref[0]
    nxt_up = up(*chunks[0])
    for n, (c0, width) in enumerate(chunks):
        ug, uv = nxt_up
        if n + 1 < len(chunks):
            nxt_up = up(*chunks[n + 1])
        gate = conv(gbuf_ref, ug, c0, width)
        val = conv(vbuf_ref, uv, ff + c0, width)
        act = gate / (1.0 + jnp.exp(-gate)) * val
        y = y + jnp.dot(act.astype(BF16), wdn_ref[c0:c0 + width, :], preferred_element_type=F32)
    o_ref[0] = y


def _mlp(x1, h2, layer, w_up, w_down, conv_par, halo_up, chunks):
    b, s, d = x1.shape
    tf = TOK_TILE
    ff2 = w_up.shape[2]
    nh = halo_up.shape[0]
    whole = lambda i, j: (0, 0)
    body = functools.partial(_mlp_body, chunks=chunks)
    const = dict(pipeline_mode=pl.Buffered(1))
    wmax = max(w for _, w in chunks)
    return pl.pallas_call(
        body,
        out_shape=jax.ShapeDtypeStruct((b, s, d), F32),
        grid=(b, s // tf),
        in_specs=[pl.BlockSpec((1, tf, d), lambda i, j: (i, j, 0)),
                  pl.BlockSpec((1, tf, d), lambda i, j: (i, j, 0)),
                  pl.BlockSpec((None, d, ff2), lambda i, j: (layer, 0, 0), **const),
                  pl.BlockSpec((None, ff2 // 2, d), lambda i, j: (layer, 0, 0), **const),
                  pl.BlockSpec((8, ff2), whole, **const),
                  pl.BlockSpec((nh, ff2), whole, **const)],
        out_specs=pl.BlockSpec((1, tf, d), lambda i, j: (i, j, 0)),
        scratch_shapes=[pltpu.VMEM((tf + 2 * MLP_PAD, wmax), F32)] * 2,
        compiler_params=_cparams(("arbitrary", "arbitrary")),
        name="gated_conv_mlp",
    )(x1, h2, w_up, w_down, conv_par, halo_up)


def _halo_rows(edges, nt):
    b, _, d = edges.shape
    ez = jnp.concatenate([edges, jnp.zeros((b, 1, d), edges.dtype)], axis=1)
    idx = []
    for j in range(nt):
        idx += [2 * j - 1 if j > 0 else 2 * nt, 2 * j + 2 if j < nt - 1 else 2 * nt]
    rows = jnp.take(ez, jnp.asarray(idx, jnp.int32), axis=1).reshape(b * nt * 2, d)
    pad = (-rows.shape[0]) % BF16_ROWS
    return jnp.pad(rows, ((0, pad), (0, 0))).astype(BF16)


def _rope_tables(s):
    def freqs(dim):
        return 1.0 / (ROPE_THETA ** (jnp.arange(0, dim, 2, dtype=F32) / dim))
    t = jnp.arange(s)
    rows = (t // GRID_W).astype(F32)
    cols = (t % GRID_W).astype(F32)
    pos = t.astype(F32)
    fa = freqs(HEAD_DIM // 2)
    ar, ac = fa[:, None] * rows[None, :], fa[:, None] * cols[None, :]
    am = freqs(C_ROPE)[:, None] * pos[None, :]
    cosa = jnp.concatenate([jnp.cos(ar), jnp.cos(ar), jnp.cos(ac), jnp.cos(ac)], axis=0)
    sina = jnp.concatenate([-jnp.sin(ar), jnp.sin(ar), -jnp.sin(ac), jnp.sin(ac)], axis=0)
    cosm = jnp.concatenate([jnp.cos(am), jnp.cos(am)], axis=0)
    sinm = jnp.concatenate([-jnp.sin(am), jnp.sin(am)], axis=0)
    return cosa, sina, cosm, sinm


def kernel(x, norm1_g, w_in, a_qn_g, a_kn_g, b_qn_g, b_kn_g, b_lam_q1, b_lam_k1, b_lam_q2, b_lam_k2, b_sub_g,
           c_qa_g, c_kva_g, c_wqb, c_wkvb, c_qn_g, c_kn_g, d_qn_g, d_kn_g, w_out, norm2_g, w_up, conv_w, conv_b,
           w_down):
    b, s, d = x.shape
    depth = w_in.shape[0]
    ff = w_down.shape[1]
    nt = s // TOK_TILE
    assert s % TOK_TILE == 0 and TOK_TILE % KSTEP == 0 and TQ % KSTEP == 0 and s % TQ == 0 and s % GRID_W == 0
    assert s <= 1 << (POS_SPLIT_BITS + BF16_MANTISSA_BITS)
    for window, dil in DILATED_CONFIGS:
        assert window // (2 * dil) == BAND_RADIUS and (s // dil) % min(BAND_Q, s // dil) == 0
        assert (s // dil) % min(BAND_QSTEP, s // dil) == 0 and BAND_QSTEP % BAND_Q == 0
        assert TOK_TILE % (dil * BF16_ROWS) == 0

    tr = lambda w: jnp.swapaxes(w, 1, 2).astype(BF16)
    gains = jnp.concatenate([norm1_g, a_qn_g, a_kn_g, b_qn_g, b_kn_g, c_qa_g, c_kva_g, c_qn_g, c_kn_g,
                             d_qn_g, d_kn_g], axis=1)[:, :, None]
    conv_par = jnp.concatenate([conv_w, conv_b[:, None, :], jnp.zeros((depth, 4, 2 * ff), F32)], axis=1)
    slopes = [2.0 ** (-8.0 * i / (2 * N_HEADS)) for i in range(1, 2 * N_HEADS + 1)]
    slopes_b, slopes_d = slopes[:N_HEADS], tuple(slopes[N_HEADS:])
    lam_init = jnp.asarray([0.8 - 0.6 * math.exp(-0.3 * l) for l in range(depth)], F32)
    lam = (jnp.exp(jnp.sum(b_lam_q1 * b_lam_k1, axis=1)) - jnp.exp(jnp.sum(b_lam_q2 * b_lam_k2, axis=1)) + lam_init)
    scal = jnp.concatenate([lam[:, None], (1.0 - lam_init)[:, None],
                            jnp.broadcast_to(jnp.asarray(slopes_b, F32), (depth, N_HEADS)),
                            jnp.zeros((depth, 2), F32)], axis=1)
    w_in_t, w_out_t, w_up_b, w_down_b = tr(w_in), tr(w_out), w_up.astype(BF16), w_down.astype(BF16)
    wqb_t, wkvb_t = tr(c_wqb), tr(c_wkvb)
    g2, sub_g = norm2_g[:, None, :], b_sub_g[:, :, None]
    cosa, sina, cosm, sinm = _rope_tables(s)
    assert ff % LANES == 0
    chunks = tuple((c0, min(MLP_CHUNK, ff - c0)) for c0 in range(0, ff, MLP_CHUNK))

    out = x
    for l in range(depth):
        (qa, ka, va, qb, kb, vb, qc, kc, vc, dq1, dq4, dq16, dk1, dk4, dk16, dv1, dv4, dv16) = _project(
            out, l, w_in_t, wqb_t[l], wkvb_t[l], gains[l], cosa, sina, cosm, sinm)
        oa = _dense_attention(qa, ka, va, HEAD_DIM, N_HEADS // A_KV_HEADS)
        ob = _diff_attention(scal[l], slopes_b, qb, kb, vb, sub_g[l])
        oc = _dense_attention(qc, kc, vc, C_QK, 1)
        branches = [_band_attention(q_, k_, v_, dil, slopes_d)
                    for (_, dil), q_, k_, v_ in zip(DILATED_CONFIGS, (dq1, dq4, dq16), (dk1, dk4, dk16), (dv1, dv4, dv16))]
        x1, h2, edges = _out_project(out, oa, ob, oc, branches, l, w_out_t, g2[l])
        halo = _halo_up(_halo_rows(edges, nt), l, w_up_b)
        out = _mlp(x1, h2, l, w_up_b, w_down_b, conv_par[l], halo, chunks)
    return out
```

```python
import functools
import math

import jax
import jax.numpy as jnp
import numpy as np
from jax import lax
from jax.experimental import pallas as pl
from jax.experimental.pallas import tpu as pltpu

F32 = jnp.float32
BF16 = jnp.bfloat16

HEAD_DIM = 64
N_HEADS = 4
A_KV_HEADS = 2
B_QK_DIM = 32
C_Q_LORA = 256
C_KV_LORA = 128
C_NOPE = 64
C_ROPE = 32
C_QK = C_NOPE + C_ROPE
DILATED_CONFIGS = ((128, 1), (512, 4), (2048, 16))
D_WIDTH = N_HEADS * HEAD_DIM
GRID_W = 64
ROPE_THETA = 10000.0
NORM_EPS = 1e-6
LOG2E = 1.4426950408889634


def _bf16_pieces(x, n):
    out = []
    for _ in range(n):
        u = np.float32(x).view(np.uint32)
        u = np.uint32((int(u) + 0x7FFF + ((int(u) >> 16) & 1)) & 0xFFFF0000)
        piece = float(u.view(np.float32))
        out.append(piece)
        x = x - piece
    return tuple(out)


LOG2E_PIECES = _bf16_pieces(LOG2E, 3)
POS_SPLIT_BITS = 6
BF16_MANTISSA_BITS = 8

_SPLITS = (256, 128, 128, 256, 256, 256, C_Q_LORA, C_KV_LORA, C_ROPE, 256, 256, 256)
_OFF = [0]
for _n in _SPLITS:
    _OFF.append(_OFF[-1] + _n)
(O_AQ, O_AK, O_AV, O_BQ, O_BK, O_BV, O_CQL, O_CKVL, O_CKR, O_DQ, O_DK, O_DV, IN_WIDTH) = _OFF

_GAIN_SIZES = (("n1", 1024), ("aq", 64), ("ak", 64), ("bq", 32), ("bk", 32), ("cqa", 256), ("ckva", 128),
               ("cqn", 96), ("ckn", 96), ("dq", 64), ("dk", 64))
G_OFF = {}
_o = 0
for _k, _n in _GAIN_SIZES:
    G_OFF[_k] = (_o, _n)
    _o += _n
G_TOTAL = _o

LANES = 128
BF16_ROWS = 16
TOK_TILE = 512
MLP_CHUNK = 1536
MLP_PAD = 8
TQ = 512
TQ_STEP = 2048
DENSE_TQ_STEP = 4096
KSTEP = 256
SCORES_AHEAD = 2
BAND_Q = 128
BAND_QSTEP = 2048
LSE_LANES = LANES // N_HEADS
BAND_MASKED = 1e30
BAND_RADIUS = 64
V7X_VMEM_BYTES = 64 * 1024 * 1024
VMEM_LIMIT = V7X_VMEM_BYTES * 7 // 8


def _cparams(sem):
    return pltpu.CompilerParams(dimension_semantics=sem, vmem_limit_bytes=VMEM_LIMIT)


def _rms_rows(v, g_col):
    ms = jnp.mean(v * v, axis=0, keepdims=True)
    return v * lax.rsqrt(ms + NORM_EPS) * g_col


def _swap_halves(x, n):
    parts = []
    for i in range(0, x.shape[0], 2 * n):
        parts.append(x[i + n:i + 2 * n])
        parts.append(x[i:i + n])
    return jnp.concatenate(parts, axis=0)


def _rope_rows(x, cos, sin_signed, n):
    return x * cos + _swap_halves(x, n) * sin_signed


def _store_dilated(scr_ref, val, out_refs):
    tm, c = val.shape
    for g in range(c // LANES):
        scr_ref[g] = val[:, LANES * g:LANES * (g + 1)]
    for (_, dil), out in zip(DILATED_CONFIGS, out_refs):
        if dil == 1:
            out[0] = val.astype(BF16)
            continue
        n = tm // dil
        for r in range(dil):
            for g in range(c // LANES):
                lo = c * r + LANES * g
                out[0, :, lo:lo + LANES] = scr_ref[g, pl.ds(r, n, stride=dil), :].astype(BF16)


def _store_key_blocks(v_ref, v):
    for c in range(v.shape[1] // KSTEP):
        v_ref[0, c] = v[:, c * KSTEP:(c + 1) * KSTEP]


def _proj_body(x_ref, w_ref, wqb_ref, wkvb_ref, g_ref, cosa_ref, sina_ref, cosm_ref, sinm_ref,
               qa_ref, ka_ref, va_ref, qb_ref, kb_ref, vb_ref, qc_ref, kc_ref, vc_ref,
               dq1_ref, dq4_ref, dq16_ref, dk1_ref, dk4_ref, dk16_ref, dv1_ref, dv4_ref, dv16_ref,
               scr_ref):
    tm = x_ref.shape[1]

    def gain(name):
        o, n = G_OFF[name]
        return g_ref[o:o + n, :]

    x = x_ref[0].T
    h = (x * gain("n1")).astype(BF16)
    r = lax.rsqrt(jnp.mean(x * x, axis=0, keepdims=True) + NORM_EPS)

    def section(lo, hi):
        part = jnp.dot(w_ref[lo:hi, :], h, preferred_element_type=F32) * r
        return lambda o, n: part[o - lo:o - lo + n]

    cosa, sina = cosa_ref[...], sina_ref[...]
    cosm, sinm = cosm_ref[...], sinm_ref[...]
    zeros64 = jnp.zeros((64, tm), F32)
    zeros32 = jnp.zeros((32, tm), F32)

    def prep_a(proj):
        sa = HEAD_DIM ** -0.5 * LOG2E
        for hh in range(N_HEADS):
            q = _rope_rows(_rms_rows(proj(O_AQ + 64 * hh, 64), gain("aq")), cosa, sina, 16) * sa
            qa_ref[0, 64 * hh:64 * hh + 64, :] = q.astype(BF16)
        for g in range(A_KV_HEADS):
            k = _rope_rows(_rms_rows(proj(O_AK + 64 * g, 64), gain("ak")), cosa, sina, 16)
            kpad = jnp.concatenate([k, zeros64], axis=0)
            ka_ref[0, :, LANES * g:LANES * (g + 1)] = kpad.T.astype(BF16)
        _store_key_blocks(va_ref, proj(O_AV, 128).astype(BF16))

    def prep_b(proj):
        sb = B_QK_DIM ** -0.5 * LOG2E
        for i in range(2 * N_HEADS):
            qb_ref[0, 32 * i:32 * i + 32, :] = (_rms_rows(proj(O_BQ + 32 * i, 32), gain("bq")) * sb).astype(BF16)
        kpos = pl.program_id(1) * tm + lax.broadcasted_iota(jnp.int32, (BF16_ROWS, tm), 1)
        frow = lax.broadcasted_iota(jnp.int32, (BF16_ROWS, tm), 0)
        pos_hi = ((kpos >> POS_SPLIT_BITS) << POS_SPLIT_BITS).astype(F32)
        pos_lo = (kpos & ((1 << POS_SPLIT_BITS) - 1)).astype(F32)
        kfeat = jnp.where(frow < 3, pos_hi, jnp.where(frow < 6, pos_lo, 0.0))
        zeros48 = jnp.zeros((48, tm), F32)
        for hh in range(N_HEADS):
            k1 = _rms_rows(proj(O_BK + 64 * hh, 32), gain("bk"))
            k2 = _rms_rows(proj(O_BK + 64 * hh + 32, 32), gain("bk"))
            kpad = jnp.concatenate([k1, k2, kfeat, zeros48], axis=0)
            kb_ref[0, :, LANES * hh:LANES * (hh + 1)] = kpad.T.astype(BF16)
        _store_key_blocks(vb_ref, proj(O_BV, 256).astype(BF16))

    def prep_c(proj):
        sc = C_QK ** -0.5 * LOG2E
        cq = _rms_rows(proj(O_CQL, C_Q_LORA), gain("cqa")).astype(BF16)
        qc = jnp.dot(wqb_ref[...], cq, preferred_element_type=F32)
        ckv = _rms_rows(proj(O_CKVL, C_KV_LORA), gain("ckva")).astype(BF16)
        kvc = jnp.dot(wkvb_ref[...], ckv, preferred_element_type=F32)
        kr = proj(O_CKR, C_ROPE)
        vcs = []
        for hh in range(N_HEADS):
            q = _rms_rows(qc[C_QK * hh:C_QK * (hh + 1)], gain("cqn"))
            q = jnp.concatenate([q[:C_NOPE], _rope_rows(q[C_NOPE:], cosm, sinm, 16)], axis=0) * sc
            qc_ref[0, C_QK * hh:C_QK * (hh + 1), :] = q.astype(BF16)
            kk = jnp.concatenate([kvc[128 * hh:128 * hh + C_NOPE], kr], axis=0)
            kk = _rms_rows(kk, gain("ckn"))
            kk = jnp.concatenate([kk[:C_NOPE], _rope_rows(kk[C_NOPE:], cosm, sinm, 16), zeros32], axis=0)
            kc_ref[0, :, LANES * hh:LANES * (hh + 1)] = kk.T.astype(BF16)
            vcs.append(kvc[128 * hh + C_NOPE:128 * (hh + 1)])
        _store_key_blocks(vc_ref, jnp.concatenate(vcs, axis=0).astype(BF16))

    def prep_d(proj):
        sd = HEAD_DIM ** -0.5 * LOG2E
        qs, ks, vs = [], [], []
        for g in range(2):
            qg, kg = [], []
            for hh in (2 * g, 2 * g + 1):
                qg.append(_rms_rows(proj(O_DQ + 64 * hh, 64), gain("dq")) * sd)
                kg.append(_rms_rows(proj(O_DK + 64 * hh, 64), gain("dk")))
            qs.append(jnp.concatenate(qg, axis=0).T)
            ks.append(jnp.concatenate(kg, axis=0).T)
            vs.append(proj(O_DV + 128 * g, 128).T)
        _store_dilated(scr_ref.at[0], jnp.concatenate(qs, axis=1), (dq1_ref, dq4_ref, dq16_ref))
        _store_dilated(scr_ref.at[1], jnp.concatenate(ks, axis=1), (dk1_ref, dk4_ref, dk16_ref))
        _store_dilated(scr_ref.at[2], jnp.concatenate(vs, axis=1), (dv1_ref, dv4_ref, dv16_ref))

    sec_c = section(O_CQL, O_DQ)
    sec_d = section(O_DQ, IN_WIDTH)
    prep_c(sec_c)
    sec_b = section(O_BQ, O_CQL)
    prep_d(sec_d)
    sec_a = section(O_AQ, O_BQ)
    prep_b(sec_b)
    prep_a(sec_a)


def _project(x, layer, w_in_t, wqb_t, wkvb_t, gains, cosa, sina, cosm, sinm):
    b, s, d = x.shape
    tm = TOK_TILE
    nt = s // tm
    fm = lambda rows: jax.ShapeDtypeStruct((b, rows, s), BF16)
    tmj = lambda cols: jax.ShapeDtypeStruct((b, s, cols), BF16)
    vblk = lambda rows: jax.ShapeDtypeStruct((b, s // KSTEP, rows, KSTEP), BF16)
    dviews = tuple(jax.ShapeDtypeStruct((b, s // dil, dil * D_WIDTH), BF16) for _, dil in DILATED_CONFIGS)
    out_shape = (fm(256), tmj(256), vblk(128),
                 fm(256), tmj(512), vblk(256),
                 fm(384), tmj(512), vblk(256)) + dviews * 3
    fm_spec = lambda rows: pl.BlockSpec((1, rows, tm), lambda i, j: (i, 0, j))
    tm_spec = lambda cols: pl.BlockSpec((1, tm, cols), lambda i, j: (i, j, 0))
    v_spec = lambda rows: pl.BlockSpec((1, tm // KSTEP, rows, KSTEP), lambda i, j: (i, j, 0, 0))
    dspecs = tuple(pl.BlockSpec((1, tm // dil, dil * D_WIDTH), lambda i, j: (i, j, 0)) for _, dil in DILATED_CONFIGS)
    out_specs = (fm_spec(256), tm_spec(256), v_spec(128),
                 fm_spec(256), tm_spec(512), v_spec(256),
                 fm_spec(384), tm_spec(512), v_spec(256)) + dspecs * 3
    whole = lambda i, j: (0, 0)
    in_specs = [
        pl.BlockSpec((1, tm, d), lambda i, j: (i, j, 0)),
        pl.BlockSpec((None, IN_WIDTH, d), lambda i, j: (layer, 0, 0)),
        pl.BlockSpec((N_HEADS * C_QK, C_Q_LORA), whole),
        pl.BlockSpec((N_HEADS * 128, C_KV_LORA), whole),
        pl.BlockSpec((G_TOTAL, 1), whole),
        pl.BlockSpec((64, tm), lambda i, j: (0, j)),
        pl.BlockSpec((64, tm), lambda i, j: (0, j)),
        pl.BlockSpec((32, tm), lambda i, j: (0, j)),
        pl.BlockSpec((32, tm), lambda i, j: (0, j)),
    ]
    return pl.pallas_call(
        _proj_body,
        out_shape=out_shape,
        grid=(b, nt),
        in_specs=in_specs,
        out_specs=out_specs,
        scratch_shapes=[pltpu.VMEM((3, D_WIDTH // LANES, tm, LANES), F32)],
        compiler_params=_cparams(("parallel", "arbitrary")),
        name="project_heads",
    )(x, w_in_t, wqb_t, wkvb_t, gains, cosa, sina, cosm, sinm)


def _online_update(s, v_aug, carry, shift=None):
    m, acc = carry
    smax = jnp.max(s, axis=0, keepdims=True)
    if shift is None:
        m_new = jnp.maximum(m, smax)
        p = jnp.exp2(s - m_new).astype(BF16)
    else:
        m_new = jnp.maximum(m, smax + shift)
        p = jnp.exp2(s - (m_new - shift)).astype(BF16)
    alpha = jnp.exp2(m - m_new)
    acc = alpha * acc + jnp.dot(v_aug, p, preferred_element_type=F32)
    return m_new, acc


def _init_carry(dv, tq):
    return (jnp.full((1, tq), -jnp.inf, F32), jnp.zeros((dv + BF16_ROWS, tq), F32))


def _finish(carry, dv):
    _, acc = carry
    return acc[:dv] / acc[dv:dv + 1]


def _run_pipelined(n_jobs, n_blocks, scores_fn, update_fn, init_fn, finish_fn):
    steps = [(t, i) for t in range(n_jobs) for i in range(n_blocks)]
    pending = [scores_fn(*steps[n]) for n in range(min(SCORES_AHEAD, len(steps)))]
    carry = None
    for n, (t, i) in enumerate(steps):
        s = pending.pop(0)
        if n + SCORES_AHEAD < len(steps):
            pending.append(scores_fn(*steps[n + SCORES_AHEAD]))
        if i == 0:
            carry = init_fn()
        carry = update_fn(t, i, s, carry)
        if i == n_blocks - 1:
            finish_fn(t, carry)


def _dense_body(q_ref, k_ref, v_ref, o_ref):
    dq = q_ref.shape[1]
    nkb, dv, ks = v_ref.shape[1], v_ref.shape[2], v_ref.shape[3]
    tq = TQ
    ones = jnp.ones((BF16_ROWS, ks), BF16)
    zpad = jnp.zeros((LANES - dq, tq), BF16)
    qpads = [jnp.concatenate([q_ref[0, :, t * tq:(t + 1) * tq], zpad], axis=0) for t in range(q_ref.shape[2] // tq)]

    def scores(t, i):
        return jnp.dot(k_ref[0, i * ks:(i + 1) * ks, :], qpads[t], preferred_element_type=F32)

    def update(t, i, s, carry):
        return _online_update(s, jnp.concatenate([v_ref[0, i], ones], axis=0), carry)

    def finish(t, carry):
        o_ref[0, :, t * tq:(t + 1) * tq] = _finish(carry, dv).astype(o_ref.dtype)

    _run_pipelined(len(qpads), nkb, scores, update, lambda: _init_carry(dv, tq), finish)


def _dense_attention(q, k, v, dq, q_per_kv):
    b, rows, s = q.shape
    nh = rows // dq
    nkb, tk = v.shape[1], v.shape[3]
    step = min(DENSE_TQ_STEP, s)
    assert s % step == 0 and step % TQ == 0
    return pl.pallas_call(
        _dense_body,
        out_shape=jax.ShapeDtypeStruct((b, nh * HEAD_DIM, s), BF16),
        grid=(b, nh, s // step),
        in_specs=[
            pl.BlockSpec((1, dq, step), lambda i, h, j: (i, h, j)),
            pl.BlockSpec((1, s, LANES), lambda i, h, j: (i, 0, h // q_per_kv)),
            pl.BlockSpec((1, nkb, HEAD_DIM, tk), lambda i, h, j: (i, 0, h // q_per_kv, 0)),
        ],
        out_specs=pl.BlockSpec((1, HEAD_DIM, step), lambda i, h, j: (i, h, j)),
        compiler_params=_cparams(("parallel", "arbitrary", "arbitrary")),
        name="dense_attention",
    )(q, k, v)


def _diff_body(sc_ref, q_ref, k_ref, v_ref, g_ref, bias_ref, o_ref):
    tq = TQ
    n_tiles = q_ref.shape[2] // tq
    nkb, dv, ks = v_ref.shape[1], v_ref.shape[2], v_ref.shape[3]
    hh = pl.program_id(1)
    lam = sc_ref[0]
    out_scale = sc_ref[1]
    slope = sc_ref[2 + hh]
    z32 = jnp.zeros((B_QK_DIM, tq), BF16)
    z48 = jnp.zeros((48, tq), BF16)
    zfeat = jnp.zeros((BF16_ROWS, tq), BF16)
    ones = jnp.ones((BF16_ROWS, ks), BF16)
    frow = lax.broadcasted_iota(jnp.int32, (BF16_ROWS, tq), 0)
    piece = jnp.where(frow % 3 == 0, LOG2E_PIECES[0], jnp.where(frow % 3 == 1, LOG2E_PIECES[1], LOG2E_PIECES[2]))
    cfeat = jnp.where(frow < 6, piece, 0.0)
    lane_pos = lax.broadcasted_iota(jnp.int32, (1, tq), 1)

    n_mixed = tq // ks

    def tile_index(job):
        return pl.program_id(2) * n_tiles + job // 2

    def block_index(job, ii):
        return lax.rem(tile_index(job) * n_mixed + ii, nkb)

    def side(job, ii):
        return jnp.where(block_index(job, ii) < tile_index(job) * n_mixed, 1.0, -1.0).astype(F32)

    def qrows(job):
        q = q_ref[0, :, (job // 2) * tq:(job // 2 + 1) * tq]
        if job % 2 == 0:
            return jnp.concatenate([q[:B_QK_DIM], z32], axis=0)
        return jnp.concatenate([z32, q[B_QK_DIM:]], axis=0)

    def scores(job, ii):
        i = block_index(job, ii)
        k = k_ref[0, pl.ds(pl.multiple_of(i * ks, ks), ks), :]
        feat = zfeat if ii < n_mixed else (cfeat * (side(job, ii) * slope)).astype(BF16)
        return jnp.dot(k, jnp.concatenate([qrows(job), feat, z48], axis=0), preferred_element_type=F32)

    def update(job, ii, s, carry):
        v_aug = jnp.concatenate([v_ref[0, block_index(job, ii)], ones], axis=0)
        if ii < n_mixed:
            return _online_update(s - bias_ref[0, ii], v_aug, carry)
        qpos = (tile_index(job) * tq + lane_pos).astype(F32)
        return _online_update(s, v_aug, carry, qpos * (-(side(job, ii) * slope * LOG2E)))

    res = []

    def finish(job, carry):
        res.append(_finish(carry, dv))
        if job % 2 == 1:
            o = res[job - 1] - lam * res[job]
            t = job // 2
            o_ref[0, :, t * tq:(t + 1) * tq] = (_rms_rows(o, g_ref[...]) * out_scale).astype(o_ref.dtype)

    _run_pipelined(2 * n_tiles, nkb, scores, update, lambda: _init_carry(dv, tq), finish)


def _diff_attention(scal, slopes, q, k, v, sub_g):
    b, rows, s = q.shape
    nkb, tk = v.shape[1], v.shape[3]
    n_mixed = TQ // tk
    jj = jnp.arange(tk)[None, :, None] + tk * jnp.arange(n_mixed)[:, None, None]
    dist = jnp.abs(jj - jnp.arange(TQ)[None, None, :]).astype(F32)
    bias = dist[None] * (LOG2E * jnp.asarray(slopes, F32))[:, None, None, None]
    step = min(TQ_STEP, s)
    assert s % step == 0 and step % TQ == 0
    return pl.pallas_call(
        _diff_body,
        out_shape=jax.ShapeDtypeStruct((b, rows, s), BF16),
        grid=(b, N_HEADS, s // step),
        in_specs=[
            pl.BlockSpec(memory_space=pltpu.SMEM),
            pl.BlockSpec((1, 2 * B_QK_DIM, step), lambda i, h, j: (i, h, j)),
            pl.BlockSpec((1, s, LANES), lambda i, h, j: (i, 0, h)),
            pl.BlockSpec((1, nkb, HEAD_DIM, tk), lambda i, h, j: (i, 0, h, 0)),
            pl.BlockSpec((HEAD_DIM, 1), lambda i, h, j: (0, 0)),
            pl.BlockSpec((1, n_mixed, tk, TQ), lambda i, h, j: (h, 0, 0, 0)),
        ],
        out_specs=pl.BlockSpec((1, HEAD_DIM, step), lambda i, h, j: (i, h, j)),
        compiler_params=_cparams(("parallel", "arbitrary", "arbitrary")),
        name="diff_attention",
    )(scal, q, k, v, sub_g, bias)


def _band_body(q_ref, k_ref, v_ref, pen_ref, o_ref, lse_ref, *, qb):
    l_len = k_ref.shape[1]
    win = pen_ref.shape[3]
    lane = lax.broadcasted_iota(jnp.int32, (qb, LANES), 1)
    n_sub = q_ref.shape[1] // qb
    n_groups = q_ref.shape[2] // LANES
    for sb in range(n_sub):
        jb = pl.program_id(2) * n_sub + sb
        start = jnp.clip(jb * qb - BAND_RADIUS, 0, l_len - win)
        start = pl.multiple_of(start, BAND_RADIUS)
        variant = (jb * qb - start) // BAND_RADIUS
        q = q_ref[0, sb * qb:(sb + 1) * qb, :]
        kwin = k_ref[0, pl.ds(start, win), :]
        vwin = v_ref[0, pl.ds(start, win), :]
        for gc in range(n_groups):
            g = gc % 2
            q2 = q[:, LANES * gc:LANES * (gc + 1)].astype(F32)
            k2 = kwin[:, LANES * gc:LANES * (gc + 1)]
            v2 = vwin[:, LANES * gc:LANES * (gc + 1)]
            out_g = jnp.zeros((qb, LANES), F32)
            if g == 0:
                lse_c = jnp.zeros((qb, LANES), F32)
            for hh in range(2):
                own = (lane >= HEAD_DIM * hh) & (lane < HEAD_DIM * (hh + 1))
                qm = jnp.where(own, q2, 0.0).astype(BF16)
                s = lax.dot_general(qm, k2, (((1,), (1,)), ((), ())), preferred_element_type=F32)
                s = s - pen_ref[variant, 2 * g + hh]
                m = jnp.max(s, axis=1, keepdims=True)
                p = jnp.exp2(s - m)
                lsum = jnp.sum(p, axis=1, keepdims=True)
                o = jnp.dot(p.astype(BF16), v2, preferred_element_type=F32) / lsum
                lse = m + jnp.log2(lsum)
                out_g = jnp.where(own, o, out_g)
                head = 2 * g + hh
                lse_c = jnp.where((lane >= LSE_LANES * head) & (lane < LSE_LANES * (head + 1)), lse, lse_c)
            o_ref[0, sb * qb:(sb + 1) * qb, LANES * gc:LANES * (gc + 1)] = out_g.astype(o_ref.dtype)
            if g == 1:
                cl = gc // 2
                lse_ref[0, sb * qb:(sb + 1) * qb, LANES * cl:LANES * (cl + 1)] = lse_c


def _band_penalty(qb, win, dil, slopes):
    rows = jnp.arange(qb)[None, :, None] + BAND_RADIUS * jnp.arange(3)[:, None, None]
    rel = jnp.abs(rows - jnp.arange(win)[None, None, :])
    scale = jnp.asarray(slopes, F32) * (float(dil) * LOG2E)
    pen = rel.astype(F32)[:, None] * scale[None, :, None, None]
    return jnp.where((rel <= BAND_RADIUS)[:, None], pen, BAND_MASKED)


def _band_attention(q, k, v, dil, slopes):
    b, l_len, cw = q.shape
    c = cw // dil
    qb = min(BAND_Q, l_len)
    qstep = min(BAND_QSTEP, l_len)
    cps = min(dil, max(1, BAND_QSTEP // l_len))
    assert dil % cps == 0
    win = min(qb + 2 * BAND_RADIUS, l_len)
    pen = _band_penalty(qb, win, dil, slopes)
    body = functools.partial(_band_body, qb=qb)
    return pl.pallas_call(
        body,
        out_shape=(jax.ShapeDtypeStruct((b, l_len, cw), BF16), jax.ShapeDtypeStruct((b, l_len, dil * LANES), F32)),
        grid=(b, dil // cps, l_len // qstep),
        in_specs=[
            pl.BlockSpec((1, qstep, c * cps), lambda i, r, j: (i, j, r)),
            pl.BlockSpec((1, l_len, c * cps), lambda i, r, j: (i, 0, r)),
            pl.BlockSpec((1, l_len, c * cps), lambda i, r, j: (i, 0, r)),
            pl.BlockSpec(pen.shape, lambda i, r, j: (0, 0, 0, 0)),
        ],
        out_specs=(pl.BlockSpec((1, qstep, c * cps), lambda i, r, j: (i, j, r)),
                   pl.BlockSpec((1, qstep, LANES * cps), lambda i, r, j: (i, j, r))),
        compiler_params=_cparams(("parallel", "arbitrary", "arbitrary")),
        name="band_attention_d%d" % dil,
    )(q, k, v, pen)


def _band_combine(in_refs, scr_ref, tm, c):
    n_g = c // LANES
    outs, lses = [], []
    for idx, (_, dil) in enumerate(DILATED_CONFIGS):
        out_src, lse_src = in_refs[2 * idx], in_refs[2 * idx + 1]
        if dil == 1:
            outs.append(out_src[0].astype(F32))
            lses.append(lse_src[0])
            continue
        n = tm // dil
        groups = []
        for g in range(n_g):
            slot = scr_ref.at[idx * (n_g + 1) + g]
            for r in range(dil):
                lo = c * r + LANES * g
                slot[pl.ds(r, n, stride=dil), :] = out_src[0, :, lo:lo + LANES].astype(F32)
            groups.append(slot[...])
        outs.append(jnp.concatenate(groups, axis=1))
        slot = scr_ref.at[idx * (n_g + 1) + n_g]
        for r in range(dil):
            slot[pl.ds(r, n, stride=dil), :] = lse_src[0, :, LANES * r:LANES * (r + 1)]
        lses.append(slot[...])
    m = functools.reduce(jnp.maximum, lses)
    ws = [jnp.exp2(x - m) for x in lses]
    den = functools.reduce(lambda a, t: a + t, ws)
    lane = lax.broadcasted_iota(jnp.int32, (tm, LANES), 1)
    res = []
    for g in range(n_g):
        acc = None
        for w, o in zip(ws, outs):
            wn = w / den
            lo = 2 * g * LSE_LANES
            wide = jnp.where(lane < HEAD_DIM, wn[:, lo:lo + 1], wn[:, lo + LSE_LANES:lo + LSE_LANES + 1])
            term = wide * o[:, LANES * g:LANES * (g + 1)]
            acc = term if acc is None else acc + term
        res.append(acc)
    return jnp.concatenate(res, axis=1)


def _outproj_body(*refs):
    n_d = 2 * len(DILATED_CONFIGS)
    x_ref, oa_ref, ob_ref, oc_ref = refs[:4]
    d_refs = refs[4:4 + n_d]
    w_ref, g_ref, x1_ref, h2_ref, edge_ref, scr_ref = refs[4 + n_d:]
    tm = x_ref.shape[1]
    j = pl.program_id(1)
    mix = jnp.concatenate([oa_ref[0], ob_ref[0], oc_ref[0]], axis=0)
    od = _band_combine(d_refs, scr_ref, tm, D_WIDTH).astype(BF16)
    w = w_ref[...]
    y = jnp.dot(w[:, :768], mix, preferred_element_type=F32)
    y = y + lax.dot_general(w[:, 768:], od, (((1,), (1,)), ((), ())), preferred_element_type=F32)
    x1 = x_ref[0] + y.T
    x1_ref[0] = x1
    ms = jnp.mean(x1 * x1, axis=1, keepdims=True)
    h2 = (x1 * lax.rsqrt(ms + NORM_EPS) * g_ref[...]).astype(BF16)
    h2_ref[0] = h2
    edge_ref[0, pl.ds(2 * j, 1), :] = h2[0:1].astype(F32)
    edge_ref[0, pl.ds(2 * j + 1, 1), :] = h2[tm - 1:tm].astype(F32)


def _out_project(x, oa, ob, oc, branches, layer, w_out_t, g2):
    b, s, d = x.shape
    tm = TOK_TILE
    nt = s // tm
    fm = lambda rows: pl.BlockSpec((1, rows, tm), lambda i, j: (i, 0, j))
    tmj = pl.BlockSpec((1, tm, d), lambda i, j: (i, j, 0))
    d_specs = []
    for (_, dil) in DILATED_CONFIGS:
        d_specs += [pl.BlockSpec((1, tm // dil, dil * D_WIDTH), lambda i, j: (i, j, 0)),
                    pl.BlockSpec((1, tm // dil, dil * LANES), lambda i, j: (i, j, 0))]
    flat = [a for pair in branches for a in pair]
    return pl.pallas_call(
        _outproj_body,
        out_shape=(jax.ShapeDtypeStruct((b, s, d), F32), jax.ShapeDtypeStruct((b, s, d), BF16),
                   jax.ShapeDtypeStruct((b, 2 * nt, d), F32)),
        grid=(b, nt),
        in_specs=[tmj, fm(256), fm(256), fm(256)] + d_specs + [
            pl.BlockSpec((None, d, d), lambda i, j: (layer, 0, 0)),
            pl.BlockSpec((1, d), lambda i, j: (0, 0))],
        out_specs=(tmj, tmj, pl.BlockSpec((1, 2 * nt, d), lambda i, j: (i, 0, 0))),
        scratch_shapes=[pltpu.VMEM((len(DILATED_CONFIGS) * (D_WIDTH // LANES + 1), tm, LANES), F32)],
        compiler_params=_cparams(("parallel", "arbitrary")),
        name="out_project",
    )(x, oa, ob, oc, *flat, w_out_t, g2)


def _halo_body(h_ref, w_ref, o_ref):
    o_ref[...] = jnp.dot(h_ref[...], w_ref[...], preferred_element_type=F32)


def _halo_up(halo_h, layer, w_up):
    nh, d = halo_h.shape
    ff2 = w_up.shape[2]
    n_blk = 4
    cols = ff2 // n_blk
    return pl.pallas_call(
        _halo_body,
        out_shape=jax.ShapeDtypeStruct((nh, ff2), F32),
        grid=(n_blk,),
        in_specs=[pl.BlockSpec((nh, d), lambda i: (0, 0)),
                  pl.BlockSpec((None, d, cols), lambda i: (layer, 0, i))],
        out_specs=pl.BlockSpec((nh, cols), lambda i: (0, i)),
        compiler_params=_cparams(("arbitrary",)),
        name="mlp_halo_up",
    )(halo_h, w_up)


def _mlp_body(x1_ref, h2_ref, wup_ref, wdn_ref, par_ref, halo_ref, o_ref, gbuf_ref, vbuf_ref, *, chunks):
    tf = h2_ref.shape[1]
    ff = wdn_ref.shape[0]
    nt = pl.num_programs(1)
    row = (pl.program_id(0) * nt + pl.program_id(1)) * 2
    h = h2_ref[0]

    def conv(buf_ref, u, c0, width):
        buf_ref[MLP_PAD:MLP_PAD + tf, :width] = u
        buf_ref[MLP_PAD - 1:MLP_PAD, :width] = halo_ref[pl.ds(row, 1), c0:c0 + width]
        buf_ref[MLP_PAD + tf:MLP_PAD + tf + 1, :width] = halo_ref[pl.ds(row + 1, 1), c0:c0 + width]
        prev = buf_ref[MLP_PAD - 1:MLP_PAD - 1 + tf, :width]
        nxt = buf_ref[MLP_PAD + 1:MLP_PAD + 1 + tf, :width]
        p = par_ref[:, c0:c0 + width]
        return p[0:1] * prev + p[1:2] * u + p[2:3] * nxt + p[3:4]

    def up(c0, width):
        return (jnp.dot(h, wup_ref[:, c0:c0 + width], preferred_element_type=F32),
                jnp.dot(h, wup_ref[:, ff + c0:ff + c0 + width], preferred_element_type=F32))

    y = x1_ref[0]
    nxt_up = up(*chunks[0])
    for n, (c0, width) in enumerate(chunks):
        ug, uv = nxt_up
        if n + 1 < len(chunks):
            nxt_up = up(*chunks[n + 1])
        gate = conv(gbuf_ref, ug, c0, width)
        val = conv(vbuf_ref, uv, ff + c0, width)
        act = gate / (1.0 + jnp.exp(-gate)) * val
        y = y + jnp.dot(act.astype(BF16), wdn_ref[c0:c0 + width, :], preferred_element_type=F32)
    o_ref[0] = y


def _mlp(x1, h2, layer, w_up, w_down, conv_par, halo_up, chunks):
    b, s, d = x1.shape
    tf = TOK_TILE
    ff2 = w_up.shape[2]
    nh = halo_up.shape[0]
    whole = lambda i, j: (0, 0)
    body = functools.partial(_mlp_body, chunks=chunks)
    const = dict(pipeline_mode=pl.Buffered(1))
    wmax = max(w for _, w in chunks)
    return pl.pallas_call(
        body,
        out_shape=jax.ShapeDtypeStruct((b, s, d), F32),
        grid=(b, s // tf),
        in_specs=[pl.BlockSpec((1, tf, d), lambda i, j: (i, j, 0)),
                  pl.BlockSpec((1, tf, d), lambda i, j: (i, j, 0)),
                  pl.BlockSpec((None, d, ff2), lambda i, j: (layer, 0, 0), **const),
                  pl.BlockSpec((None, ff2 // 2, d), lambda i, j: (layer, 0, 0), **const),
                  pl.BlockSpec((8, ff2), whole, **const),
                  pl.BlockSpec((nh, ff2), whole, **const)],
        out_specs=pl.BlockSpec((1, tf, d), lambda i, j: (i, j, 0)),
        scratch_shapes=[pltpu.VMEM((tf + 2 * MLP_PAD, wmax), F32)] * 2,
        compiler_params=_cparams(("arbitrary", "arbitrary")),
        name="gated_conv_mlp",
    )(x1, h2, w_up, w_down, conv_par, halo_up)


def _halo_rows(edges, nt):
    b, _, d = edges.shape
    ez = jnp.concatenate([edges, jnp.zeros((b, 1, d), edges.dtype)], axis=1)
    idx = []
    for j in range(nt):
        idx += [2 * j - 1 if j > 0 else 2 * nt, 2 * j + 2 if j < nt - 1 else 2 * nt]
    rows = jnp.take(ez, jnp.asarray(idx, jnp.int32), axis=1).reshape(b * nt * 2, d)
    pad = (-rows.shape[0]) % BF16_ROWS
    return jnp.pad(rows, ((0, pad), (0, 0))).astype(BF16)


def _rope_tables(s):
    def freqs(dim):
        return 1.0 / (ROPE_THETA ** (jnp.arange(0, dim, 2, dtype=F32) / dim))
    t = jnp.arange(s)
    rows = (t // GRID_W).astype(F32)
    cols = (t % GRID_W).astype(F32)
    pos = t.astype(F32)
    fa = freqs(HEAD_DIM // 2)
    ar, ac = fa[:, None] * rows[None, :], fa[:, None] * cols[None, :]
    am = freqs(C_ROPE)[:, None] * pos[None, :]
    cosa = jnp.concatenate([jnp.cos(ar), jnp.cos(ar), jnp.cos(ac), jnp.cos(ac)], axis=0)
    sina = jnp.concatenate([-jnp.sin(ar), jnp.sin(ar), -jnp.sin(ac), jnp.sin(ac)], axis=0)
    cosm = jnp.concatenate([jnp.cos(am), jnp.cos(am)], axis=0)
    sinm = jnp.concatenate([-jnp.sin(am), jnp.sin(am)], axis=0)
    return cosa, sina, cosm, sinm


def kernel(x, norm1_g, w_in, a_qn_g, a_kn_g, b_qn_g, b_kn_g, b_lam_q1, b_lam_k1, b_lam_q2, b_lam_k2, b_sub_g,
           c_qa_g, c_kva_g, c_wqb, c_wkvb, c_qn_g, c_kn_g, d_qn_g, d_kn_g, w_out, norm2_g, w_up, conv_w, conv_b,
           w_down):
    b, s, d = x.shape
    depth = w_in.shape[0]
    ff = w_down.shape[1]
    nt = s // TOK_TILE
    assert s % TOK_TILE == 0 and TOK_TILE % KSTEP == 0 and TQ % KSTEP == 0 and s % TQ == 0 and s % GRID_W == 0
    assert s <= 1 << (POS_SPLIT_BITS + BF16_MANTISSA_BITS)
    for window, dil in DILATED_CONFIGS:
        assert window // (2 * dil) == BAND_RADIUS and (s // dil) % min(BAND_Q, s // dil) == 0
        assert (s // dil) % min(BAND_QSTEP, s // dil) == 0 and BAND_QSTEP % BAND_Q == 0
        assert TOK_TILE % (dil * BF16_ROWS) == 0

    tr = lambda w: jnp.swapaxes(w, 1, 2).astype(BF16)
    gains = jnp.concatenate([norm1_g, a_qn_g, a_kn_g, b_qn_g, b_kn_g, c_qa_g, c_kva_g, c_qn_g, c_kn_g,
                             d_qn_g, d_kn_g], axis=1)[:, :, None]
    conv_par = jnp.concatenate([conv_w, conv_b[:, None, :], jnp.zeros((depth, 4, 2 * ff), F32)], axis=1)
    slopes = [2.0 ** (-8.0 * i / (2 * N_HEADS)) for i in range(1, 2 * N_HEADS + 1)]
    slopes_b, slopes_d = slopes[:N_HEADS], tuple(slopes[N_HEADS:])
    lam_init = jnp.asarray([0.8 - 0.6 * math.exp(-0.3 * l) for l in range(depth)], F32)
    lam = (jnp.exp(jnp.sum(b_lam_q1 * b_lam_k1, axis=1)) - jnp.exp(jnp.sum(b_lam_q2 * b_lam_k2, axis=1)) + lam_init)
    scal = jnp.concatenate([lam[:, None], (1.0 - lam_init)[:, None],
                            jnp.broadcast_to(jnp.asarray(slopes_b, F32), (depth, N_HEADS)),
                            jnp.zeros((depth, 2), F32)], axis=1)
    w_in_t, w_out_t, w_up_b, w_down_b = tr(w_in), tr(w_out), w_up.astype(BF16), w_down.astype(BF16)
    wqb_t, wkvb_t = tr(c_wqb), tr(c_wkvb)
    g2, sub_g = norm2_g[:, None, :], b_sub_g[:, :, None]
    cosa, sina, cosm, sinm = _rope_tables(s)
    assert ff % LANES == 0
    chunks = tuple((c0, min(MLP_CHUNK, ff - c0)) for c0 in range(0, ff, MLP_CHUNK))

    out = x
    for l in range(depth):
        (qa, ka, va, qb, kb, vb, qc, kc, vc, dq1, dq4, dq16, dk1, dk4, dk16, dv1, dv4, dv16) = _project(
            out, l, w_in_t, wqb_t[l], wkvb_t[l], gains[l], cosa, sina, cosm, sinm)
        oa = _dense_attention(qa, ka, va, HEAD_DIM, N_HEADS // A_KV_HEADS)
        ob = _diff_attention(scal[l], slopes_b, qb, kb, vb, sub_g[l])
        oc = _dense_attention(qc, kc, vc, C_QK, 1)
        branches = [_band_attention(q_, k_, v_, dil, slopes_d)
                    for (_, dil), q_, k_, v_ in zip(DILATED_CONFIGS, (dq1, dq4, dq16), (dk1, dk4, dk16), (dv1, dv4, dv16))]
        x1, h2, edges = _out_project(out, oa, ob, oc, branches, l, w_out_t, g2[l])
        halo = _halo_up(_halo_rows(edges, nt), l, w_up_b)
        out = _mlp(x1, h2, l, w_up_b, w_down_b, conv_par[l], halo, chunks)
    return out
```

```python
import functools
import math

import jax
import jax.numpy as jnp
import numpy as np
from jax import lax
from jax.experimental import pallas as pl
from jax.experimental.pallas import tpu as pltpu

F32 = jnp.float32
BF16 = jnp.bfloat16

HEAD_DIM = 64
N_HEADS = 4
A_KV_HEADS = 2
B_QK_DIM = 32
C_Q_LORA = 256
C_KV_LORA = 128
C_NOPE = 64
C_ROPE = 32
C_QK = C_NOPE + C_ROPE
DILATED_CONFIGS = ((128, 1), (512, 4), (2048, 16))
D_WIDTH = N_HEADS * HEAD_DIM
GRID_W = 64
ROPE_THETA = 10000.0
NORM_EPS = 1e-6
LOG2E = 1.4426950408889634


def _bf16_pieces(x, n):
    out = []
    for _ in range(n):
        u = np.float32(x).view(np.uint32)
        u = np.uint32((int(u) + 0x7FFF + ((int(u) >> 16) & 1)) & 0xFFFF0000)
        piece = float(u.view(np.float32))
        out.append(piece)
        x = x - piece
    return tuple(out)


LOG2E_PIECES = _bf16_pieces(LOG2E, 3)
POS_SPLIT_BITS = 6
BF16_MANTISSA_BITS = 8

_SPLITS = (256, 128, 128, 256, 256, 256, C_Q_LORA, C_KV_LORA, C_ROPE, 256, 256, 256)
_OFF = [0]
for _n in _SPLITS:
    _OFF.append(_OFF[-1] + _n)
(O_AQ, O_AK, O_AV, O_BQ, O_BK, O_BV, O_CQL, O_CKVL, O_CKR, O_DQ, O_DK, O_DV, IN_WIDTH) = _OFF

_GAIN_SIZES = (("n1", 1024), ("aq", 64), ("ak", 64), ("bq", 32), ("bk", 32), ("cqa", 256), ("ckva", 128),
               ("cqn", 96), ("ckn", 96), ("dq", 64), ("dk", 64))
G_OFF = {}
_o = 0
for _k, _n in _GAIN_SIZES:
    G_OFF[_k] = (_o, _n)
    _o += _n
G_TOTAL = _o

LANES = 128
BF16_ROWS = 16
TOK_TILE = 512
PROJ_TILE = 1024
MLP_CHUNK = 1536
MLP_PAD = 8
TQ = 512
TQ_STEP = 2048
DENSE_TQ_STEP = 4096
KSTEP = 256
SCORES_AHEAD = 2
BAND_Q = 128
BAND_QSTEP = 2048
BAND_MASKED = 1e30
BAND_RADIUS = 64
V7X_VMEM_BYTES = 64 * 1024 * 1024
VMEM_LIMIT = V7X_VMEM_BYTES * 7 // 8


def _cparams(sem):
    return pltpu.CompilerParams(dimension_semantics=sem, vmem_limit_bytes=VMEM_LIMIT)


def _rms_rows(v, g_col):
    ms = jnp.mean(v * v, axis=0, keepdims=True)
    return v * lax.rsqrt(ms + NORM_EPS) * g_col


def _swap_halves(x, n):
    parts = []
    for i in range(0, x.shape[0], 2 * n):
        parts.append(x[i + n:i + 2 * n])
        parts.append(x[i:i + n])
    return jnp.concatenate(parts, axis=0)


def _rope_rows(x, cos, sin_signed, n):
    return x * cos + _swap_halves(x, n) * sin_signed


def _store_dilated(scr_ref, val, out_refs):
    tm, c = val.shape
    for g in range(c // LANES):
        scr_ref[g] = val[:, LANES * g:LANES * (g + 1)]
    for (_, dil), out in zip(DILATED_CONFIGS, out_refs):
        if dil == 1:
            out[0] = val.astype(BF16)
            continue
        n = tm // dil
        for r in range(dil):
            for g in range(c // LANES):
                lo = c * r + LANES * g
                out[0, :, lo:lo + LANES] = scr_ref[g, pl.ds(r, n, stride=dil), :].astype(BF16)


def _store_key_blocks(v_ref, v):
    for c in range(v.shape[1] // KSTEP):
        v_ref[0, c] = v[:, c * KSTEP:(c + 1) * KSTEP]


def _proj_body(x_ref, w_ref, wqb_ref, wkvb_ref, g_ref, cosa_ref, sina_ref, cosm_ref, sinm_ref,
               qa_ref, ka_ref, va_ref, qb_ref, kb_ref, vb_ref, qc_ref, kc_ref, vc_ref,
               dq1_ref, dq4_ref, dq16_ref, dk1_ref, dk4_ref, dk16_ref, dv1_ref, dv4_ref, dv16_ref,
               scr_ref):
    tm = x_ref.shape[1]

    def gain(name):
        o, n = G_OFF[name]
        return g_ref[o:o + n, :]

    x = x_ref[0].T
    h = (x * gain("n1")).astype(BF16)
    r = lax.rsqrt(jnp.mean(x * x, axis=0, keepdims=True) + NORM_EPS)

    def section(lo, hi):
        part = jnp.dot(w_ref[lo:hi, :], h, preferred_element_type=F32) * r
        return lambda o, n: part[o - lo:o - lo + n]

    cosa, sina = cosa_ref[...], sina_ref[...]
    cosm, sinm = cosm_ref[...], sinm_ref[...]
    zeros64 = jnp.zeros((64, tm), F32)
    zeros32 = jnp.zeros((32, tm), F32)

    def prep_a(proj):
        sa = HEAD_DIM ** -0.5 * LOG2E
        for hh in range(N_HEADS):
            q = _rope_rows(_rms_rows(proj(O_AQ + 64 * hh, 64), gain("aq")), cosa, sina, 16) * sa
            qa_ref[0, 64 * hh:64 * hh + 64, :] = q.astype(BF16)
        for g in range(A_KV_HEADS):
            k = _rope_rows(_rms_rows(proj(O_AK + 64 * g, 64), gain("ak")), cosa, sina, 16)
            kpad = jnp.concatenate([k, zeros64], axis=0)
            ka_ref[0, :, LANES * g:LANES * (g + 1)] = kpad.T.astype(BF16)
        _store_key_blocks(va_ref, proj(O_AV, 128).astype(BF16))

    def prep_b(proj):
        sb = B_QK_DIM ** -0.5 * LOG2E
        for i in range(2 * N_HEADS):
            qb_ref[0, 32 * i:32 * i + 32, :] = (_rms_rows(proj(O_BQ + 32 * i, 32), gain("bq")) * sb).astype(BF16)
        kpos = pl.program_id(1) * tm + lax.broadcasted_iota(jnp.int32, (BF16_ROWS, tm), 1)
        frow = lax.broadcasted_iota(jnp.int32, (BF16_ROWS, tm), 0)
        pos_hi = ((kpos >> POS_SPLIT_BITS) << POS_SPLIT_BITS).astype(F32)
        pos_lo = (kpos & ((1 << POS_SPLIT_BITS) - 1)).astype(F32)
        kfeat = jnp.where(frow < 3, pos_hi, jnp.where(frow < 6, pos_lo, 0.0))
        zeros48 = jnp.zeros((48, tm), F32)
        for hh in range(N_HEADS):
            k1 = _rms_rows(proj(O_BK + 64 * hh, 32), gain("bk"))
            k2 = _rms_rows(proj(O_BK + 64 * hh + 32, 32), gain("bk"))
            kpad = jnp.concatenate([k1, k2, kfeat, zeros48], axis=0)
            kb_ref[0, :, LANES * hh:LANES * (hh + 1)] = kpad.T.astype(BF16)
        _store_key_blocks(vb_ref, proj(O_BV, 256).astype(BF16))

    def prep_c(proj):
        sc = C_QK ** -0.5 * LOG2E
        cq = _rms_rows(proj(O_CQL, C_Q_LORA), gain("cqa")).astype(BF16)
        qc = jnp.dot(wqb_ref[...], cq, preferred_element_type=F32)
        ckv = _rms_rows(proj(O_CKVL, C_KV_LORA), gain("ckva")).astype(BF16)
        kvc = jnp.dot(wkvb_ref[...], ckv, preferred_element_type=F32)
        kr = proj(O_CKR, C_ROPE)
        vcs = []
        for hh in range(N_HEADS):
            q = _rms_rows(qc[C_QK * hh:C_QK * (hh + 1)], gain("cqn"))
            q = jnp.concatenate([q[:C_NOPE], _rope_rows(q[C_NOPE:], cosm, sinm, 16)], axis=0) * sc
            qc_ref[0, C_QK * hh:C_QK * (hh + 1), :] = q.astype(BF16)
            kk = jnp.concatenate([kvc[128 * hh:128 * hh + C_NOPE], kr], axis=0)
            kk = _rms_rows(kk, gain("ckn"))
            kk = jnp.concatenate([kk[:C_NOPE], _rope_rows(kk[C_NOPE:], cosm, sinm, 16), zeros32], axis=0)
            kc_ref[0, :, LANES * hh:LANES * (hh + 1)] = kk.T.astype(BF16)
            vcs.append(kvc[128 * hh + C_NOPE:128 * (hh + 1)])
        _store_key_blocks(vc_ref, jnp.concatenate(vcs, axis=0).astype(BF16))

    def prep_d(proj):
        sd = HEAD_DIM ** -0.5 * LOG2E
        qs, ks, vs = [], [], []
        for g in range(2):
            qg, kg = [], []
            for hh in (2 * g, 2 * g + 1):
                qg.append(_rms_rows(proj(O_DQ + 64 * hh, 64), gain("dq")) * sd)
                kg.append(_rms_rows(proj(O_DK + 64 * hh, 64), gain("dk")))
            qs.append(jnp.concatenate(qg, axis=0).T)
            ks.append(jnp.concatenate(kg, axis=0).T)
            vs.append(proj(O_DV + 128 * g, 128).T)
        _store_dilated(scr_ref.at[0], jnp.concatenate(qs, axis=1), (dq1_ref, dq4_ref, dq16_ref))
        _store_dilated(scr_ref.at[1], jnp.concatenate(ks, axis=1), (dk1_ref, dk4_ref, dk16_ref))
        _store_dilated(scr_ref.at[2], jnp.concatenate(vs, axis=1), (dv1_ref, dv4_ref, dv16_ref))

    sec_c = section(O_CQL, O_DQ)
    sec_d = section(O_DQ, IN_WIDTH)
    prep_c(sec_c)
    sec_b = section(O_BQ, O_CQL)
    prep_d(sec_d)
    sec_a = section(O_AQ, O_BQ)
    prep_b(sec_b)
    prep_a(sec_a)


def _project(x, layer, w_in_t, wqb_t, wkvb_t, gains, cosa, sina, cosm, sinm):
    b, s, d = x.shape
    tm = min(PROJ_TILE, s)
    assert s % tm == 0 and tm % KSTEP == 0
    nt = s // tm
    fm = lambda rows: jax.ShapeDtypeStruct((b, rows, s), BF16)
    tmj = lambda cols: jax.ShapeDtypeStruct((b, s, cols), BF16)
    vblk = lambda rows: jax.ShapeDtypeStruct((b, s // KSTEP, rows, KSTEP), BF16)
    dviews = tuple(jax.ShapeDtypeStruct((b, s // dil, dil * D_WIDTH), BF16) for _, dil in DILATED_CONFIGS)
    out_shape = (fm(256), tmj(256), vblk(128),
                 fm(256), tmj(512), vblk(256),
                 fm(384), tmj(512), vblk(256)) + dviews * 3
    fm_spec = lambda rows: pl.BlockSpec((1, rows, tm), lambda i, j: (i, 0, j))
    tm_spec = lambda cols: pl.BlockSpec((1, tm, cols), lambda i, j: (i, j, 0))
    v_spec = lambda rows: pl.BlockSpec((1, tm // KSTEP, rows, KSTEP), lambda i, j: (i, j, 0, 0))
    dspecs = tuple(pl.BlockSpec((1, tm // dil, dil * D_WIDTH), lambda i, j: (i, j, 0)) for _, dil in DILATED_CONFIGS)
    out_specs = (fm_spec(256), tm_spec(256), v_spec(128),
                 fm_spec(256), tm_spec(512), v_spec(256),
                 fm_spec(384), tm_spec(512), v_spec(256)) + dspecs * 3
    whole = lambda i, j: (0, 0)
    in_specs = [
        pl.BlockSpec((1, tm, d), lambda i, j: (i, j, 0)),
        pl.BlockSpec((None, IN_WIDTH, d), lambda i, j: (layer, 0, 0)),
        pl.BlockSpec((N_HEADS * C_QK, C_Q_LORA), whole),
        pl.BlockSpec((N_HEADS * 128, C_KV_LORA), whole),
        pl.BlockSpec((G_TOTAL, 1), whole),
        pl.BlockSpec((64, tm), lambda i, j: (0, j)),
        pl.BlockSpec((64, tm), lambda i, j: (0, j)),
        pl.BlockSpec((32, tm), lambda i, j: (0, j)),
        pl.BlockSpec((32, tm), lambda i, j: (0, j)),
    ]
    return pl.pallas_call(
        _proj_body,
        out_shape=out_shape,
        grid=(b, nt),
        in_specs=in_specs,
        out_specs=out_specs,
        scratch_shapes=[pltpu.VMEM((3, D_WIDTH // LANES, tm, LANES), F32)],
        compiler_params=_cparams(("parallel", "arbitrary")),
        name="project_heads",
    )(x, w_in_t, wqb_t, wkvb_t, gains, cosa, sina, cosm, sinm)


def _online_update(s, v_aug, carry, shift=None):
    m, acc = carry
    smax = jnp.max(s, axis=0, keepdims=True)
    if shift is None:
        m_new = jnp.maximum(m, smax)
        p = jnp.exp2(s - m_new).astype(BF16)
    else:
        m_new = jnp.maximum(m, smax + shift)
        p = jnp.exp2(s - (m_new - shift)).astype(BF16)
    alpha = jnp.exp2(m - m_new)
    acc = alpha * acc + jnp.dot(v_aug, p, preferred_element_type=F32)
    return m_new, acc


def _init_carry(dv, tq):
    return (jnp.full((1, tq), -jnp.inf, F32), jnp.zeros((dv + BF16_ROWS, tq), F32))


def _finish(carry, dv):
    _, acc = carry
    return acc[:dv] / acc[dv:dv + 1]


def _run_pipelined(n_jobs, n_blocks, scores_fn, update_fn, init_fn, finish_fn):
    steps = [(t, i) for t in range(n_jobs) for i in range(n_blocks)]
    pending = [scores_fn(*steps[n]) for n in range(min(SCORES_AHEAD, len(steps)))]
    carry = None
    for n, (t, i) in enumerate(steps):
        s = pending.pop(0)
        if n + SCORES_AHEAD < len(steps):
            pending.append(scores_fn(*steps[n + SCORES_AHEAD]))
        if i == 0:
            carry = init_fn()
        carry = update_fn(t, i, s, carry)
        if i == n_blocks - 1:
            finish_fn(t, carry)


def _dense_body(q_ref, k_ref, v_ref, o_ref):
    dq = q_ref.shape[1]
    nkb, dv, ks = v_ref.shape[1], v_ref.shape[2], v_ref.shape[3]
    tq = TQ
    ones = jnp.ones((BF16_ROWS, ks), BF16)
    zpad = jnp.zeros((LANES - dq, tq), BF16)
    qpads = [jnp.concatenate([q_ref[0, :, t * tq:(t + 1) * tq], zpad], axis=0) for t in range(q_ref.shape[2] // tq)]

    def scores(t, i):
        return jnp.dot(k_ref[0, i * ks:(i + 1) * ks, :], qpads[t], preferred_element_type=F32)

    def update(t, i, s, carry):
        return _online_update(s, jnp.concatenate([v_ref[0, i], ones], axis=0), carry)

    def finish(t, carry):
        o_ref[0, :, t * tq:(t + 1) * tq] = _finish(carry, dv).astype(o_ref.dtype)

    _run_pipelined(len(qpads), nkb, scores, update, lambda: _init_carry(dv, tq), finish)


def _dense_attention(q, k, v, dq, q_per_kv):
    b, rows, s = q.shape
    nh = rows // dq
    nkb, tk = v.shape[1], v.shape[3]
    step = min(DENSE_TQ_STEP, s)
    assert s % step == 0 and step % TQ == 0
    return pl.pallas_call(
        _dense_body,
        out_shape=jax.ShapeDtypeStruct((b, nh * HEAD_DIM, s), BF16),
        grid=(b, nh, s // step),
        in_specs=[
            pl.BlockSpec((1, dq, step), lambda i, h, j: (i, h, j)),
            pl.BlockSpec((1, s, LANES), lambda i, h, j: (i, 0, h // q_per_kv)),
            pl.BlockSpec((1, nkb, HEAD_DIM, tk), lambda i, h, j: (i, 0, h // q_per_kv, 0)),
        ],
        out_specs=pl.BlockSpec((1, HEAD_DIM, step), lambda i, h, j: (i, h, j)),
        compiler_params=_cparams(("parallel", "arbitrary", "arbitrary")),
        name="dense_attention",
    )(q, k, v)


def _diff_body(sc_ref, q_ref, k_ref, v_ref, g_ref, bias_ref, o_ref):
    tq = TQ
    n_tiles = q_ref.shape[2] // tq
    nkb, dv, ks = v_ref.shape[1], v_ref.shape[2], v_ref.shape[3]
    hh = pl.program_id(1)
    lam = sc_ref[0]
    out_scale = sc_ref[1]
    slope = sc_ref[2 + hh]
    z32 = jnp.zeros((B_QK_DIM, tq), BF16)
    z48 = jnp.zeros((48, tq), BF16)
    zfeat = jnp.zeros((BF16_ROWS, tq), BF16)
    ones = jnp.ones((BF16_ROWS, ks), BF16)
    frow = lax.broadcasted_iota(jnp.int32, (BF16_ROWS, tq), 0)
    piece = jnp.where(frow % 3 == 0, LOG2E_PIECES[0], jnp.where(frow % 3 == 1, LOG2E_PIECES[1], LOG2E_PIECES[2]))
    cfeat = jnp.where(frow < 6, piece, 0.0)
    lane_pos = lax.broadcasted_iota(jnp.int32, (1, tq), 1)

    n_mixed = tq // ks

    def tile_index(job):
        return pl.program_id(2) * n_tiles + job // 2

    def block_index(job, ii):
        return lax.rem(tile_index(job) * n_mixed + ii, nkb)

    def side(job, ii):
        return jnp.where(block_index(job, ii) < tile_index(job) * n_mixed, 1.0, -1.0).astype(F32)

    def qrows(job):
        q = q_ref[0, :, (job // 2) * tq:(job // 2 + 1) * tq]
        if job % 2 == 0:
            return jnp.concatenate([q[:B_QK_DIM], z32], axis=0)
        return jnp.concatenate([z32, q[B_QK_DIM:]], axis=0)

    def scores(job, ii):
        i = block_index(job, ii)
        k = k_ref[0, pl.ds(pl.multiple_of(i * ks, ks), ks), :]
        feat = zfeat if ii < n_mixed else (cfeat * (side(job, ii) * slope)).astype(BF16)
        return jnp.dot(k, jnp.concatenate([qrows(job), feat, z48], axis=0), preferred_element_type=F32)

    def update(job, ii, s, carry):
        v_aug = jnp.concatenate([v_ref[0, block_index(job, ii)], ones], axis=0)
        if ii < n_mixed:
            return _online_update(s - bias_ref[0, ii], v_aug, carry)
        qpos = (tile_index(job) * tq + lane_pos).astype(F32)
        return _online_update(s, v_aug, carry, qpos * (-(side(job, ii) * slope * LOG2E)))

    res = []

    def finish(job, carry):
        res.append(_finish(carry, dv))
        if job % 2 == 1:
            o = res[job - 1] - lam * res[job]
            t = job // 2
            o_ref[0, :, t * tq:(t + 1) * tq] = (_rms_rows(o, g_ref[...]) * out_scale).astype(o_ref.dtype)

    _run_pipelined(2 * n_tiles, nkb, scores, update, lambda: _init_carry(dv, tq), finish)


def _diff_attention(scal, slopes, q, k, v, sub_g):
    b, rows, s = q.shape
    nkb, tk = v.shape[1], v.shape[3]
    n_mixed = TQ // tk
    jj = jnp.arange(tk)[None, :, None] + tk * jnp.arange(n_mixed)[:, None, None]
    dist = jnp.abs(jj - jnp.arange(TQ)[None, None, :]).astype(F32)
    bias = dist[None] * (LOG2E * jnp.asarray(slopes, F32))[:, None, None, None]
    step = min(TQ_STEP, s)
    assert s % step == 0 and step % TQ == 0
    return pl.pallas_call(
        _diff_body,
        out_shape=jax.ShapeDtypeStruct((b, rows, s), BF16),
        grid=(b, N_HEADS, s // step),
        in_specs=[
            pl.BlockSpec(memory_space=pltpu.SMEM),
            pl.BlockSpec((1, 2 * B_QK_DIM, step), lambda i, h, j: (i, h, j)),
            pl.BlockSpec((1, s, LANES), lambda i, h, j: (i, 0, h)),
            pl.BlockSpec((1, nkb, HEAD_DIM, tk), lambda i, h, j: (i, 0, h, 0)),
            pl.BlockSpec((HEAD_DIM, 1), lambda i, h, j: (0, 0)),
            pl.BlockSpec((1, n_mixed, tk, TQ), lambda i, h, j: (h, 0, 0, 0)),
        ],
        out_specs=pl.BlockSpec((1, HEAD_DIM, step), lambda i, h, j: (i, h, j)),
        compiler_params=_cparams(("parallel", "arbitrary", "arbitrary")),
        name="diff_attention",
    )(scal, q, k, v, sub_g, bias)


def _band_body(q_ref, k_ref, v_ref, pen_ref, o_ref, lse_ref, *, qb):
    l_len = k_ref.shape[1]
    win = pen_ref.shape[3]
    lane = lax.broadcasted_iota(jnp.int32, (qb, LANES), 1)
    n_sub = q_ref.shape[1] // qb
    n_groups = q_ref.shape[2] // LANES
    for sb in range(n_sub):
        jb = pl.program_id(2) * n_sub + sb
        start = jnp.clip(jb * qb - BAND_RADIUS, 0, l_len - win)
        start = pl.multiple_of(start, BAND_RADIUS)
        variant = (jb * qb - start) // BAND_RADIUS
        q = q_ref[0, sb * qb:(sb + 1) * qb, :]
        kwin = k_ref[0, pl.ds(start, win), :]
        vwin = v_ref[0, pl.ds(start, win), :]
        for gc in range(n_groups):
            g = gc % 2
            q2 = q[:, LANES * gc:LANES * (gc + 1)].astype(F32)
            k2 = kwin[:, LANES * gc:LANES * (gc + 1)]
            v2 = vwin[:, LANES * gc:LANES * (gc + 1)]
            out_g = jnp.zeros((qb, LANES), F32)
            lse_g = jnp.zeros((qb, LANES), F32)
            for hh in range(2):
                own = (lane >= HEAD_DIM * hh) & (lane < HEAD_DIM * (hh + 1))
                qm = jnp.where(own, q2, 0.0).astype(BF16)
                s = lax.dot_general(qm, k2, (((1,), (1,)), ((), ())), preferred_element_type=F32)
                s = s - pen_ref[variant, 2 * g + hh]
                m = jnp.max(s, axis=1, keepdims=True)
                p = jnp.exp2(s - m)
                lsum = jnp.sum(p, axis=1, keepdims=True)
                o = jnp.dot(p.astype(BF16), v2, preferred_element_type=F32) / lsum
                lse = m + jnp.log2(lsum)
                out_g = jnp.where(own, o, out_g)
                lse_g = jnp.where(own, lse, lse_g)
            o_ref[0, sb * qb:(sb + 1) * qb, LANES * gc:LANES * (gc + 1)] = out_g.astype(o_ref.dtype)
            lse_ref[0, sb * qb:(sb + 1) * qb, LANES * gc:LANES * (gc + 1)] = lse_g


def _band_penalty(qb, win, dil, slopes):
    rows = jnp.arange(qb)[None, :, None] + BAND_RADIUS * jnp.arange(3)[:, None, None]
    rel = jnp.abs(rows - jnp.arange(win)[None, None, :])
    scale = jnp.asarray(slopes, F32) * (float(dil) * LOG2E)
    pen = rel.astype(F32)[:, None] * scale[None, :, None, None]
    return jnp.where((rel <= BAND_RADIUS)[:, None], pen, BAND_MASKED)


def _band_attention(q, k, v, dil, slopes):
    b, l_len, cw = q.shape
    c = cw // dil
    qb = min(BAND_Q, l_len)
    qstep = min(BAND_QSTEP, l_len)
    cps = min(dil, max(1, BAND_QSTEP // l_len))
    assert dil % cps == 0
    win = min(qb + 2 * BAND_RADIUS, l_len)
    pen = _band_penalty(qb, win, dil, slopes)
    body = functools.partial(_band_body, qb=qb)
    return pl.pallas_call(
        body,
        out_shape=(jax.ShapeDtypeStruct((b, l_len, cw), BF16), jax.ShapeDtypeStruct((b, l_len, cw), F32)),
        grid=(b, dil // cps, l_len // qstep),
        in_specs=[
            pl.BlockSpec((1, qstep, c * cps), lambda i, r, j: (i, j, r)),
            pl.BlockSpec((1, l_len, c * cps), lambda i, r, j: (i, 0, r)),
            pl.BlockSpec((1, l_len, c * cps), lambda i, r, j: (i, 0, r)),
            pl.BlockSpec(pen.shape, lambda i, r, j: (0, 0, 0, 0)),
        ],
        out_specs=(pl.BlockSpec((1, qstep, c * cps), lambda i, r, j: (i, j, r)),) * 2,
        compiler_params=_cparams(("parallel", "arbitrary", "arbitrary")),
        name="band_attention_d%d" % dil,
    )(q, k, v, pen)


def _band_combine(in_refs, scr_ref, tm, c):
    vals = []
    for idx, (_, dil) in enumerate(DILATED_CONFIGS):
        for which in range(2):
            src = in_refs[2 * idx + which]
            if dil == 1:
                vals.append(src[0].astype(F32))
                continue
            groups = []
            for g in range(c // LANES):
                slot = scr_ref.at[(2 * idx + which) * (c // LANES) + g]
                for r in range(dil):
                    lo = c * r + LANES * g
                    slot[pl.ds(r, tm // dil, stride=dil), :] = src[0, :, lo:lo + LANES].astype(F32)
                groups.append(slot[...])
            vals.append(jnp.concatenate(groups, axis=1))
    outs, lses = vals[0::2], vals[1::2]
    m = functools.reduce(jnp.maximum, lses)
    ws = [jnp.exp2(x - m) for x in lses]
    num = functools.reduce(lambda a, t: a + t, [w * o for w, o in zip(ws, outs)])
    den = functools.reduce(lambda a, t: a + t, ws)
    return num / den


def _outproj_body(*refs):
    n_d = 2 * len(DILATED_CONFIGS)
    x_ref, oa_ref, ob_ref, oc_ref = refs[:4]
    d_refs = refs[4:4 + n_d]
    w_ref, g_ref, x1_ref, h2_ref, edge_ref, scr_ref = refs[4 + n_d:]
    tm = x_ref.shape[1]
    j = pl.program_id(1)
    mix = jnp.concatenate([oa_ref[0], ob_ref[0], oc_ref[0]], axis=0)
    od = _band_combine(d_refs, scr_ref, tm, D_WIDTH).astype(BF16)
    w = w_ref[...]
    y = jnp.dot(w[:, :768], mix, preferred_element_type=F32)
    y = y + lax.dot_general(w[:, 768:], od, (((1,), (1,)), ((), ())), preferred_element_type=F32)
    x1 = x_ref[0] + y.T
    x1_ref[0] = x1
    ms = jnp.mean(x1 * x1, axis=1, keepdims=True)
    h2 = (x1 * lax.rsqrt(ms + NORM_EPS) * g_ref[...]).astype(BF16)
    h2_ref[0] = h2
    edge_ref[0, pl.ds(2 * j, 1), :] = h2[0:1].astype(F32)
    edge_ref[0, pl.ds(2 * j + 1, 1), :] = h2[tm - 1:tm].astype(F32)


def _out_project(x, oa, ob, oc, branches, layer, w_out_t, g2):
    b, s, d = x.shape
    tm = TOK_TILE
    nt = s // tm
    fm = lambda rows: pl.BlockSpec((1, rows, tm), lambda i, j: (i, 0, j))
    tmj = pl.BlockSpec((1, tm, d), lambda i, j: (i, j, 0))
    d_specs = []
    for (_, dil) in DILATED_CONFIGS:
        d_specs += [pl.BlockSpec((1, tm // dil, dil * D_WIDTH), lambda i, j: (i, j, 0))] * 2
    flat = [a for pair in branches for a in pair]
    return pl.pallas_call(
        _outproj_body,
        out_shape=(jax.ShapeDtypeStruct((b, s, d), F32), jax.ShapeDtypeStruct((b, s, d), BF16),
                   jax.ShapeDtypeStruct((b, 2 * nt, d), F32)),
        grid=(b, nt),
        in_specs=[tmj, fm(256), fm(256), fm(256)] + d_specs + [
            pl.BlockSpec((None, d, d), lambda i, j: (layer, 0, 0)),
            pl.BlockSpec((1, d), lambda i, j: (0, 0))],
        out_specs=(tmj, tmj, pl.BlockSpec((1, 2 * nt, d), lambda i, j: (i, 0, 0))),
        scratch_shapes=[pltpu.VMEM((2 * len(DILATED_CONFIGS) * (D_WIDTH // LANES), tm, LANES), F32)],
        compiler_params=_cparams(("parallel", "arbitrary")),
        name="out_project",
    )(x, oa, ob, oc, *flat, w_out_t, g2)


def _halo_body(h_ref, w_ref, o_ref):
    o_ref[...] = jnp.dot(h_ref[...], w_ref[...], preferred_element_type=F32)


def _halo_up(halo_h, layer, w_up):
    nh, d = halo_h.shape
    ff2 = w_up.shape[2]
    n_blk = 4
    cols = ff2 // n_blk
    return pl.pallas_call(
        _halo_body,
        out_shape=jax.ShapeDtypeStruct((nh, ff2), F32),
        grid=(n_blk,),
        in_specs=[pl.BlockSpec((nh, d), lambda i: (0, 0)),
                  pl.BlockSpec((None, d, cols), lambda i: (layer, 0, i))],
        out_specs=pl.BlockSpec((nh, cols), lambda i: (0, i)),
        compiler_params=_cparams(("arbitrary",)),
        name="mlp_halo_up",
    )(halo_h, w_up)


def _mlp_body(x1_ref, h2_ref, wup_ref, wdn_ref, par_ref, halo_ref, o_ref, gbuf_ref, vbuf_ref, *, chunks):
    tf = h2_ref.shape[1]
    ff = wdn_ref.shape[0]
    nt = pl.num_programs(1)
    row = (pl.program_id(0) * nt + pl.program_id(1)) * 2
    h = h2_ref[0]

    def conv(buf_ref, u, c0, width):
        buf_ref[MLP_PAD:MLP_PAD + tf, :width] = u
        buf_ref[MLP_PAD - 1:MLP_PAD, :width] = halo_ref[pl.ds(row, 1), c0:c0 + width]
        buf_ref[MLP_PAD + tf:MLP_PAD + tf + 1, :width] = halo_ref[pl.ds(row + 1, 1), c0:c0 + width]
        prev = buf_ref[MLP_PAD - 1:MLP_PAD - 1 + tf, :width]
        nxt = buf_ref[MLP_PAD + 1:MLP_PAD + 1 + tf, :width]
        p = par_ref[:, c0:c0 + width]
        return p[0:1] * prev + p[1:2] * u + p[2:3] * nxt + p[3:4]

    def up(c0, width):
        return (jnp.dot(h, wup_ref[:, c0:c0 + width], preferred_element_type=F32),
                jnp.dot(h, wup_ref[:, ff + c0:ff + c0 + width], preferred_element_type=F32))

    y = x1_ref[0]
    nxt_up = up(*chunks[0])
    for n, (c0, width) in enumerate(chunks):
        ug, uv = nxt_up
        if n + 1 < len(chunks):
            nxt_up = up(*chunks[n + 1])
        gate = conv(gbuf_ref, ug, c0, width)
        val = conv(vbuf_ref, uv, ff + c0, width)
        act = gate / (1.0 + jnp.exp(-gate)) * val
        y = y + jnp.dot(act.astype(BF16), wdn_ref[c0:c0 + width, :], preferred_element_type=F32)
    o_ref[0] = y


def _mlp(x1, h2, layer, w_up, w_down, conv_par, halo_up, chunks):
    b, s, d = x1.shape
    tf = TOK_TILE
    ff2 = w_up.shape[2]
    nh = halo_up.shape[0]
    whole = lambda i, j: (0, 0)
    body = functools.partial(_mlp_body, chunks=chunks)
    const = dict(pipeline_mode=pl.Buffered(1))
    wmax = max(w for _, w in chunks)
    return pl.pallas_call(
        body,
        out_shape=jax.ShapeDtypeStruct((b, s, d), F32),
        grid=(b, s // tf),
        in_specs=[pl.BlockSpec((1, tf, d), lambda i, j: (i, j, 0)),
                  pl.BlockSpec((1, tf, d), lambda i, j: (i, j, 0)),
                  pl.BlockSpec((None, d, ff2), lambda i, j: (layer, 0, 0), **const),
                  pl.BlockSpec((None, ff2 // 2, d), lambda i, j: (layer, 0, 0), **const),
                  pl.BlockSpec((8, ff2), whole, **const),
                  pl.BlockSpec((nh, ff2), whole, **const)],
        out_specs=pl.BlockSpec((1, tf, d), lambda i, j: (i, j, 0)),
        scratch_shapes=[pltpu.VMEM((tf + 2 * MLP_PAD, wmax), F32)] * 2,
        compiler_params=_cparams(("arbitrary", "arbitrary")),
        name="gated_conv_mlp",
    )(x1, h2, w_up, w_down, conv_par, halo_up)


def _halo_rows(edges, nt):
    b, _, d = edges.shape
    ez = jnp.concatenate([edges, jnp.zeros((b, 1, d), edges.dtype)], axis=1)
    idx = []
    for j in range(nt):
        idx += [2 * j - 1 if j > 0 else 2 * nt, 2 * j + 2 if j < nt - 1 else 2 * nt]
    rows = jnp.take(ez, jnp.asarray(idx, jnp.int32), axis=1).reshape(b * nt * 2, d)
    pad = (-rows.shape[0]) % BF16_ROWS
    return jnp.pad(rows, ((0, pad), (0, 0))).astype(BF16)


def _rope_tables(s):
    def freqs(dim):
        return 1.0 / (ROPE_THETA ** (jnp.arange(0, dim, 2, dtype=F32) / dim))
    t = jnp.arange(s)
    rows = (t // GRID_W).astype(F32)
    cols = (t % GRID_W).astype(F32)
    pos = t.astype(F32)
    fa = freqs(HEAD_DIM // 2)
    ar, ac = fa[:, None] * rows[None, :], fa[:, None] * cols[None, :]
    am = freqs(C_ROPE)[:, None] * pos[None, :]
    cosa = jnp.concatenate([jnp.cos(ar), jnp.cos(ar), jnp.cos(ac), jnp.cos(ac)], axis=0)
    sina = jnp.concatenate([-jnp.sin(ar), jnp.sin(ar), -jnp.sin(ac), jnp.sin(ac)], axis=0)
    cosm = jnp.concatenate([jnp.cos(am), jnp.cos(am)], axis=0)
    sinm = jnp.concatenate([-jnp.sin(am), jnp.sin(am)], axis=0)
    return cosa, sina, cosm, sinm


def kernel(x, norm1_g, w_in, a_qn_g, a_kn_g, b_qn_g, b_kn_g, b_lam_q1, b_lam_k1, b_lam_q2, b_lam_k2, b_sub_g,
           c_qa_g, c_kva_g, c_wqb, c_wkvb, c_qn_g, c_kn_g, d_qn_g, d_kn_g, w_out, norm2_g, w_up, conv_w, conv_b,
           w_down):
    b, s, d = x.shape
    depth = w_in.shape[0]
    ff = w_down.shape[1]
    nt = s // TOK_TILE
    assert s % TOK_TILE == 0 and TOK_TILE % KSTEP == 0 and TQ % KSTEP == 0 and s % TQ == 0 and s % GRID_W == 0
    assert s <= 1 << (POS_SPLIT_BITS + BF16_MANTISSA_BITS)
    for window, dil in DILATED_CONFIGS:
        assert window // (2 * dil) == BAND_RADIUS and (s // dil) % min(BAND_Q, s // dil) == 0
        assert (s // dil) % min(BAND_QSTEP, s // dil) == 0 and BAND_QSTEP % BAND_Q == 0
        assert TOK_TILE % (dil * BF16_ROWS) == 0

    tr = lambda w: jnp.swapaxes(w, 1, 2).astype(BF16)
    gains = jnp.concatenate([norm1_g, a_qn_g, a_kn_g, b_qn_g, b_kn_g, c_qa_g, c_kva_g, c_qn_g, c_kn_g,
                             d_qn_g, d_kn_g], axis=1)[:, :, None]
    conv_par = jnp.concatenate([conv_w, conv_b[:, None, :], jnp.zeros((depth, 4, 2 * ff), F32)], axis=1)
    slopes = [2.0 ** (-8.0 * i / (2 * N_HEADS)) for i in range(1, 2 * N_HEADS + 1)]
    slopes_b, slopes_d = slopes[:N_HEADS], tuple(slopes[N_HEADS:])
    lam_init = jnp.asarray([0.8 - 0.6 * math.exp(-0.3 * l) for l in range(depth)], F32)
    lam = (jnp.exp(jnp.sum(b_lam_q1 * b_lam_k1, axis=1)) - jnp.exp(jnp.sum(b_lam_q2 * b_lam_k2, axis=1)) + lam_init)
    scal = jnp.concatenate([lam[:, None], (1.0 - lam_init)[:, None],
                            jnp.broadcast_to(jnp.asarray(slopes_b, F32), (depth, N_HEADS)),
                            jnp.zeros((depth, 2), F32)], axis=1)
    w_in_t, w_out_t, w_up_b, w_down_b = tr(w_in), tr(w_out), w_up.astype(BF16), w_down.astype(BF16)
    wqb_t, wkvb_t = tr(c_wqb), tr(c_wkvb)
    g2, sub_g = norm2_g[:, None, :], b_sub_g[:, :, None]
    cosa, sina, cosm, sinm = _rope_tables(s)
    assert ff % LANES == 0
    chunks = tuple((c0, min(MLP_CHUNK, ff - c0)) for c0 in range(0, ff, MLP_CHUNK))

    out = x
    for l in range(depth):
        (qa, ka, va, qb, kb, vb, qc, kc, vc, dq1, dq4, dq16, dk1, dk4, dk16, dv1, dv4, dv16) = _project(
            out, l, w_in_t, wqb_t[l], wkvb_t[l], gains[l], cosa, sina, cosm, sinm)
        oa = _dense_attention(qa, ka, va, HEAD_DIM, N_HEADS // A_KV_HEADS)
        ob = _diff_attention(scal[l], slopes_b, qb, kb, vb, sub_g[l])
        oc = _dense_attention(qc, kc, vc, C_QK, 1)
        branches = [_band_attention(q_, k_, v_, dil, slopes_d)
                    for (_, dil), q_, k_, v_ in zip(DILATED_CONFIGS, (dq1, dq4, dq16), (dk1, dk4, dk16), (dv1, dv4, dv16))]
        x1, h2, edges = _out_project(out, oa, ob, oc, branches, l, w_out_t, g2[l])
        halo = _halo_up(_halo_rows(edges, nt), l, w_up_b)
        out = _mlp(x1, h2, l, w_up_b, w_down_b, conv_par[l], halo, chunks)
    return out
```

```python
import functools
import math

import jax
import jax.numpy as jnp
import numpy as np
from jax import lax
from jax.experimental import pallas as pl
from jax.experimental.pallas import tpu as pltpu

F32 = jnp.float32
BF16 = jnp.bfloat16

HEAD_DIM = 64
N_HEADS = 4
A_KV_HEADS = 2
B_QK_DIM = 32
C_Q_LORA = 256
C_KV_LORA = 128
C_NOPE = 64
C_ROPE = 32
C_QK = C_NOPE + C_ROPE
DILATED_CONFIGS = ((128, 1), (512, 4), (2048, 16))
D_WIDTH = N_HEADS * HEAD_DIM
GRID_W = 64
ROPE_THETA = 10000.0
NORM_EPS = 1e-6
LOG2E = 1.4426950408889634


def _bf16_pieces(x, n):
    out = []
    for _ in range(n):
        u = np.float32(x).view(np.uint32)
        u = np.uint32((int(u) + 0x7FFF + ((int(u) >> 16) & 1)) & 0xFFFF0000)
        piece = float(u.view(np.float32))
        out.append(piece)
        x = x - piece
    return tuple(out)


LOG2E_PIECES = _bf16_pieces(LOG2E, 3)
POS_SPLIT_BITS = 6
BF16_MANTISSA_BITS = 8

_SPLITS = (256, 128, 128, 256, 256, 256, C_Q_LORA, C_KV_LORA, C_ROPE, 256, 256, 256)
_OFF = [0]
for _n in _SPLITS:
    _OFF.append(_OFF[-1] + _n)
(O_AQ, O_AK, O_AV, O_BQ, O_BK, O_BV, O_CQL, O_CKVL, O_CKR, O_DQ, O_DK, O_DV, IN_WIDTH) = _OFF

_GAIN_SIZES = (("n1", 1024), ("aq", 64), ("ak", 64), ("bq", 32), ("bk", 32), ("cqa", 256), ("ckva", 128),
               ("cqn", 96), ("ckn", 96), ("dq", 64), ("dk", 64))
G_OFF = {}
_o = 0
for _k, _n in _GAIN_SIZES:
    G_OFF[_k] = (_o, _n)
    _o += _n
G_TOTAL = _o

LANES = 128
BF16_ROWS = 16
TOK_TILE = 512
PROJ_TILE = 1024
MLP_CHUNK = 1536
MLP_PAD = 8
TQ = 512
TQ_STEP = 2048
DENSE_TQ_STEP = 4096
KSTEP = 256
SCORES_AHEAD = 2
BAND_Q = 128
BAND_QSTEP = 4096
BAND_MASKED = 1e30
BAND_RADIUS = 64
V7X_VMEM_BYTES = 64 * 1024 * 1024
VMEM_LIMIT = V7X_VMEM_BYTES * 7 // 8


def _cparams(sem):
    return pltpu.CompilerParams(dimension_semantics=sem, vmem_limit_bytes=VMEM_LIMIT)


def _rms_rows(v, g_col):
    ms = jnp.mean(v * v, axis=0, keepdims=True)
    return v * lax.rsqrt(ms + NORM_EPS) * g_col


def _swap_halves(x, n):
    parts = []
    for i in range(0, x.shape[0], 2 * n):
        parts.append(x[i + n:i + 2 * n])
        parts.append(x[i:i + n])
    return jnp.concatenate(parts, axis=0)


def _rope_rows(x, cos, sin_signed, n):
    return x * cos + _swap_halves(x, n) * sin_signed


def _store_dilated(scr_ref, val, out_refs):
    tm, c = val.shape
    for g in range(c // LANES):
        scr_ref[g] = val[:, LANES * g:LANES * (g + 1)]
    for (_, dil), out in zip(DILATED_CONFIGS, out_refs):
        if dil == 1:
            out[0] = val.astype(BF16)
            continue
        n = tm // dil
        for r in range(dil):
            for g in range(c // LANES):
                lo = c * r + LANES * g
                out[0, :, lo:lo + LANES] = scr_ref[g, pl.ds(r, n, stride=dil), :].astype(BF16)


def _store_key_blocks(v_ref, v):
    for c in range(v.shape[1] // KSTEP):
        v_ref[0, c] = v[:, c * KSTEP:(c + 1) * KSTEP]


def _proj_body(x_ref, w_ref, wqb_ref, wkvb_ref, g_ref, cosa_ref, sina_ref, cosm_ref, sinm_ref,
               qa_ref, ka_ref, va_ref, qb_ref, kb_ref, vb_ref, qc_ref, kc_ref, vc_ref,
               dq1_ref, dq4_ref, dq16_ref, dk1_ref, dk4_ref, dk16_ref, dv1_ref, dv4_ref, dv16_ref,
               scr_ref):
    tm = x_ref.shape[1]

    def gain(name):
        o, n = G_OFF[name]
        return g_ref[o:o + n, :]

    x = x_ref[0].T
    h = (x * gain("n1")).astype(BF16)
    r = lax.rsqrt(jnp.mean(x * x, axis=0, keepdims=True) + NORM_EPS)

    def section(lo, hi):
        part = jnp.dot(w_ref[lo:hi, :], h, preferred_element_type=F32) * r
        return lambda o, n: part[o - lo:o - lo + n]

    cosa, sina = cosa_ref[...], sina_ref[...]
    cosm, sinm = cosm_ref[...], sinm_ref[...]
    zeros64 = jnp.zeros((64, tm), F32)
    zeros32 = jnp.zeros((32, tm), F32)

    def prep_a(proj):
        sa = HEAD_DIM ** -0.5 * LOG2E
        for hh in range(N_HEADS):
            q = _rope_rows(_rms_rows(proj(O_AQ + 64 * hh, 64), gain("aq")), cosa, sina, 16) * sa
            qa_ref[0, 64 * hh:64 * hh + 64, :] = q.astype(BF16)
        for g in range(A_KV_HEADS):
            k = _rope_rows(_rms_rows(proj(O_AK + 64 * g, 64), gain("ak")), cosa, sina, 16)
            kpad = jnp.concatenate([k, zeros64], axis=0)
            ka_ref[0, :, LANES * g:LANES * (g + 1)] = kpad.T.astype(BF16)
        _store_key_blocks(va_ref, proj(O_AV, 128).astype(BF16))

    def prep_b(proj):
        sb = B_QK_DIM ** -0.5 * LOG2E
        for i in range(2 * N_HEADS):
            qb_ref[0, 32 * i:32 * i + 32, :] = (_rms_rows(proj(O_BQ + 32 * i, 32), gain("bq")) * sb).astype(BF16)
        kpos = pl.program_id(1) * tm + lax.broadcasted_iota(jnp.int32, (BF16_ROWS, tm), 1)
        frow = lax.broadcasted_iota(jnp.int32, (BF16_ROWS, tm), 0)
        pos_hi = ((kpos >> POS_SPLIT_BITS) << POS_SPLIT_BITS).astype(F32)
        pos_lo = (kpos & ((1 << POS_SPLIT_BITS) - 1)).astype(F32)
        kfeat = jnp.where(frow < 3, pos_hi, jnp.where(frow < 6, pos_lo, 0.0))
        zeros48 = jnp.zeros((48, tm), F32)
        for hh in range(N_HEADS):
            k1 = _rms_rows(proj(O_BK + 64 * hh, 32), gain("bk"))
            k2 = _rms_rows(proj(O_BK + 64 * hh + 32, 32), gain("bk"))
            kpad = jnp.concatenate([k1, k2, kfeat, zeros48], axis=0)
            kb_ref[0, :, LANES * hh:LANES * (hh + 1)] = kpad.T.astype(BF16)
        _store_key_blocks(vb_ref, proj(O_BV, 256).astype(BF16))

    def prep_c(proj):
        sc = C_QK ** -0.5 * LOG2E
        cq = _rms_rows(proj(O_CQL, C_Q_LORA), gain("cqa")).astype(BF16)
        qc = jnp.dot(wqb_ref[...], cq, preferred_element_type=F32)
        ckv = _rms_rows(proj(O_CKVL, C_KV_LORA), gain("ckva")).astype(BF16)
        kvc = jnp.dot(wkvb_ref[...], ckv, preferred_element_type=F32)
        kr = proj(O_CKR, C_ROPE)
        vcs = []
        for hh in range(N_HEADS):
            q = _rms_rows(qc[C_QK * hh:C_QK * (hh + 1)], gain("cqn"))
            q = jnp.concatenate([q[:C_NOPE], _rope_rows(q[C_NOPE:], cosm, sinm, 16)], axis=0) * sc
            qc_ref[0, C_QK * hh:C_QK * (hh + 1), :] = q.astype(BF16)
            kk = jnp.concatenate([kvc[128 * hh:128 * hh + C_NOPE], kr], axis=0)
            kk = _rms_rows(kk, gain("ckn"))
            kk = jnp.concatenate([kk[:C_NOPE], _rope_rows(kk[C_NOPE:], cosm, sinm, 16), zeros32], axis=0)
            kc_ref[0, :, LANES * hh:LANES * (hh + 1)] = kk.T.astype(BF16)
            vcs.append(kvc[128 * hh + C_NOPE:128 * (hh + 1)])
        _store_key_blocks(vc_ref, jnp.concatenate(vcs, axis=0).astype(BF16))

    def prep_d(proj):
        sd = HEAD_DIM ** -0.5 * LOG2E
        qs, ks, vs = [], [], []
        for g in range(2):
            qg, kg = [], []
            for hh in (2 * g, 2 * g + 1):
                qg.append(_rms_rows(proj(O_DQ + 64 * hh, 64), gain("dq")) * sd)
                kg.append(_rms_rows(proj(O_DK + 64 * hh, 64), gain("dk")))
            qs.append(jnp.concatenate(qg, axis=0).T)
            ks.append(jnp.concatenate(kg, axis=0).T)
            vs.append(proj(O_DV + 128 * g, 128).T)
        _store_dilated(scr_ref.at[0], jnp.concatenate(qs, axis=1), (dq1_ref, dq4_ref, dq16_ref))
        _store_dilated(scr_ref.at[1], jnp.concatenate(ks, axis=1), (dk1_ref, dk4_ref, dk16_ref))
        _store_dilated(scr_ref.at[2], jnp.concatenate(vs, axis=1), (dv1_ref, dv4_ref, dv16_ref))

    sec_c = section(O_CQL, O_DQ)
    sec_d = section(O_DQ, IN_WIDTH)
    prep_c(sec_c)
    sec_b = section(O_BQ, O_CQL)
    prep_d(sec_d)
    sec_a = section(O_AQ, O_BQ)
    prep_b(sec_b)
    prep_a(sec_a)


def _project(x, layer, w_in_t, wqb_t, wkvb_t, gains, cosa, sina, cosm, sinm):
    b, s, d = x.shape
    tm = min(PROJ_TILE, s)
    assert s % tm == 0 and tm % KSTEP == 0
    nt = s // tm
    fm = lambda rows: jax.ShapeDtypeStruct((b, rows, s), BF16)
    tmj = lambda cols: jax.ShapeDtypeStruct((b, s, cols), BF16)
    vblk = lambda rows: jax.ShapeDtypeStruct((b, s // KSTEP, rows, KSTEP), BF16)
    dviews = tuple(jax.ShapeDtypeStruct((b, s // dil, dil * D_WIDTH), BF16) for _, dil in DILATED_CONFIGS)
    out_shape = (fm(256), tmj(256), vblk(128),
                 fm(256), tmj(512), vblk(256),
                 fm(384), tmj(512), vblk(256)) + dviews * 3
    fm_spec = lambda rows: pl.BlockSpec((1, rows, tm), lambda i, j: (i, 0, j))
    tm_spec = lambda cols: pl.BlockSpec((1, tm, cols), lambda i, j: (i, j, 0))
    v_spec = lambda rows: pl.BlockSpec((1, tm // KSTEP, rows, KSTEP), lambda i, j: (i, j, 0, 0))
    dspecs = tuple(pl.BlockSpec((1, tm // dil, dil * D_WIDTH), lambda i, j: (i, j, 0)) for _, dil in DILATED_CONFIGS)
    out_specs = (fm_spec(256), tm_spec(256), v_spec(128),
                 fm_spec(256), tm_spec(512), v_spec(256),
                 fm_spec(384), tm_spec(512), v_spec(256)) + dspecs * 3
    whole = lambda i, j: (0, 0)
    in_specs = [
        pl.BlockSpec((1, tm, d), lambda i, j: (i, j, 0)),
        pl.BlockSpec((None, IN_WIDTH, d), lambda i, j: (layer, 0, 0)),
        pl.BlockSpec((N_HEADS * C_QK, C_Q_LORA), whole),
        pl.BlockSpec((N_HEADS * 128, C_KV_LORA), whole),
        pl.BlockSpec((G_TOTAL, 1), whole),
        pl.BlockSpec((64, tm), lambda i, j: (0, j)),
        pl.BlockSpec((64, tm), lambda i, j: (0, j)),
        pl.BlockSpec((32, tm), lambda i, j: (0, j)),
        pl.BlockSpec((32, tm), lambda i, j: (0, j)),
    ]
    return pl.pallas_call(
        _proj_body,
        out_shape=out_shape,
        grid=(b, nt),
        in_specs=in_specs,
        out_specs=out_specs,
        scratch_shapes=[pltpu.VMEM((3, D_WIDTH // LANES, tm, LANES), F32)],
        compiler_params=_cparams(("parallel", "arbitrary")),
        name="project_heads",
    )(x, w_in_t, wqb_t, wkvb_t, gains, cosa, sina, cosm, sinm)


def _online_update(s, v_aug, carry, shift=None):
    m, acc = carry
    smax = jnp.max(s, axis=0, keepdims=True)
    if shift is None:
        m_new = jnp.maximum(m, smax)
        p = jnp.exp2(s - m_new).astype(BF16)
    else:
        m_new = jnp.maximum(m, smax + shift)
        p = jnp.exp2(s - (m_new - shift)).astype(BF16)
    alpha = jnp.exp2(m - m_new)
    acc = alpha * acc + jnp.dot(v_aug, p, preferred_element_type=F32)
    return m_new, acc


def _init_carry(dv, tq):
    return (jnp.full((1, tq), -jnp.inf, F32), jnp.zeros((dv + BF16_ROWS, tq), F32))


def _finish(carry, dv):
    _, acc = carry
    return acc[:dv] / acc[dv:dv + 1]


def _run_pipelined(n_jobs, n_blocks, scores_fn, update_fn, init_fn, finish_fn):
    steps = [(t, i) for t in range(n_jobs) for i in range(n_blocks)]
    pending = [scores_fn(*steps[n]) for n in range(min(SCORES_AHEAD, len(steps)))]
    carry = None
    for n, (t, i) in enumerate(steps):
        s = pending.pop(0)
        if n + SCORES_AHEAD < len(steps):
            pending.append(scores_fn(*steps[n + SCORES_AHEAD]))
        if i == 0:
            carry = init_fn()
        carry = update_fn(t, i, s, carry)
        if i == n_blocks - 1:
            finish_fn(t, carry)


def _dense_body(q_ref, k_ref, v_ref, o_ref):
    dq = q_ref.shape[1]
    nkb, dv, ks = v_ref.shape[1], v_ref.shape[2], v_ref.shape[3]
    tq = TQ
    ones = jnp.ones((BF16_ROWS, ks), BF16)
    zpad = jnp.zeros((LANES - dq, tq), BF16)
    qpads = [jnp.concatenate([q_ref[0, :, t * tq:(t + 1) * tq], zpad], axis=0) for t in range(q_ref.shape[2] // tq)]

    def scores(t, i):
        return jnp.dot(k_ref[0, i * ks:(i + 1) * ks, :], qpads[t], preferred_element_type=F32)

    def update(t, i, s, carry):
        return _online_update(s, jnp.concatenate([v_ref[0, i], ones], axis=0), carry)

    def finish(t, carry):
        o_ref[0, :, t * tq:(t + 1) * tq] = _finish(carry, dv).astype(o_ref.dtype)

    _run_pipelined(len(qpads), nkb, scores, update, lambda: _init_carry(dv, tq), finish)


def _dense_attention(q, k, v, dq, q_per_kv):
    b, rows, s = q.shape
    nh = rows // dq
    nkb, tk = v.shape[1], v.shape[3]
    step = min(DENSE_TQ_STEP, s)
    assert s % step == 0 and step % TQ == 0
    return pl.pallas_call(
        _dense_body,
        out_shape=jax.ShapeDtypeStruct((b, nh * HEAD_DIM, s), BF16),
        grid=(b, nh, s // step),
        in_specs=[
            pl.BlockSpec((1, dq, step), lambda i, h, j: (i, h, j)),
            pl.BlockSpec((1, s, LANES), lambda i, h, j: (i, 0, h // q_per_kv)),
            pl.BlockSpec((1, nkb, HEAD_DIM, tk), lambda i, h, j: (i, 0, h // q_per_kv, 0)),
        ],
        out_specs=pl.BlockSpec((1, HEAD_DIM, step), lambda i, h, j: (i, h, j)),
        compiler_params=_cparams(("parallel", "arbitrary", "arbitrary")),
        name="dense_attention",
    )(q, k, v)


def _diff_body(sc_ref, q_ref, k_ref, v_ref, g_ref, bias_ref, o_ref):
    tq = TQ
    n_tiles = q_ref.shape[2] // tq
    nkb, dv, ks = v_ref.shape[1], v_ref.shape[2], v_ref.shape[3]
    hh = pl.program_id(1)
    lam = sc_ref[0]
    out_scale = sc_ref[1]
    slope = sc_ref[2 + hh]
    z32 = jnp.zeros((B_QK_DIM, tq), BF16)
    z48 = jnp.zeros((48, tq), BF16)
    zfeat = jnp.zeros((BF16_ROWS, tq), BF16)
    ones = jnp.ones((BF16_ROWS, ks), BF16)
    frow = lax.broadcasted_iota(jnp.int32, (BF16_ROWS, tq), 0)
    piece = jnp.where(frow % 3 == 0, LOG2E_PIECES[0], jnp.where(frow % 3 == 1, LOG2E_PIECES[1], LOG2E_PIECES[2]))
    cfeat = jnp.where(frow < 6, piece, 0.0)
    lane_pos = lax.broadcasted_iota(jnp.int32, (1, tq), 1)

    n_mixed = tq // ks

    def tile_index(job):
        return pl.program_id(2) * n_tiles + job // 2

    def block_index(job, ii):
        return lax.rem(tile_index(job) * n_mixed + ii, nkb)

    def side(job, ii):
        return jnp.where(block_index(job, ii) < tile_index(job) * n_mixed, 1.0, -1.0).astype(F32)

    def qrows(job):
        q = q_ref[0, :, (job // 2) * tq:(job // 2 + 1) * tq]
        if job % 2 == 0:
            return jnp.concatenate([q[:B_QK_DIM], z32], axis=0)
        return jnp.concatenate([z32, q[B_QK_DIM:]], axis=0)

    def scores(job, ii):
        i = block_index(job, ii)
        k = k_ref[0, pl.ds(pl.multiple_of(i * ks, ks), ks), :]
        feat = zfeat if ii < n_mixed else (cfeat * (side(job, ii) * slope)).astype(BF16)
        return jnp.dot(k, jnp.concatenate([qrows(job), feat, z48], axis=0), preferred_element_type=F32)

    def update(job, ii, s, carry):
        v_aug = jnp.concatenate([v_ref[0, block_index(job, ii)], ones], axis=0)
        if ii < n_mixed:
            return _online_update(s - bias_ref[0, ii], v_aug, carry)
        qpos = (tile_index(job) * tq + lane_pos).astype(F32)
        return _online_update(s, v_aug, carry, qpos * (-(side(job, ii) * slope * LOG2E)))

    res = []

    def finish(job, carry):
        res.append(_finish(carry, dv))
        if job % 2 == 1:
            o = res[job - 1] - lam * res[job]
            t = job // 2
            o_ref[0, :, t * tq:(t + 1) * tq] = (_rms_rows(o, g_ref[...]) * out_scale).astype(o_ref.dtype)

    _run_pipelined(2 * n_tiles, nkb, scores, update, lambda: _init_carry(dv, tq), finish)


def _diff_attention(scal, slopes, q, k, v, sub_g):
    b, rows, s = q.shape
    nkb, tk = v.shape[1], v.shape[3]
    n_mixed = TQ // tk
    jj = jnp.arange(tk)[None, :, None] + tk * jnp.arange(n_mixed)[:, None, None]
    dist = jnp.abs(jj - jnp.arange(TQ)[None, None, :]).astype(F32)
    bias = dist[None] * (LOG2E * jnp.asarray(slopes, F32))[:, None, None, None]
    step = min(TQ_STEP, s)
    assert s % step == 0 and step % TQ == 0
    return pl.pallas_call(
        _diff_body,
        out_shape=jax.ShapeDtypeStruct((b, rows, s), BF16),
        grid=(b, N_HEADS, s // step),
        in_specs=[
            pl.BlockSpec(memory_space=pltpu.SMEM),
            pl.BlockSpec((1, 2 * B_QK_DIM, step), lambda i, h, j: (i, h, j)),
            pl.BlockSpec((1, s, LANES), lambda i, h, j: (i, 0, h)),
            pl.BlockSpec((1, nkb, HEAD_DIM, tk), lambda i, h, j: (i, 0, h, 0)),
            pl.BlockSpec((HEAD_DIM, 1), lambda i, h, j: (0, 0)),
            pl.BlockSpec((1, n_mixed, tk, TQ), lambda i, h, j: (h, 0, 0, 0)),
        ],
        out_specs=pl.BlockSpec((1, HEAD_DIM, step), lambda i, h, j: (i, h, j)),
        compiler_params=_cparams(("parallel", "arbitrary", "arbitrary")),
        name="diff_attention",
    )(scal, q, k, v, sub_g, bias)


def _band_body(q_ref, k_ref, v_ref, pen_ref, o_ref, lse_ref, *, qb):
    l_len = k_ref.shape[1]
    win = pen_ref.shape[3]
    lane = lax.broadcasted_iota(jnp.int32, (qb, LANES), 1)
    n_sub = q_ref.shape[1] // qb
    n_groups = q_ref.shape[2] // LANES
    for sb in range(n_sub):
        jb = pl.program_id(2) * n_sub + sb
        start = jnp.clip(jb * qb - BAND_RADIUS, 0, l_len - win)
        start = pl.multiple_of(start, BAND_RADIUS)
        variant = (jb * qb - start) // BAND_RADIUS
        q = q_ref[0, sb * qb:(sb + 1) * qb, :]
        kwin = k_ref[0, pl.ds(start, win), :]
        vwin = v_ref[0, pl.ds(start, win), :]
        for gc in range(n_groups):
            g = gc % 2
            q2 = q[:, LANES * gc:LANES * (gc + 1)].astype(F32)
            k2 = kwin[:, LANES * gc:LANES * (gc + 1)]
            v2 = vwin[:, LANES * gc:LANES * (gc + 1)]
            out_g = jnp.zeros((qb, LANES), F32)
            lse_g = jnp.zeros((qb, LANES), F32)
            for hh in range(2):
                own = (lane >= HEAD_DIM * hh) & (lane < HEAD_DIM * (hh + 1))
                qm = jnp.where(own, q2, 0.0).astype(BF16)
                s = lax.dot_general(qm, k2, (((1,), (1,)), ((), ())), preferred_element_type=F32)
                s = s - pen_ref[variant, 2 * g + hh]
                m = jnp.max(s, axis=1, keepdims=True)
                p = jnp.exp2(s - m)
                lsum = jnp.sum(p, axis=1, keepdims=True)
                o = jnp.dot(p.astype(BF16), v2, preferred_element_type=F32) / lsum
                lse = m + jnp.log2(lsum)
                out_g = jnp.where(own, o, out_g)
                lse_g = jnp.where(own, lse, lse_g)
            o_ref[0, sb * qb:(sb + 1) * qb, LANES * gc:LANES * (gc + 1)] = out_g.astype(o_ref.dtype)
            lse_ref[0, sb * qb:(sb + 1) * qb, LANES * gc:LANES * (gc + 1)] = lse_g


def _band_penalty(qb, win, dil, slopes):
    rows = jnp.arange(qb)[None, :, None] + BAND_RADIUS * jnp.arange(3)[:, None, None]
    rel = jnp.abs(rows - jnp.arange(win)[None, None, :])
    scale = jnp.asarray(slopes, F32) * (float(dil) * LOG2E)
    pen = rel.astype(F32)[:, None] * scale[None, :, None, None]
    return jnp.where((rel <= BAND_RADIUS)[:, None], pen, BAND_MASKED)


def _band_attention(q, k, v, dil, slopes):
    b, l_len, cw = q.shape
    c = cw // dil
    qb = min(BAND_Q, l_len)
    qstep = min(BAND_QSTEP, l_len)
    cps = min(dil, max(1, BAND_QSTEP // l_len))
    assert dil % cps == 0
    win = min(qb + 2 * BAND_RADIUS, l_len)
    pen = _band_penalty(qb, win, dil, slopes)
    body = functools.partial(_band_body, qb=qb)
    return pl.pallas_call(
        body,
        out_shape=(jax.ShapeDtypeStruct((b, l_len, cw), BF16), jax.ShapeDtypeStruct((b, l_len, cw), F32)),
        grid=(b, dil // cps, l_len // qstep),
        in_specs=[
            pl.BlockSpec((1, qstep, c * cps), lambda i, r, j: (i, j, r)),
            pl.BlockSpec((1, l_len, c * cps), lambda i, r, j: (i, 0, r)),
            pl.BlockSpec((1, l_len, c * cps), lambda i, r, j: (i, 0, r)),
            pl.BlockSpec(pen.shape, lambda i, r, j: (0, 0, 0, 0)),
        ],
        out_specs=(pl.BlockSpec((1, qstep, c * cps), lambda i, r, j: (i, j, r)),) * 2,
        compiler_params=_cparams(("parallel", "arbitrary", "arbitrary")),
        name="band_attention_d%d" % dil,
    )(q, k, v, pen)


def _band_combine(in_refs, scr_ref, tm, c):
    vals = []
    for idx, (_, dil) in enumerate(DILATED_CONFIGS):
        for which in range(2):
            src = in_refs[2 * idx + which]
            if dil == 1:
                vals.append(src[0].astype(F32))
                continue
            groups = []
            for g in range(c // LANES):
                slot = scr_ref.at[(2 * idx + which) * (c // LANES) + g]
                for r in range(dil):
                    lo = c * r + LANES * g
                    slot[pl.ds(r, tm // dil, stride=dil), :] = src[0, :, lo:lo + LANES].astype(F32)
                groups.append(slot[...])
            vals.append(jnp.concatenate(groups, axis=1))
    outs, lses = vals[0::2], vals[1::2]
    m = functools.reduce(jnp.maximum, lses)
    ws = [jnp.exp2(x - m) for x in lses]
    num = functools.reduce(lambda a, t: a + t, [w * o for w, o in zip(ws, outs)])
    den = functools.reduce(lambda a, t: a + t, ws)
    return num / den


def _outproj_body(*refs):
    n_d = 2 * len(DILATED_CONFIGS)
    x_ref, oa_ref, ob_ref, oc_ref = refs[:4]
    d_refs = refs[4:4 + n_d]
    w_ref, g_ref, x1_ref, h2_ref, edge_ref, scr_ref = refs[4 + n_d:]
    tm = x_ref.shape[1]
    j = pl.program_id(1)
    mix = jnp.concatenate([oa_ref[0], ob_ref[0], oc_ref[0]], axis=0)
    od = _band_combine(d_refs, scr_ref, tm, D_WIDTH).astype(BF16)
    w = w_ref[...]
    y = jnp.dot(w[:, :768], mix, preferred_element_type=F32)
    y = y + lax.dot_general(w[:, 768:], od, (((1,), (1,)), ((), ())), preferred_element_type=F32)
    x1 = x_ref[0] + y.T
    x1_ref[0] = x1
    ms = jnp.mean(x1 * x1, axis=1, keepdims=True)
    h2 = (x1 * lax.rsqrt(ms + NORM_EPS) * g_ref[...]).astype(BF16)
    h2_ref[0] = h2
    edge_ref[0, pl.ds(2 * j, 1), :] = h2[0:1].astype(F32)
    edge_ref[0, pl.ds(2 * j + 1, 1), :] = h2[tm - 1:tm].astype(F32)


def _out_project(x, oa, ob, oc, branches, layer, w_out_t, g2):
    b, s, d = x.shape
    tm = TOK_TILE
    nt = s // tm
    fm = lambda rows: pl.BlockSpec((1, rows, tm), lambda i, j: (i, 0, j))
    tmj = pl.BlockSpec((1, tm, d), lambda i, j: (i, j, 0))
    d_specs = []
    for (_, dil) in DILATED_CONFIGS:
        d_specs += [pl.BlockSpec((1, tm // dil, dil * D_WIDTH), lambda i, j: (i, j, 0))] * 2
    flat = [a for pair in branches for a in pair]
    return pl.pallas_call(
        _outproj_body,
        out_shape=(jax.ShapeDtypeStruct((b, s, d), F32), jax.ShapeDtypeStruct((b, s, d), BF16),
                   jax.ShapeDtypeStruct((b, 2 * nt, d), F32)),
        grid=(b, nt),
        in_specs=[tmj, fm(256), fm(256), fm(256)] + d_specs + [
            pl.BlockSpec((None, d, d), lambda i, j: (layer, 0, 0)),
            pl.BlockSpec((1, d), lambda i, j: (0, 0))],
        out_specs=(tmj, tmj, pl.BlockSpec((1, 2 * nt, d), lambda i, j: (i, 0, 0))),
        scratch_shapes=[pltpu.VMEM((2 * len(DILATED_CONFIGS) * (D_WIDTH // LANES), tm, LANES), F32)],
        compiler_params=_cparams(("parallel", "arbitrary")),
        name="out_project",
    )(x, oa, ob, oc, *flat, w_out_t, g2)


def _halo_body(h_ref, w_ref, o_ref):
    o_ref[...] = jnp.dot(h_ref[...], w_ref[...], preferred_element_type=F32)


def _halo_up(halo_h, layer, w_up):
    nh, d = halo_h.shape
    ff2 = w_up.shape[2]
    n_blk = 4
    cols = ff2 // n_blk
    return pl.pallas_call(
        _halo_body,
        out_shape=jax.ShapeDtypeStruct((nh, ff2), F32),
        grid=(n_blk,),
        in_specs=[pl.BlockSpec((nh, d), lambda i: (0, 0)),
                  pl.BlockSpec((None, d, cols), lambda i: (layer, 0, i))],
        out_specs=pl.BlockSpec((nh, cols), lambda i: (0, i)),
        compiler_params=_cparams(("arbitrary",)),
        name="mlp_halo_up",
    )(halo_h, w_up)


def _mlp_body(x1_ref, h2_ref, wup_ref, wdn_ref, par_ref, halo_ref, o_ref, gbuf_ref, vbuf_ref, *, chunks):
    tf = h2_ref.shape[1]
    ff = wdn_ref.shape[0]
    nt = pl.num_programs(1)
    row = (pl.program_id(0) * nt + pl.program_id(1)) * 2
    h = h2_ref[0]

    def conv(buf_ref, u, c0, width):
        buf_ref[MLP_PAD:MLP_PAD + tf, :width] = u
        buf_ref[MLP_PAD - 1:MLP_PAD, :width] = halo_ref[pl.ds(row, 1), c0:c0 + width]
        buf_ref[MLP_PAD + tf:MLP_PAD + tf + 1, :width] = halo_ref[pl.ds(row + 1, 1), c0:c0 + width]
        prev = buf_ref[MLP_PAD - 1:MLP_PAD - 1 + tf, :width]
        nxt = buf_ref[MLP_PAD + 1:MLP_PAD + 1 + tf, :width]
        p = par_ref[:, c0:c0 + width]
        return p[0:1] * prev + p[1:2] * u + p[2:3] * nxt + p[3:4]

    def up(c0, width):
        return (jnp.dot(h, wup_ref[:, c0:c0 + width], preferred_element_type=F32),
                jnp.dot(h, wup_ref[:, ff + c0:ff + c0 + width], preferred_element_type=F32))

    y = x1_ref[0]
    nxt_up = up(*chunks[0])
    for n, (c0, width) in enumerate(chunks):
        ug, uv = nxt_up
        if n + 1 < len(chunks):
            nxt_up = up(*chunks[n + 1])
        gate = conv(gbuf_ref, ug, c0, width)
        val = conv(vbuf_ref, uv, ff + c0, width)
        act = gate / (1.0 + jnp.exp(-gate)) * val
        y = y + jnp.dot(act.astype(BF16), wdn_ref[c0:c0 + width, :], preferred_element_type=F32)
    o_ref[0] = y


def _mlp(x1, h2, layer, w_up, w_down, conv_par, halo_up, chunks):
    b, s, d = x1.shape
    tf = TOK_TILE
    ff2 = w_up.shape[2]
    nh = halo_up.shape[0]
    whole = lambda i, j: (0, 0)
    body = functools.partial(_mlp_body, chunks=chunks)
    const = dict(pipeline_mode=pl.Buffered(1))
    wmax = max(w for _, w in chunks)
    return pl.pallas_call(
        body,
        out_shape=jax.ShapeDtypeStruct((b, s, d), F32),
        grid=(b, s // tf),
        in_specs=[pl.BlockSpec((1, tf, d), lambda i, j: (i, j, 0)),
                  pl.BlockSpec((1, tf, d), lambda i, j: (i, j, 0)),
                  pl.BlockSpec((None, d, ff2), lambda i, j: (layer, 0, 0), **const),
                  pl.BlockSpec((None, ff2 // 2, d), lambda i, j: (layer, 0, 0), **const),
                  pl.BlockSpec((8, ff2), whole, **const),
                  pl.BlockSpec((nh, ff2), whole, **const)],
        out_specs=pl.BlockSpec((1, tf, d), lambda i, j: (i, j, 0)),
        scratch_shapes=[pltpu.VMEM((tf + 2 * MLP_PAD, wmax), F32)] * 2,
        compiler_params=_cparams(("arbitrary", "arbitrary")),
        name="gated_conv_mlp",
    )(x1, h2, w_up, w_down, conv_par, halo_up)


def _halo_rows(edges, nt):
    b, _, d = edges.shape
    ez = jnp.concatenate([edges, jnp.zeros((b, 1, d), edges.dtype)], axis=1)
    idx = []
    for j in range(nt):
        idx += [2 * j - 1 if j > 0 else 2 * nt, 2 * j + 2 if j < nt - 1 else 2 * nt]
    rows = jnp.take(ez, jnp.asarray(idx, jnp.int32), axis=1).reshape(b * nt * 2, d)
    pad = (-rows.shape[0]) % BF16_ROWS
    return jnp.pad(rows, ((0, pad), (0, 0))).astype(BF16)


def _rope_tables(s):
    def freqs(dim):
        return 1.0 / (ROPE_THETA ** (jnp.arange(0, dim, 2, dtype=F32) / dim))
    t = jnp.arange(s)
    rows = (t // GRID_W).astype(F32)
    cols = (t % GRID_W).astype(F32)
    pos = t.astype(F32)
    fa = freqs(HEAD_DIM // 2)
    ar, ac = fa[:, None] * rows[None, :], fa[:, None] * cols[None, :]
    am = freqs(C_ROPE)[:, None] * pos[None, :]
    cosa = jnp.concatenate([jnp.cos(ar), jnp.cos(ar), jnp.cos(ac), jnp.cos(ac)], axis=0)
    sina = jnp.concatenate([-jnp.sin(ar), jnp.sin(ar), -jnp.sin(ac), jnp.sin(ac)], axis=0)
    cosm = jnp.concatenate([jnp.cos(am), jnp.cos(am)], axis=0)
    sinm = jnp.concatenate([-jnp.sin(am), jnp.sin(am)], axis=0)
    return cosa, sina, cosm, sinm


def kernel(x, norm1_g, w_in, a_qn_g, a_kn_g, b_qn_g, b_kn_g, b_lam_q1, b_lam_k1, b_lam_q2, b_lam_k2, b_sub_g,
           c_qa_g, c_kva_g, c_wqb, c_wkvb, c_qn_g, c_kn_g, d_qn_g, d_kn_g, w_out, norm2_g, w_up, conv_w, conv_b,
           w_down):
    b, s, d = x.shape
    depth = w_in.shape[0]
    ff = w_down.shape[1]
    nt = s // TOK_TILE
    assert s % TOK_TILE == 0 and TOK_TILE % KSTEP == 0 and TQ % KSTEP == 0 and s % TQ == 0 and s % GRID_W == 0
    assert s <= 1 << (POS_SPLIT_BITS + BF16_MANTISSA_BITS)
    for window, dil in DILATED_CONFIGS:
        assert window // (2 * dil) == BAND_RADIUS and (s // dil) % min(BAND_Q, s // dil) == 0
        assert (s // dil) % min(BAND_QSTEP, s // dil) == 0 and BAND_QSTEP % BAND_Q == 0
        assert TOK_TILE % (dil * BF16_ROWS) == 0

    tr = lambda w: jnp.swapaxes(w, 1, 2).astype(BF16)
    gains = jnp.concatenate([norm1_g, a_qn_g, a_kn_g, b_qn_g, b_kn_g, c_qa_g, c_kva_g, c_qn_g, c_kn_g,
                             d_qn_g, d_kn_g], axis=1)[:, :, None]
    conv_par = jnp.concatenate([conv_w, conv_b[:, None, :], jnp.zeros((depth, 4, 2 * ff), F32)], axis=1)
    slopes = [2.0 ** (-8.0 * i / (2 * N_HEADS)) for i in range(1, 2 * N_HEADS + 1)]
    slopes_b, slopes_d = slopes[:N_HEADS], tuple(slopes[N_HEADS:])
    lam_init = jnp.asarray([0.8 - 0.6 * math.exp(-0.3 * l) for l in range(depth)], F32)
    lam = (jnp.exp(jnp.sum(b_lam_q1 * b_lam_k1, axis=1)) - jnp.exp(jnp.sum(b_lam_q2 * b_lam_k2, axis=1)) + lam_init)
    scal = jnp.concatenate([lam[:, None], (1.0 - lam_init)[:, None],
                            jnp.broadcast_to(jnp.asarray(slopes_b, F32), (depth, N_HEADS)),
                            jnp.zeros((depth, 2), F32)], axis=1)
    w_in_t, w_out_t, w_up_b, w_down_b = tr(w_in), tr(w_out), w_up.astype(BF16), w_down.astype(BF16)
    wqb_t, wkvb_t = tr(c_wqb), tr(c_wkvb)
    g2, sub_g = norm2_g[:, None, :], b_sub_g[:, :, None]
    cosa, sina, cosm, sinm = _rope_tables(s)
    assert ff % LANES == 0
    chunks = tuple((c0, min(MLP_CHUNK, ff - c0)) for c0 in range(0, ff, MLP_CHUNK))

    out = x
    for l in range(depth):
        (qa, ka, va, qb, kb, vb, qc, kc, vc, dq1, dq4, dq16, dk1, dk4, dk16, dv1, dv4, dv16) = _project(
            out, l, w_in_t, wqb_t[l], wkvb_t[l], gains[l], cosa, sina, cosm, sinm)
        oa = _dense_attention(qa, ka, va, HEAD_DIM, N_HEADS // A_KV_HEADS)
        ob = _diff_attention(scal[l], slopes_b, qb, kb, vb, sub_g[l])
        oc = _dense_attention(qc, kc, vc, C_QK, 1)
        branches = [_band_attention(q_, k_, v_, dil, slopes_d)
                    for (_, dil), q_, k_, v_ in zip(DILATED_CONFIGS, (dq1, dq4, dq16), (dk1, dk4, dk16), (dv1, dv4, dv16))]
        x1, h2, edges = _out_project(out, oa, ob, oc, branches, l, w_out_t, g2[l])
        halo = _halo_up(_halo_rows(edges, nt), l, w_up_b)
        out = _mlp(x1, h2, l, w_up_b, w_down_b, conv_par[l], halo, chunks)
    return out
```

```python
import functools
import math

import jax
import jax.numpy as jnp
import numpy as np
from jax import lax
from jax.experimental import pallas as pl
from jax.experimental.pallas import tpu as pltpu

F32 = jnp.float32
BF16 = jnp.bfloat16

HEAD_DIM = 64
N_HEADS = 4
A_KV_HEADS = 2
B_QK_DIM = 32
C_Q_LORA = 256
C_KV_LORA = 128
C_NOPE = 64
C_ROPE = 32
C_QK = C_NOPE + C_ROPE
DILATED_CONFIGS = ((128, 1), (512, 4), (2048, 16))
D_WIDTH = N_HEADS * HEAD_DIM
GRID_W = 64
ROPE_THETA = 10000.0
NORM_EPS = 1e-6
LOG2E = 1.4426950408889634


def _bf16_pieces(x, n):
    out = []
    for _ in range(n):
        u = np.float32(x).view(np.uint32)
        u = np.uint32((int(u) + 0x7FFF + ((int(u) >> 16) & 1)) & 0xFFFF0000)
        piece = float(u.view(np.float32))
        out.append(piece)
        x = x - piece
    return tuple(out)


LOG2E_PIECES = _bf16_pieces(LOG2E, 3)
POS_SPLIT_BITS = 6
BF16_MANTISSA_BITS = 8

_SPLITS = (256, 128, 128, 256, 256, 256, C_Q_LORA, C_KV_LORA, C_ROPE, 256, 256, 256)
_OFF = [0]
for _n in _SPLITS:
    _OFF.append(_OFF[-1] + _n)
(O_AQ, O_AK, O_AV, O_BQ, O_BK, O_BV, O_CQL, O_CKVL, O_CKR, O_DQ, O_DK, O_DV, IN_WIDTH) = _OFF

_GAIN_SIZES = (("n1", 1024), ("aq", 64), ("ak", 64), ("bq", 32), ("bk", 32), ("cqa", 256), ("ckva", 128),
               ("cqn", 96), ("ckn", 96), ("dq", 64), ("dk", 64))
G_OFF = {}
_o = 0
for _k, _n in _GAIN_SIZES:
    G_OFF[_k] = (_o, _n)
    _o += _n
G_TOTAL = _o

LANES = 128
BF16_ROWS = 16
TOK_TILE = 512
PROJ_TILE = 1024
OUT_TILE = 1024
MLP_CHUNK = 1536
MLP_PAD = 8
TQ = 512
TQ_STEP = 2048
DENSE_TQ_STEP = 4096
KSTEP = 256
SCORES_AHEAD = 2
BAND_Q = 128
BAND_QSTEP = 2048
BAND_MASKED = 1e30
BAND_RADIUS = 64
V7X_VMEM_BYTES = 64 * 1024 * 1024
VMEM_LIMIT = V7X_VMEM_BYTES * 7 // 8


def _cparams(sem):
    return pltpu.CompilerParams(dimension_semantics=sem, vmem_limit_bytes=VMEM_LIMIT)


def _rms_rows(v, g_col):
    ms = jnp.mean(v * v, axis=0, keepdims=True)
    return v * lax.rsqrt(ms + NORM_EPS) * g_col


def _swap_halves(x, n):
    parts = []
    for i in range(0, x.shape[0], 2 * n):
        parts.append(x[i + n:i + 2 * n])
        parts.append(x[i:i + n])
    return jnp.concatenate(parts, axis=0)


def _rope_rows(x, cos, sin_signed, n):
    return x * cos + _swap_halves(x, n) * sin_signed


def _store_dilated(scr_ref, val, out_refs):
    tm, c = val.shape
    for g in range(c // LANES):
        scr_ref[g] = val[:, LANES * g:LANES * (g + 1)]
    for (_, dil), out in zip(DILATED_CONFIGS, out_refs):
        if dil == 1:
            out[0] = val.astype(BF16)
            continue
        n = tm // dil
        for r in range(dil):
            for g in range(c // LANES):
                lo = c * r + LANES * g
                out[0, :, lo:lo + LANES] = scr_ref[g, pl.ds(r, n, stride=dil), :].astype(BF16)


def _store_key_blocks(v_ref, v):
    for c in range(v.shape[1] // KSTEP):
        v_ref[0, c] = v[:, c * KSTEP:(c + 1) * KSTEP]


def _proj_body(x_ref, w_ref, wqb_ref, wkvb_ref, g_ref, cosa_ref, sina_ref, cosm_ref, sinm_ref,
               qa_ref, ka_ref, va_ref, qb_ref, kb_ref, vb_ref, qc_ref, kc_ref, vc_ref,
               dq1_ref, dq4_ref, dq16_ref, dk1_ref, dk4_ref, dk16_ref, dv1_ref, dv4_ref, dv16_ref,
               scr_ref):
    tm = x_ref.shape[1]

    def gain(name):
        o, n = G_OFF[name]
        return g_ref[o:o + n, :]

    x = x_ref[0].T
    h = (x * gain("n1")).astype(BF16)
    r = lax.rsqrt(jnp.mean(x * x, axis=0, keepdims=True) + NORM_EPS)

    def section(lo, hi):
        part = jnp.dot(w_ref[lo:hi, :], h, preferred_element_type=F32) * r
        return lambda o, n: part[o - lo:o - lo + n]

    cosa, sina = cosa_ref[...], sina_ref[...]
    cosm, sinm = cosm_ref[...], sinm_ref[...]
    zeros64 = jnp.zeros((64, tm), F32)
    zeros32 = jnp.zeros((32, tm), F32)

    def prep_a(proj):
        sa = HEAD_DIM ** -0.5 * LOG2E
        for hh in range(N_HEADS):
            q = _rope_rows(_rms_rows(proj(O_AQ + 64 * hh, 64), gain("aq")), cosa, sina, 16) * sa
            qa_ref[0, 64 * hh:64 * hh + 64, :] = q.astype(BF16)
        for g in range(A_KV_HEADS):
            k = _rope_rows(_rms_rows(proj(O_AK + 64 * g, 64), gain("ak")), cosa, sina, 16)
            kpad = jnp.concatenate([k, zeros64], axis=0)
            ka_ref[0, :, LANES * g:LANES * (g + 1)] = kpad.T.astype(BF16)
        _store_key_blocks(va_ref, proj(O_AV, 128).astype(BF16))

    def prep_b(proj):
        sb = B_QK_DIM ** -0.5 * LOG2E
        for i in range(2 * N_HEADS):
            qb_ref[0, 32 * i:32 * i + 32, :] = (_rms_rows(proj(O_BQ + 32 * i, 32), gain("bq")) * sb).astype(BF16)
        kpos = pl.program_id(1) * tm + lax.broadcasted_iota(jnp.int32, (BF16_ROWS, tm), 1)
        frow = lax.broadcasted_iota(jnp.int32, (BF16_ROWS, tm), 0)
        pos_hi = ((kpos >> POS_SPLIT_BITS) << POS_SPLIT_BITS).astype(F32)
        pos_lo = (kpos & ((1 << POS_SPLIT_BITS) - 1)).astype(F32)
        kfeat = jnp.where(frow < 3, pos_hi, jnp.where(frow < 6, pos_lo, 0.0))
        zeros48 = jnp.zeros((48, tm), F32)
        for hh in range(N_HEADS):
            k1 = _rms_rows(proj(O_BK + 64 * hh, 32), gain("bk"))
            k2 = _rms_rows(proj(O_BK + 64 * hh + 32, 32), gain("bk"))
            kpad = jnp.concatenate([k1, k2, kfeat, zeros48], axis=0)
            kb_ref[0, :, LANES * hh:LANES * (hh + 1)] = kpad.T.astype(BF16)
        _store_key_blocks(vb_ref, proj(O_BV, 256).astype(BF16))

    def prep_c(proj):
        sc = C_QK ** -0.5 * LOG2E
        cq = _rms_rows(proj(O_CQL, C_Q_LORA), gain("cqa")).astype(BF16)
        qc = jnp.dot(wqb_ref[...], cq, preferred_element_type=F32)
        ckv = _rms_rows(proj(O_CKVL, C_KV_LORA), gain("ckva")).astype(BF16)
        kvc = jnp.dot(wkvb_ref[...], ckv, preferred_element_type=F32)
        kr = proj(O_CKR, C_ROPE)
        vcs = []
        for hh in range(N_HEADS):
            q = _rms_rows(qc[C_QK * hh:C_QK * (hh + 1)], gain("cqn"))
            q = jnp.concatenate([q[:C_NOPE], _rope_rows(q[C_NOPE:], cosm, sinm, 16)], axis=0) * sc
            qc_ref[0, C_QK * hh:C_QK * (hh + 1), :] = q.astype(BF16)
            kk = jnp.concatenate([kvc[128 * hh:128 * hh + C_NOPE], kr], axis=0)
            kk = _rms_rows(kk, gain("ckn"))
            kk = jnp.concatenate([kk[:C_NOPE], _rope_rows(kk[C_NOPE:], cosm, sinm, 16), zeros32], axis=0)
            kc_ref[0, :, LANES * hh:LANES * (hh + 1)] = kk.T.astype(BF16)
            vcs.append(kvc[128 * hh + C_NOPE:128 * (hh + 1)])
        _store_key_blocks(vc_ref, jnp.concatenate(vcs, axis=0).astype(BF16))

    def prep_d(proj):
        sd = HEAD_DIM ** -0.5 * LOG2E
        qs, ks, vs = [], [], []
        for g in range(2):
            qg, kg = [], []
            for hh in (2 * g, 2 * g + 1):
                qg.append(_rms_rows(proj(O_DQ + 64 * hh, 64), gain("dq")) * sd)
                kg.append(_rms_rows(proj(O_DK + 64 * hh, 64), gain("dk")))
            qs.append(jnp.concatenate(qg, axis=0).T)
            ks.append(jnp.concatenate(kg, axis=0).T)
            vs.append(proj(O_DV + 128 * g, 128).T)
        _store_dilated(scr_ref.at[0], jnp.concatenate(qs, axis=1), (dq1_ref, dq4_ref, dq16_ref))
        _store_dilated(scr_ref.at[1], jnp.concatenate(ks, axis=1), (dk1_ref, dk4_ref, dk16_ref))
        _store_dilated(scr_ref.at[2], jnp.concatenate(vs, axis=1), (dv1_ref, dv4_ref, dv16_ref))

    sec_c = section(O_CQL, O_DQ)
    sec_d = section(O_DQ, IN_WIDTH)
    prep_c(sec_c)
    sec_b = section(O_BQ, O_CQL)
    prep_d(sec_d)
    sec_a = section(O_AQ, O_BQ)
    prep_b(sec_b)
    prep_a(sec_a)


def _project(x, layer, w_in_t, wqb_t, wkvb_t, gains, cosa, sina, cosm, sinm):
    b, s, d = x.shape
    tm = min(PROJ_TILE, s)
    assert s % tm == 0 and tm % KSTEP == 0
    nt = s // tm
    fm = lambda rows: jax.ShapeDtypeStruct((b, rows, s), BF16)
    tmj = lambda cols: jax.ShapeDtypeStruct((b, s, cols), BF16)
    vblk = lambda rows: jax.ShapeDtypeStruct((b, s // KSTEP, rows, KSTEP), BF16)
    dviews = tuple(jax.ShapeDtypeStruct((b, s // dil, dil * D_WIDTH), BF16) for _, dil in DILATED_CONFIGS)
    out_shape = (fm(256), tmj(256), vblk(128),
                 fm(256), tmj(512), vblk(256),
                 fm(384), tmj(512), vblk(256)) + dviews * 3
    fm_spec = lambda rows: pl.BlockSpec((1, rows, tm), lambda i, j: (i, 0, j))
    tm_spec = lambda cols: pl.BlockSpec((1, tm, cols), lambda i, j: (i, j, 0))
    v_spec = lambda rows: pl.BlockSpec((1, tm // KSTEP, rows, KSTEP), lambda i, j: (i, j, 0, 0))
    dspecs = tuple(pl.BlockSpec((1, tm // dil, dil * D_WIDTH), lambda i, j: (i, j, 0)) for _, dil in DILATED_CONFIGS)
    out_specs = (fm_spec(256), tm_spec(256), v_spec(128),
                 fm_spec(256), tm_spec(512), v_spec(256),
                 fm_spec(384), tm_spec(512), v_spec(256)) + dspecs * 3
    whole = lambda i, j: (0, 0)
    in_specs = [
        pl.BlockSpec((1, tm, d), lambda i, j: (i, j, 0)),
        pl.BlockSpec((None, IN_WIDTH, d), lambda i, j: (layer, 0, 0)),
        pl.BlockSpec((N_HEADS * C_QK, C_Q_LORA), whole),
        pl.BlockSpec((N_HEADS * 128, C_KV_LORA), whole),
        pl.BlockSpec((G_TOTAL, 1), whole),
        pl.BlockSpec((64, tm), lambda i, j: (0, j)),
        pl.BlockSpec((64, tm), lambda i, j: (0, j)),
        pl.BlockSpec((32, tm), lambda i, j: (0, j)),
        pl.BlockSpec((32, tm), lambda i, j: (0, j)),
    ]
    return pl.pallas_call(
        _proj_body,
        out_shape=out_shape,
        grid=(b, nt),
        in_specs=in_specs,
        out_specs=out_specs,
        scratch_shapes=[pltpu.VMEM((3, D_WIDTH // LANES, tm, LANES), F32)],
        compiler_params=_cparams(("parallel", "arbitrary")),
        name="project_heads",
    )(x, w_in_t, wqb_t, wkvb_t, gains, cosa, sina, cosm, sinm)


def _online_update(s, v_aug, carry, shift=None):
    m, acc = carry
    smax = jnp.max(s, axis=0, keepdims=True)
    if shift is None:
        m_new = jnp.maximum(m, smax)
        p = jnp.exp2(s - m_new).astype(BF16)
    else:
        m_new = jnp.maximum(m, smax + shift)
        p = jnp.exp2(s - (m_new - shift)).astype(BF16)
    alpha = jnp.exp2(m - m_new)
    acc = alpha * acc + jnp.dot(v_aug, p, preferred_element_type=F32)
    return m_new, acc


def _init_carry(dv, tq):
    return (jnp.full((1, tq), -jnp.inf, F32), jnp.zeros((dv + BF16_ROWS, tq), F32))


def _finish(carry, dv):
    _, acc = carry
    return acc[:dv] / acc[dv:dv + 1]


def _run_pipelined(n_jobs, n_blocks, scores_fn, update_fn, init_fn, finish_fn):
    steps = [(t, i) for t in range(n_jobs) for i in range(n_blocks)]
    pending = [scores_fn(*steps[n]) for n in range(min(SCORES_AHEAD, len(steps)))]
    carry = None
    for n, (t, i) in enumerate(steps):
        s = pending.pop(0)
        if n + SCORES_AHEAD < len(steps):
            pending.append(scores_fn(*steps[n + SCORES_AHEAD]))
        if i == 0:
            carry = init_fn()
        carry = update_fn(t, i, s, carry)
        if i == n_blocks - 1:
            finish_fn(t, carry)


def _dense_body(q_ref, k_ref, v_ref, o_ref):
    dq = q_ref.shape[1]
    nkb, dv, ks = v_ref.shape[1], v_ref.shape[2], v_ref.shape[3]
    tq = TQ
    ones = jnp.ones((BF16_ROWS, ks), BF16)
    zpad = jnp.zeros((LANES - dq, tq), BF16)
    qpads = [jnp.concatenate([q_ref[0, :, t * tq:(t + 1) * tq], zpad], axis=0) for t in range(q_ref.shape[2] // tq)]

    def scores(t, i):
        return jnp.dot(k_ref[0, i * ks:(i + 1) * ks, :], qpads[t], preferred_element_type=F32)

    def update(t, i, s, carry):
        return _online_update(s, jnp.concatenate([v_ref[0, i], ones], axis=0), carry)

    def finish(t, carry):
        o_ref[0, :, t * tq:(t + 1) * tq] = _finish(carry, dv).astype(o_ref.dtype)

    _run_pipelined(len(qpads), nkb, scores, update, lambda: _init_carry(dv, tq), finish)


def _dense_attention(q, k, v, dq, q_per_kv):
    b, rows, s = q.shape
    nh = rows // dq
    nkb, tk = v.shape[1], v.shape[3]
    step = min(DENSE_TQ_STEP, s)
    assert s % step == 0 and step % TQ == 0
    return pl.pallas_call(
        _dense_body,
        out_shape=jax.ShapeDtypeStruct((b, nh * HEAD_DIM, s), BF16),
        grid=(b, nh, s // step),
        in_specs=[
            pl.BlockSpec((1, dq, step), lambda i, h, j: (i, h, j)),
            pl.BlockSpec((1, s, LANES), lambda i, h, j: (i, 0, h // q_per_kv)),
            pl.BlockSpec((1, nkb, HEAD_DIM, tk), lambda i, h, j: (i, 0, h // q_per_kv, 0)),
        ],
        out_specs=pl.BlockSpec((1, HEAD_DIM, step), lambda i, h, j: (i, h, j)),
        compiler_params=_cparams(("parallel", "arbitrary", "arbitrary")),
        name="dense_attention",
    )(q, k, v)


def _diff_body(sc_ref, q_ref, k_ref, v_ref, g_ref, bias_ref, o_ref):
    tq = TQ
    n_tiles = q_ref.shape[2] // tq
    nkb, dv, ks = v_ref.shape[1], v_ref.shape[2], v_ref.shape[3]
    hh = pl.program_id(1)
    lam = sc_ref[0]
    out_scale = sc_ref[1]
    slope = sc_ref[2 + hh]
    z32 = jnp.zeros((B_QK_DIM, tq), BF16)
    z48 = jnp.zeros((48, tq), BF16)
    zfeat = jnp.zeros((BF16_ROWS, tq), BF16)
    ones = jnp.ones((BF16_ROWS, ks), BF16)
    frow = lax.broadcasted_iota(jnp.int32, (BF16_ROWS, tq), 0)
    piece = jnp.where(frow % 3 == 0, LOG2E_PIECES[0], jnp.where(frow % 3 == 1, LOG2E_PIECES[1], LOG2E_PIECES[2]))
    cfeat = jnp.where(frow < 6, piece, 0.0)
    lane_pos = lax.broadcasted_iota(jnp.int32, (1, tq), 1)

    n_mixed = tq // ks

    def tile_index(job):
        return pl.program_id(2) * n_tiles + job // 2

    def block_index(job, ii):
        return lax.rem(tile_index(job) * n_mixed + ii, nkb)

    def side(job, ii):
        return jnp.where(block_index(job, ii) < tile_index(job) * n_mixed, 1.0, -1.0).astype(F32)

    def qrows(job):
        q = q_ref[0, :, (job // 2) * tq:(job // 2 + 1) * tq]
        if job % 2 == 0:
            return jnp.concatenate([q[:B_QK_DIM], z32], axis=0)
        return jnp.concatenate([z32, q[B_QK_DIM:]], axis=0)

    def scores(job, ii):
        i = block_index(job, ii)
        k = k_ref[0, pl.ds(pl.multiple_of(i * ks, ks), ks), :]
        feat = zfeat if ii < n_mixed else (cfeat * (side(job, ii) * slope)).astype(BF16)
        return jnp.dot(k, jnp.concatenate([qrows(job), feat, z48], axis=0), preferred_element_type=F32)

    def update(job, ii, s, carry):
        v_aug = jnp.concatenate([v_ref[0, block_index(job, ii)], ones], axis=0)
        if ii < n_mixed:
            return _online_update(s - bias_ref[0, ii], v_aug, carry)
        qpos = (tile_index(job) * tq + lane_pos).astype(F32)
        return _online_update(s, v_aug, carry, qpos * (-(side(job, ii) * slope * LOG2E)))

    res = []

    def finish(job, carry):
        res.append(_finish(carry, dv))
        if job % 2 == 1:
            o = res[job - 1] - lam * res[job]
            t = job // 2
            o_ref[0, :, t * tq:(t + 1) * tq] = (_rms_rows(o, g_ref[...]) * out_scale).astype(o_ref.dtype)

    _run_pipelined(2 * n_tiles, nkb, scores, update, lambda: _init_carry(dv, tq), finish)


def _diff_attention(scal, slopes, q, k, v, sub_g):
    b, rows, s = q.shape
    nkb, tk = v.shape[1], v.shape[3]
    n_mixed = TQ // tk
    jj = jnp.arange(tk)[None, :, None] + tk * jnp.arange(n_mixed)[:, None, None]
    dist = jnp.abs(jj - jnp.arange(TQ)[None, None, :]).astype(F32)
    bias = dist[None] * (LOG2E * jnp.asarray(slopes, F32))[:, None, None, None]
    step = min(TQ_STEP, s)
    assert s % step == 0 and step % TQ == 0
    return pl.pallas_call(
        _diff_body,
        out_shape=jax.ShapeDtypeStruct((b, rows, s), BF16),
        grid=(b, N_HEADS, s // step),
        in_specs=[
            pl.BlockSpec(memory_space=pltpu.SMEM),
            pl.BlockSpec((1, 2 * B_QK_DIM, step), lambda i, h, j: (i, h, j)),
            pl.BlockSpec((1, s, LANES), lambda i, h, j: (i, 0, h)),
            pl.BlockSpec((1, nkb, HEAD_DIM, tk), lambda i, h, j: (i, 0, h, 0)),
            pl.BlockSpec((HEAD_DIM, 1), lambda i, h, j: (0, 0)),
            pl.BlockSpec((1, n_mixed, tk, TQ), lambda i, h, j: (h, 0, 0, 0)),
        ],
        out_specs=pl.BlockSpec((1, HEAD_DIM, step), lambda i, h, j: (i, h, j)),
        compiler_params=_cparams(("parallel", "arbitrary", "arbitrary")),
        name="diff_attention",
    )(scal, q, k, v, sub_g, bias)


def _band_body(q_ref, k_ref, v_ref, pen_ref, o_ref, lse_ref, *, qb):
    l_len = k_ref.shape[1]
    win = pen_ref.shape[3]
    lane = lax.broadcasted_iota(jnp.int32, (qb, LANES), 1)
    n_sub = q_ref.shape[1] // qb
    n_groups = q_ref.shape[2] // LANES
    for sb in range(n_sub):
        jb = pl.program_id(2) * n_sub + sb
        start = jnp.clip(jb * qb - BAND_RADIUS, 0, l_len - win)
        start = pl.multiple_of(start, BAND_RADIUS)
        variant = (jb * qb - start) // BAND_RADIUS
        q = q_ref[0, sb * qb:(sb + 1) * qb, :]
        kwin = k_ref[0, pl.ds(start, win), :]
        vwin = v_ref[0, pl.ds(start, win), :]
        for gc in range(n_groups):
            g = gc % 2
            q2 = q[:, LANES * gc:LANES * (gc + 1)].astype(F32)
            k2 = kwin[:, LANES * gc:LANES * (gc + 1)]
            v2 = vwin[:, LANES * gc:LANES * (gc + 1)]
            out_g = jnp.zeros((qb, LANES), F32)
            lse_g = jnp.zeros((qb, LANES), F32)
            for hh in range(2):
                own = (lane >= HEAD_DIM * hh) & (lane < HEAD_DIM * (hh + 1))
                qm = jnp.where(own, q2, 0.0).astype(BF16)
                s = lax.dot_general(qm, k2, (((1,), (1,)), ((), ())), preferred_element_type=F32)
                s = s - pen_ref[variant, 2 * g + hh]
                m = jnp.max(s, axis=1, keepdims=True)
                p = jnp.exp2(s - m)
                lsum = jnp.sum(p, axis=1, keepdims=True)
                o = jnp.dot(p.astype(BF16), v2, preferred_element_type=F32) / lsum
                lse = m + jnp.log2(lsum)
                out_g = jnp.where(own, o, out_g)
                lse_g = jnp.where(own, lse, lse_g)
            o_ref[0, sb * qb:(sb + 1) * qb, LANES * gc:LANES * (gc + 1)] = out_g.astype(o_ref.dtype)
            lse_ref[0, sb * qb:(sb + 1) * qb, LANES * gc:LANES * (gc + 1)] = lse_g


def _band_penalty(qb, win, dil, slopes):
    rows = jnp.arange(qb)[None, :, None] + BAND_RADIUS * jnp.arange(3)[:, None, None]
    rel = jnp.abs(rows - jnp.arange(win)[None, None, :])
    scale = jnp.asarray(slopes, F32) * (float(dil) * LOG2E)
    pen = rel.astype(F32)[:, None] * scale[None, :, None, None]
    return jnp.where((rel <= BAND_RADIUS)[:, None], pen, BAND_MASKED)


def _band_attention(q, k, v, dil, slopes):
    b, l_len, cw = q.shape
    c = cw // dil
    qb = min(BAND_Q, l_len)
    qstep = min(BAND_QSTEP, l_len)
    cps = min(dil, max(1, BAND_QSTEP // l_len))
    assert dil % cps == 0
    win = min(qb + 2 * BAND_RADIUS, l_len)
    pen = _band_penalty(qb, win, dil, slopes)
    body = functools.partial(_band_body, qb=qb)
    return pl.pallas_call(
        body,
        out_shape=(jax.ShapeDtypeStruct((b, l_len, cw), BF16), jax.ShapeDtypeStruct((b, l_len, cw), F32)),
        grid=(b, dil // cps, l_len // qstep),
        in_specs=[
            pl.BlockSpec((1, qstep, c * cps), lambda i, r, j: (i, j, r)),
            pl.BlockSpec((1, l_len, c * cps), lambda i, r, j: (i, 0, r)),
            pl.BlockSpec((1, l_len, c * cps), lambda i, r, j: (i, 0, r)),
            pl.BlockSpec(pen.shape, lambda i, r, j: (0, 0, 0, 0)),
        ],
        out_specs=(pl.BlockSpec((1, qstep, c * cps), lambda i, r, j: (i, j, r)),) * 2,
        compiler_params=_cparams(("parallel", "arbitrary", "arbitrary")),
        name="band_attention_d%d" % dil,
    )(q, k, v, pen)


def _band_combine(in_refs, scr_ref, tm, c):
    vals = []
    for idx, (_, dil) in enumerate(DILATED_CONFIGS):
        for which in range(2):
            src = in_refs[2 * idx + which]
            if dil == 1:
                vals.append(src[0].astype(F32))
                continue
            groups = []
            for g in range(c // LANES):
                slot = scr_ref.at[(2 * idx + which) * (c // LANES) + g]
                for r in range(dil):
                    lo = c * r + LANES * g
                    slot[pl.ds(r, tm // dil, stride=dil), :] = src[0, :, lo:lo + LANES].astype(F32)
                groups.append(slot[...])
            vals.append(jnp.concatenate(groups, axis=1))
    outs, lses = vals[0::2], vals[1::2]
    m = functools.reduce(jnp.maximum, lses)
    ws = [jnp.exp2(x - m) for x in lses]
    num = functools.reduce(lambda a, t: a + t, [w * o for w, o in zip(ws, outs)])
    den = functools.reduce(lambda a, t: a + t, ws)
    return num / den


def _outproj_body(*refs):
    n_d = 2 * len(DILATED_CONFIGS)
    x_ref, oa_ref, ob_ref, oc_ref = refs[:4]
    d_refs = refs[4:4 + n_d]
    w_ref, g_ref, x1_ref, h2_ref, edge_ref, scr_ref = refs[4 + n_d:]
    tm = x_ref.shape[1]
    j = pl.program_id(1)
    mix = jnp.concatenate([oa_ref[0], ob_ref[0], oc_ref[0]], axis=0)
    od = _band_combine(d_refs, scr_ref, tm, D_WIDTH).astype(BF16)
    w = w_ref[...]
    y = jnp.dot(w[:, :768], mix, preferred_element_type=F32)
    y = y + lax.dot_general(w[:, 768:], od, (((1,), (1,)), ((), ())), preferred_element_type=F32)
    x1 = x_ref[0] + y.T
    x1_ref[0] = x1
    ms = jnp.mean(x1 * x1, axis=1, keepdims=True)
    h2 = (x1 * lax.rsqrt(ms + NORM_EPS) * g_ref[...]).astype(BF16)
    h2_ref[0] = h2
    per = tm // TOK_TILE
    for t in range(per):
        row = 2 * (j * per + t)
        edge_ref[0, pl.ds(row, 1), :] = h2[t * TOK_TILE:t * TOK_TILE + 1].astype(F32)
        edge_ref[0, pl.ds(row + 1, 1), :] = h2[(t + 1) * TOK_TILE - 1:(t + 1) * TOK_TILE].astype(F32)


def _out_project(x, oa, ob, oc, branches, layer, w_out_t, g2):
    b, s, d = x.shape
    tm = min(OUT_TILE, s)
    assert s % tm == 0 and tm % TOK_TILE == 0
    nt = s // TOK_TILE
    fm = lambda rows: pl.BlockSpec((1, rows, tm), lambda i, j: (i, 0, j))
    tmj = pl.BlockSpec((1, tm, d), lambda i, j: (i, j, 0))
    d_specs = []
    for (_, dil) in DILATED_CONFIGS:
        d_specs += [pl.BlockSpec((1, tm // dil, dil * D_WIDTH), lambda i, j: (i, j, 0))] * 2
    flat = [a for pair in branches for a in pair]
    return pl.pallas_call(
        _outproj_body,
        out_shape=(jax.ShapeDtypeStruct((b, s, d), F32), jax.ShapeDtypeStruct((b, s, d), BF16),
                   jax.ShapeDtypeStruct((b, 2 * nt, d), F32)),
        grid=(b, s // tm),
        in_specs=[tmj, fm(256), fm(256), fm(256)] + d_specs + [
            pl.BlockSpec((None, d, d), lambda i, j: (layer, 0, 0)),
            pl.BlockSpec((1, d), lambda i, j: (0, 0))],
        out_specs=(tmj, tmj, pl.BlockSpec((1, 2 * nt, d), lambda i, j: (i, 0, 0))),
        scratch_shapes=[pltpu.VMEM((2 * len(DILATED_CONFIGS) * (D_WIDTH // LANES), tm, LANES), F32)],
        compiler_params=_cparams(("parallel", "arbitrary")),
        name="out_project",
    )(x, oa, ob, oc, *flat, w_out_t, g2)


def _halo_body(h_ref, w_ref, o_ref):
    o_ref[...] = jnp.dot(h_ref[...], w_ref[...], preferred_element_type=F32)


def _halo_up(halo_h, layer, w_up):
    nh, d = halo_h.shape
    ff2 = w_up.shape[2]
    n_blk = 4
    cols = ff2 // n_blk
    return pl.pallas_call(
        _halo_body,
        out_shape=jax.ShapeDtypeStruct((nh, ff2), F32),
        grid=(n_blk,),
        in_specs=[pl.BlockSpec((nh, d), lambda i: (0, 0)),
                  pl.BlockSpec((None, d, cols), lambda i: (layer, 0, i))],
        out_specs=pl.BlockSpec((nh, cols), lambda i: (0, i)),
        compiler_params=_cparams(("arbitrary",)),
        name="mlp_halo_up",
    )(halo_h, w_up)


def _mlp_body(x1_ref, h2_ref, wup_ref, wdn_ref, par_ref, halo_ref, o_ref, gbuf_ref, vbuf_ref, *, chunks):
    tf = h2_ref.shape[1]
    ff = wdn_ref.shape[0]
    nt = pl.num_programs(1)
    row = (pl.program_id(0) * nt + pl.program_id(1)) * 2
    h = h2_ref[0]

    def conv(buf_ref, u, c0, width):
        buf_ref[MLP_PAD:MLP_PAD + tf, :width] = u
        buf_ref[MLP_PAD - 1:MLP_PAD, :width] = halo_ref[pl.ds(row, 1), c0:c0 + width]
        buf_ref[MLP_PAD + tf:MLP_PAD + tf + 1, :width] = halo_ref[pl.ds(row + 1, 1), c0:c0 + width]
        prev = buf_ref[MLP_PAD - 1:MLP_PAD - 1 + tf, :width]
        nxt = buf_ref[MLP_PAD + 1:MLP_PAD + 1 + tf, :width]
        p = par_ref[:, c0:c0 + width]
        return p[0:1] * prev + p[1:2] * u + p[2:3] * nxt + p[3:4]

    def up(c0, width):
        return (jnp.dot(h, wup_ref[:, c0:c0 + width], preferred_element_type=F32),
                jnp.dot(h, wup_ref[:, ff + c0:ff + c0 + width], preferred_element_type=F32))

    y = x1_ref[0]
    nxt_up = up(*chunks[0])
    for n, (c0, width) in enumerate(chunks):
        ug, uv = nxt_up
        if n + 1 < len(chunks):
            nxt_up = up(*chunks[n + 1])
        gate = conv(gbuf_ref, ug, c0, width)
        val = conv(vbuf_ref, uv, ff + c0, width)
        act = gate / (1.0 + jnp.exp(-gate)) * val
        y = y + jnp.dot(act.astype(BF16), wdn_ref[c0:c0 + width, :], preferred_element_type=F32)
    o_ref[0] = y


def _mlp(x1, h2, layer, w_up, w_down, conv_par, halo_up, chunks):
    b, s, d = x1.shape
    tf = TOK_TILE
    ff2 = w_up.shape[2]
    nh = halo_up.shape[0]
    whole = lambda i, j: (0, 0)
    body = functools.partial(_mlp_body, chunks=chunks)
    const = dict(pipeline_mode=pl.Buffered(1))
    wmax = max(w for _, w in chunks)
    return pl.pallas_call(
        body,
        out_shape=jax.ShapeDtypeStruct((b, s, d), F32),
        grid=(b, s // tf),
        in_specs=[pl.BlockSpec((1, tf, d), lambda i, j: (i, j, 0)),
                  pl.BlockSpec((1, tf, d), lambda i, j: (i, j, 0)),
                  pl.BlockSpec((None, d, ff2), lambda i, j: (layer, 0, 0), **const),
                  pl.BlockSpec((None, ff2 // 2, d), lambda i, j: (layer, 0, 0), **const),
                  pl.BlockSpec((8, ff2), whole, **const),
                  pl.BlockSpec((nh, ff2), whole, **const)],
        out_specs=pl.BlockSpec((1, tf, d), lambda i, j: (i, j, 0)),
        scratch_shapes=[pltpu.VMEM((tf + 2 * MLP_PAD, wmax), F32)] * 2,
        compiler_params=_cparams(("arbitrary", "arbitrary")),
        name="gated_conv_mlp",
    )(x1, h2, w_up, w_down, conv_par, halo_up)


def _halo_rows(edges, nt):
    b, _, d = edges.shape
    ez = jnp.concatenate([edges, jnp.zeros((b, 1, d), edges.dtype)], axis=1)
    idx = []
    for j in range(nt):
        idx += [2 * j - 1 if j > 0 else 2 * nt, 2 * j + 2 if j < nt - 1 else 2 * nt]
    rows = jnp.take(ez, jnp.asarray(idx, jnp.int32), axis=1).reshape(b * nt * 2, d)
    pad = (-rows.shape[0]) % BF16_ROWS
    return jnp.pad(rows, ((0, pad), (0, 0))).astype(BF16)


def _rope_tables(s):
    def freqs(dim):
        return 1.0 / (ROPE_THETA ** (jnp.arange(0, dim, 2, dtype=F32) / dim))
    t = jnp.arange(s)
    rows = (t // GRID_W).astype(F32)
    cols = (t % GRID_W).astype(F32)
    pos = t.astype(F32)
    fa = freqs(HEAD_DIM // 2)
    ar, ac = fa[:, None] * rows[None, :], fa[:, None] * cols[None, :]
    am = freqs(C_ROPE)[:, None] * pos[None, :]
    cosa = jnp.concatenate([jnp.cos(ar), jnp.cos(ar), jnp.cos(ac), jnp.cos(ac)], axis=0)
    sina = jnp.concatenate([-jnp.sin(ar), jnp.sin(ar), -jnp.sin(ac), jnp.sin(ac)], axis=0)
    cosm = jnp.concatenate([jnp.cos(am), jnp.cos(am)], axis=0)
    sinm = jnp.concatenate([-jnp.sin(am), jnp.sin(am)], axis=0)
    return cosa, sina, cosm, sinm


def kernel(x, norm1_g, w_in, a_qn_g, a_kn_g, b_qn_g, b_kn_g, b_lam_q1, b_lam_k1, b_lam_q2, b_lam_k2, b_sub_g,
           c_qa_g, c_kva_g, c_wqb, c_wkvb, c_qn_g, c_kn_g, d_qn_g, d_kn_g, w_out, norm2_g, w_up, conv_w, conv_b,
           w_down):
    b, s, d = x.shape
    depth = w_in.shape[0]
    ff = w_down.shape[1]
    nt = s // TOK_TILE
    assert s % TOK_TILE == 0 and TOK_TILE % KSTEP == 0 and TQ % KSTEP == 0 and s % TQ == 0 and s % GRID_W == 0
    assert s <= 1 << (POS_SPLIT_BITS + BF16_MANTISSA_BITS)
    for window, dil in DILATED_CONFIGS:
        assert window // (2 * dil) == BAND_RADIUS and (s // dil) % min(BAND_Q, s // dil) == 0
        assert (s // dil) % min(BAND_QSTEP, s // dil) == 0 and BAND_QSTEP % BAND_Q == 0
        assert TOK_TILE % (dil * BF16_ROWS) == 0

    tr = lambda w: jnp.swapaxes(w, 1, 2).astype(BF16)
    gains = jnp.concatenate([norm1_g, a_qn_g, a_kn_g, b_qn_g, b_kn_g, c_qa_g, c_kva_g, c_qn_g, c_kn_g,
                             d_qn_g, d_kn_g], axis=1)[:, :, None]
    conv_par = jnp.concatenate([conv_w, conv_b[:, None, :], jnp.zeros((depth, 4, 2 * ff), F32)], axis=1)
    slopes = [2.0 ** (-8.0 * i / (2 * N_HEADS)) for i in range(1, 2 * N_HEADS + 1)]
    slopes_b, slopes_d = slopes[:N_HEADS], tuple(slopes[N_HEADS:])
    lam_init = jnp.asarray([0.8 - 0.6 * math.exp(-0.3 * l) for l in range(depth)], F32)
    lam = (jnp.exp(jnp.sum(b_lam_q1 * b_lam_k1, axis=1)) - jnp.exp(jnp.sum(b_lam_q2 * b_lam_k2, axis=1)) + lam_init)
    scal = jnp.concatenate([lam[:, None], (1.0 - lam_init)[:, None],
                            jnp.broadcast_to(jnp.asarray(slopes_b, F32), (depth, N_HEADS)),
                            jnp.zeros((depth, 2), F32)], axis=1)
    w_in_t, w_out_t, w_up_b, w_down_b = tr(w_in), tr(w_out), w_up.astype(BF16), w_down.astype(BF16)
    wqb_t, wkvb_t = tr(c_wqb), tr(c_wkvb)
    g2, sub_g = norm2_g[:, None, :], b_sub_g[:, :, None]
    cosa, sina, cosm, sinm = _rope_tables(s)
    assert ff % LANES == 0
    chunks = tuple((c0, min(MLP_CHUNK, ff - c0)) for c0 in range(0, ff, MLP_CHUNK))

    out = x
    for l in range(depth):
        (qa, ka, va, qb, kb, vb, qc, kc, vc, dq1, dq4, dq16, dk1, dk4, dk16, dv1, dv4, dv16) = _project(
            out, l, w_in_t, wqb_t[l], wkvb_t[l], gains[l], cosa, sina, cosm, sinm)
        oa = _dense_attention(qa, ka, va, HEAD_DIM, N_HEADS // A_KV_HEADS)
        ob = _diff_attention(scal[l], slopes_b, qb, kb, vb, sub_g[l])
        oc = _dense_attention(qc, kc, vc, C_QK, 1)
        branches = [_band_attention(q_, k_, v_, dil, slopes_d)
                    for (_, dil), q_, k_, v_ in zip(DILATED_CONFIGS, (dq1, dq4, dq16), (dk1, dk4, dk16), (dv1, dv4, dv16))]
        x1, h2, edges = _out_project(out, oa, ob, oc, branches, l, w_out_t, g2[l])
        halo = _halo_up(_halo_rows(edges, nt), l, w_up_b)
        out = _mlp(x1, h2, l, w_up_b, w_down_b, conv_par[l], halo, chunks)
    return out
```

```python
import functools
import math

import jax
import jax.numpy as jnp
import numpy as np
from jax import lax
from jax.experimental import pallas as pl
from jax.experimental.pallas import tpu as pltpu

F32 = jnp.float32
BF16 = jnp.bfloat16

HEAD_DIM = 64
N_HEADS = 4
A_KV_HEADS = 2
B_QK_DIM = 32
C_Q_LORA = 256
C_KV_LORA = 128
C_NOPE = 64
C_ROPE = 32
C_QK = C_NOPE + C_ROPE
DILATED_CONFIGS = ((128, 1), (512, 4), (2048, 16))
D_WIDTH = N_HEADS * HEAD_DIM
GRID_W = 64
ROPE_THETA = 10000.0
NORM_EPS = 1e-6
LOG2E = 1.4426950408889634


def _bf16_pieces(x, n):
    out = []
    for _ in range(n):
        u = np.float32(x).view(np.uint32)
        u = np.uint32((int(u) + 0x7FFF + ((int(u) >> 16) & 1)) & 0xFFFF0000)
        piece = float(u.view(np.float32))
        out.append(piece)
        x = x - piece
    return tuple(out)


LOG2E_PIECES = _bf16_pieces(LOG2E, 3)
POS_SPLIT_BITS = 6
BF16_MANTISSA_BITS = 8

_SPLITS = (256, 128, 128, 256, 256, 256, C_Q_LORA, C_KV_LORA, C_ROPE, 256, 256, 256)
_OFF = [0]
for _n in _SPLITS:
    _OFF.append(_OFF[-1] + _n)
(O_AQ, O_AK, O_AV, O_BQ, O_BK, O_BV, O_CQL, O_CKVL, O_CKR, O_DQ, O_DK, O_DV, IN_WIDTH) = _OFF

_GAIN_SIZES = (("n1", 1024), ("aq", 64), ("ak", 64), ("bq", 32), ("bk", 32), ("cqa", 256), ("ckva", 128),
               ("cqn", 96), ("ckn", 96), ("dq", 64), ("dk", 64))
G_OFF = {}
_o = 0
for _k, _n in _GAIN_SIZES:
    G_OFF[_k] = (_o, _n)
    _o += _n
G_TOTAL = _o

LANES = 128
BF16_ROWS = 16
TOK_TILE = 512
PROJ_TILE = 1024
PREP_COLS = 256
OUT_TILE = 1024
MLP_CHUNK = 1536
MLP_PAD = 8
TQ = 512
TQ_STEP = 2048
DENSE_TQ_STEP = 4096
KSTEP = 256
SCORES_AHEAD = 2
BAND_Q = 128
BAND_QSTEP = 2048
BAND_MASKED = 1e30
BAND_RADIUS = 64
V7X_VMEM_BYTES = 64 * 1024 * 1024
VMEM_LIMIT = V7X_VMEM_BYTES * 7 // 8


def _cparams(sem):
    return pltpu.CompilerParams(dimension_semantics=sem, vmem_limit_bytes=VMEM_LIMIT)


def _rms_rows(v, g_col):
    ms = jnp.mean(v * v, axis=0, keepdims=True)
    return v * lax.rsqrt(ms + NORM_EPS) * g_col


def _swap_halves(x, n):
    parts = []
    for i in range(0, x.shape[0], 2 * n):
        parts.append(x[i + n:i + 2 * n])
        parts.append(x[i:i + n])
    return jnp.concatenate(parts, axis=0)


def _rope_rows(x, cos, sin_signed, n):
    return x * cos + _swap_halves(x, n) * sin_signed


def _store_dilated(scr_ref, val, out_refs):
    tm, c = val.shape
    for g in range(c // LANES):
        scr_ref[g] = val[:, LANES * g:LANES * (g + 1)]
    for (_, dil), out in zip(DILATED_CONFIGS, out_refs):
        if dil == 1:
            out[0] = val.astype(BF16)
            continue
        n = tm // dil
        for r in range(dil):
            for g in range(c // LANES):
                lo = c * r + LANES * g
                out[0, :, lo:lo + LANES] = scr_ref[g, pl.ds(r, n, stride=dil), :].astype(BF16)


def _store_key_blocks(v_ref, v):
    for c in range(v.shape[1] // KSTEP):
        v_ref[0, c] = v[:, c * KSTEP:(c + 1) * KSTEP]


def _proj_body(x_ref, w_ref, wqb_ref, wkvb_ref, g_ref, cosa_ref, sina_ref, cosm_ref, sinm_ref,
               qa_ref, ka_ref, va_ref, qb_ref, kb_ref, vb_ref, qc_ref, kc_ref, vc_ref,
               dq1_ref, dq4_ref, dq16_ref, dk1_ref, dk4_ref, dk16_ref, dv1_ref, dv4_ref, dv16_ref,
               scr_ref):
    tm = x_ref.shape[1]

    def gain(name):
        o, n = G_OFF[name]
        return g_ref[o:o + n, :]

    x = x_ref[0].T
    h = (x * gain("n1")).astype(BF16)
    r = lax.rsqrt(jnp.mean(x * x, axis=0, keepdims=True) + NORM_EPS)

    def section(lo, hi):
        part = jnp.dot(w_ref[lo:hi, :], h, preferred_element_type=F32) * r
        return lambda o, n: part[o - lo:o - lo + n]

    cosa, sina = cosa_ref[...], sina_ref[...]
    cosm, sinm = cosm_ref[...], sinm_ref[...]
    zeros64 = jnp.zeros((64, tm), F32)
    zeros32 = jnp.zeros((32, tm), F32)

    def prep_a(proj):
        sa = HEAD_DIM ** -0.5 * LOG2E
        for c0 in range(0, tm, PREP_COLS):
            cs = slice(c0, c0 + PREP_COLS)
            for hh in range(N_HEADS):
                q = _rope_rows(_rms_rows(proj(O_AQ + 64 * hh, 64)[:, cs], gain("aq")), cosa[:, cs], sina[:, cs], 16) * sa
                qa_ref[0, 64 * hh:64 * hh + 64, cs] = q.astype(BF16)
            for g in range(A_KV_HEADS):
                k = _rope_rows(_rms_rows(proj(O_AK + 64 * g, 64)[:, cs], gain("ak")), cosa[:, cs], sina[:, cs], 16)
                kpad = jnp.concatenate([k, zeros64[:, cs]], axis=0)
                ka_ref[0, cs, LANES * g:LANES * (g + 1)] = kpad.T.astype(BF16)
        _store_key_blocks(va_ref, proj(O_AV, 128).astype(BF16))

    def prep_b(proj):
        sb = B_QK_DIM ** -0.5 * LOG2E
        for c0 in range(0, tm, PREP_COLS):
            cs = slice(c0, c0 + PREP_COLS)
            for i in range(2 * N_HEADS):
                q = _rms_rows(proj(O_BQ + 32 * i, 32)[:, cs], gain("bq")) * sb
                qb_ref[0, 32 * i:32 * i + 32, cs] = q.astype(BF16)
        kpos = pl.program_id(1) * tm + lax.broadcasted_iota(jnp.int32, (BF16_ROWS, tm), 1)
        frow = lax.broadcasted_iota(jnp.int32, (BF16_ROWS, tm), 0)
        pos_hi = ((kpos >> POS_SPLIT_BITS) << POS_SPLIT_BITS).astype(F32)
        pos_lo = (kpos & ((1 << POS_SPLIT_BITS) - 1)).astype(F32)
        kfeat = jnp.where(frow < 3, pos_hi, jnp.where(frow < 6, pos_lo, 0.0))
        zeros48 = jnp.zeros((48, tm), F32)
        for c0 in range(0, tm, PREP_COLS):
            cs = slice(c0, c0 + PREP_COLS)
            for hh in range(N_HEADS):
                k1 = _rms_rows(proj(O_BK + 64 * hh, 32)[:, cs], gain("bk"))
                k2 = _rms_rows(proj(O_BK + 64 * hh + 32, 32)[:, cs], gain("bk"))
                kpad = jnp.concatenate([k1, k2, kfeat[:, cs], zeros48[:, cs]], axis=0)
                kb_ref[0, cs, LANES * hh:LANES * (hh + 1)] = kpad.T.astype(BF16)
        _store_key_blocks(vb_ref, proj(O_BV, 256).astype(BF16))

    def prep_c(proj):
        sc = C_QK ** -0.5 * LOG2E
        cq = _rms_rows(proj(O_CQL, C_Q_LORA), gain("cqa")).astype(BF16)
        qc = jnp.dot(wqb_ref[...], cq, preferred_element_type=F32)
        ckv = _rms_rows(proj(O_CKVL, C_KV_LORA), gain("ckva")).astype(BF16)
        kvc = jnp.dot(wkvb_ref[...], ckv, preferred_element_type=F32)
        kr = proj(O_CKR, C_ROPE)
        vcs = []
        for hh in range(N_HEADS):
            q = _rms_rows(qc[C_QK * hh:C_QK * (hh + 1)], gain("cqn"))
            q = jnp.concatenate([q[:C_NOPE], _rope_rows(q[C_NOPE:], cosm, sinm, 16)], axis=0) * sc
            qc_ref[0, C_QK * hh:C_QK * (hh + 1), :] = q.astype(BF16)
            kk = jnp.concatenate([kvc[128 * hh:128 * hh + C_NOPE], kr], axis=0)
            kk = _rms_rows(kk, gain("ckn"))
            kk = jnp.concatenate([kk[:C_NOPE], _rope_rows(kk[C_NOPE:], cosm, sinm, 16), zeros32], axis=0)
            kc_ref[0, :, LANES * hh:LANES * (hh + 1)] = kk.T.astype(BF16)
            vcs.append(kvc[128 * hh + C_NOPE:128 * (hh + 1)])
        _store_key_blocks(vc_ref, jnp.concatenate(vcs, axis=0).astype(BF16))

    def prep_d(proj):
        sd = HEAD_DIM ** -0.5 * LOG2E
        qs, ks, vs = [], [], []
        for g in range(2):
            qg, kg = [], []
            for hh in (2 * g, 2 * g + 1):
                qg.append(_rms_rows(proj(O_DQ + 64 * hh, 64), gain("dq")) * sd)
                kg.append(_rms_rows(proj(O_DK + 64 * hh, 64), gain("dk")))
            qs.append(jnp.concatenate(qg, axis=0).T)
            ks.append(jnp.concatenate(kg, axis=0).T)
            vs.append(proj(O_DV + 128 * g, 128).T)
        _store_dilated(scr_ref.at[0], jnp.concatenate(qs, axis=1), (dq1_ref, dq4_ref, dq16_ref))
        _store_dilated(scr_ref.at[1], jnp.concatenate(ks, axis=1), (dk1_ref, dk4_ref, dk16_ref))
        _store_dilated(scr_ref.at[2], jnp.concatenate(vs, axis=1), (dv1_ref, dv4_ref, dv16_ref))

    sec_c = section(O_CQL, O_DQ)
    sec_d = section(O_DQ, IN_WIDTH)
    prep_c(sec_c)
    sec_b = section(O_BQ, O_CQL)
    prep_d(sec_d)
    sec_a = section(O_AQ, O_BQ)
    prep_b(sec_b)
    prep_a(sec_a)


def _project(x, layer, w_in_t, wqb_t, wkvb_t, gains, cosa, sina, cosm, sinm):
    b, s, d = x.shape
    tm = min(PROJ_TILE, s)
    assert s % tm == 0 and tm % KSTEP == 0
    nt = s // tm
    fm = lambda rows: jax.ShapeDtypeStruct((b, rows, s), BF16)
    tmj = lambda cols: jax.ShapeDtypeStruct((b, s, cols), BF16)
    vblk = lambda rows: jax.ShapeDtypeStruct((b, s // KSTEP, rows, KSTEP), BF16)
    dviews = tuple(jax.ShapeDtypeStruct((b, s // dil, dil * D_WIDTH), BF16) for _, dil in DILATED_CONFIGS)
    out_shape = (fm(256), tmj(256), vblk(128),
                 fm(256), tmj(512), vblk(256),
                 fm(384), tmj(512), vblk(256)) + dviews * 3
    fm_spec = lambda rows: pl.BlockSpec((1, rows, tm), lambda i, j: (i, 0, j))
    tm_spec = lambda cols: pl.BlockSpec((1, tm, cols), lambda i, j: (i, j, 0))
    v_spec = lambda rows: pl.BlockSpec((1, tm // KSTEP, rows, KSTEP), lambda i, j: (i, j, 0, 0))
    dspecs = tuple(pl.BlockSpec((1, tm // dil, dil * D_WIDTH), lambda i, j: (i, j, 0)) for _, dil in DILATED_CONFIGS)
    out_specs = (fm_spec(256), tm_spec(256), v_spec(128),
                 fm_spec(256), tm_spec(512), v_spec(256),
                 fm_spec(384), tm_spec(512), v_spec(256)) + dspecs * 3
    whole = lambda i, j: (0, 0)
    in_specs = [
        pl.BlockSpec((1, tm, d), lambda i, j: (i, j, 0)),
        pl.BlockSpec((None, IN_WIDTH, d), lambda i, j: (layer, 0, 0)),
        pl.BlockSpec((N_HEADS * C_QK, C_Q_LORA), whole),
        pl.BlockSpec((N_HEADS * 128, C_KV_LORA), whole),
        pl.BlockSpec((G_TOTAL, 1), whole),
        pl.BlockSpec((64, tm), lambda i, j: (0, j)),
        pl.BlockSpec((64, tm), lambda i, j: (0, j)),
        pl.BlockSpec((32, tm), lambda i, j: (0, j)),
        pl.BlockSpec((32, tm), lambda i, j: (0, j)),
    ]
    return pl.pallas_call(
        _proj_body,
        out_shape=out_shape,
        grid=(b, nt),
        in_specs=in_specs,
        out_specs=out_specs,
        scratch_shapes=[pltpu.VMEM((3, D_WIDTH // LANES, tm, LANES), F32)],
        compiler_params=_cparams(("parallel", "arbitrary")),
        name="project_heads",
    )(x, w_in_t, wqb_t, wkvb_t, gains, cosa, sina, cosm, sinm)


def _online_update(s, v_aug, carry, shift=None):
    m, acc = carry
    smax = jnp.max(s, axis=0, keepdims=True)
    if shift is None:
        m_new = jnp.maximum(m, smax)
        p = jnp.exp2(s - m_new).astype(BF16)
    else:
        m_new = jnp.maximum(m, smax + shift)
        p = jnp.exp2(s - (m_new - shift)).astype(BF16)
    alpha = jnp.exp2(m - m_new)
    acc = alpha * acc + jnp.dot(v_aug, p, preferred_element_type=F32)
    return m_new, acc


def _init_carry(dv, tq):
    return (jnp.full((1, tq), -jnp.inf, F32), jnp.zeros((dv + BF16_ROWS, tq), F32))


def _finish(carry, dv):
    _, acc = carry
    return acc[:dv] / acc[dv:dv + 1]


def _run_pipelined(n_jobs, n_blocks, scores_fn, update_fn, init_fn, finish_fn):
    steps = [(t, i) for t in range(n_jobs) for i in range(n_blocks)]
    pending = [scores_fn(*steps[n]) for n in range(min(SCORES_AHEAD, len(steps)))]
    carry = None
    for n, (t, i) in enumerate(steps):
        s = pending.pop(0)
        if n + SCORES_AHEAD < len(steps):
            pending.append(scores_fn(*steps[n + SCORES_AHEAD]))
        if i == 0:
            carry = init_fn()
        carry = update_fn(t, i, s, carry)
        if i == n_blocks - 1:
            finish_fn(t, carry)


def _dense_body(q_ref, k_ref, v_ref, o_ref):
    dq = q_ref.shape[1]
    nkb, dv, ks = v_ref.shape[1], v_ref.shape[2], v_ref.shape[3]
    tq = TQ
    ones = jnp.ones((BF16_ROWS, ks), BF16)
    zpad = jnp.zeros((LANES - dq, tq), BF16)
    qpads = [jnp.concatenate([q_ref[0, :, t * tq:(t + 1) * tq], zpad], axis=0) for t in range(q_ref.shape[2] // tq)]

    def scores(t, i):
        return jnp.dot(k_ref[0, i * ks:(i + 1) * ks, :], qpads[t], preferred_element_type=F32)

    def update(t, i, s, carry):
        return _online_update(s, jnp.concatenate([v_ref[0, i], ones], axis=0), carry)

    def finish(t, carry):
        o_ref[0, :, t * tq:(t + 1) * tq] = _finish(carry, dv).astype(o_ref.dtype)

    _run_pipelined(len(qpads), nkb, scores, update, lambda: _init_carry(dv, tq), finish)


def _dense_attention(q, k, v, dq, q_per_kv):
    b, rows, s = q.shape
    nh = rows // dq
    nkb, tk = v.shape[1], v.shape[3]
    step = min(DENSE_TQ_STEP, s)
    assert s % step == 0 and step % TQ == 0
    return pl.pallas_call(
        _dense_body,
        out_shape=jax.ShapeDtypeStruct((b, nh * HEAD_DIM, s), BF16),
        grid=(b, nh, s // step),
        in_specs=[
            pl.BlockSpec((1, dq, step), lambda i, h, j: (i, h, j)),
            pl.BlockSpec((1, s, LANES), lambda i, h, j: (i, 0, h // q_per_kv)),
            pl.BlockSpec((1, nkb, HEAD_DIM, tk), lambda i, h, j: (i, 0, h // q_per_kv, 0)),
        ],
        out_specs=pl.BlockSpec((1, HEAD_DIM, step), lambda i, h, j: (i, h, j)),
        compiler_params=_cparams(("parallel", "arbitrary", "arbitrary")),
        name="dense_attention",
    )(q, k, v)


def _diff_body(sc_ref, q_ref, k_ref, v_ref, g_ref, bias_ref, o_ref):
    tq = TQ
    n_tiles = q_ref.shape[2] // tq
    nkb, dv, ks = v_ref.shape[1], v_ref.shape[2], v_ref.shape[3]
    hh = pl.program_id(1)
    lam = sc_ref[0]
    out_scale = sc_ref[1]
    slope = sc_ref[2 + hh]
    z32 = jnp.zeros((B_QK_DIM, tq), BF16)
    z48 = jnp.zeros((48, tq), BF16)
    zfeat = jnp.zeros((BF16_ROWS, tq), BF16)
    ones = jnp.ones((BF16_ROWS, ks), BF16)
    frow = lax.broadcasted_iota(jnp.int32, (BF16_ROWS, tq), 0)
    piece = jnp.where(frow % 3 == 0, LOG2E_PIECES[0], jnp.where(frow % 3 == 1, LOG2E_PIECES[1], LOG2E_PIECES[2]))
    cfeat = jnp.where(frow < 6, piece, 0.0)
    lane_pos = lax.broadcasted_iota(jnp.int32, (1, tq), 1)

    n_mixed = tq // ks

    def tile_index(job):
        return pl.program_id(2) * n_tiles + job // 2

    def block_index(job, ii):
        return lax.rem(tile_index(job) * n_mixed + ii, nkb)

    def side(job, ii):
        return jnp.where(block_index(job, ii) < tile_index(job) * n_mixed, 1.0, -1.0).astype(F32)

    def qrows(job):
        q = q_ref[0, :, (job // 2) * tq:(job // 2 + 1) * tq]
        if job % 2 == 0:
            return jnp.concatenate([q[:B_QK_DIM], z32], axis=0)
        return jnp.concatenate([z32, q[B_QK_DIM:]], axis=0)

    def scores(job, ii):
        i = block_index(job, ii)
        k = k_ref[0, pl.ds(pl.multiple_of(i * ks, ks), ks), :]
        feat = zfeat if ii < n_mixed else (cfeat * (side(job, ii) * slope)).astype(BF16)
        return jnp.dot(k, jnp.concatenate([qrows(job), feat, z48], axis=0), preferred_element_type=F32)

    def update(job, ii, s, carry):
        v_aug = jnp.concatenate([v_ref[0, block_index(job, ii)], ones], axis=0)
        if ii < n_mixed:
            return _online_update(s - bias_ref[0, ii], v_aug, carry)
        qpos = (tile_index(job) * tq + lane_pos).astype(F32)
        return _online_update(s, v_aug, carry, qpos * (-(side(job, ii) * slope * LOG2E)))

    res = []

    def finish(job, carry):
        res.append(_finish(carry, dv))
        if job % 2 == 1:
            o = res[job - 1] - lam * res[job]
            t = job // 2
            o_ref[0, :, t * tq:(t + 1) * tq] = (_rms_rows(o, g_ref[...]) * out_scale).astype(o_ref.dtype)

    _run_pipelined(2 * n_tiles, nkb, scores, update, lambda: _init_carry(dv, tq), finish)


def _diff_attention(scal, slopes, q, k, v, sub_g):
    b, rows, s = q.shape
    nkb, tk = v.shape[1], v.shape[3]
    n_mixed = TQ // tk
    jj = jnp.arange(tk)[None, :, None] + tk * jnp.arange(n_mixed)[:, None, None]
    dist = jnp.abs(jj - jnp.arange(TQ)[None, None, :]).astype(F32)
    bias = dist[None] * (LOG2E * jnp.asarray(slopes, F32))[:, None, None, None]
    step = min(TQ_STEP, s)
    assert s % step == 0 and step % TQ == 0
    return pl.pallas_call(
        _diff_body,
        out_shape=jax.ShapeDtypeStruct((b, rows, s), BF16),
        grid=(b, N_HEADS, s // step),
        in_specs=[
            pl.BlockSpec(memory_space=pltpu.SMEM),
            pl.BlockSpec((1, 2 * B_QK_DIM, step), lambda i, h, j: (i, h, j)),
            pl.BlockSpec((1, s, LANES), lambda i, h, j: (i, 0, h)),
            pl.BlockSpec((1, nkb, HEAD_DIM, tk), lambda i, h, j: (i, 0, h, 0)),
            pl.BlockSpec((HEAD_DIM, 1), lambda i, h, j: (0, 0)),
            pl.BlockSpec((1, n_mixed, tk, TQ), lambda i, h, j: (h, 0, 0, 0)),
        ],
        out_specs=pl.BlockSpec((1, HEAD_DIM, step), lambda i, h, j: (i, h, j)),
        compiler_params=_cparams(("parallel", "arbitrary", "arbitrary")),
        name="diff_attention",
    )(scal, q, k, v, sub_g, bias)


def _band_body(q_ref, k_ref, v_ref, pen_ref, o_ref, lse_ref, *, qb):
    l_len = k_ref.shape[1]
    win = pen_ref.shape[3]
    lane = lax.broadcasted_iota(jnp.int32, (qb, LANES), 1)
    n_sub = q_ref.shape[1] // qb
    n_groups = q_ref.shape[2] // LANES
    for sb in range(n_sub):
        jb = pl.program_id(2) * n_sub + sb
        start = jnp.clip(jb * qb - BAND_RADIUS, 0, l_len - win)
        start = pl.multiple_of(start, BAND_RADIUS)
        variant = (jb * qb - start) // BAND_RADIUS
        q = q_ref[0, sb * qb:(sb + 1) * qb, :]
        kwin = k_ref[0, pl.ds(start, win), :]
        vwin = v_ref[0, pl.ds(start, win), :]
        for gc in range(n_groups):
            g = gc % 2
            q2 = q[:, LANES * gc:LANES * (gc + 1)].astype(F32)
            k2 = kwin[:, LANES * gc:LANES * (gc + 1)]
            v2 = vwin[:, LANES * gc:LANES * (gc + 1)]
            out_g = jnp.zeros((qb, LANES), F32)
            lse_g = jnp.zeros((qb, LANES), F32)
            for hh in range(2):
                own = (lane >= HEAD_DIM * hh) & (lane < HEAD_DIM * (hh + 1))
                qm = jnp.where(own, q2, 0.0).astype(BF16)
                s = lax.dot_general(qm, k2, (((1,), (1,)), ((), ())), preferred_element_type=F32)
                s = s - pen_ref[variant, 2 * g + hh]
                m = jnp.max(s, axis=1, keepdims=True)
                p = jnp.exp2(s - m)
                lsum = jnp.sum(p, axis=1, keepdims=True)
                o = jnp.dot(p.astype(BF16), v2, preferred_element_type=F32) / lsum
                lse = m + jnp.log2(lsum)
                out_g = jnp.where(own, o, out_g)
                lse_g = jnp.where(own, lse, lse_g)
            o_ref[0, sb * qb:(sb + 1) * qb, LANES * gc:LANES * (gc + 1)] = out_g.astype(o_ref.dtype)
            lse_ref[0, sb * qb:(sb + 1) * qb, LANES * gc:LANES * (gc + 1)] = lse_g


def _band_penalty(qb, win, dil, slopes):
    rows = jnp.arange(qb)[None, :, None] + BAND_RADIUS * jnp.arange(3)[:, None, None]
    rel = jnp.abs(rows - jnp.arange(win)[None, None, :])
    scale = jnp.asarray(slopes, F32) * (float(dil) * LOG2E)
    pen = rel.astype(F32)[:, None] * scale[None, :, None, None]
    return jnp.where((rel <= BAND_RADIUS)[:, None], pen, BAND_MASKED)


def _band_attention(q, k, v, dil, slopes):
    b, l_len, cw = q.shape
    c = cw // dil
    qb = min(BAND_Q, l_len)
    qstep = min(BAND_QSTEP, l_len)
    cps = min(dil, max(1, BAND_QSTEP // l_len))
    assert dil % cps == 0
    win = min(qb + 2 * BAND_RADIUS, l_len)
    pen = _band_penalty(qb, win, dil, slopes)
    body = functools.partial(_band_body, qb=qb)
    return pl.pallas_call(
        body,
        out_shape=(jax.ShapeDtypeStruct((b, l_len, cw), BF16), jax.ShapeDtypeStruct((b, l_len, cw), F32)),
        grid=(b, dil // cps, l_len // qstep),
        in_specs=[
            pl.BlockSpec((1, qstep, c * cps), lambda i, r, j: (i, j, r)),
            pl.BlockSpec((1, l_len, c * cps), lambda i, r, j: (i, 0, r)),
            pl.BlockSpec((1, l_len, c * cps), lambda i, r, j: (i, 0, r)),
            pl.BlockSpec(pen.shape, lambda i, r, j: (0, 0, 0, 0)),
        ],
        out_specs=(pl.BlockSpec((1, qstep, c * cps), lambda i, r, j: (i, j, r)),) * 2,
        compiler_params=_cparams(("parallel", "arbitrary", "arbitrary")),
        name="band_attention_d%d" % dil,
    )(q, k, v, pen)


def _band_combine(in_refs, scr_ref, tm, c):
    vals = []
    for idx, (_, dil) in enumerate(DILATED_CONFIGS):
        for which in range(2):
            src = in_refs[2 * idx + which]
            if dil == 1:
                vals.append(src[0].astype(F32))
                continue
            groups = []
            for g in range(c // LANES):
                slot = scr_ref.at[(2 * idx + which) * (c // LANES) + g]
                for r in range(dil):
                    lo = c * r + LANES * g
                    slot[pl.ds(r, tm // dil, stride=dil), :] = src[0, :, lo:lo + LANES].astype(F32)
                groups.append(slot[...])
            vals.append(jnp.concatenate(groups, axis=1))
    outs, lses = vals[0::2], vals[1::2]
    m = functools.reduce(jnp.maximum, lses)
    ws = [jnp.exp2(x - m) for x in lses]
    num = functools.reduce(lambda a, t: a + t, [w * o for w, o in zip(ws, outs)])
    den = functools.reduce(lambda a, t: a + t, ws)
    return num / den


def _outproj_body(*refs):
    n_d = 2 * len(DILATED_CONFIGS)
    x_ref, oa_ref, ob_ref, oc_ref = refs[:4]
    d_refs = refs[4:4 + n_d]
    w_ref, g_ref, x1_ref, h2_ref, edge_ref, scr_ref = refs[4 + n_d:]
    tm = x_ref.shape[1]
    j = pl.program_id(1)
    mix = jnp.concatenate([oa_ref[0], ob_ref[0], oc_ref[0]], axis=0)
    od = _band_combine(d_refs, scr_ref, tm, D_WIDTH).astype(BF16)
    w = w_ref[...]
    y = jnp.dot(w[:, :768], mix, preferred_element_type=F32)
    y = y + lax.dot_general(w[:, 768:], od, (((1,), (1,)), ((), ())), preferred_element_type=F32)
    x1 = x_ref[0] + y.T
    x1_ref[0] = x1
    ms = jnp.mean(x1 * x1, axis=1, keepdims=True)
    h2 = (x1 * lax.rsqrt(ms + NORM_EPS) * g_ref[...]).astype(BF16)
    h2_ref[0] = h2
    per = tm // TOK_TILE
    for t in range(per):
        row = 2 * (j * per + t)
        edge_ref[0, pl.ds(row, 1), :] = h2[t * TOK_TILE:t * TOK_TILE + 1].astype(F32)
        edge_ref[0, pl.ds(row + 1, 1), :] = h2[(t + 1) * TOK_TILE - 1:(t + 1) * TOK_TILE].astype(F32)


def _out_project(x, oa, ob, oc, branches, layer, w_out_t, g2):
    b, s, d = x.shape
    tm = min(OUT_TILE, s)
    assert s % tm == 0 and tm % TOK_TILE == 0
    nt = s // TOK_TILE
    fm = lambda rows: pl.BlockSpec((1, rows, tm), lambda i, j: (i, 0, j))
    tmj = pl.BlockSpec((1, tm, d), lambda i, j: (i, j, 0))
    d_specs = []
    for (_, dil) in DILATED_CONFIGS:
        d_specs += [pl.BlockSpec((1, tm // dil, dil * D_WIDTH), lambda i, j: (i, j, 0))] * 2
    flat = [a for pair in branches for a in pair]
    return pl.pallas_call(
        _outproj_body,
        out_shape=(jax.ShapeDtypeStruct((b, s, d), F32), jax.ShapeDtypeStruct((b, s, d), BF16),
                   jax.ShapeDtypeStruct((b, 2 * nt, d), F32)),
        grid=(b, s // tm),
        in_specs=[tmj, fm(256), fm(256), fm(256)] + d_specs + [
            pl.BlockSpec((None, d, d), lambda i, j: (layer, 0, 0)),
            pl.BlockSpec((1, d), lambda i, j: (0, 0))],
        out_specs=(tmj, tmj, pl.BlockSpec((1, 2 * nt, d), lambda i, j: (i, 0, 0))),
        scratch_shapes=[pltpu.VMEM((2 * len(DILATED_CONFIGS) * (D_WIDTH // LANES), tm, LANES), F32)],
        compiler_params=_cparams(("parallel", "arbitrary")),
        name="out_project",
    )(x, oa, ob, oc, *flat, w_out_t, g2)


def _halo_body(h_ref, w_ref, o_ref):
    o_ref[...] = jnp.dot(h_ref[...], w_ref[...], preferred_element_type=F32)


def _halo_up(halo_h, layer, w_up):
    nh, d = halo_h.shape
    ff2 = w_up.shape[2]
    n_blk = 4
    cols = ff2 // n_blk
    return pl.pallas_call(
        _halo_body,
        out_shape=jax.ShapeDtypeStruct((nh, ff2), F32),
        grid=(n_blk,),
        in_specs=[pl.BlockSpec((nh, d), lambda i: (0, 0)),
                  pl.BlockSpec((None, d, cols), lambda i: (layer, 0, i))],
        out_specs=pl.BlockSpec((nh, cols), lambda i: (0, i)),
        compiler_params=_cparams(("arbitrary",)),
        name="mlp_halo_up",
    )(halo_h, w_up)


def _mlp_body(x1_ref, h2_ref, wup_ref, wdn_ref, par_ref, halo_ref, o_ref, gbuf_ref, vbuf_ref, *, chunks):
    tf = h2_ref.shape[1]
    ff = wdn_ref.shape[0]
    nt = pl.num_programs(1)
    row = (pl.program_id(0) * nt + pl.program_id(1)) * 2
    h = h2_ref[0]

    def conv(buf_ref, u, c0, width):
        buf_ref[MLP_PAD:MLP_PAD + tf, :width] = u
        buf_ref[MLP_PAD - 1:MLP_PAD, :width] = halo_ref[pl.ds(row, 1), c0:c0 + width]
        buf_ref[MLP_PAD + tf:MLP_PAD + tf + 1, :width] = halo_ref[pl.ds(row + 1, 1), c0:c0 + width]
        prev = buf_ref[MLP_PAD - 1:MLP_PAD - 1 + tf, :width]
        nxt = buf_ref[MLP_PAD + 1:MLP_PAD + 1 + tf, :width]
        p = par_ref[:, c0:c0 + width]
        return p[0:1] * prev + p[1:2] * u + p[2:3] * nxt + p[3:4]

    def up(c0, width):
        return (jnp.dot(h, wup_ref[:, c0:c0 + width], preferred_element_type=F32),
                jnp.dot(h, wup_ref[:, ff + c0:ff + c0 + width], preferred_element_type=F32))

    y = x1_ref[0]
    nxt_up = up(*chunks[0])
    for n, (c0, width) in enumerate(chunks):
        ug, uv = nxt_up
        if n + 1 < len(chunks):
            nxt_up = up(*chunks[n + 1])
        gate = conv(gbuf_ref, ug, c0, width)
        val = conv(vbuf_ref, uv, ff + c0, width)
        act = gate / (1.0 + jnp.exp(-gate)) * val
        y = y + jnp.dot(act.astype(BF16), wdn_ref[c0:c0 + width, :], preferred_element_type=F32)
    o_ref[0] = y


def _mlp(x1, h2, layer, w_up, w_down, conv_par, halo_up, chunks):
    b, s, d = x1.shape
    tf = TOK_TILE
    ff2 = w_up.shape[2]
    nh = halo_up.shape[0]
    whole = lambda i, j: (0, 0)
    body = functools.partial(_mlp_body, chunks=chunks)
    const = dict(pipeline_mode=pl.Buffered(1))
    wmax = max(w for _, w in chunks)
    return pl.pallas_call(
        body,
        out_shape=jax.ShapeDtypeStruct((b, s, d), F32),
        grid=(b, s // tf),
        in_specs=[pl.BlockSpec((1, tf, d), lambda i, j: (i, j, 0)),
                  pl.BlockSpec((1, tf, d), lambda i, j: (i, j, 0)),
                  pl.BlockSpec((None, d, ff2), lambda i, j: (layer, 0, 0), **const),
                  pl.BlockSpec((None, ff2 // 2, d), lambda i, j: (layer, 0, 0), **const),
                  pl.BlockSpec((8, ff2), whole, **const),
                  pl.BlockSpec((nh, ff2), whole, **const)],
        out_specs=pl.BlockSpec((1, tf, d), lambda i, j: (i, j, 0)),
        scratch_shapes=[pltpu.VMEM((tf + 2 * MLP_PAD, wmax), F32)] * 2,
        compiler_params=_cparams(("arbitrary", "arbitrary")),
        name="gated_conv_mlp",
    )(x1, h2, w_up, w_down, conv_par, halo_up)


def _halo_rows(edges, nt):
    b, _, d = edges.shape
    ez = jnp.concatenate([edges, jnp.zeros((b, 1, d), edges.dtype)], axis=1)
    idx = []
    for j in range(nt):
        idx += [2 * j - 1 if j > 0 else 2 * nt, 2 * j + 2 if j < nt - 1 else 2 * nt]
    rows = jnp.take(ez, jnp.asarray(idx, jnp.int32), axis=1).reshape(b * nt * 2, d)
    pad = (-rows.shape[0]) % BF16_ROWS
    return jnp.pad(rows, ((0, pad), (0, 0))).astype(BF16)


def _rope_tables(s):
    def freqs(dim):
        return 1.0 / (ROPE_THETA ** (jnp.arange(0, dim, 2, dtype=F32) / dim))
    t = jnp.arange(s)
    rows = (t // GRID_W).astype(F32)
    cols = (t % GRID_W).astype(F32)
    pos = t.astype(F32)
    fa = freqs(HEAD_DIM // 2)
    ar, ac = fa[:, None] * rows[None, :], fa[:, None] * cols[None, :]
    am = freqs(C_ROPE)[:, None] * pos[None, :]
    cosa = jnp.concatenate([jnp.cos(ar), jnp.cos(ar), jnp.cos(ac), jnp.cos(ac)], axis=0)
    sina = jnp.concatenate([-jnp.sin(ar), jnp.sin(ar), -jnp.sin(ac), jnp.sin(ac)], axis=0)
    cosm = jnp.concatenate([jnp.cos(am), jnp.cos(am)], axis=0)
    sinm = jnp.concatenate([-jnp.sin(am), jnp.sin(am)], axis=0)
    return cosa, sina, cosm, sinm


def kernel(x, norm1_g, w_in, a_qn_g, a_kn_g, b_qn_g, b_kn_g, b_lam_q1, b_lam_k1, b_lam_q2, b_lam_k2, b_sub_g,
           c_qa_g, c_kva_g, c_wqb, c_wkvb, c_qn_g, c_kn_g, d_qn_g, d_kn_g, w_out, norm2_g, w_up, conv_w, conv_b,
           w_down):
    b, s, d = x.shape
    depth = w_in.shape[0]
    ff = w_down.shape[1]
    nt = s // TOK_TILE
    assert s % TOK_TILE == 0 and TOK_TILE % KSTEP == 0 and TQ % KSTEP == 0 and s % TQ == 0 and s % GRID_W == 0
    assert s <= 1 << (POS_SPLIT_BITS + BF16_MANTISSA_BITS)
    for window, dil in DILATED_CONFIGS:
        assert window // (2 * dil) == BAND_RADIUS and (s // dil) % min(BAND_Q, s // dil) == 0
        assert (s // dil) % min(BAND_QSTEP, s // dil) == 0 and BAND_QSTEP % BAND_Q == 0
        assert TOK_TILE % (dil * BF16_ROWS) == 0

    tr = lambda w: jnp.swapaxes(w, 1, 2).astype(BF16)
    gains = jnp.concatenate([norm1_g, a_qn_g, a_kn_g, b_qn_g, b_kn_g, c_qa_g, c_kva_g, c_qn_g, c_kn_g,
                             d_qn_g, d_kn_g], axis=1)[:, :, None]
    conv_par = jnp.concatenate([conv_w, conv_b[:, None, :], jnp.zeros((depth, 4, 2 * ff), F32)], axis=1)
    slopes = [2.0 ** (-8.0 * i / (2 * N_HEADS)) for i in range(1, 2 * N_HEADS + 1)]
    slopes_b, slopes_d = slopes[:N_HEADS], tuple(slopes[N_HEADS:])
    lam_init = jnp.asarray([0.8 - 0.6 * math.exp(-0.3 * l) for l in range(depth)], F32)
    lam = (jnp.exp(jnp.sum(b_lam_q1 * b_lam_k1, axis=1)) - jnp.exp(jnp.sum(b_lam_q2 * b_lam_k2, axis=1)) + lam_init)
    scal = jnp.concatenate([lam[:, None], (1.0 - lam_init)[:, None],
                            jnp.broadcast_to(jnp.asarray(slopes_b, F32), (depth, N_HEADS)),
                            jnp.zeros((depth, 2), F32)], axis=1)
    w_in_t, w_out_t, w_up_b, w_down_b = tr(w_in), tr(w_out), w_up.astype(BF16), w_down.astype(BF16)
    wqb_t, wkvb_t = tr(c_wqb), tr(c_wkvb)
    g2, sub_g = norm2_g[:, None, :], b_sub_g[:, :, None]
    cosa, sina, cosm, sinm = _rope_tables(s)
    assert ff % LANES == 0
    chunks = tuple((c0, min(MLP_CHUNK, ff - c0)) for c0 in range(0, ff, MLP_CHUNK))

    out = x
    for l in range(depth):
        (qa, ka, va, qb, kb, vb, qc, kc, vc, dq1, dq4, dq16, dk1, dk4, dk16, dv1, dv4, dv16) = _project(
            out, l, w_in_t, wqb_t[l], wkvb_t[l], gains[l], cosa, sina, cosm, sinm)
        oa = _dense_attention(qa, ka, va, HEAD_DIM, N_HEADS // A_KV_HEADS)
        ob = _diff_attention(scal[l], slopes_b, qb, kb, vb, sub_g[l])
        oc = _dense_attention(qc, kc, vc, C_QK, 1)
        branches = [_band_attention(q_, k_, v_, dil, slopes_d)
                    for (_, dil), q_, k_, v_ in zip(DILATED_CONFIGS, (dq1, dq4, dq16), (dk1, dk4, dk16), (dv1, dv4, dv16))]
        x1, h2, edges = _out_project(out, oa, ob, oc, branches, l, w_out_t, g2[l])
        halo = _halo_up(_halo_rows(edges, nt), l, w_up_b)
        out = _mlp(x1, h2, l, w_up_b, w_down_b, conv_par[l], halo, chunks)
    return out
```
